```python
import math
import jax
import jax.numpy as jnp
from jax import lax
import numpy as np

D_MODEL = 1024
BATCH = 16
SEQ = 256
DEPTH = 2
DEC_BATCH = 4
DEC_SEQ = 2048
PAST_LEN = 512

GRID_W = 64
A_HEADS = 4
A_QK = 32
A_V = 64
B_HEADS = 4
B_DK = 32
B_DV = 64
B_GATE_RANK = 16
B_TAU = 16.0
GLA_CHUNK = 64
C_CH = 256
C_KSIZE = 31
D_HEADS = 4
D_HEAD = 64
NA_ROWS = 8
NA_COLS = 16
NA_QCOLS = 16
Q_BLOCK = 128
ROPE_BASE = 10000.0
D_FF = 2816
N_EXPERTS = 8
TOP_K = 2
D_FF_EXPERT = 3584
EPS = 1e-6
NEG_INF = -1e30
MIX_W = A_HEADS * A_V + B_HEADS * B_DV + C_CH + D_HEADS * D_HEAD
IN_WIDTHS = (A_HEADS * 2 * A_QK, A_HEADS * 2 * A_QK, A_HEADS * A_V,
             B_HEADS * B_DK, B_HEADS * B_DK, B_HEADS * B_DV, B_HEADS * B_DV, 2 * B_GATE_RANK,
             2 * C_CH,
             D_HEADS * D_HEAD, D_HEADS * D_HEAD, D_HEADS * D_HEAD)
IN_W = sum(IN_WIDTHS)

kernel_name = 'hybrid_parallel_heads_dit_step'

F32 = jnp.float32


def rmsnorm(x, g):
    xf = x.astype(F32)
    y = xf * lax.rsqrt(jnp.mean(xf * xf, axis=-1, keepdims=True) + EPS)
    return y.astype(x.dtype) * g


def layernorm(x, g, b):
    xf = x.astype(F32)
    mu = jnp.mean(xf, axis=-1, keepdims=True)
    xc = xf - mu
    y = xc * lax.rsqrt(jnp.mean(xc * xc, axis=-1, keepdims=True) + EPS)
    return y.astype(x.dtype) * g + b


def adaln_mods(cond, w, b):
    m = jax.nn.silu(cond) @ w + b
    return [t[..., None, :] for t in jnp.split(m, 6, axis=-1)]


def modulate(x, g, shift, scale):
    return rmsnorm(x, g) * (1.0 + scale) + shift


def lambda_init(i):
    return 0.8 - 0.6 * math.exp(-0.3 * i)


def rope_2d(x):
    L = x.shape[1]
    d = x.shape[-1]
    half = d // 2
    nf = half // 2
    t = jnp.arange(L)
    row = (t // GRID_W).astype(F32)
    col = (t % GRID_W).astype(F32)
    inv = ROPE_BASE ** (-jnp.arange(nf, dtype=F32) / nf)
    bshape = (1, L) + (1,) * (x.ndim - 3) + (half,)

    def rot(xa, pos):
        ang = pos[:, None] * inv[None, :]
        cos = jnp.concatenate([jnp.cos(ang), jnp.cos(ang)], -1).reshape(bshape)
        sin = jnp.concatenate([jnp.sin(ang), jnp.sin(ang)], -1).reshape(bshape)
        x1, x2 = xa[..., :nf], xa[..., nf:]
        return xa * cos + jnp.concatenate([-x2, x1], -1) * sin

    xf = x.astype(F32)
    return jnp.concatenate([rot(xf[..., :half], row), rot(xf[..., half:], col)], -1).astype(x.dtype)


def gla_log_gate(z_lr, w, b):
    return jax.nn.log_sigmoid((z_lr @ w + b).astype(F32)) / B_TAU


def project(h, lp):
    bsz, L, _ = h.shape
    offs = np.cumsum(IN_WIDTHS)[:-1].tolist()
    (aq, ak, av, bq, bk, bv, bg, blr, cin, dq, dk, dv) = jnp.split(h @ lp['w_in'], offs, axis=-1)
    lr_f, lr_b = jnp.split(blr, 2, axis=-1)
    return dict(
        aq=rmsnorm(aq.reshape(bsz, L, A_HEADS, 2, A_QK), lp['a_qnorm']),
        ak=rmsnorm(ak.reshape(bsz, L, A_HEADS, 2, A_QK), lp['a_knorm']),
        av=av.reshape(bsz, L, A_HEADS, A_V),
        bq=bq.reshape(bsz, L, B_HEADS, B_DK) * (B_DK ** -0.5),
        bk=bk.reshape(bsz, L, B_HEADS, B_DK),
        bv=bv.reshape(bsz, L, B_HEADS, B_DV),
        bg=bg,
        la_f=gla_log_gate(lr_f, lp['b_gate'][0], lp['b_gate_bias'][0]).reshape(bsz, L, B_HEADS, B_DK),
        la_b=gla_log_gate(lr_b, lp['b_gate'][1], lp['b_gate_bias'][1]).reshape(bsz, L, B_HEADS, B_DK),
        cin=cin,
        dq=rmsnorm(dq.reshape(bsz, L, D_HEADS, D_HEAD), lp['d_qnorm']),
        dk=rmsnorm(dk.reshape(bsz, L, D_HEADS, D_HEAD), lp['d_knorm']),
        dv=dv.reshape(bsz, L, D_HEADS, D_HEAD),
    )


def diff_lambda(a_lam, lam_init):
    lam = a_lam.astype(F32)
    return jnp.exp(jnp.sum(lam[0] * lam[1])) - jnp.exp(jnp.sum(lam[2] * lam[3])) + lam_init


def diff_attn_blocked(q, k, v, lam):
    bsz, Lq, H = q.shape[:3]
    nb = Lq // Q_BLOCK
    scale = q.shape[-1] ** -0.5
    qb = jnp.moveaxis(q.reshape((bsz, nb, Q_BLOCK) + q.shape[2:]), 1, 0)

    def one(qblk):
        s = jnp.einsum('bqhmd,bkhmd->bhmqk', qblk, k, preferred_element_type=F32) * scale
        p = jax.nn.softmax(s, axis=-1)
        w = p[:, :, 0] - lam * p[:, :, 1]
        return jnp.einsum('bhqk,bkhe->bqhe', w.astype(v.dtype), v)

    out = lax.map(one, qb)
    return jnp.moveaxis(out, 0, 1).reshape(bsz, Lq, H, v.shape[-1])


def softmax_attn_blocked(q, k, v):
    bsz, Lq, H, dh = q.shape
    nb = Lq // Q_BLOCK
    qb = jnp.moveaxis(q.reshape(bsz, nb, Q_BLOCK, H, dh), 1, 0)

    def one(qblk):
        s = jnp.einsum('bqhd,bkhd->bhqk', qblk, k, preferred_element_type=F32) * (dh ** -0.5)
        p = jax.nn.softmax(s, axis=-1)
        return jnp.einsum('bhqk,bkhd->bqhd', p.astype(v.dtype), v)

    out = lax.map(one, qb)
    return jnp.moveaxis(out, 0, 1).reshape(bsz, Lq, H, v.shape[-1])


def neighbourhood_attn(q, k, v, k_ctx, v_ctx, rpb):
    bsz, L, H, dh = q.shape
    rows = L // GRID_W
    wr = min(NA_ROWS, rows)
    wc = NA_COLS
    span = min(GRID_W, NA_QCOLS + wc)
    ncb = GRID_W // NA_QCOLS
    nb = rows * ncb
    nk = wr * span
    r = np.arange(rows)
    rs = np.clip(r - wr // 2, 0, rows - wr)
    c0 = np.arange(ncb) * NA_QCOLS
    ss = np.clip(c0 - wc // 2, 0, GRID_W - span)
    qcol = c0[:, None] + np.arange(NA_QCOLS)
    cs = np.clip(qcol - wc // 2, 0, GRID_W - wc)
    krow = rs[:, None] + np.arange(wr)
    kcol = ss[:, None] + np.arange(span)
    key_idx = (krow[:, None, :, None] * GRID_W + kcol[None, :, None, :]).reshape(nb, nk)
    in_win = (kcol[:, None, :] >= cs[:, :, None]) & (kcol[:, None, :] < cs[:, :, None] + wc)
    mask = np.broadcast_to(in_win[None, :, :, None, :], (rows, ncb, NA_QCOLS, wr, span)).reshape(nb, NA_QCOLS, nk)
    dr_idx = krow - r[:, None] + (NA_ROWS - 1)
    dc_idx = np.clip(kcol[:, None, :] - qcol[:, :, None] + (NA_COLS - 1), 0, 2 * NA_COLS - 2)
    bias = rpb[:, dr_idx[:, None, None, :, None], dc_idx[None, :, :, None, :]]
    bias = jnp.where(mask, bias.reshape(H, nb, NA_QCOLS, nk).astype(F32), NEG_INF)
    qb = q.reshape(bsz, nb, NA_QCOLS, H, dh)
    kb = k[:, key_idx]
    vb = v[:, key_idx]
    scale = dh ** -0.5
    s_loc = jnp.einsum('bnqhd,bnkhd->bhnqk', qb, kb, preferred_element_type=F32) * scale + bias
    s_ctx = jnp.einsum('bnqhd,bchd->bhnqc', qb, k_ctx, preferred_element_type=F32) * scale
    p = jax.nn.softmax(jnp.concatenate([s_loc, s_ctx], axis=-1), axis=-1)
    o = (jnp.einsum('bhnqk,bnkhd->bnqhd', p[..., :nk].astype(v.dtype), vb)
         + jnp.einsum('bhnqc,bchd->bnqhd', p[..., nk:].astype(v.dtype), v_ctx))
    return o.reshape(bsz, L, H, dh)


def gla_scan(q, k, v, log_a, s0):
    bsz, L, H, _ = q.shape
    dv = v.shape[-1]
    n = L // GLA_CHUNK

    def to_chunks(t):
        return t.astype(F32).reshape(bsz, n, GLA_CHUNK, H, t.shape[-1]).transpose(1, 0, 3, 2, 4)

    qc, kc, vc, gc = to_chunks(q), to_chunks(k), to_chunks(v), to_chunks(log_a)
    b = jnp.cumsum(gc, axis=-2)
    causal = jnp.tril(jnp.ones((GLA_CHUNK, GLA_CHUNK), dtype=bool))[:, :, None]
    diff = b[..., :, None, :] - b[..., None, :, :]
    decay = jnp.where(causal, jnp.exp(jnp.where(causal, diff, 0.0)), 0.0)
    scores = jnp.einsum('nbhid,nbhjd,nbhijd->nbhij', qc, kc, decay)
    o_intra = jnp.einsum('nbhij,nbhje->nbhie', scores, vc)
    b_last = b[..., -1:, :]
    kv = jnp.einsum('nbhcd,nbhce->nbhde', kc * jnp.exp(b_last - b), vc)
    a_tot = jnp.exp(b_last[..., 0, :])

    def step(s, inp):
        kv_c, a_c = inp
        return a_c[..., None] * s + kv_c, s

    s_fin, s_prev = lax.scan(step, s0.astype(F32), (kv, a_tot))
    o_inter = jnp.einsum('nbhcd,nbhde->nbhce', qc * jnp.exp(b), s_prev)
    o = (o_intra + o_inter).transpose(1, 0, 3, 2, 4).reshape(bsz, L, H, dv)
    return o.astype(v.dtype), s_fin


def gla_bidir(q, k, v, la_f, la_b, s0_f, s0_b):
    o_f, s_f = gla_scan(q, k, v, la_f, s0_f)
    o_b, s_b = gla_scan(q[:, ::-1], k[:, ::-1], v[:, ::-1], la_b[:, ::-1], s0_b)
    return o_f + o_b[:, ::-1], s_f, s_b


def conv_module(cin, lp):
    a, g = jnp.split(cin, 2, axis=-1)
    u = a * jax.nn.sigmoid(g)
    u = lax.conv_general_dilated(u, lp['c_dw'][:, None, :], window_strides=(1,),
                                 padding=[(C_KSIZE // 2, C_KSIZE // 2)],
                                 dimension_numbers=('NWC', 'WIO', 'NWC'),
                                 feature_group_count=C_CH) + lp['c_dw_b']
    return jax.nn.silu(layernorm(u, lp['c_ln_g'], lp['c_ln_b']))


def merge_groups(o_a, o_b, bg, o_c, o_d, lp, lam_init):
    bsz, L = o_a.shape[:2]
    o_a = (rmsnorm(o_a, lp['a_subln']) * (1.0 - lam_init)).reshape(bsz, L, A_HEADS * A_V)
    o_b = rmsnorm(o_b, lp['b_onorm']).reshape(bsz, L, B_HEADS * B_DV) * jax.nn.silu(bg)
    o_d = o_d.reshape(bsz, L, D_HEADS * D_HEAD)
    return jnp.concatenate([o_a, o_b, o_c, o_d], axis=-1) @ lp['w_out']


def mix_context(h, lp, lam_init):
    p = project(h, lp)
    bsz, L = h.shape[:2]
    lam = diff_lambda(lp['a_lam'], lam_init)
    o_a = diff_attn_blocked(p['aq'], p['ak'], p['av'], lam)
    zeros = jnp.zeros((bsz, B_HEADS, B_DK, B_DV), F32)
    o_b, s_f, s_b = gla_bidir(p['bq'], p['bk'], p['bv'], p['la_f'], p['la_b'], zeros, zeros)
    o_c = conv_module(p['cin'], lp)
    o_d = softmax_attn_blocked(p['dq'], p['dk'], p['dv'])
    out = merge_groups(o_a, o_b, p['bg'], o_c, o_d, lp, lam_init)
    ctx = (p['ak'].reshape(bsz, L, A_HEADS, 2 * A_QK), p['av'], jnp.stack([s_f, s_b], axis=1), p['dk'], p['dv'])
    return out, ctx


def mix_latent(h, lp, lam_init, ak_c, av_c, sb_c, dk_c, dv_c):
    p = project(h, lp)
    bsz, L = h.shape[:2]
    lc = ak_c.shape[1]
    lam = diff_lambda(lp['a_lam'], lam_init)
    aq = rope_2d(p['aq'])
    k_all = jnp.concatenate([rope_2d(p['ak']), ak_c.reshape(bsz, lc, A_HEADS, 2, A_QK)], axis=1)
    v_all = jnp.concatenate([p['av'], av_c], axis=1)
    o_a = diff_attn_blocked(aq, k_all, v_all, lam)
    o_b, _, _ = gla_bidir(p['bq'], p['bk'], p['bv'], p['la_f'], p['la_b'], sb_c[:, 0], sb_c[:, 1])
    o_c = conv_module(p['cin'], lp)
    o_d = neighbourhood_attn(p['dq'], p['dk'], p['dv'], dk_c, dv_c, lp['d_rpb'])
    return merge_groups(o_a, o_b, p['bg'], o_c, o_d, lp, lam_init)


def swiglu(h, w1, w3, w2):
    return (jax.nn.silu(h @ w1) * (h @ w3)) @ w2


def moe_swiglu(h, router, router_b, w1, w3, w2):
    logits = (h @ router).astype(F32) + router_b.astype(F32)
    top_v, top_i = lax.top_k(logits, TOP_K)
    top_w = jax.nn.softmax(top_v, axis=-1)
    gates = jnp.sum(jax.nn.one_hot(top_i, N_EXPERTS, dtype=F32) * top_w[..., None], axis=-2).astype(h.dtype)
    out = jnp.zeros_like(h)
    for e in range(N_EXPERTS):
        out = out + gates[..., e:e + 1] * swiglu(h, w1[e], w3[e], w2[e])
    return out


def setup_inputs(seed: int = 0) -> dict:
    key = jax.random.key(seed)
    ks = iter(jax.random.split(key, 48))

    def nrm(shape, scale=1.0):
        return scale * jax.random.normal(next(ks), shape, F32)

    n_dense = (DEPTH + 1) // 2
    n_moe = DEPTH // 2
    return {
        'x_prompt': nrm((BATCH, SEQ, D_MODEL)),
        'x_sample': nrm((DEC_BATCH, DEC_SEQ, D_MODEL)),
        'cache_a_k': nrm((DEC_BATCH, DEPTH, PAST_LEN, A_HEADS, 2 * A_QK)),
        'cache_a_v': nrm((DEC_BATCH, DEPTH, PAST_LEN, A_HEADS, A_V)),
        'state_b': nrm((DEC_BATCH, DEPTH, 2, B_HEADS, B_DK, B_DV)),
        'cache_d_k': nrm((DEC_BATCH, DEPTH, PAST_LEN, D_HEADS, D_HEAD)),
        'cache_d_v': nrm((DEC_BATCH, DEPTH, PAST_LEN, D_HEADS, D_HEAD)),
        'c': nrm((DEC_BATCH, D_MODEL)),
        'c_ctx': nrm((D_MODEL,)),
        'norm1_g': 1.0 + nrm((DEPTH, D_MODEL), 0.05),
        'norm2_g': 1.0 + nrm((DEPTH, D_MODEL), 0.05),
        'w_ada': nrm((DEPTH, D_MODEL, 6 * D_MODEL), 0.5 * D_MODEL ** -0.5),
        'b_ada': nrm((DEPTH, 6 * D_MODEL), 0.02),
        'w_in': nrm((DEPTH, D_MODEL, IN_W), D_MODEL ** -0.5),
        'w_out': nrm((DEPTH, MIX_W, D_MODEL), MIX_W ** -0.5),
        'a_qnorm': 1.0 + nrm((DEPTH, A_QK), 0.05),
        'a_knorm': 1.0 + nrm((DEPTH, A_QK), 0.05),
        'a_lam': nrm((DEPTH, 4, A_QK), 0.1),
        'a_subln': 1.0 + nrm((DEPTH, A_V), 0.05),
        'b_gate': nrm((DEPTH, 2, B_GATE_RANK, B_HEADS * B_DK), B_GATE_RANK ** -0.5),
        'b_gate_bias': nrm((DEPTH, 2, B_HEADS * B_DK), 0.1),
        'b_onorm': 1.0 + nrm((DEPTH, B_DV), 0.05),
        'c_dw': nrm((DEPTH, C_KSIZE, C_CH), C_KSIZE ** -0.5),
        'c_dw_b': nrm((DEPTH, C_CH), 0.02),
        'c_ln_g': 1.0 + nrm((DEPTH, C_CH), 0.05),
        'c_ln_b': nrm((DEPTH, C_CH), 0.02),
        'd_qnorm': 1.0 + nrm((DEPTH, D_HEAD), 0.05),
        'd_knorm': 1.0 + nrm((DEPTH, D_HEAD), 0.05),
        'd_rpb': nrm((DEPTH, D_HEADS, 2 * NA_ROWS - 1, 2 * NA_COLS - 1), 0.1),
        'ffn_w1': nrm((n_dense, D_MODEL, D_FF), D_MODEL ** -0.5),
        'ffn_w3': nrm((n_dense, D_MODEL, D_FF), D_MODEL ** -0.5),
        'ffn_w2': nrm((n_dense, D_FF, D_MODEL), D_FF ** -0.5),
        'moe_router': nrm((n_moe, D_MODEL, N_EXPERTS), D_MODEL ** -0.5),
        'moe_router_b': nrm((n_moe, N_EXPERTS), 0.01),
        'moe_w1': nrm((n_moe, N_EXPERTS, D_MODEL, D_FF_EXPERT), D_MODEL ** -0.5),
        'moe_w3': nrm((n_moe, N_EXPERTS, D_MODEL, D_FF_EXPERT), D_MODEL ** -0.5),
        'moe_w2': nrm((n_moe, N_EXPERTS, D_FF_EXPERT, D_MODEL), D_FF_EXPERT ** -0.5),
    }


def reference(x_prompt, x_sample, cache_a_k, cache_a_v, state_b, cache_d_k, cache_d_v, c, c_ctx,
              norm1_g, norm2_g, w_ada, b_ada, w_in, w_out, a_qnorm, a_knorm, a_lam, a_subln,
              b_gate, b_gate_bias, b_onorm, c_dw, c_dw_b, c_ln_g, c_ln_b, d_qnorm, d_knorm, d_rpb,
              ffn_w1, ffn_w3, ffn_w2, moe_router, moe_router_b, moe_w1, moe_w3, moe_w2):

    def layer_params(i):
        return dict(w_in=w_in[i], w_out=w_out[i], a_qnorm=a_qnorm[i], a_knorm=a_knorm[i], a_lam=a_lam[i],
                    a_subln=a_subln[i], b_gate=b_gate[i], b_gate_bias=b_gate_bias[i], b_onorm=b_onorm[i],
                    c_dw=c_dw[i], c_dw_b=c_dw_b[i], c_ln_g=c_ln_g[i], c_ln_b=c_ln_b[i],
                    d_qnorm=d_qnorm[i], d_knorm=d_knorm[i], d_rpb=d_rpb[i])

    def channel_mixer(h, i):
        j = i // 2
        if i % 2 == 0:
            return swiglu(h, ffn_w1[j], ffn_w3[j], ffn_w2[j])
        return moe_swiglu(h, moe_router[j], moe_router_b[j], moe_w1[j], moe_w3[j], moe_w2[j])

    x = x_prompt
    new_ak, new_av, new_sb, new_dk, new_dv = [], [], [], [], []
    for i in range(DEPTH):
        lp = layer_params(i)
        sh1, sc1, g1, sh2, sc2, g2 = adaln_mods(c_ctx, w_ada[i], b_ada[i])
        mix, (ak, av, sb, dk, dv) = mix_context(modulate(x, norm1_g[i], sh1, sc1), lp, lambda_init(i))
        x = x + g1 * mix
        x = x + g2 * channel_mixer(modulate(x, norm2_g[i], sh2, sc2), i)
        new_ak.append(ak)
        new_av.append(av)
        new_sb.append(sb)
        new_dk.append(dk)
        new_dv.append(dv)
    y_prompt = x
    new_cache_a_k = jnp.stack(new_ak, axis=1)
    new_cache_a_v = jnp.stack(new_av, axis=1)
    new_state_b = jnp.stack(new_sb, axis=1)
    new_cache_d_k = jnp.stack(new_dk, axis=1)
    new_cache_d_v = jnp.stack(new_dv, axis=1)

    x = x_sample
    for i in range(DEPTH):
        lp = layer_params(i)
        sh1, sc1, g1, sh2, sc2, g2 = adaln_mods(c, w_ada[i], b_ada[i])
        mix = mix_latent(modulate(x, norm1_g[i], sh1, sc1), lp, lambda_init(i),
                         cache_a_k[:, i], cache_a_v[:, i], state_b[:, i], cache_d_k[:, i], cache_d_v[:, i])
        x = x + g1 * mix
        x = x + g2 * channel_mixer(modulate(x, norm2_g[i], sh2, sc2), i)
    y_sample = x

    return (y_prompt, y_sample, new_cache_a_k, new_cache_a_v, new_state_b, new_cache_d_k, new_cache_d_v)
```

```python
import functools
import math

import numpy as np
import jax
import jax.numpy as jnp
from jax import lax
from jax.experimental import pallas as pl
from jax.experimental.pallas import tpu as pltpu

F32 = jnp.float32
BF16 = jnp.bfloat16

D_MODEL = 1024
BATCH = 16
SEQ = 256
DEPTH = 2
DEC_BATCH = 4
DEC_SEQ = 2048
PAST_LEN = 512
GRID_W = 64
A_HEADS = 4
A_QK = 32
A_V = 64
B_HEADS = 4
B_DK = 32
B_DV = 64
B_GATE_RANK = 16
B_TAU = 16.0
C_CH = 256
C_KSIZE = 31
D_HEADS = 4
D_HEAD = 64
NA_ROWS = 8
NA_COLS = 16
ROPE_BASE = 10000.0
D_FF = 2816
N_EXPERTS = 8
D_FF_EXPERT = 3584
EPS = 1e-6
NEG_INF = -1e30

GW = 256
N_PROJ_BLOCKS = 12
PROJ_W = GW * N_PROJ_BLOCKS
BLK_AQ, BLK_AK, BLK_AV, BLK_BQK, BLK_BV, BLK_BG = 0, 1, 2, 3, 4, 5
BLK_C = 3
BLK_DQ, BLK_DK, BLK_DV, BLK_LR = 8, 9, 10, 11

GLA_SUB = 16
GLA_ROWS = 256
VMEM_LIMIT = 56 * 1024 * 1024


def _params(sem, vmem=VMEM_LIMIT):
    return pltpu.CompilerParams(dimension_semantics=sem, vmem_limit_bytes=vmem)


def _dot(a, b):
    return jnp.dot(a.astype(BF16), b.astype(BF16), preferred_element_type=F32)


def _dot_nt(a, b):
    return lax.dot_general(a.astype(BF16), b.astype(BF16), (((1,), (1,)), ((), ())),
                           preferred_element_type=F32)


def _dot_exact_rhs(x, m):
    hi = x.astype(BF16)
    lo = (x - hi.astype(F32)).astype(BF16)
    return (jnp.dot(hi, m, preferred_element_type=F32) + jnp.dot(lo, m, preferred_element_type=F32))


def _sigmoid(x):
    return 1.0 / (1.0 + jnp.exp(-x))


def _silu(x):
    return x * _sigmoid(x)


def _seg_matrix(n, seg):
    r = lax.broadcasted_iota(jnp.int32, (n, n), 0)
    c = lax.broadcasted_iota(jnp.int32, (n, n), 1)
    return jnp.where((r ^ c) < seg, 1.0, 0.0).astype(BF16)


def _seg_rmsnorm(x, seg, w):
    ms = _dot_exact_rhs(x * x, _seg_matrix(x.shape[-1], seg)) * (1.0 / seg)
    return x * lax.rsqrt(ms + EPS) * w


def _lane_mask(shape, lo, width):
    lane = lax.broadcasted_iota(jnp.int32, shape, len(shape) - 1)
    return (lane >= lo) & (lane < lo + width)


def _modulate(x, g, shift, scale):
    y = x * lax.rsqrt(jnp.mean(x * x, axis=-1, keepdims=True) + EPS)
    return y * g * (1.0 + scale) + shift


def _rope(x, cos, sin):
    lane = lax.broadcasted_iota(jnp.int32, x.shape, 1)
    w = x.shape[1]
    rot = jnp.where((lane & 15) < 8, -pltpu.roll(x, w - 8, 1), pltpu.roll(x, 8, 1))
    return x * cos + rot * sin


def _adaln_kernel(c_ref, w_ref, b_ref, o_ref):
    o_ref[...] = _dot(_silu(c_ref[...]), w_ref[...]) + b_ref[...]


def _adaln(cond8, w_ada, b_ada3, layer):
    tn = 1536
    n = 6 * D_MODEL
    return pl.pallas_call(
        _adaln_kernel,
        out_shape=jax.ShapeDtypeStruct((8, n), F32),
        grid=(n // tn,),
        in_specs=[pl.BlockSpec((8, D_MODEL), lambda j: (0, 0)),
                  pl.BlockSpec((None, D_MODEL, tn), lambda j: (layer, 0, j)),
                  pl.BlockSpec((None, 1, tn), lambda j: (layer, 0, j))],
        out_specs=pl.BlockSpec((8, tn), lambda j: (0, j)),
        compiler_params=_params(("arbitrary",)),
        name="adaln",
    )(cond8, w_ada, b_ada3)


def _mod_row(latent, tm, seq_len):
    if latent:
        return lambda t: 1 + (t * tm) // seq_len
    return lambda t: 0


def _inproj_kernel(x_ref, g_ref, mod_ref, w_ref, o_ref):
    d = D_MODEL
    h = _modulate(x_ref[...], g_ref[...], mod_ref[:, 0:d], mod_ref[:, d:2 * d])
    o_ref[...] = _dot(h, w_ref[...])


def _inproj(x, norm_g3, mods3, w_in, layer, latent, seq_len):
    t_tokens = x.shape[0]
    tm = 512
    row = _mod_row(latent, tm, seq_len)
    return pl.pallas_call(
        _inproj_kernel,
        out_shape=jax.ShapeDtypeStruct((t_tokens, PROJ_W), F32),
        grid=(t_tokens // tm,),
        in_specs=[pl.BlockSpec((tm, D_MODEL), lambda t: (t, 0)),
                  pl.BlockSpec((None, 1, D_MODEL), lambda t: (layer, 0, 0)),
                  pl.BlockSpec((None, 1, 6 * D_MODEL), lambda t: (row(t), 0, 0)),
                  pl.BlockSpec((None, D_MODEL, PROJ_W), lambda t: (layer, 0, 0))],
        out_specs=pl.BlockSpec((tm, PROJ_W), lambda t: (t, 0)),
        compiler_params=_params(("arbitrary",)),
        name="inproj",
    )(x, norm_g3, mods3, w_in)


def _diff_lambda(lam_ref, lam_init):
    lam = lam_ref[...]
    s1 = jnp.sum(lam[0:1] * lam[1:2], axis=1, keepdims=True)
    s2 = jnp.sum(lam[2:3] * lam[3:4], axis=1, keepdims=True)
    return jnp.exp(s1) - jnp.exp(s2) + lam_init


def _attn_a_kernel(*refs, n_own, n_ctx, tq, latent, lam_init):
    if latent:
        (q_ref, k_ref, v_ref, ck_ref, cv_ref, cos_ref, sin_ref, qn_ref, kn_ref, lam_ref, sub_ref,
         o_ref, ks, vs) = refs
    else:
        (q_ref, k_ref, v_ref, qn_ref, kn_ref, lam_ref, sub_ref, o_ref, ko_ref, ks, vs) = refs
    t = pl.program_id(1)

    @pl.when(t == 0)
    def _():
        kn = _seg_rmsnorm(k_ref[...], A_QK, kn_ref[...])
        if latent:
            kn = _rope(kn, cos_ref[...], sin_ref[...])
            ks[n_own:n_own + n_ctx, :] = ck_ref[...].astype(BF16)
            vs[n_own:n_own + n_ctx, :] = cv_ref[...].astype(BF16)
        else:
            ko_ref[...] = kn
        ks[0:n_own, :] = kn.astype(BF16)
        vs[0:n_own, :] = v_ref[...].astype(BF16)

    qn = _seg_rmsnorm(q_ref[...], A_QK, qn_ref[...])
    if latent:
        r0 = pl.multiple_of(t * tq, tq)
        qn = _rope(qn, cos_ref[pl.ds(r0, tq), :], sin_ref[pl.ds(r0, tq), :])
    qn = qn * (A_QK ** -0.5)
    lam = _diff_lambda(lam_ref, lam_init)
    k_all = ks[...]
    v_all = vs[...]
    o = jnp.zeros((tq, GW), F32)
    for h in range(A_HEADS):
        ps = []
        for m in range(2):
            qm = jnp.where(_lane_mask(qn.shape, (2 * h + m) * A_QK, A_QK), qn, 0.0)
            s = _dot_nt(qm, k_all)
            e = jnp.exp(s - jnp.max(s, axis=-1, keepdims=True))
            ps.append(e * (1.0 / jnp.sum(e, axis=-1, keepdims=True)))
        w = ps[0] - lam * ps[1]
        o = o + jnp.where(_lane_mask(o.shape, h * A_V, A_V), _dot(w, v_all), 0.0)
    o_ref[...] = _seg_rmsnorm(o, A_V, sub_ref[...]) * (1.0 - lam_init)


def _attn_a(proj, lp, layer, latent, nseq, seq_len, lam_init, cache_k=None, cache_v=None, rope=None):
    tq = 256
    nt = seq_len // tq
    n_ctx = PAST_LEN if latent else 0
    kern = functools.partial(_attn_a_kernel, n_own=seq_len, n_ctx=n_ctx, tq=tq, latent=latent,
                             lam_init=lam_init)
    vec = lambda name: pl.BlockSpec((None, 1, GW), lambda b, t: (layer, 0, 0))
    in_specs = [pl.BlockSpec((tq, GW), lambda b, t: (b * nt + t, BLK_AQ)),
                pl.BlockSpec((seq_len, GW), lambda b, t: (b, BLK_AK)),
                pl.BlockSpec((seq_len, GW), lambda b, t: (b, BLK_AV))]
    args = [proj, proj, proj]
    if latent:
        in_specs += [pl.BlockSpec((None, None, PAST_LEN, GW), lambda b, t: (b, layer, 0, 0)),
                     pl.BlockSpec((None, None, PAST_LEN, GW), lambda b, t: (b, layer, 0, 0)),
                     pl.BlockSpec((seq_len, GW), lambda b, t: (0, 0)),
                     pl.BlockSpec((seq_len, GW), lambda b, t: (0, 0))]
        args += [cache_k, cache_v, rope[0], rope[1]]
    in_specs += [vec("q"), vec("k"),
                 pl.BlockSpec((None, 4, A_QK), lambda b, t: (layer, 0, 0)),
                 vec("s")]
    args += [lp["a_qnorm"], lp["a_knorm"], lp["a_lam"], lp["a_subln"]]
    out_shape = [jax.ShapeDtypeStruct((nseq * seq_len, GW), F32)]
    out_specs = [pl.BlockSpec((tq, GW), lambda b, t: (b * nt + t, 0))]
    if not latent:
        out_shape.append(jax.ShapeDtypeStruct((nseq * seq_len, GW), F32))
        out_specs.append(pl.BlockSpec((seq_len, GW), lambda b, t: (b, 0)))
    return pl.pallas_call(
        kern,
        out_shape=out_shape,
        grid=(nseq, nt),
        in_specs=in_specs,
        out_specs=out_specs,
        scratch_shapes=[pltpu.VMEM((seq_len + n_ctx, GW), BF16), pltpu.VMEM((seq_len + n_ctx, GW), BF16)],
        compiler_params=_params(("arbitrary", "arbitrary")),
        name="attn_a_lat" if latent else "attn_a_ctx",
    )(*args)


def _attn_d_ctx_kernel(q_ref, k_ref, v_ref, qn_ref, kn_ref, o_ref, ko_ref):
    kn = _seg_rmsnorm(k_ref[...], D_HEAD, kn_ref[...])
    ko_ref[...] = kn
    qn = _seg_rmsnorm(q_ref[...], D_HEAD, qn_ref[...]) * (D_HEAD ** -0.5)
    kb = kn.astype(BF16)
    vb = v_ref[...].astype(BF16)
    o = jnp.zeros(qn.shape, F32)
    for h in range(D_HEADS):
        hm = _lane_mask(qn.shape, h * D_HEAD, D_HEAD)
        s = _dot_nt(jnp.where(hm, qn, 0.0), kb)
        e = jnp.exp(s - jnp.max(s, axis=-1, keepdims=True))
        p = e * (1.0 / jnp.sum(e, axis=-1, keepdims=True))
        o = o + jnp.where(hm, _dot(p, vb), 0.0)
    o_ref[...] = o


def _attn_d_ctx(proj, lp, layer, nseq, seq_len):
    vec = pl.BlockSpec((None, 1, GW), lambda b: (layer, 0, 0))
    blk = lambda c: pl.BlockSpec((seq_len, GW), lambda b: (b, c))
    return pl.pallas_call(
        _attn_d_ctx_kernel,
        out_shape=[jax.ShapeDtypeStruct((nseq * seq_len, GW), F32)] * 2,
        grid=(nseq,),
        in_specs=[blk(BLK_DQ), blk(BLK_DK), blk(BLK_DV), vec, vec],
        out_specs=[blk(0), blk(0)],
        compiler_params=_params(("arbitrary",)),
        name="attn_d_ctx",
    )(proj, proj, proj, lp["d_qnorm"], lp["d_knorm"])


def _attn_d_lat_kernel(q_ref, k_ref, v_ref, ck_ref, cv_ref, qn_ref, kn_ref, bias_ref, o_ref,
                       ks, vs, cks, cvs):
    r = pl.program_id(1)
    n_rows = DEC_SEQ // GRID_W
    n_loc = NA_ROWS * GRID_W

    @pl.when(r == 0)
    def _():
        ks[...] = _seg_rmsnorm(k_ref[...], D_HEAD, kn_ref[...]).astype(BF16)
        vs[...] = v_ref[...].astype(BF16)
        cks[...] = ck_ref[...].astype(BF16)
        cvs[...] = cv_ref[...].astype(BF16)

    qn = _seg_rmsnorm(q_ref[...], D_HEAD, qn_ref[...]) * (D_HEAD ** -0.5)
    row_start = jnp.clip(r - NA_ROWS // 2, 0, n_rows - NA_ROWS)
    off = r - row_start
    k0 = pl.multiple_of(row_start * GRID_W, GRID_W)
    kl = ks[pl.ds(k0, n_loc), :]
    vl = vs[pl.ds(k0, n_loc), :]
    kc = cks[...]
    vc = cvs[...]
    o = jnp.zeros(qn.shape, F32)
    for h in range(D_HEADS):
        hm = _lane_mask(qn.shape, h * D_HEAD, D_HEAD)
        qm = jnp.where(hm, qn, 0.0)
        s_loc = _dot_nt(qm, kl) + bias_ref[h, off]
        s_ctx = _dot_nt(qm, kc)
        mx = jnp.maximum(jnp.max(s_loc, axis=-1, keepdims=True), jnp.max(s_ctx, axis=-1, keepdims=True))
        e_loc = jnp.exp(s_loc - mx)
        e_ctx = jnp.exp(s_ctx - mx)
        inv = 1.0 / (jnp.sum(e_loc, axis=-1, keepdims=True) + jnp.sum(e_ctx, axis=-1, keepdims=True))
        o = o + jnp.where(hm, _dot(e_loc * inv, vl) + _dot(e_ctx * inv, vc), 0.0)
    o_ref[...] = o


def _na_bias_table(rpb):
    q = np.arange(GRID_W)
    kcol = np.arange(GRID_W)
    cs = np.clip(q - NA_COLS // 2, 0, GRID_W - NA_COLS)
    in_win = (kcol[None, :] >= cs[:, None]) & (kcol[None, :] < cs[:, None] + NA_COLS)
    dc = np.clip(kcol[None, :] - q[:, None] + (NA_COLS - 1), 0, 2 * NA_COLS - 2)
    off = np.arange(NA_ROWS)
    dr = np.arange(NA_ROWS)[None, :] - off[:, None] + (NA_ROWS - 1)
    b = rpb[:, dr[:, None, :, None], dc[None, :, None, :]]
    b = jnp.where(in_win[None, None, :, None, :], b.astype(F32), NEG_INF)
    return b.reshape(D_HEADS, NA_ROWS, GRID_W, NA_ROWS * GRID_W)


def _attn_d_lat(proj, lp, layer, cache_k, cache_v, bias):
    n_rows = DEC_SEQ // GRID_W
    vec = pl.BlockSpec((None, 1, GW), lambda b, r: (layer, 0, 0))
    seq = lambda c: pl.BlockSpec((DEC_SEQ, GW), lambda b, r: (b, c))
    cache = pl.BlockSpec((None, None, PAST_LEN, GW), lambda b, r: (b, layer, 0, 0))
    return pl.pallas_call(
        _attn_d_lat_kernel,
        out_shape=jax.ShapeDtypeStruct((DEC_BATCH * DEC_SEQ, GW), F32),
        grid=(DEC_BATCH, n_rows),
        in_specs=[pl.BlockSpec((GRID_W, GW), lambda b, r: (b * n_rows + r, BLK_DQ)),
                  seq(BLK_DK), seq(BLK_DV), cache, cache, vec, vec,
                  pl.BlockSpec(bias.shape, lambda b, r: (0, 0, 0, 0))],
        out_specs=pl.BlockSpec((GRID_W, GW), lambda b, r: (b * n_rows + r, 0)),
        scratch_shapes=[pltpu.VMEM((DEC_SEQ, GW), BF16), pltpu.VMEM((DEC_SEQ, GW), BF16),
                        pltpu.VMEM((PAST_LEN, GW), BF16), pltpu.VMEM((PAST_LEN, GW), BF16)],
        compiler_params=_params(("arbitrary", "arbitrary")),
        name="attn_d_lat",
    )(proj, proj, proj, cache_k, cache_v, lp["d_qnorm"], lp["d_knorm"], bias)


def _conv_kernel(c_ref, w_ref, b_ref, g_ref, beta_ref, o_ref, pad, *, seq_len):
    half = C_KSIZE // 2
    top = 16
    cin = c_ref[...]
    u = cin[:, :C_CH] * _sigmoid(cin[:, C_CH:])
    pad[0:top, :] = jnp.zeros((top, C_CH), F32)
    pad[top + seq_len:top + seq_len + top, :] = jnp.zeros((top, C_CH), F32)
    pad[top:top + seq_len, :] = u
    w = w_ref[...]
    rb = 256
    for r0 in range(0, seq_len, rb):
        acc = jnp.zeros((rb, C_CH), F32)
        for k in range(C_KSIZE):
            acc = acc + pad[r0 + top - half + k:r0 + top - half + k + rb, :] * w[k:k + 1, :]
        acc = acc + b_ref[...]
        mu = jnp.mean(acc, axis=-1, keepdims=True)
        xc = acc - mu
        y = xc * lax.rsqrt(jnp.mean(xc * xc, axis=-1, keepdims=True) + EPS) * g_ref[...] + beta_ref[...]
        o_ref[r0:r0 + rb, :] = _silu(y)


def _conv(proj, lp, layer, nseq, seq_len):
    vec = pl.BlockSpec((None, 1, C_CH), lambda b: (layer, 0, 0))
    return pl.pallas_call(
        functools.partial(_conv_kernel, seq_len=seq_len),
        out_shape=jax.ShapeDtypeStruct((nseq * seq_len, C_CH), F32),
        grid=(nseq,),
        in_specs=[pl.BlockSpec((seq_len, 2 * C_CH), lambda b: (b, BLK_C)),
                  pl.BlockSpec((None, C_KSIZE, C_CH), lambda b: (layer, 0, 0)),
                  vec, vec, vec],
        out_specs=pl.BlockSpec((seq_len, C_CH), lambda b: (b, 0)),
        scratch_shapes=[pltpu.VMEM((seq_len + 32, C_CH), F32)],
        compiler_params=_params(("arbitrary",)),
        name="conv",
    )(proj, lp["c_dw"], lp["c_dw_b"], lp["c_ln_g"], lp["c_ln_b"])


def _gla_kernel(*refs, reverse, has_s0, n_blocks):
    refs = list(refs)
    qk_ref, v_ref, lr_ref, wg_ref, gb_ref = refs[:5]
    refs = refs[5:]
    s0_ref = refs.pop(0) if has_s0 else None
    if reverse:
        of_ref, bg_ref, on_ref = refs[:3]
        refs = refs[3:]
    o_ref, so_ref, st = refs
    rb, cs = GLA_ROWS, GLA_SUB
    hk = B_HEADS * B_DK
    j = pl.program_id(1)

    @pl.when(j == 0)
    def _():
        st[...] = s0_ref[...] if has_s0 else jnp.zeros(st.shape, F32)

    qk = qk_ref[...]
    q = qk[:, :hk] * (B_DK ** -0.5)
    k = qk[:, hk:]
    v = v_ref[...]
    pre = _dot(lr_ref[...], wg_ref[...]) + gb_ref[...]
    g = (jnp.minimum(pre, 0.0) - jnp.log(1.0 + jnp.exp(-jnp.abs(pre)))) * (1.0 / B_TAU)

    pos = lax.broadcasted_iota(jnp.int32, (rb, hk), 0) & (cs - 1)
    pre_sum = g
    suf_sum = g
    step = 1
    while step < cs:
        pre_sum = pre_sum + jnp.where(pos >= step, pltpu.roll(pre_sum, step, 0), 0.0)
        suf_sum = suf_sum + jnp.where(pos < cs - step, pltpu.roll(suf_sum, rb - step, 0), 0.0)
        step *= 2
    total = pre_sum + suf_sum - g
    if reverse:
        z = suf_sum
        k_dec = k * jnp.exp(pre_sum - g)
    else:
        z = pre_sum
        k_dec = k * jnp.exp(suf_sum - g)
    q_dec = q * jnp.exp(z)

    r_i = lax.broadcasted_iota(jnp.int32, (hk, GW), 0)
    c_i = lax.broadcasted_iota(jnp.int32, (hk, GW), 1)
    head_sum = jnp.where((r_i >> 5) == (c_i >> 6), 1.0, 0.0).astype(BF16)

    o = jnp.dot((q * k).astype(BF16), head_sum, preferred_element_type=F32) * v
    for d in range(1, cs):
        sh = rb - d if reverse else d
        valid = (pos < cs - d) if reverse else (pos >= d)
        decay = jnp.exp(jnp.minimum(z - pltpu.roll(z, sh, 0), 0.0))
        t = jnp.where(valid, q * pltpu.roll(k, sh, 0) * decay, 0.0)
        o = o + jnp.dot(t.astype(BF16), head_sum, preferred_element_type=F32) * pltpu.roll(v, sh, 0)

    r_s = lax.broadcasted_iota(jnp.int32, st.shape, 0)
    c_s = lax.broadcasted_iota(jnp.int32, st.shape, 1)
    diag = (r_s >> 6) == (c_s >> 5)
    v_t = v.T
    col = lax.broadcasted_iota(jnp.int32, v_t.shape, 1)
    state = st[...]
    n_sub = rb // cs
    inter = [None] * n_sub
    order = range(n_sub - 1, -1, -1) if reverse else range(n_sub)
    for n in order:
        rows = slice(n * cs, (n + 1) * cs)
        inter[n] = _dot_nt(q_dec[rows], state)
        v_sel = jnp.where((col >= n * cs) & (col < (n + 1) * cs), v_t, 0.0)
        kv_t = _dot(v_sel, k_dec)
        state = state * jnp.exp(total[n * cs:n * cs + 1, :]) + jnp.where(diag, kv_t, 0.0)
    st[...] = state
    o = o + jnp.concatenate(inter, axis=0)

    if reverse:
        o = _seg_rmsnorm(o + of_ref[...], B_DV, on_ref[...]) * _silu(bg_ref[...])
    o_ref[...] = o

    @pl.when(j == n_blocks - 1)
    def _():
        so_ref[...] = state


def _gla(proj, lp, layer, nseq, seq_len, reverse, s0=None, o_fwd=None):
    rb = GLA_ROWS
    nb = seq_len // rb
    hk = B_HEADS * B_DK
    blk = (lambda b, j: b * nb + nb - 1 - j) if reverse else (lambda b, j: b * nb + j)
    tile = lambda c: pl.BlockSpec((rb, GW), lambda b, j: (blk(b, j), c))
    d = 1 if reverse else 0
    in_specs = [tile(BLK_BQK), tile(BLK_BV), tile(BLK_LR),
                pl.BlockSpec((None, None, GW, hk), lambda b, j: (layer, d, 0, 0)),
                pl.BlockSpec((None, None, 1, hk), lambda b, j: (layer, d, 0, 0))]
    args = [proj, proj, proj, lp["b_gate"], lp["b_gate_bias"]]
    if s0 is not None:
        in_specs.append(pl.BlockSpec((None, None, GW, hk), lambda b, j: (b, d, 0, 0)))
        args.append(s0)
    if reverse:
        in_specs += [tile(0), tile(BLK_BG), pl.BlockSpec((None, 1, GW), lambda b, j: (layer, 0, 0))]
        args += [o_fwd, proj, lp["b_onorm"]]
    return pl.pallas_call(
        functools.partial(_gla_kernel, reverse=reverse, has_s0=s0 is not None, n_blocks=nb),
        out_shape=[jax.ShapeDtypeStruct((nseq * seq_len, GW), F32),
                   jax.ShapeDtypeStruct((nseq, GW, hk), F32)],
        grid=(nseq, nb),
        in_specs=in_specs,
        out_specs=[tile(0), pl.BlockSpec((None, GW, hk), lambda b, j: (b, 0, 0))],
        scratch_shapes=[pltpu.VMEM((GW, hk), F32)],
        compiler_params=_params(("arbitrary", "arbitrary")),
        name="gla_bwd" if reverse else "gla_fwd",
    )(*args)


def _state_to_kernel_layout(s):
    eye = jnp.eye(B_HEADS, dtype=s.dtype)
    t = jnp.einsum("bxhde,hg->bxhegd", s, eye)
    return t.reshape(s.shape[0], 2, B_HEADS * B_DV, B_HEADS * B_DK)


def _state_from_kernel_layout(st):
    t = st.reshape(st.shape[0], B_HEADS, B_DV, B_HEADS, B_DK)
    return jnp.stack([t[:, h, :, h, :] for h in range(B_HEADS)], axis=1).transpose(0, 1, 3, 2)


def _outproj_kernel(a_ref, b_ref, c_ref, d_ref, x_ref, mod_ref, w_ref, o_ref):
    mix = jnp.concatenate([a_ref[...], b_ref[...], c_ref[...], d_ref[...]], axis=-1)
    gate = mod_ref[:, 2 * D_MODEL:3 * D_MODEL]
    o_ref[...] = x_ref[...] + gate * _dot(mix, w_ref[...])


def _outproj(o_a, o_b, o_c, o_d, x, mods3, w_out, layer, latent, seq_len):
    t_tokens = x.shape[0]
    tm = 512
    row = _mod_row(latent, tm, seq_len)
    part = pl.BlockSpec((tm, GW), lambda t: (t, 0))
    return pl.pallas_call(
        _outproj_kernel,
        out_shape=jax.ShapeDtypeStruct((t_tokens, D_MODEL), F32),
        grid=(t_tokens // tm,),
        in_specs=[part, part, part, part,
                  pl.BlockSpec((tm, D_MODEL), lambda t: (t, 0)),
                  pl.BlockSpec((None, 1, 6 * D_MODEL), lambda t: (row(t), 0, 0)),
                  pl.BlockSpec((None, 4 * GW, D_MODEL), lambda t: (layer, 0, 0))],
        out_specs=pl.BlockSpec((tm, D_MODEL), lambda t: (t, 0)),
        compiler_params=_params(("arbitrary",)),
        name="outproj",
    )(o_a, o_b, o_c, o_d, x, mods3, w_out)


def _router_kernel(x_ref, g_ref, mod_ref, r_ref, rb_ref, o_ref):
    d = D_MODEL
    h = _modulate(x_ref[...], g_ref[...], mod_ref[:, 3 * d:4 * d], mod_ref[:, 4 * d:5 * d])
    r = r_ref[...]
    h_hi = h.astype(BF16)
    h_lo = (h - h_hi.astype(F32)).astype(BF16)
    r_hi = r.astype(BF16)
    r_lo = (r - r_hi.astype(F32)).astype(BF16)
    dot = lambda a, b: jnp.dot(a, b, preferred_element_type=F32)
    logits = dot(h_hi, r_hi) + dot(h_lo, r_hi) + dot(h_hi, r_lo) + rb_ref[...]
    lane = lax.broadcasted_iota(jnp.int32, logits.shape, 1).astype(F32)
    big = float(logits.shape[1])
    logits = jnp.where(lane < N_EXPERTS, logits, -jnp.inf)
    v1 = jnp.max(logits, axis=-1, keepdims=True)
    i1 = jnp.min(jnp.where(logits == v1, lane, big), axis=-1, keepdims=True)
    rest = jnp.where(lane == i1, -jnp.inf, logits)
    v2 = jnp.max(rest, axis=-1, keepdims=True)
    i2 = jnp.min(jnp.where(rest == v2, lane, big), axis=-1, keepdims=True)
    e2 = jnp.exp(v2 - v1)
    inv = 1.0 / (1.0 + e2)
    o_ref[...] = jnp.where(lane == i1, inv, 0.0) + jnp.where(lane == i2, e2 * inv, 0.0)


def _router(x, norm_g3, mods3, router, router_b, layer, j, latent, seq_len):
    t_tokens = x.shape[0]
    tm = 512
    row = _mod_row(latent, tm, seq_len)
    lanes = router.shape[-1]
    return pl.pallas_call(
        _router_kernel,
        out_shape=jax.ShapeDtypeStruct((t_tokens, lanes), F32),
        grid=(t_tokens // tm,),
        in_specs=[pl.BlockSpec((tm, D_MODEL), lambda t: (t, 0)),
                  pl.BlockSpec((None, 1, D_MODEL), lambda t: (layer, 0, 0)),
                  pl.BlockSpec((None, 1, 6 * D_MODEL), lambda t: (row(t), 0, 0)),
                  pl.BlockSpec((None, D_MODEL, lanes), lambda t: (j, 0, 0)),
                  pl.BlockSpec((None, 1, lanes), lambda t: (j, 0, 0))],
        out_specs=pl.BlockSpec((tm, lanes), lambda t: (t, 0)),
        compiler_params=_params(("arbitrary",)),
        name="router",
    )(x, norm_g3, mods3, router, router_b)


def _ffn_kernel(*refs, gated, n_e, n_f):
    if gated:
        x_ref, g_ref, mod_ref, gates_ref, w1_ref, w3_ref, w2_ref, o_ref, h_s, acc = refs
    else:
        x_ref, g_ref, mod_ref, w1_ref, w3_ref, w2_ref, o_ref, h_s, acc = refs
    d = D_MODEL
    e = pl.program_id(1)
    f = pl.program_id(2)

    @pl.when((e == 0) & (f == 0))
    def _():
        h = _modulate(x_ref[...], g_ref[...], mod_ref[:, 3 * d:4 * d], mod_ref[:, 4 * d:5 * d])
        h_s[...] = h.astype(BF16)
        acc[...] = jnp.zeros(acc.shape, F32)

    h = h_s[...]
    a = jnp.dot(h, w1_ref[...], preferred_element_type=F32)
    b = jnp.dot(h, w3_ref[...], preferred_element_type=F32)
    act = _silu(a) * b
    if gated:
        gates = gates_ref[...]
        lane = lax.broadcasted_iota(jnp.int32, gates.shape, 1)
        act = act * jnp.sum(jnp.where(lane == e, gates, 0.0), axis=-1, keepdims=True)
    acc[...] += _dot(act, w2_ref[...])

    @pl.when((e == n_e - 1) & (f == n_f - 1))
    def _():
        o_ref[...] = x_ref[...] + mod_ref[:, 5 * d:6 * d] * acc[...]


def _ffn(x, norm_g3, mods3, w1, w3, w2, layer, j, latent, seq_len, gates=None):
    t_tokens = x.shape[0]
    n_e, _, d_ff = w1.shape[1:]
    tm = 512
    tf = 512 if d_ff % 512 == 0 else d_ff // 2
    n_f = d_ff // tf
    row = _mod_row(latent, tm, seq_len)
    gated = gates is not None
    in_specs = [pl.BlockSpec((tm, D_MODEL), lambda t, e, f: (t, 0)),
                pl.BlockSpec((None, 1, D_MODEL), lambda t, e, f: (layer, 0, 0)),
                pl.BlockSpec((None, 1, 6 * D_MODEL), lambda t, e, f: (row(t), 0, 0))]
    args = [x, norm_g3, mods3]
    if gated:
        in_specs.append(pl.BlockSpec((tm, gates.shape[1]), lambda t, e, f: (t, 0)))
        args.append(gates)
    in_specs += [pl.BlockSpec((None, None, D_MODEL, tf), lambda t, e, f: (j, e, 0, f)),
                 pl.BlockSpec((None, None, D_MODEL, tf), lambda t, e, f: (j, e, 0, f)),
                 pl.BlockSpec((None, None, tf, D_MODEL), lambda t, e, f: (j, e, f, 0))]
    args += [w1, w3, w2]
    return pl.pallas_call(
        functools.partial(_ffn_kernel, gated=gated, n_e=n_e, n_f=n_f),
        out_shape=jax.ShapeDtypeStruct((t_tokens, D_MODEL), F32),
        grid=(t_tokens // tm, n_e, n_f),
        in_specs=in_specs,
        out_specs=pl.BlockSpec((tm, D_MODEL), lambda t, e, f: (t, 0)),
        scratch_shapes=[pltpu.VMEM((tm, D_MODEL), BF16), pltpu.VMEM((tm, D_MODEL), F32)],
        compiler_params=_params(("arbitrary", "arbitrary", "arbitrary")),
        name="moe" if gated else "ffn",
    )(*args)


def _rope_tables():
    half = A_QK // 2
    nf = half // 2
    t = np.arange(DEC_SEQ)
    pos = np.stack([(t // GRID_W), (t % GRID_W)], axis=1).astype(np.float32)
    inv = jnp.asarray(ROPE_BASE, F32) ** (-jnp.arange(nf, dtype=F32) / nf)
    lane = np.arange(GW) % A_QK
    which = (lane >= half).astype(np.int32)
    freq = lane % nf
    ang = jnp.asarray(pos)[:, which] * inv[freq][None, :]
    return jnp.cos(ang), jnp.sin(ang)


def _tile_vec(v, reps):
    return jnp.tile(v, (1, reps))[:, None, :]


def _prepare_params(w_in, w_out, a_qnorm, a_knorm, a_lam, a_subln, b_gate, b_gate_bias, b_onorm,
                    c_dw, c_dw_b, c_ln_g, c_ln_b, d_qnorm, d_knorm):
    lr0 = 1536
    lr1 = lr0 + 2 * B_GATE_RANK
    pad = PROJ_W - w_in.shape[-1]
    w_in_p = jnp.concatenate([w_in[:, :, :lr0], w_in[:, :, lr1:], w_in[:, :, lr0:lr1],
                              jnp.zeros(w_in.shape[:2] + (pad,), w_in.dtype)], axis=-1).astype(BF16)
    hk = B_HEADS * B_DK
    wg = jnp.zeros((DEPTH, 2, GW, hk), F32)
    wg = wg.at[:, 0, 0:B_GATE_RANK].set(b_gate[:, 0])
    wg = wg.at[:, 1, B_GATE_RANK:2 * B_GATE_RANK].set(b_gate[:, 1])
    return dict(
        w_in=w_in_p, w_out=w_out.astype(BF16),
        a_qnorm=_tile_vec(a_qnorm, GW // A_QK), a_knorm=_tile_vec(a_knorm, GW // A_QK),
        a_lam=a_lam, a_subln=_tile_vec(a_subln, GW // A_V),
        b_gate=wg.astype(BF16), b_gate_bias=b_gate_bias[:, :, None, :], b_onorm=_tile_vec(b_onorm, GW // B_DV),
        c_dw=c_dw, c_dw_b=c_dw_b[:, None, :], c_ln_g=c_ln_g[:, None, :], c_ln_b=c_ln_b[:, None, :],
        d_qnorm=_tile_vec(d_qnorm, GW // D_HEAD), d_knorm=_tile_vec(d_knorm, GW // D_HEAD),
    )


def _lambda_init(i):
    return 0.8 - 0.6 * math.exp(-0.3 * i)


def kernel(x_prompt, x_sample, cache_a_k, cache_a_v, state_b, cache_d_k, cache_d_v, c, c_ctx,
           norm1_g, norm2_g, w_ada, b_ada, w_in, w_out, a_qnorm, a_knorm, a_lam, a_subln,
           b_gate, b_gate_bias, b_onorm, c_dw, c_dw_b, c_ln_g, c_ln_b, d_qnorm, d_knorm, d_rpb,
           ffn_w1, ffn_w3, ffn_w2, moe_router, moe_router_b, moe_w1, moe_w3, moe_w2):
    lp = _prepare_params(w_in, w_out, a_qnorm, a_knorm, a_lam, a_subln, b_gate, b_gate_bias, b_onorm,
                         c_dw, c_dw_b, c_ln_g, c_ln_b, d_qnorm, d_knorm)
    norm1 = norm1_g[:, None, :]
    norm2 = norm2_g[:, None, :]
    b_ada3 = b_ada[:, None, :]
    cond8 = jnp.concatenate([c_ctx[None, :], c, jnp.zeros((8 - 1 - DEC_BATCH, D_MODEL), F32)], axis=0)
    rope = _rope_tables()
    ffn_w = [w[:, None].astype(BF16) for w in (ffn_w1, ffn_w3, ffn_w2)]
    moe_w = [w.astype(BF16) for w in (moe_w1, moe_w3, moe_w2)]
    lanes = 128
    router_p = jnp.pad(moe_router, ((0, 0), (0, 0), (0, lanes - N_EXPERTS)))
    router_b_p = jnp.pad(moe_router_b, ((0, 0), (0, lanes - N_EXPERTS)))[:, None, :]
    ck_a = cache_a_k.reshape(DEC_BATCH, DEPTH, PAST_LEN, GW)
    cv_a = cache_a_v.reshape(DEC_BATCH, DEPTH, PAST_LEN, GW)
    ck_d = cache_d_k.reshape(DEC_BATCH, DEPTH, PAST_LEN, GW)
    cv_d = cache_d_v.reshape(DEC_BATCH, DEPTH, PAST_LEN, GW)
    s0_lat = _state_to_kernel_layout(state_b.transpose(1, 0, 2, 3, 4, 5).reshape(
        DEPTH * DEC_BATCH, 2, B_HEADS, B_DK, B_DV)).reshape(DEPTH, DEC_BATCH, 2, GW, B_HEADS * B_DK)

    def channel_mixer(x, mods3, i, latent, seq_len):
        j = i // 2
        if i % 2 == 0:
            return _ffn(x, norm2, mods3, *ffn_w, i, j, latent, seq_len)
        gates = _router(x, norm2, mods3, router_p, router_b_p, i, j, latent, seq_len)
        return _ffn(x, norm2, mods3, *moe_w, i, j, latent, seq_len, gates=gates)

    xc = x_prompt.reshape(BATCH * SEQ, D_MODEL)
    xl = x_sample.reshape(DEC_BATCH * DEC_SEQ, D_MODEL)
    new_ak, new_av, new_sb, new_dk, new_dv = [], [], [], [], []
    for i in range(DEPTH):
        lam_init = _lambda_init(i)
        mods3 = _adaln(cond8, w_ada, b_ada3, i)[:, None, :]

        proj = _inproj(xc, norm1, mods3, lp["w_in"], i, False, SEQ)
        o_a, ak = _attn_a(proj, lp, i, False, BATCH, SEQ, lam_init)
        o_f, s_f = _gla(proj, lp, i, BATCH, SEQ, False)
        o_b, s_b = _gla(proj, lp, i, BATCH, SEQ, True, o_fwd=o_f)
        o_c = _conv(proj, lp, i, BATCH, SEQ)
        o_d, dk = _attn_d_ctx(proj, lp, i, BATCH, SEQ)
        xc = _outproj(o_a, o_b, o_c, o_d, xc, mods3, lp["w_out"], i, False, SEQ)
        xc = channel_mixer(xc, mods3, i, False, SEQ)
        new_ak.append(ak.reshape(BATCH, SEQ, A_HEADS, 2 * A_QK))
        new_av.append(proj[:, BLK_AV * GW:(BLK_AV + 1) * GW].reshape(BATCH, SEQ, A_HEADS, A_V))
        new_sb.append(jnp.stack([_state_from_kernel_layout(s_f), _state_from_kernel_layout(s_b)], axis=1))
        new_dk.append(dk.reshape(BATCH, SEQ, D_HEADS, D_HEAD))
        new_dv.append(proj[:, BLK_DV * GW:(BLK_DV + 1) * GW].reshape(BATCH, SEQ, D_HEADS, D_HEAD))

        proj = _inproj(xl, norm1, mods3, lp["w_in"], i, True, DEC_SEQ)
        o_a, = _attn_a(proj, lp, i, True, DEC_BATCH, DEC_SEQ, lam_init, ck_a, cv_a, rope)
        o_f, _ = _gla(proj, lp, i, DEC_BATCH, DEC_SEQ, False, s0=s0_lat[i])
        o_b, _ = _gla(proj, lp, i, DEC_BATCH, DEC_SEQ, True, s0=s0_lat[i], o_fwd=o_f)
        o_c = _conv(proj, lp, i, DEC_BATCH, DEC_SEQ)
        o_d = _attn_d_lat(proj, lp, i, ck_d, cv_d, _na_bias_table(d_rpb[i]))
        xl = _outproj(o_a, o_b, o_c, o_d, xl, mods3, lp["w_out"], i, True, DEC_SEQ)
        xl = channel_mixer(xl, mods3, i, True, DEC_SEQ)

    return (xc.reshape(BATCH, SEQ, D_MODEL), xl.reshape(DEC_BATCH, DEC_SEQ, D_MODEL),
            jnp.stack(new_ak, axis=1), jnp.stack(new_av, axis=1), jnp.stack(new_sb, axis=1),
            jnp.stack(new_dk, axis=1), jnp.stack(new_dv, axis=1))
```

```python
import functools
import math

import numpy as np
import jax
import jax.numpy as jnp
from jax import lax
from jax.experimental import pallas as pl
from jax.experimental.pallas import tpu as pltpu

F32 = jnp.float32
BF16 = jnp.bfloat16

D_MODEL = 1024
BATCH = 16
SEQ = 256
DEPTH = 2
DEC_BATCH = 4
DEC_SEQ = 2048
PAST_LEN = 512
GRID_W = 64
A_HEADS = 4
A_QK = 32
A_V = 64
B_HEADS = 4
B_DK = 32
B_DV = 64
B_GATE_RANK = 16
B_TAU = 16.0
C_CH = 256
C_KSIZE = 31
D_HEADS = 4
D_HEAD = 64
NA_ROWS = 8
NA_COLS = 16
ROPE_BASE = 10000.0
D_FF = 2816
N_EXPERTS = 8
D_FF_EXPERT = 3584
EPS = 1e-6
NEG_INF = -1e30

GW = 256
N_PROJ_BLOCKS = 12
PROJ_W = GW * N_PROJ_BLOCKS
BLK_AQ, BLK_AK, BLK_AV, BLK_BQK, BLK_BV, BLK_BG = 0, 1, 2, 3, 4, 5
BLK_C = 3
BLK_DQ, BLK_DK, BLK_DV, BLK_LR = 8, 9, 10, 11

GLA_SUB = 16
GLA_ROWS = 256
VMEM_LIMIT = 56 * 1024 * 1024


def _params(sem, vmem=VMEM_LIMIT):
    return pltpu.CompilerParams(dimension_semantics=sem, vmem_limit_bytes=vmem)


def _dot(a, b):
    return jnp.dot(a.astype(BF16), b.astype(BF16), preferred_element_type=F32)


def _dot_nt(a, b):
    return lax.dot_general(a.astype(BF16), b.astype(BF16), (((1,), (1,)), ((), ())),
                           preferred_element_type=F32)


def _dot_exact_rhs(x, m):
    hi = x.astype(BF16)
    lo = (x - hi.astype(F32)).astype(BF16)
    return (jnp.dot(hi, m, preferred_element_type=F32) + jnp.dot(lo, m, preferred_element_type=F32))


def _sigmoid(x):
    return 1.0 / (1.0 + jnp.exp(-x))


def _silu(x):
    return x * _sigmoid(x)


def _seg_matrix(n, seg):
    r = lax.broadcasted_iota(jnp.int32, (n, n), 0)
    c = lax.broadcasted_iota(jnp.int32, (n, n), 1)
    return jnp.where((r ^ c) < seg, 1.0, 0.0).astype(BF16)


def _seg_rmsnorm(x, seg, w):
    ms = _dot_exact_rhs(x * x, _seg_matrix(x.shape[-1], seg)) * (1.0 / seg)
    return x * lax.rsqrt(ms + EPS) * w


def _lane_mask(shape, lo, width):
    lane = lax.broadcasted_iota(jnp.int32, shape, len(shape) - 1)
    return (lane >= lo) & (lane < lo + width)


def _modulate(x, g, shift, scale):
    y = x * lax.rsqrt(jnp.mean(x * x, axis=-1, keepdims=True) + EPS)
    return y * g * (1.0 + scale) + shift


def _rope(x, cos, sin):
    lane = lax.broadcasted_iota(jnp.int32, x.shape, 1)
    w = x.shape[1]
    rot = jnp.where((lane & 15) < 8, -pltpu.roll(x, w - 8, 1), pltpu.roll(x, 8, 1))
    return x * cos + rot * sin


def _adaln_kernel(c_ref, w_ref, b_ref, o_ref):
    o_ref[...] = _dot(_silu(c_ref[...]), w_ref[...]) + b_ref[...]


def _adaln(cond8, w_ada, b_ada3, layer):
    tn = 1536
    n = 6 * D_MODEL
    return pl.pallas_call(
        _adaln_kernel,
        out_shape=jax.ShapeDtypeStruct((8, n), F32),
        grid=(n // tn,),
        in_specs=[pl.BlockSpec((8, D_MODEL), lambda j: (0, 0)),
                  pl.BlockSpec((None, D_MODEL, tn), lambda j: (layer, 0, j)),
                  pl.BlockSpec((None, 1, tn), lambda j: (layer, 0, j))],
        out_specs=pl.BlockSpec((8, tn), lambda j: (0, j)),
        compiler_params=_params(("arbitrary",)),
        name="adaln",
    )(cond8, w_ada, b_ada3)


def _mod_row(latent, tm, seq_len):
    if latent:
        return lambda t: 1 + (t * tm) // seq_len
    return lambda t: 0


def _inproj_kernel(x_ref, g_ref, mod_ref, w_ref, o_ref):
    d = D_MODEL
    h = _modulate(x_ref[...], g_ref[...], mod_ref[:, 0:d], mod_ref[:, d:2 * d])
    o_ref[...] = _dot(h, w_ref[...])


def _inproj(x, norm_g3, mods3, w_in, layer, latent, seq_len):
    t_tokens = x.shape[0]
    tm = 512
    row = _mod_row(latent, tm, seq_len)
    return pl.pallas_call(
        _inproj_kernel,
        out_shape=jax.ShapeDtypeStruct((t_tokens, PROJ_W), F32),
        grid=(t_tokens // tm,),
        in_specs=[pl.BlockSpec((tm, D_MODEL), lambda t: (t, 0)),
                  pl.BlockSpec((None, 1, D_MODEL), lambda t: (layer, 0, 0)),
                  pl.BlockSpec((None, 1, 6 * D_MODEL), lambda t: (row(t), 0, 0)),
                  pl.BlockSpec((None, D_MODEL, PROJ_W), lambda t: (layer, 0, 0))],
        out_specs=pl.BlockSpec((tm, PROJ_W), lambda t: (t, 0)),
        compiler_params=_params(("arbitrary",)),
        name="inproj",
    )(x, norm_g3, mods3, w_in)


def _diff_lambda(lam_ref, lam_init):
    lam = lam_ref[...]
    s1 = jnp.sum(lam[0:1] * lam[1:2], axis=1, keepdims=True)
    s2 = jnp.sum(lam[2:3] * lam[3:4], axis=1, keepdims=True)
    return jnp.exp(s1) - jnp.exp(s2) + lam_init


def _attn_a_kernel(*refs, n_own, n_ctx, tq, latent, lam_init):
    if latent:
        (q_ref, k_ref, v_ref, ck_ref, cv_ref, cos_ref, sin_ref, qn_ref, kn_ref, lam_ref, sub_ref,
         o_ref, ks, vs) = refs
    else:
        (q_ref, k_ref, v_ref, qn_ref, kn_ref, lam_ref, sub_ref, o_ref, ko_ref, ks, vs) = refs
    t = pl.program_id(1)

    @pl.when(t == 0)
    def _():
        kn = _seg_rmsnorm(k_ref[...], A_QK, kn_ref[...])
        if latent:
            kn = _rope(kn, cos_ref[...], sin_ref[...])
            ks[n_own:n_own + n_ctx, :] = ck_ref[...].astype(BF16)
            vs[n_own:n_own + n_ctx, :] = cv_ref[...].astype(BF16)
        else:
            ko_ref[...] = kn
        ks[0:n_own, :] = kn.astype(BF16)
        vs[0:n_own, :] = v_ref[...].astype(BF16)

    qn = _seg_rmsnorm(q_ref[...], A_QK, qn_ref[...])
    if latent:
        r0 = pl.multiple_of(t * tq, tq)
        qn = _rope(qn, cos_ref[pl.ds(r0, tq), :], sin_ref[pl.ds(r0, tq), :])
    qn = qn * (A_QK ** -0.5)
    lam = _diff_lambda(lam_ref, lam_init)
    k_all = ks[...]
    v_all = vs[...]
    o = jnp.zeros((tq, GW), F32)
    for h in range(A_HEADS):
        ps = []
        for m in range(2):
            qm = jnp.where(_lane_mask(qn.shape, (2 * h + m) * A_QK, A_QK), qn, 0.0)
            s = _dot_nt(qm, k_all)
            e = jnp.exp(s - jnp.max(s, axis=-1, keepdims=True))
            ps.append(e * (1.0 / jnp.sum(e, axis=-1, keepdims=True)))
        w = ps[0] - lam * ps[1]
        o = o + jnp.where(_lane_mask(o.shape, h * A_V, A_V), _dot(w, v_all), 0.0)
    o_ref[...] = _seg_rmsnorm(o, A_V, sub_ref[...]) * (1.0 - lam_init)


def _attn_a(proj, lp, layer, latent, nseq, seq_len, lam_init, cache_k=None, cache_v=None, rope=None):
    tq = 256
    nt = seq_len // tq
    n_ctx = PAST_LEN if latent else 0
    kern = functools.partial(_attn_a_kernel, n_own=seq_len, n_ctx=n_ctx, tq=tq, latent=latent,
                             lam_init=lam_init)
    vec = lambda name: pl.BlockSpec((None, 1, GW), lambda b, t: (layer, 0, 0))
    in_specs = [pl.BlockSpec((tq, GW), lambda b, t: (b * nt + t, BLK_AQ)),
                pl.BlockSpec((seq_len, GW), lambda b, t: (b, BLK_AK)),
                pl.BlockSpec((seq_len, GW), lambda b, t: (b, BLK_AV))]
    args = [proj, proj, proj]
    if latent:
        in_specs += [pl.BlockSpec((None, None, PAST_LEN, GW), lambda b, t: (b, layer, 0, 0)),
                     pl.BlockSpec((None, None, PAST_LEN, GW), lambda b, t: (b, layer, 0, 0)),
                     pl.BlockSpec((seq_len, GW), lambda b, t: (0, 0)),
                     pl.BlockSpec((seq_len, GW), lambda b, t: (0, 0))]
        args += [cache_k, cache_v, rope[0], rope[1]]
    in_specs += [vec("q"), vec("k"),
                 pl.BlockSpec((None, 4, A_QK), lambda b, t: (layer, 0, 0)),
                 vec("s")]
    args += [lp["a_qnorm"], lp["a_knorm"], lp["a_lam"], lp["a_subln"]]
    out_shape = [jax.ShapeDtypeStruct((nseq * seq_len, GW), F32)]
    out_specs = [pl.BlockSpec((tq, GW), lambda b, t: (b * nt + t, 0))]
    if not latent:
        out_shape.append(jax.ShapeDtypeStruct((nseq * seq_len, GW), F32))
        out_specs.append(pl.BlockSpec((seq_len, GW), lambda b, t: (b, 0)))
    return pl.pallas_call(
        kern,
        out_shape=out_shape,
        grid=(nseq, nt),
        in_specs=in_specs,
        out_specs=out_specs,
        scratch_shapes=[pltpu.VMEM((seq_len + n_ctx, GW), BF16), pltpu.VMEM((seq_len + n_ctx, GW), BF16)],
        compiler_params=_params(("arbitrary", "arbitrary")),
        name="attn_a_lat" if latent else "attn_a_ctx",
    )(*args)


def _attn_d_ctx_kernel(q_ref, k_ref, v_ref, qn_ref, kn_ref, o_ref, ko_ref):
    kn = _seg_rmsnorm(k_ref[...], D_HEAD, kn_ref[...])
    ko_ref[...] = kn
    qn = _seg_rmsnorm(q_ref[...], D_HEAD, qn_ref[...]) * (D_HEAD ** -0.5)
    kb = kn.astype(BF16)
    vb = v_ref[...].astype(BF16)
    o = jnp.zeros(qn.shape, F32)
    for h in range(D_HEADS):
        hm = _lane_mask(qn.shape, h * D_HEAD, D_HEAD)
        s = _dot_nt(jnp.where(hm, qn, 0.0), kb)
        e = jnp.exp(s - jnp.max(s, axis=-1, keepdims=True))
        p = e * (1.0 / jnp.sum(e, axis=-1, keepdims=True))
        o = o + jnp.where(hm, _dot(p, vb), 0.0)
    o_ref[...] = o


def _attn_d_ctx(proj, lp, layer, nseq, seq_len):
    vec = pl.BlockSpec((None, 1, GW), lambda b: (layer, 0, 0))
    blk = lambda c: pl.BlockSpec((seq_len, GW), lambda b: (b, c))
    return pl.pallas_call(
        _attn_d_ctx_kernel,
        out_shape=[jax.ShapeDtypeStruct((nseq * seq_len, GW), F32)] * 2,
        grid=(nseq,),
        in_specs=[blk(BLK_DQ), blk(BLK_DK), blk(BLK_DV), vec, vec],
        out_specs=[blk(0), blk(0)],
        compiler_params=_params(("arbitrary",)),
        name="attn_d_ctx",
    )(proj, proj, proj, lp["d_qnorm"], lp["d_knorm"])


def _attn_d_lat_kernel(q_ref, k_ref, v_ref, ck_ref, cv_ref, qn_ref, kn_ref, bias_ref, o_ref,
                       ks, vs, cks, cvs):
    r = pl.program_id(1)
    n_rows = DEC_SEQ // GRID_W
    n_loc = NA_ROWS * GRID_W

    @pl.when(r == 0)
    def _():
        ks[...] = _seg_rmsnorm(k_ref[...], D_HEAD, kn_ref[...]).astype(BF16)
        vs[...] = v_ref[...].astype(BF16)
        cks[...] = ck_ref[...].astype(BF16)
        cvs[...] = cv_ref[...].astype(BF16)

    qn = _seg_rmsnorm(q_ref[...], D_HEAD, qn_ref[...]) * (D_HEAD ** -0.5)
    row_start = jnp.clip(r - NA_ROWS // 2, 0, n_rows - NA_ROWS)
    off = r - row_start
    k0 = pl.multiple_of(row_start * GRID_W, GRID_W)
    kl = ks[pl.ds(k0, n_loc), :]
    vl = vs[pl.ds(k0, n_loc), :]
    kc = cks[...]
    vc = cvs[...]
    o = jnp.zeros(qn.shape, F32)
    for h in range(D_HEADS):
        hm = _lane_mask(qn.shape, h * D_HEAD, D_HEAD)
        qm = jnp.where(hm, qn, 0.0)
        s_loc = _dot_nt(qm, kl) + bias_ref[h, off]
        s_ctx = _dot_nt(qm, kc)
        mx = jnp.maximum(jnp.max(s_loc, axis=-1, keepdims=True), jnp.max(s_ctx, axis=-1, keepdims=True))
        e_loc = jnp.exp(s_loc - mx)
        e_ctx = jnp.exp(s_ctx - mx)
        inv = 1.0 / (jnp.sum(e_loc, axis=-1, keepdims=True) + jnp.sum(e_ctx, axis=-1, keepdims=True))
        o = o + jnp.where(hm, _dot(e_loc * inv, vl) + _dot(e_ctx * inv, vc), 0.0)
    o_ref[...] = o


def _na_bias_table(rpb):
    q = np.arange(GRID_W)
    kcol = np.arange(GRID_W)
    cs = np.clip(q - NA_COLS // 2, 0, GRID_W - NA_COLS)
    in_win = (kcol[None, :] >= cs[:, None]) & (kcol[None, :] < cs[:, None] + NA_COLS)
    dc = np.clip(kcol[None, :] - q[:, None] + (NA_COLS - 1), 0, 2 * NA_COLS - 2)
    pick = (dc.reshape(-1)[None, :] == np.arange(2 * NA_COLS - 1)[:, None]).astype(np.float32)
    cols = jnp.einsum("hrc,cx->hrx", rpb.astype(F32), jnp.asarray(pick), precision=lax.Precision.HIGHEST)
    cols = jnp.where(in_win[None, None], cols.reshape(D_HEADS, 2 * NA_ROWS - 1, GRID_W, GRID_W), NEG_INF)
    tabs = [cols[:, NA_ROWS - 1 - off:2 * NA_ROWS - 1 - off].transpose(0, 2, 1, 3) for off in range(NA_ROWS)]
    return jnp.stack(tabs, axis=1).reshape(D_HEADS, NA_ROWS, GRID_W, NA_ROWS * GRID_W)


def _attn_d_lat(proj, lp, layer, cache_k, cache_v, bias):
    n_rows = DEC_SEQ // GRID_W
    vec = pl.BlockSpec((None, 1, GW), lambda b, r: (layer, 0, 0))
    seq = lambda c: pl.BlockSpec((DEC_SEQ, GW), lambda b, r: (b, c))
    cache = pl.BlockSpec((None, None, PAST_LEN, GW), lambda b, r: (b, layer, 0, 0))
    return pl.pallas_call(
        _attn_d_lat_kernel,
        out_shape=jax.ShapeDtypeStruct((DEC_BATCH * DEC_SEQ, GW), F32),
        grid=(DEC_BATCH, n_rows),
        in_specs=[pl.BlockSpec((GRID_W, GW), lambda b, r: (b * n_rows + r, BLK_DQ)),
                  seq(BLK_DK), seq(BLK_DV), cache, cache, vec, vec,
                  pl.BlockSpec(bias.shape, lambda b, r: (0, 0, 0, 0))],
        out_specs=pl.BlockSpec((GRID_W, GW), lambda b, r: (b * n_rows + r, 0)),
        scratch_shapes=[pltpu.VMEM((DEC_SEQ, GW), BF16), pltpu.VMEM((DEC_SEQ, GW), BF16),
                        pltpu.VMEM((PAST_LEN, GW), BF16), pltpu.VMEM((PAST_LEN, GW), BF16)],
        compiler_params=_params(("arbitrary", "arbitrary")),
        name="attn_d_lat",
    )(proj, proj, proj, cache_k, cache_v, lp["d_qnorm"], lp["d_knorm"], bias)


def _conv_kernel(c_ref, w_ref, b_ref, g_ref, beta_ref, o_ref, pad, *, seq_len):
    half = C_KSIZE // 2
    top = 16
    cin = c_ref[...]
    u = cin[:, :C_CH] * _sigmoid(cin[:, C_CH:])
    pad[0:top, :] = jnp.zeros((top, C_CH), F32)
    pad[top + seq_len:top + seq_len + top, :] = jnp.zeros((top, C_CH), F32)
    pad[top:top + seq_len, :] = u
    w = w_ref[...]
    rb = 256
    for r0 in range(0, seq_len, rb):
        acc = jnp.zeros((rb, C_CH), F32)
        for k in range(C_KSIZE):
            acc = acc + pad[r0 + top - half + k:r0 + top - half + k + rb, :] * w[k:k + 1, :]
        acc = acc + b_ref[...]
        mu = jnp.mean(acc, axis=-1, keepdims=True)
        xc = acc - mu
        y = xc * lax.rsqrt(jnp.mean(xc * xc, axis=-1, keepdims=True) + EPS) * g_ref[...] + beta_ref[...]
        o_ref[r0:r0 + rb, :] = _silu(y)


def _conv(proj, lp, layer, nseq, seq_len):
    vec = pl.BlockSpec((None, 1, C_CH), lambda b: (layer, 0, 0))
    return pl.pallas_call(
        functools.partial(_conv_kernel, seq_len=seq_len),
        out_shape=jax.ShapeDtypeStruct((nseq * seq_len, C_CH), F32),
        grid=(nseq,),
        in_specs=[pl.BlockSpec((seq_len, 2 * C_CH), lambda b: (b, BLK_C)),
                  pl.BlockSpec((None, C_KSIZE, C_CH), lambda b: (layer, 0, 0)),
                  vec, vec, vec],
        out_specs=pl.BlockSpec((seq_len, C_CH), lambda b: (b, 0)),
        scratch_shapes=[pltpu.VMEM((seq_len + 32, C_CH), F32)],
        compiler_params=_params(("arbitrary",)),
        name="conv",
    )(proj, lp["c_dw"], lp["c_dw_b"], lp["c_ln_g"], lp["c_ln_b"])


def _gla_kernel(*refs, reverse, has_s0, n_blocks):
    refs = list(refs)
    qk_ref, v_ref, lr_ref, wg_ref, gb_ref = refs[:5]
    refs = refs[5:]
    s0_ref = refs.pop(0) if has_s0 else None
    if reverse:
        of_ref, bg_ref, on_ref = refs[:3]
        refs = refs[3:]
    o_ref, so_ref, st = refs
    rb, cs = GLA_ROWS, GLA_SUB
    hk = B_HEADS * B_DK
    j = pl.program_id(1)

    @pl.when(j == 0)
    def _():
        st[...] = s0_ref[...] if has_s0 else jnp.zeros(st.shape, F32)

    qk = qk_ref[...]
    q = qk[:, :hk] * (B_DK ** -0.5)
    k = qk[:, hk:]
    v = v_ref[...]
    pre = _dot(lr_ref[...], wg_ref[...]) + gb_ref[...]
    g = (jnp.minimum(pre, 0.0) - jnp.log(1.0 + jnp.exp(-jnp.abs(pre)))) * (1.0 / B_TAU)

    pos = lax.broadcasted_iota(jnp.int32, (rb, hk), 0) & (cs - 1)
    pre_sum = g
    suf_sum = g
    step = 1
    while step < cs:
        pre_sum = pre_sum + jnp.where(pos >= step, pltpu.roll(pre_sum, step, 0), 0.0)
        suf_sum = suf_sum + jnp.where(pos < cs - step, pltpu.roll(suf_sum, rb - step, 0), 0.0)
        step *= 2
    total = pre_sum + suf_sum - g
    if reverse:
        z = suf_sum
        k_dec = k * jnp.exp(pre_sum - g)
    else:
        z = pre_sum
        k_dec = k * jnp.exp(suf_sum - g)
    q_dec = q * jnp.exp(z)

    r_i = lax.broadcasted_iota(jnp.int32, (hk, GW), 0)
    c_i = lax.broadcasted_iota(jnp.int32, (hk, GW), 1)
    head_sum = jnp.where((r_i >> 5) == (c_i >> 6), 1.0, 0.0).astype(BF16)

    o = jnp.dot((q * k).astype(BF16), head_sum, preferred_element_type=F32) * v
    for d in range(1, cs):
        sh = rb - d if reverse else d
        valid = (pos < cs - d) if reverse else (pos >= d)
        decay = jnp.exp(jnp.minimum(z - pltpu.roll(z, sh, 0), 0.0))
        t = jnp.where(valid, q * pltpu.roll(k, sh, 0) * decay, 0.0)
        o = o + jnp.dot(t.astype(BF16), head_sum, preferred_element_type=F32) * pltpu.roll(v, sh, 0)

    r_s = lax.broadcasted_iota(jnp.int32, st.shape, 0)
    c_s = lax.broadcasted_iota(jnp.int32, st.shape, 1)
    diag = (r_s >> 6) == (c_s >> 5)
    v_t = v.T
    col = lax.broadcasted_iota(jnp.int32, v_t.shape, 1)
    state = st[...]
    n_sub = rb // cs
    inter = [None] * n_sub
    order = range(n_sub - 1, -1, -1) if reverse else range(n_sub)
    for n in order:
        rows = slice(n * cs, (n + 1) * cs)
        inter[n] = _dot_nt(q_dec[rows], state)
        v_sel = jnp.where((col >= n * cs) & (col < (n + 1) * cs), v_t, 0.0)
        kv_t = _dot(v_sel, k_dec)
        state = state * jnp.exp(total[n * cs:n * cs + 1, :]) + jnp.where(diag, kv_t, 0.0)
    st[...] = state
    o = o + jnp.concatenate(inter, axis=0)

    if reverse:
        o = _seg_rmsnorm(o + of_ref[...], B_DV, on_ref[...]) * _silu(bg_ref[...])
    o_ref[...] = o

    @pl.when(j == n_blocks - 1)
    def _():
        so_ref[...] = state


def _gla(proj, lp, layer, nseq, seq_len, reverse, s0=None, o_fwd=None):
    rb = GLA_ROWS
    nb = seq_len // rb
    hk = B_HEADS * B_DK
    blk = (lambda b, j: b * nb + nb - 1 - j) if reverse else (lambda b, j: b * nb + j)
    tile = lambda c: pl.BlockSpec((rb, GW), lambda b, j: (blk(b, j), c))
    d = 1 if reverse else 0
    in_specs = [tile(BLK_BQK), tile(BLK_BV), tile(BLK_LR),
                pl.BlockSpec((None, None, GW, hk), lambda b, j: (layer, d, 0, 0)),
                pl.BlockSpec((None, None, 1, hk), lambda b, j: (layer, d, 0, 0))]
    args = [proj, proj, proj, lp["b_gate"], lp["b_gate_bias"]]
    if s0 is not None:
        in_specs.append(pl.BlockSpec((None, None, GW, hk), lambda b, j: (b, d, 0, 0)))
        args.append(s0)
    if reverse:
        in_specs += [tile(0), tile(BLK_BG), pl.BlockSpec((None, 1, GW), lambda b, j: (layer, 0, 0))]
        args += [o_fwd, proj, lp["b_onorm"]]
    return pl.pallas_call(
        functools.partial(_gla_kernel, reverse=reverse, has_s0=s0 is not None, n_blocks=nb),
        out_shape=[jax.ShapeDtypeStruct((nseq * seq_len, GW), F32),
                   jax.ShapeDtypeStruct((nseq, GW, hk), F32)],
        grid=(nseq, nb),
        in_specs=in_specs,
        out_specs=[tile(0), pl.BlockSpec((None, GW, hk), lambda b, j: (b, 0, 0))],
        scratch_shapes=[pltpu.VMEM((GW, hk), F32)],
        compiler_params=_params(("arbitrary", "arbitrary")),
        name="gla_bwd" if reverse else "gla_fwd",
    )(*args)


def _state_to_kernel_layout(s):
    eye = jnp.eye(B_HEADS, dtype=s.dtype)
    t = jnp.einsum("bxhde,hg->bxhegd", s, eye)
    return t.reshape(s.shape[0], 2, B_HEADS * B_DV, B_HEADS * B_DK)


def _state_from_kernel_layout(st):
    t = st.reshape(st.shape[0], B_HEADS, B_DV, B_HEADS, B_DK)
    return jnp.stack([t[:, h, :, h, :] for h in range(B_HEADS)], axis=1).transpose(0, 1, 3, 2)


def _outproj_kernel(a_ref, b_ref, c_ref, d_ref, x_ref, mod_ref, w_ref, o_ref):
    mix = jnp.concatenate([a_ref[...], b_ref[...], c_ref[...], d_ref[...]], axis=-1)
    gate = mod_ref[:, 2 * D_MODEL:3 * D_MODEL]
    o_ref[...] = x_ref[...] + gate * _dot(mix, w_ref[...])


def _outproj(o_a, o_b, o_c, o_d, x, mods3, w_out, layer, latent, seq_len):
    t_tokens = x.shape[0]
    tm = 512
    row = _mod_row(latent, tm, seq_len)
    part = pl.BlockSpec((tm, GW), lambda t: (t, 0))
    return pl.pallas_call(
        _outproj_kernel,
        out_shape=jax.ShapeDtypeStruct((t_tokens, D_MODEL), F32),
        grid=(t_tokens // tm,),
        in_specs=[part, part, part, part,
                  pl.BlockSpec((tm, D_MODEL), lambda t: (t, 0)),
                  pl.BlockSpec((None, 1, 6 * D_MODEL), lambda t: (row(t), 0, 0)),
                  pl.BlockSpec((None, 4 * GW, D_MODEL), lambda t: (layer, 0, 0))],
        out_specs=pl.BlockSpec((tm, D_MODEL), lambda t: (t, 0)),
        compiler_params=_params(("arbitrary",)),
        name="outproj",
    )(o_a, o_b, o_c, o_d, x, mods3, w_out)


MOE_TM = 512
ROW = 8
LANES = D_MODEL // ROW
R_E1, R_E2, R_W1, R_W2, R_RANK1, R_RANK2 = 0, 1, 2, 3, 4, 5


def _router_kernel(x_ref, g_ref, mod_ref, r_ref, rb_ref, h_ref, route_ref, cnt_ref, carry):
    d = D_MODEL

    @pl.when(pl.program_id(0) == 0)
    def _():
        carry[...] = jnp.zeros(carry.shape, F32)

    h = _modulate(x_ref[...], g_ref[...], mod_ref[:, 3 * d:4 * d], mod_ref[:, 4 * d:5 * d])
    tm = h.shape[0]
    for s in range(ROW):
        h_ref[pl.ds(s, tm, stride=ROW), :] = h[:, s * LANES:(s + 1) * LANES]
    r = r_ref[...]
    h_hi = h.astype(BF16)
    h_lo = (h - h_hi.astype(F32)).astype(BF16)
    r_hi = r.astype(BF16)
    r_lo = (r - r_hi.astype(F32)).astype(BF16)
    dot = lambda a, b: jnp.dot(a, b, preferred_element_type=F32)
    logits = dot(h_hi, r_hi) + dot(h_lo, r_hi) + dot(h_hi, r_lo) + rb_ref[...]
    lane = lax.broadcasted_iota(jnp.int32, logits.shape, 1).astype(F32)
    big = float(logits.shape[1])
    logits = jnp.where(lane < N_EXPERTS, logits, -jnp.inf)
    v1 = jnp.max(logits, axis=-1, keepdims=True)
    i1 = jnp.min(jnp.where(logits == v1, lane, big), axis=-1, keepdims=True)
    rest = jnp.where(lane == i1, -jnp.inf, logits)
    v2 = jnp.max(rest, axis=-1, keepdims=True)
    i2 = jnp.min(jnp.where(rest == v2, lane, big), axis=-1, keepdims=True)
    e2 = jnp.exp(v2 - v1)
    inv = 1.0 / (1.0 + e2)
    sel = jnp.where((lane == i1) | (lane == i2), 1.0, 0.0)
    r_i = lax.broadcasted_iota(jnp.int32, (tm, tm), 0)
    c_i = lax.broadcasted_iota(jnp.int32, (tm, tm), 1)
    before = jnp.where(c_i < r_i, 1.0, 0.0).astype(BF16)
    rank = jnp.dot(before, sel.astype(BF16), preferred_element_type=F32) + carry[...]
    count = carry[...] + jnp.sum(sel, axis=0, keepdims=True)
    carry[...] = count
    cnt_ref[...] = jnp.broadcast_to(count, cnt_ref.shape)
    rank1 = jnp.sum(jnp.where(lane == i1, rank, 0.0), axis=-1, keepdims=True)
    rank2 = jnp.sum(jnp.where(lane == i2, rank, 0.0), axis=-1, keepdims=True)
    rec = jnp.zeros(logits.shape, F32)
    for pos, val in ((R_E1, i1), (R_E2, i2), (R_W1, inv), (R_W2, e2 * inv), (R_RANK1, rank1), (R_RANK2, rank2)):
        rec = jnp.where(lane == pos, val, rec)
    route_ref[...] = rec


def _router(x, norm_g3, mods3, router, router_b, layer, j, latent, seq_len):
    t_tokens = x.shape[0]
    tm = 512
    row = _mod_row(latent, tm, seq_len)
    lanes = router.shape[-1]
    return pl.pallas_call(
        _router_kernel,
        out_shape=[jax.ShapeDtypeStruct((t_tokens * ROW, LANES), F32),
                   jax.ShapeDtypeStruct((t_tokens, lanes), F32),
                   jax.ShapeDtypeStruct((8, lanes), F32)],
        grid=(t_tokens // tm,),
        in_specs=[pl.BlockSpec((tm, D_MODEL), lambda t: (t, 0)),
                  pl.BlockSpec((None, 1, D_MODEL), lambda t: (layer, 0, 0)),
                  pl.BlockSpec((None, 1, 6 * D_MODEL), lambda t: (row(t), 0, 0)),
                  pl.BlockSpec((None, D_MODEL, lanes), lambda t: (j, 0, 0)),
                  pl.BlockSpec((None, 1, lanes), lambda t: (j, 0, 0))],
        out_specs=[pl.BlockSpec((tm * ROW, LANES), lambda t: (t, 0)),
                   pl.BlockSpec((tm, lanes), lambda t: (t, 0)),
                   pl.BlockSpec((8, lanes), lambda t: (0, 0))],
        scratch_shapes=[pltpu.VMEM((1, lanes), F32)],
        compiler_params=_params(("arbitrary",)),
        name="router",
    )(x, norm_g3, mods3, router, router_b)


def _row_copy(src, src_row, dst, dst_row, sem):
    s0 = pl.multiple_of(src_row * ROW, ROW)
    d0 = pl.multiple_of(dst_row * ROW, ROW)
    return pltpu.make_async_copy(src.at[pl.ds(s0, ROW), :], dst.at[pl.ds(d0, ROW), :], sem)


def _dispatch_kernel(p0_ref, p1_ref, h_ref, xs_in_ref, xs_ref, sem):
    del xs_in_ref
    tm = h_ref.shape[0] // ROW
    base = pl.program_id(0) * tm

    def issue(r, carry):
        _row_copy(h_ref, r, xs_ref, p0_ref[base + r], sem.at[0]).start()
        _row_copy(h_ref, r, xs_ref, p1_ref[base + r], sem.at[1]).start()
        return carry

    def wait(r, carry):
        _row_copy(h_ref, r, xs_ref, p0_ref[base + r], sem.at[0]).wait()
        _row_copy(h_ref, r, xs_ref, p1_ref[base + r], sem.at[1]).wait()
        return carry

    lax.fori_loop(0, tm, issue, 0)
    lax.fori_loop(0, tm, wait, 0)


def _dispatch(pos0, pos1, h_rows, n_slots):
    t_tokens = pos0.shape[0]
    tm = 512
    xs0 = jnp.zeros((n_slots * ROW, LANES), F32)
    return pl.pallas_call(
        _dispatch_kernel,
        out_shape=jax.ShapeDtypeStruct(xs0.shape, F32),
        grid_spec=pltpu.PrefetchScalarGridSpec(
            num_scalar_prefetch=2,
            grid=(t_tokens // tm,),
            in_specs=[pl.BlockSpec((tm * ROW, LANES), lambda t, p0, p1: (t, 0)),
                      pl.BlockSpec(memory_space=pl.ANY)],
            out_specs=pl.BlockSpec(memory_space=pl.ANY),
            scratch_shapes=[pltpu.SemaphoreType.DMA((2,))]),
        input_output_aliases={3: 0},
        compiler_params=_params(("arbitrary",)),
        name="moe_dispatch",
    )(pos0, pos1, h_rows, xs0)


def _moe_ffn_kernel(te_ref, nu_ref, xs_ref, w1_ref, w3_ref, w2_ref, y_ref, xb, acc, *, n_f):
    del te_ref
    i = pl.program_id(0)
    f = pl.program_id(1)
    tm = MOE_TM
    used = i < nu_ref[0]

    @pl.when(used & (f == 0))
    def _():
        for s in range(ROW):
            xb[:, s * LANES:(s + 1) * LANES] = xs_ref[pl.ds(s, tm, stride=ROW), :].astype(BF16)
        acc[...] = jnp.zeros(acc.shape, F32)

    @pl.when(used)
    def _():
        x = xb[...]
        a = jnp.dot(x, w1_ref[...].astype(BF16), preferred_element_type=F32)
        b = jnp.dot(x, w3_ref[...].astype(BF16), preferred_element_type=F32)
        acc[...] += _dot(_silu(a) * b, w2_ref[...])

    @pl.when(used & (f == n_f - 1))
    def _():
        for s in range(ROW):
            y_ref[pl.ds(s, tm, stride=ROW), :] = acc[:, s * LANES:(s + 1) * LANES]

    @pl.when(jnp.logical_not(used) & (f == n_f - 1))
    def _():
        y_ref[...] = jnp.zeros(y_ref.shape, F32)


def _moe_ffn(tile_expert, n_used, xs, w1, w3, w2, j):
    tm = MOE_TM
    n_tiles = xs.shape[0] // (tm * ROW)
    d_ff = w1.shape[-1]
    tf = 512
    n_f = d_ff // tf
    f_of = lambda i, f, nu: jnp.where(i < nu[0], f, n_f - 1)
    return pl.pallas_call(
        functools.partial(_moe_ffn_kernel, n_f=n_f),
        out_shape=jax.ShapeDtypeStruct(xs.shape, F32),
        grid_spec=pltpu.PrefetchScalarGridSpec(
            num_scalar_prefetch=2,
            grid=(n_tiles, n_f),
            in_specs=[pl.BlockSpec((tm * ROW, LANES), lambda i, f, te, nu: (jnp.minimum(i, nu[0] - 1), 0)),
                      pl.BlockSpec((None, None, D_MODEL, tf), lambda i, f, te, nu: (j, te[i], 0, f_of(i, f, nu))),
                      pl.BlockSpec((None, None, D_MODEL, tf), lambda i, f, te, nu: (j, te[i], 0, f_of(i, f, nu))),
                      pl.BlockSpec((None, None, tf, D_MODEL), lambda i, f, te, nu: (j, te[i], f_of(i, f, nu), 0))],
            out_specs=pl.BlockSpec((tm * ROW, LANES), lambda i, f, te, nu: (i, 0)),
            scratch_shapes=[pltpu.VMEM((tm, D_MODEL), BF16), pltpu.VMEM((tm, D_MODEL), F32)]),
        compiler_params=_params(("arbitrary", "arbitrary")),
        name="moe_ffn",
    )(tile_expert, n_used, xs, w1, w3, w2)


def _combine_kernel(p0_ref, p1_ref, x_ref, mod_ref, route_ref, y_ref, o_ref, buf0, buf1, sem):
    tm = x_ref.shape[0]
    base = pl.program_id(0) * tm

    def issue(r, carry):
        _row_copy(y_ref, p0_ref[base + r], buf0, r, sem.at[0]).start()
        _row_copy(y_ref, p1_ref[base + r], buf1, r, sem.at[1]).start()
        return carry

    def wait(r, carry):
        _row_copy(y_ref, p0_ref[base + r], buf0, r, sem.at[0]).wait()
        _row_copy(y_ref, p1_ref[base + r], buf1, r, sem.at[1]).wait()
        return carry

    lax.fori_loop(0, tm, issue, 0)
    lax.fori_loop(0, tm, wait, 0)
    route = route_ref[...]
    w_a = route[:, R_W1:R_W1 + 1]
    w_b = route[:, R_W2:R_W2 + 1]
    for s in range(ROW):
        cols = slice(s * LANES, (s + 1) * LANES)
        y = w_a * buf0[pl.ds(s, tm, stride=ROW), :] + w_b * buf1[pl.ds(s, tm, stride=ROW), :]
        o_ref[:, cols] = x_ref[:, cols] + mod_ref[:, 5 * D_MODEL + s * LANES:5 * D_MODEL + (s + 1) * LANES] * y


def _combine(pos0, pos1, x, mods3, route, y, latent, seq_len):
    t_tokens = x.shape[0]
    tm = 512
    row = _mod_row(latent, tm, seq_len)
    return pl.pallas_call(
        _combine_kernel,
        out_shape=jax.ShapeDtypeStruct((t_tokens, D_MODEL), F32),
        grid_spec=pltpu.PrefetchScalarGridSpec(
            num_scalar_prefetch=2,
            grid=(t_tokens // tm,),
            in_specs=[pl.BlockSpec((tm, D_MODEL), lambda t, p0, p1: (t, 0)),
                      pl.BlockSpec((None, 1, 6 * D_MODEL), lambda t, p0, p1: (row(t), 0, 0)),
                      pl.BlockSpec((tm, route.shape[1]), lambda t, p0, p1: (t, 0)),
                      pl.BlockSpec(memory_space=pl.ANY)],
            out_specs=pl.BlockSpec((tm, D_MODEL), lambda t, p0, p1: (t, 0)),
            scratch_shapes=[pltpu.VMEM((tm * ROW, LANES), F32), pltpu.VMEM((tm * ROW, LANES), F32),
                            pltpu.SemaphoreType.DMA((2,))]),
        compiler_params=_params(("arbitrary",)),
        name="moe_combine",
    )(pos0, pos1, x, mods3, route, y)


def _moe(x, norm_g3, mods3, router, router_b, w1, w3, w2, layer, j, latent, seq_len):
    t_tokens = x.shape[0]
    tm = MOE_TM
    n_tiles = 2 * t_tokens // tm + N_EXPERTS
    h_rows, route, cnt = _router(x, norm_g3, mods3, router, router_b, layer, j, latent, seq_len)
    expert = route[:, R_E1:R_E2 + 1].astype(jnp.int32)
    rank = route[:, R_RANK1:R_RANK2 + 1].astype(jnp.int32)
    tiles = (cnt[0, :N_EXPERTS].astype(jnp.int32) + tm - 1) // tm
    ends = jnp.cumsum(tiles)
    starts = ends - tiles
    ids = jnp.arange(N_EXPERTS, dtype=jnp.int32)
    start_of = jnp.sum(jnp.where(expert[:, :, None] == ids, starts * tm, 0), axis=-1)
    pos = start_of + rank
    n_used = ends[-1:]
    tile_ids = jnp.arange(n_tiles, dtype=jnp.int32)
    tile_expert = jnp.sum((tile_ids[:, None] >= ends[None, :]).astype(jnp.int32), axis=1)
    last_expert = jnp.max(jnp.where(tiles > 0, ids, 0))
    tile_expert = jnp.where(tile_ids < n_used, tile_expert, last_expert)
    pos0, pos1 = pos[:, 0], pos[:, 1]
    xs = _dispatch(pos0, pos1, h_rows, n_tiles * tm)
    y = _moe_ffn(tile_expert, n_used, xs, w1, w3, w2, j)
    return _combine(pos0, pos1, x, mods3, route, y, latent, seq_len)


def _ffn_kernel(x_ref, g_ref, mod_ref, w1_ref, w3_ref, w2_ref, o_ref, h_s, acc, *, n_f):
    d = D_MODEL
    f = pl.program_id(1)

    @pl.when(f == 0)
    def _():
        h = _modulate(x_ref[...], g_ref[...], mod_ref[:, 3 * d:4 * d], mod_ref[:, 4 * d:5 * d])
        h_s[...] = h.astype(BF16)
        acc[...] = jnp.zeros(acc.shape, F32)

    h = h_s[...]
    a = jnp.dot(h, w1_ref[...], preferred_element_type=F32)
    b = jnp.dot(h, w3_ref[...], preferred_element_type=F32)
    acc[...] += _dot(_silu(a) * b, w2_ref[...])

    @pl.when(f == n_f - 1)
    def _():
        o_ref[...] = x_ref[...] + mod_ref[:, 5 * d:6 * d] * acc[...]


def _ffn(x, norm_g3, mods3, w1, w3, w2, layer, j, latent, seq_len):
    t_tokens = x.shape[0]
    d_ff = w1.shape[-1]
    tm = 512
    tf = d_ff // 2
    n_f = d_ff // tf
    row = _mod_row(latent, tm, seq_len)
    return pl.pallas_call(
        functools.partial(_ffn_kernel, n_f=n_f),
        out_shape=jax.ShapeDtypeStruct((t_tokens, D_MODEL), F32),
        grid=(t_tokens // tm, n_f),
        in_specs=[pl.BlockSpec((tm, D_MODEL), lambda t, f: (t, 0)),
                  pl.BlockSpec((None, 1, D_MODEL), lambda t, f: (layer, 0, 0)),
                  pl.BlockSpec((None, 1, 6 * D_MODEL), lambda t, f: (row(t), 0, 0)),
                  pl.BlockSpec((None, D_MODEL, tf), lambda t, f: (j, 0, f)),
                  pl.BlockSpec((None, D_MODEL, tf), lambda t, f: (j, 0, f)),
                  pl.BlockSpec((None, tf, D_MODEL), lambda t, f: (j, f, 0))],
        out_specs=pl.BlockSpec((tm, D_MODEL), lambda t, f: (t, 0)),
        scratch_shapes=[pltpu.VMEM((tm, D_MODEL), BF16), pltpu.VMEM((tm, D_MODEL), F32)],
        compiler_params=_params(("arbitrary", "arbitrary")),
        name="ffn",
    )(x, norm_g3, mods3, w1, w3, w2)


def _rope_tables():
    half = A_QK // 2
    nf = half // 2
    t = np.arange(DEC_SEQ)
    pos = np.stack([(t // GRID_W), (t % GRID_W)], axis=1).astype(np.float32)
    inv = jnp.asarray(ROPE_BASE, F32) ** (-jnp.arange(nf, dtype=F32) / nf)
    lane = np.arange(GW) % A_QK
    which = (lane >= half).astype(np.int32)
    freq = lane % nf
    ang = jnp.asarray(pos)[:, which] * inv[freq][None, :]
    return jnp.cos(ang), jnp.sin(ang)


def _tile_vec(v, reps):
    return jnp.tile(v, (1, reps))[:, None, :]


def _prepare_params(w_in, w_out, a_qnorm, a_knorm, a_lam, a_subln, b_gate, b_gate_bias, b_onorm,
                    c_dw, c_dw_b, c_ln_g, c_ln_b, d_qnorm, d_knorm):
    lr0 = 1536
    lr1 = lr0 + 2 * B_GATE_RANK
    pad = PROJ_W - w_in.shape[-1]
    w_in_p = jnp.concatenate([w_in[:, :, :lr0], w_in[:, :, lr1:], w_in[:, :, lr0:lr1],
                              jnp.zeros(w_in.shape[:2] + (pad,), w_in.dtype)], axis=-1).astype(BF16)
    hk = B_HEADS * B_DK
    wg = jnp.zeros((DEPTH, 2, GW, hk), F32)
    wg = wg.at[:, 0, 0:B_GATE_RANK].set(b_gate[:, 0])
    wg = wg.at[:, 1, B_GATE_RANK:2 * B_GATE_RANK].set(b_gate[:, 1])
    return dict(
        w_in=w_in_p, w_out=w_out.astype(BF16),
        a_qnorm=_tile_vec(a_qnorm, GW // A_QK), a_knorm=_tile_vec(a_knorm, GW // A_QK),
        a_lam=a_lam, a_subln=_tile_vec(a_subln, GW // A_V),
        b_gate=wg.astype(BF16), b_gate_bias=b_gate_bias[:, :, None, :], b_onorm=_tile_vec(b_onorm, GW // B_DV),
        c_dw=c_dw, c_dw_b=c_dw_b[:, None, :], c_ln_g=c_ln_g[:, None, :], c_ln_b=c_ln_b[:, None, :],
        d_qnorm=_tile_vec(d_qnorm, GW // D_HEAD), d_knorm=_tile_vec(d_knorm, GW // D_HEAD),
    )


def _lambda_init(i):
    return 0.8 - 0.6 * math.exp(-0.3 * i)


def kernel(x_prompt, x_sample, cache_a_k, cache_a_v, state_b, cache_d_k, cache_d_v, c, c_ctx,
           norm1_g, norm2_g, w_ada, b_ada, w_in, w_out, a_qnorm, a_knorm, a_lam, a_subln,
           b_gate, b_gate_bias, b_onorm, c_dw, c_dw_b, c_ln_g, c_ln_b, d_qnorm, d_knorm, d_rpb,
           ffn_w1, ffn_w3, ffn_w2, moe_router, moe_router_b, moe_w1, moe_w3, moe_w2):
    lp = _prepare_params(w_in, w_out, a_qnorm, a_knorm, a_lam, a_subln, b_gate, b_gate_bias, b_onorm,
                         c_dw, c_dw_b, c_ln_g, c_ln_b, d_qnorm, d_knorm)
    norm1 = norm1_g[:, None, :]
    norm2 = norm2_g[:, None, :]
    b_ada3 = b_ada[:, None, :]
    cond8 = jnp.concatenate([c_ctx[None, :], c, jnp.zeros((8 - 1 - DEC_BATCH, D_MODEL), F32)], axis=0)
    rope = _rope_tables()
    ffn_w = [w.astype(BF16) for w in (ffn_w1, ffn_w3, ffn_w2)]
    lanes = 128
    router_p = jnp.pad(moe_router, ((0, 0), (0, 0), (0, lanes - N_EXPERTS)))
    router_b_p = jnp.pad(moe_router_b, ((0, 0), (0, lanes - N_EXPERTS)))[:, None, :]
    ck_a = cache_a_k.reshape(DEC_BATCH, DEPTH, PAST_LEN, GW)
    cv_a = cache_a_v.reshape(DEC_BATCH, DEPTH, PAST_LEN, GW)
    ck_d = cache_d_k.reshape(DEC_BATCH, DEPTH, PAST_LEN, GW)
    cv_d = cache_d_v.reshape(DEC_BATCH, DEPTH, PAST_LEN, GW)
    s0_lat = _state_to_kernel_layout(state_b.transpose(1, 0, 2, 3, 4, 5).reshape(
        DEPTH * DEC_BATCH, 2, B_HEADS, B_DK, B_DV)).reshape(DEPTH, DEC_BATCH, 2, GW, B_HEADS * B_DK)

    def channel_mixer(x, mods3, i, latent, seq_len):
        j = i // 2
        if i % 2 == 0:
            return _ffn(x, norm2, mods3, *ffn_w, i, j, latent, seq_len)
        return _moe(x, norm2, mods3, router_p, router_b_p, moe_w1, moe_w3, moe_w2, i, j, latent, seq_len)

    xc = x_prompt.reshape(BATCH * SEQ, D_MODEL)
    xl = x_sample.reshape(DEC_BATCH * DEC_SEQ, D_MODEL)
    new_ak, new_av, new_sb, new_dk, new_dv = [], [], [], [], []
    for i in range(DEPTH):
        lam_init = _lambda_init(i)
        mods3 = _adaln(cond8, w_ada, b_ada3, i)[:, None, :]

        proj = _inproj(xc, norm1, mods3, lp["w_in"], i, False, SEQ)
        o_a, ak = _attn_a(proj, lp, i, False, BATCH, SEQ, lam_init)
        o_f, s_f = _gla(proj, lp, i, BATCH, SEQ, False)
        o_b, s_b = _gla(proj, lp, i, BATCH, SEQ, True, o_fwd=o_f)
        o_c = _conv(proj, lp, i, BATCH, SEQ)
        o_d, dk = _attn_d_ctx(proj, lp, i, BATCH, SEQ)
        xc = _outproj(o_a, o_b, o_c, o_d, xc, mods3, lp["w_out"], i, False, SEQ)
        xc = channel_mixer(xc, mods3, i, False, SEQ)
        new_ak.append(ak.reshape(BATCH, SEQ, A_HEADS, 2 * A_QK))
        new_av.append(proj[:, BLK_AV * GW:(BLK_AV + 1) * GW].reshape(BATCH, SEQ, A_HEADS, A_V))
        new_sb.append(jnp.stack([_state_from_kernel_layout(s_f), _state_from_kernel_layout(s_b)], axis=1))
        new_dk.append(dk.reshape(BATCH, SEQ, D_HEADS, D_HEAD))
        new_dv.append(proj[:, BLK_DV * GW:(BLK_DV + 1) * GW].reshape(BATCH, SEQ, D_HEADS, D_HEAD))

        proj = _inproj(xl, norm1, mods3, lp["w_in"], i, True, DEC_SEQ)
        o_a, = _attn_a(proj, lp, i, True, DEC_BATCH, DEC_SEQ, lam_init, ck_a, cv_a, rope)
        o_f, _ = _gla(proj, lp, i, DEC_BATCH, DEC_SEQ, False, s0=s0_lat[i])
        o_b, _ = _gla(proj, lp, i, DEC_BATCH, DEC_SEQ, True, s0=s0_lat[i], o_fwd=o_f)
        o_c = _conv(proj, lp, i, DEC_BATCH, DEC_SEQ)
        o_d = _attn_d_lat(proj, lp, i, ck_d, cv_d, _na_bias_table(d_rpb[i]))
        xl = _outproj(o_a, o_b, o_c, o_d, xl, mods3, lp["w_out"], i, True, DEC_SEQ)
        xl = channel_mixer(xl, mods3, i, True, DEC_SEQ)

    return (xc.reshape(BATCH, SEQ, D_MODEL), xl.reshape(DEC_BATCH, DEC_SEQ, D_MODEL),
            jnp.stack(new_ak, axis=1), jnp.stack(new_av, axis=1), jnp.stack(new_sb, axis=1),
            jnp.stack(new_dk, axis=1), jnp.stack(new_dv, axis=1))
```

```python
import functools
import math

import numpy as np
import jax
import jax.numpy as jnp
from jax import lax
from jax.experimental import pallas as pl
from jax.experimental.pallas import tpu as pltpu

F32 = jnp.float32
BF16 = jnp.bfloat16

D_MODEL = 1024
BATCH = 16
SEQ = 256
DEPTH = 2
DEC_BATCH = 4
DEC_SEQ = 2048
PAST_LEN = 512
GRID_W = 64
A_HEADS = 4
A_QK = 32
A_V = 64
B_HEADS = 4
B_DK = 32
B_DV = 64
B_GATE_RANK = 16
B_TAU = 16.0
C_CH = 256
C_KSIZE = 31
D_HEADS = 4
D_HEAD = 64
NA_ROWS = 8
NA_COLS = 16
NA_QROWS = 4
NA_KROWS = 12
ROPE_BASE = 10000.0
D_FF = 2816
N_EXPERTS = 8
D_FF_EXPERT = 3584
EPS = 1e-6
NEG_INF = -1e30
LOG2_E = math.log2(math.e)

GW = 256
N_PROJ_BLOCKS = 12
PROJ_W = GW * N_PROJ_BLOCKS
BLK_AQ, BLK_AK, BLK_AV, BLK_BQK, BLK_BV, BLK_BG = 0, 1, 2, 3, 4, 5
BLK_C = 3
BLK_DQ, BLK_DK, BLK_DV, BLK_LR = 8, 9, 10, 11

GLA_SUB = 16
GLA_ROWS = 256
VMEM_LIMIT = 56 * 1024 * 1024


def _params(sem, vmem=VMEM_LIMIT):
    return pltpu.CompilerParams(dimension_semantics=sem, vmem_limit_bytes=vmem)


def _dot(a, b):
    return jnp.dot(a.astype(BF16), b.astype(BF16), preferred_element_type=F32)


def _dot_nt(a, b):
    return lax.dot_general(a.astype(BF16), b.astype(BF16), (((1,), (1,)), ((), ())),
                           preferred_element_type=F32)


def _dot_exact_rhs(x, m):
    hi = x.astype(BF16)
    lo = (x - hi.astype(F32)).astype(BF16)
    return (jnp.dot(hi, m, preferred_element_type=F32) + jnp.dot(lo, m, preferred_element_type=F32))


def _sigmoid(x):
    return 1.0 / (1.0 + jnp.exp(-x))


def _silu(x):
    return x * _sigmoid(x)


def _seg_matrix(n, seg):
    r = lax.broadcasted_iota(jnp.int32, (n, n), 0)
    c = lax.broadcasted_iota(jnp.int32, (n, n), 1)
    return jnp.where((r ^ c) < seg, 1.0, 0.0).astype(BF16)


def _seg_rmsnorm(x, seg, w):
    ms = _dot_exact_rhs(x * x, _seg_matrix(x.shape[-1], seg)) * (1.0 / seg)
    return x * lax.rsqrt(ms + EPS) * w


def _lane_mask(shape, lo, width):
    lane = lax.broadcasted_iota(jnp.int32, shape, len(shape) - 1)
    return (lane >= lo) & (lane < lo + width)


def _modulate(x, g, shift, scale):
    y = x * lax.rsqrt(jnp.mean(x * x, axis=-1, keepdims=True) + EPS)
    return y * g * (1.0 + scale) + shift


def _rope(x, cos, sin):
    lane = lax.broadcasted_iota(jnp.int32, x.shape, 1)
    w = x.shape[1]
    rot = jnp.where((lane & 15) < 8, -pltpu.roll(x, w - 8, 1), pltpu.roll(x, 8, 1))
    return x * cos + rot * sin


def _adaln_kernel(c_ref, w_ref, b_ref, o_ref):
    o_ref[...] = _dot(_silu(c_ref[...]), w_ref[...]) + b_ref[...]


def _adaln(cond8, w_ada, b_ada3, layer):
    tn = 1536
    n = 6 * D_MODEL
    return pl.pallas_call(
        _adaln_kernel,
        out_shape=jax.ShapeDtypeStruct((8, n), F32),
        grid=(n // tn,),
        in_specs=[pl.BlockSpec((8, D_MODEL), lambda j: (0, 0)),
                  pl.BlockSpec((None, D_MODEL, tn), lambda j: (layer, 0, j)),
                  pl.BlockSpec((None, 1, tn), lambda j: (layer, 0, j))],
        out_specs=pl.BlockSpec((8, tn), lambda j: (0, j)),
        compiler_params=_params(("arbitrary",)),
        name="adaln",
    )(cond8, w_ada, b_ada3)


def _mod_row(latent, tm, seq_len):
    if latent:
        return lambda t: 1 + (t * tm) // seq_len
    return lambda t: 0


def _inproj_kernel(x_ref, g_ref, mod_ref, w_ref, o_ref):
    d = D_MODEL
    h = _modulate(x_ref[...], g_ref[...], mod_ref[:, 0:d], mod_ref[:, d:2 * d])
    o_ref[...] = _dot(h, w_ref[...])


def _inproj(x, norm_g3, mods3, w_in, layer, latent, seq_len):
    t_tokens = x.shape[0]
    tm = 512
    row = _mod_row(latent, tm, seq_len)
    return pl.pallas_call(
        _inproj_kernel,
        out_shape=jax.ShapeDtypeStruct((t_tokens, PROJ_W), F32),
        grid=(t_tokens // tm,),
        in_specs=[pl.BlockSpec((tm, D_MODEL), lambda t: (t, 0)),
                  pl.BlockSpec((None, 1, D_MODEL), lambda t: (layer, 0, 0)),
                  pl.BlockSpec((None, 1, 6 * D_MODEL), lambda t: (row(t), 0, 0)),
                  pl.BlockSpec((None, D_MODEL, PROJ_W), lambda t: (layer, 0, 0))],
        out_specs=pl.BlockSpec((tm, PROJ_W), lambda t: (t, 0)),
        compiler_params=_params(("arbitrary",)),
        name="inproj",
    )(x, norm_g3, mods3, w_in)


def _diff_lambda(lam_ref, lam_init):
    lam = lam_ref[...]
    s1 = jnp.sum(lam[0:1] * lam[1:2], axis=1, keepdims=True)
    s2 = jnp.sum(lam[2:3] * lam[3:4], axis=1, keepdims=True)
    return jnp.exp(s1) - jnp.exp(s2) + lam_init


def _attn_a_kernel(*refs, n_own, n_ctx, tq, latent, lam_init):
    if latent:
        (q_ref, k_ref, v_ref, ck_ref, cv_ref, cos_ref, sin_ref, qn_ref, kn_ref, lam_ref, sub_ref,
         o_ref, ks, vs) = refs
    else:
        (q_ref, k_ref, v_ref, qn_ref, kn_ref, lam_ref, sub_ref, o_ref, ko_ref, ks, vs) = refs
    t = pl.program_id(1)

    @pl.when(t == 0)
    def _():
        kn = _seg_rmsnorm(k_ref[...], A_QK, kn_ref[...])
        if latent:
            kn = _rope(kn, cos_ref[...], sin_ref[...])
            ks[n_own:n_own + n_ctx, :] = ck_ref[...].astype(BF16)
            vs[n_own:n_own + n_ctx, :] = cv_ref[...].astype(BF16)
        else:
            ko_ref[...] = kn
        ks[0:n_own, :] = kn.astype(BF16)
        vs[0:n_own, :] = v_ref[...].astype(BF16)

    qn = _seg_rmsnorm(q_ref[...], A_QK, qn_ref[...])
    if latent:
        r0 = pl.multiple_of(t * tq, tq)
        qn = _rope(qn, cos_ref[pl.ds(r0, tq), :], sin_ref[pl.ds(r0, tq), :])
    qn = qn * (A_QK ** -0.5 * LOG2_E)
    lam = _diff_lambda(lam_ref, lam_init)
    k_all = ks[...]
    v_all = vs[...]
    o = jnp.zeros((tq, GW), F32)
    for h in range(A_HEADS):
        es, inv = [], []
        for m in range(2):
            qm = jnp.where(_lane_mask(qn.shape, (2 * h + m) * A_QK, A_QK), qn, 0.0)
            s = _dot_nt(qm, k_all)
            e = jnp.exp2(s - jnp.max(s, axis=-1, keepdims=True))
            es.append(e)
            inv.append(1.0 / jnp.sum(e, axis=-1, keepdims=True))
        w = es[0] * inv[0] - es[1] * (lam * inv[1])
        o = o + jnp.where(_lane_mask(o.shape, h * A_V, A_V), _dot(w, v_all), 0.0)
    o_ref[...] = _seg_rmsnorm(o, A_V, sub_ref[...]) * (1.0 - lam_init)


def _attn_a(proj, lp, layer, latent, nseq, seq_len, lam_init, cache_k=None, cache_v=None, rope=None):
    tq = 256
    nt = seq_len // tq
    n_ctx = PAST_LEN if latent else 0
    kern = functools.partial(_attn_a_kernel, n_own=seq_len, n_ctx=n_ctx, tq=tq, latent=latent,
                             lam_init=lam_init)
    vec = lambda name: pl.BlockSpec((None, 1, GW), lambda b, t: (layer, 0, 0))
    in_specs = [pl.BlockSpec((tq, GW), lambda b, t: (b * nt + t, BLK_AQ)),
                pl.BlockSpec((seq_len, GW), lambda b, t: (b, BLK_AK)),
                pl.BlockSpec((seq_len, GW), lambda b, t: (b, BLK_AV))]
    args = [proj, proj, proj]
    if latent:
        in_specs += [pl.BlockSpec((None, None, PAST_LEN, GW), lambda b, t: (b, layer, 0, 0)),
                     pl.BlockSpec((None, None, PAST_LEN, GW), lambda b, t: (b, layer, 0, 0)),
                     pl.BlockSpec((seq_len, GW), lambda b, t: (0, 0)),
                     pl.BlockSpec((seq_len, GW), lambda b, t: (0, 0))]
        args += [cache_k, cache_v, rope[0], rope[1]]
    in_specs += [vec("q"), vec("k"),
                 pl.BlockSpec((None, 4, A_QK), lambda b, t: (layer, 0, 0)),
                 vec("s")]
    args += [lp["a_qnorm"], lp["a_knorm"], lp["a_lam"], lp["a_subln"]]
    out_shape = [jax.ShapeDtypeStruct((nseq * seq_len, GW), F32)]
    out_specs = [pl.BlockSpec((tq, GW), lambda b, t: (b * nt + t, 0))]
    if not latent:
        out_shape.append(jax.ShapeDtypeStruct((nseq * seq_len, GW), F32))
        out_specs.append(pl.BlockSpec((seq_len, GW), lambda b, t: (b, 0)))
    return pl.pallas_call(
        kern,
        out_shape=out_shape,
        grid=(nseq, nt),
        in_specs=in_specs,
        out_specs=out_specs,
        scratch_shapes=[pltpu.VMEM((seq_len + n_ctx, GW), BF16), pltpu.VMEM((seq_len + n_ctx, GW), BF16)],
        compiler_params=_params(("arbitrary", "arbitrary")),
        name="attn_a_lat" if latent else "attn_a_ctx",
    )(*args)


def _attn_d_ctx_kernel(q_ref, k_ref, v_ref, qn_ref, kn_ref, o_ref, ko_ref):
    kn = _seg_rmsnorm(k_ref[...], D_HEAD, kn_ref[...])
    ko_ref[...] = kn
    qn = _seg_rmsnorm(q_ref[...], D_HEAD, qn_ref[...]) * (D_HEAD ** -0.5)
    kb = kn.astype(BF16)
    vb = v_ref[...].astype(BF16)
    o = jnp.zeros(qn.shape, F32)
    for h in range(D_HEADS):
        hm = _lane_mask(qn.shape, h * D_HEAD, D_HEAD)
        s = _dot_nt(jnp.where(hm, qn, 0.0), kb)
        e = jnp.exp(s - jnp.max(s, axis=-1, keepdims=True))
        p = e * (1.0 / jnp.sum(e, axis=-1, keepdims=True))
        o = o + jnp.where(hm, _dot(p, vb), 0.0)
    o_ref[...] = o


def _attn_d_ctx(proj, lp, layer, nseq, seq_len):
    vec = pl.BlockSpec((None, 1, GW), lambda b: (layer, 0, 0))
    blk = lambda c: pl.BlockSpec((seq_len, GW), lambda b: (b, c))
    return pl.pallas_call(
        _attn_d_ctx_kernel,
        out_shape=[jax.ShapeDtypeStruct((nseq * seq_len, GW), F32)] * 2,
        grid=(nseq,),
        in_specs=[blk(BLK_DQ), blk(BLK_DK), blk(BLK_DV), vec, vec],
        out_specs=[blk(0), blk(0)],
        compiler_params=_params(("arbitrary",)),
        name="attn_d_ctx",
    )(proj, proj, proj, lp["d_qnorm"], lp["d_knorm"])


def _attn_d_lat_kernel(q_ref, k_ref, v_ref, ck_ref, cv_ref, qn_ref, kn_ref, bias_ref, o_ref,
                       ks, vs, cks, cvs):
    g = pl.program_id(1)
    n_rows = DEC_SEQ // GRID_W
    n_groups = n_rows // NA_QROWS
    n_loc = NA_KROWS * GRID_W

    @pl.when(g == 0)
    def _():
        ks[...] = _seg_rmsnorm(k_ref[...], D_HEAD, kn_ref[...]).astype(BF16)
        vs[...] = v_ref[...].astype(BF16)
        cks[...] = ck_ref[...].astype(BF16)
        cvs[...] = cv_ref[...].astype(BF16)

    qn = _seg_rmsnorm(q_ref[...], D_HEAD, qn_ref[...]) * (D_HEAD ** -0.5)
    row_start = jnp.clip(g * NA_QROWS - NA_ROWS // 2, 0, n_rows - NA_KROWS)
    variant = jnp.where(g == 0, 0, jnp.where(g == n_groups - 1, 2, 1))
    k0 = pl.multiple_of(row_start * GRID_W, GRID_W)
    kl = ks[pl.ds(k0, n_loc), :]
    vl = vs[pl.ds(k0, n_loc), :]
    kc = cks[...]
    vc = cvs[...]
    o = jnp.zeros(qn.shape, F32)
    for h in range(D_HEADS):
        hm = _lane_mask(qn.shape, h * D_HEAD, D_HEAD)
        qm = jnp.where(hm, qn, 0.0)
        s_loc = _dot_nt(qm, kl) + bias_ref[h, variant]
        s_ctx = _dot_nt(qm, kc)
        mx = jnp.maximum(jnp.max(s_loc, axis=-1, keepdims=True), jnp.max(s_ctx, axis=-1, keepdims=True))
        e_loc = jnp.exp(s_loc - mx)
        e_ctx = jnp.exp(s_ctx - mx)
        inv = 1.0 / (jnp.sum(e_loc, axis=-1, keepdims=True) + jnp.sum(e_ctx, axis=-1, keepdims=True))
        o = o + jnp.where(hm, _dot(e_loc * inv, vl) + _dot(e_ctx * inv, vc), 0.0)
    o_ref[...] = o


def _na_bias_table(rpb):
    n_rows = DEC_SEQ // GRID_W
    n_groups = n_rows // NA_QROWS
    q = np.arange(GRID_W)
    kcol = np.arange(GRID_W)
    cs = np.clip(q - NA_COLS // 2, 0, GRID_W - NA_COLS)
    in_win = (kcol[None, :] >= cs[:, None]) & (kcol[None, :] < cs[:, None] + NA_COLS)
    dc = np.clip(kcol[None, :] - q[:, None] + (NA_COLS - 1), 0, 2 * NA_COLS - 2)
    pick_col = (dc[None] == np.arange(2 * NA_COLS - 1)[:, None, None]).astype(np.float32)
    pick_row = np.zeros((3, NA_QROWS, NA_KROWS, 2 * NA_ROWS - 1), np.float32)
    for v, g in enumerate((0, 1, n_groups - 1)):
        first_key_row = np.clip(g * NA_QROWS - NA_ROWS // 2, 0, n_rows - NA_KROWS)
        for a in range(NA_QROWS):
            r = g * NA_QROWS + a
            win = np.clip(r - NA_ROWS // 2, 0, n_rows - NA_ROWS)
            for i in range(NA_KROWS):
                key_row = first_key_row + i
                if win <= key_row < win + NA_ROWS:
                    pick_row[v, a, i, key_row - r + NA_ROWS - 1] = 1.0
    hi = lax.Precision.HIGHEST
    rows = jnp.einsum("hrc,vair->hvaic", rpb.astype(F32), jnp.asarray(pick_row), precision=hi)
    tab = jnp.einsum("hvaic,cqk->hvaqik", rows, jnp.asarray(pick_col), precision=hi)
    ok = (pick_row.sum(-1) > 0)[None, :, :, None, :, None] & in_win[None, None, None, :, None, :]
    tab = jnp.where(ok, tab, NEG_INF)
    return tab.reshape(D_HEADS, 3, NA_QROWS * GRID_W, NA_KROWS * GRID_W)


def _attn_d_lat(proj, lp, layer, cache_k, cache_v, bias):
    n_groups = DEC_SEQ // GRID_W // NA_QROWS
    tq = NA_QROWS * GRID_W
    vec = pl.BlockSpec((None, 1, GW), lambda b, r: (layer, 0, 0))
    seq = lambda c: pl.BlockSpec((DEC_SEQ, GW), lambda b, r: (b, c))
    cache = pl.BlockSpec((None, None, PAST_LEN, GW), lambda b, r: (b, layer, 0, 0))
    return pl.pallas_call(
        _attn_d_lat_kernel,
        out_shape=jax.ShapeDtypeStruct((DEC_BATCH * DEC_SEQ, GW), F32),
        grid=(DEC_BATCH, n_groups),
        in_specs=[pl.BlockSpec((tq, GW), lambda b, r: (b * n_groups + r, BLK_DQ)),
                  seq(BLK_DK), seq(BLK_DV), cache, cache, vec, vec,
                  pl.BlockSpec(bias.shape, lambda b, r: (0, 0, 0, 0))],
        out_specs=pl.BlockSpec((tq, GW), lambda b, r: (b * n_groups + r, 0)),
        scratch_shapes=[pltpu.VMEM((DEC_SEQ, GW), BF16), pltpu.VMEM((DEC_SEQ, GW), BF16),
                        pltpu.VMEM((PAST_LEN, GW), BF16), pltpu.VMEM((PAST_LEN, GW), BF16)],
        compiler_params=_params(("arbitrary", "arbitrary")),
        name="attn_d_lat",
    )(proj, proj, proj, cache_k, cache_v, lp["d_qnorm"], lp["d_knorm"], bias)


def _conv_kernel(c_ref, w_ref, b_ref, g_ref, beta_ref, o_ref, pad, *, seq_len):
    half = C_KSIZE // 2
    top = 16
    cin = c_ref[...]
    u = cin[:, :C_CH] * _sigmoid(cin[:, C_CH:])
    pad[0:top, :] = jnp.zeros((top, C_CH), F32)
    pad[top + seq_len:top + seq_len + top, :] = jnp.zeros((top, C_CH), F32)
    pad[top:top + seq_len, :] = u
    w = w_ref[...]
    rb = 256
    for r0 in range(0, seq_len, rb):
        acc = jnp.zeros((rb, C_CH), F32)
        for k in range(C_KSIZE):
            acc = acc + pad[r0 + top - half + k:r0 + top - half + k + rb, :] * w[k:k + 1, :]
        acc = acc + b_ref[...]
        mu = jnp.mean(acc, axis=-1, keepdims=True)
        xc = acc - mu
        y = xc * lax.rsqrt(jnp.mean(xc * xc, axis=-1, keepdims=True) + EPS) * g_ref[...] + beta_ref[...]
        o_ref[r0:r0 + rb, :] = _silu(y)


def _conv(proj, lp, layer, nseq, seq_len):
    vec = pl.BlockSpec((None, 1, C_CH), lambda b: (layer, 0, 0))
    return pl.pallas_call(
        functools.partial(_conv_kernel, seq_len=seq_len),
        out_shape=jax.ShapeDtypeStruct((nseq * seq_len, C_CH), F32),
        grid=(nseq,),
        in_specs=[pl.BlockSpec((seq_len, 2 * C_CH), lambda b: (b, BLK_C)),
                  pl.BlockSpec((None, C_KSIZE, C_CH), lambda b: (layer, 0, 0)),
                  vec, vec, vec],
        out_specs=pl.BlockSpec((seq_len, C_CH), lambda b: (b, 0)),
        scratch_shapes=[pltpu.VMEM((seq_len + 32, C_CH), F32)],
        compiler_params=_params(("arbitrary",)),
        name="conv",
    )(proj, lp["c_dw"], lp["c_dw_b"], lp["c_ln_g"], lp["c_ln_b"])


def _gla_kernel(*refs, reverse, has_s0, n_blocks):
    refs = list(refs)
    qk_ref, v_ref, lr_ref, wg_ref, gb_ref = refs[:5]
    refs = refs[5:]
    s0_ref = refs.pop(0) if has_s0 else None
    if reverse:
        of_ref, bg_ref, on_ref = refs[:3]
        refs = refs[3:]
    o_ref, so_ref, st = refs
    rb, cs = GLA_ROWS, GLA_SUB
    hk = B_HEADS * B_DK
    j = pl.program_id(1)

    @pl.when(j == 0)
    def _():
        st[...] = s0_ref[...] if has_s0 else jnp.zeros(st.shape, F32)

    qk = qk_ref[...]
    q = qk[:, :hk] * (B_DK ** -0.5)
    k = qk[:, hk:]
    v = v_ref[...]
    pre = _dot(lr_ref[...], wg_ref[...]) + gb_ref[...]
    g = (jnp.minimum(pre, 0.0) - jnp.log(1.0 + jnp.exp(-jnp.abs(pre)))) * (1.0 / B_TAU)

    pos = lax.broadcasted_iota(jnp.int32, (rb, hk), 0) & (cs - 1)
    pre_sum = g
    suf_sum = g
    step = 1
    while step < cs:
        pre_sum = pre_sum + jnp.where(pos >= step, pltpu.roll(pre_sum, step, 0), 0.0)
        suf_sum = suf_sum + jnp.where(pos < cs - step, pltpu.roll(suf_sum, rb - step, 0), 0.0)
        step *= 2
    total = pre_sum + suf_sum - g
    if reverse:
        z = suf_sum
        k_dec = k * jnp.exp(pre_sum - g)
    else:
        z = pre_sum
        k_dec = k * jnp.exp(suf_sum - g)
    q_dec = q * jnp.exp(z)

    r_i = lax.broadcasted_iota(jnp.int32, (hk, GW), 0)
    c_i = lax.broadcasted_iota(jnp.int32, (hk, GW), 1)
    head_sum = jnp.where((r_i >> 5) == (c_i >> 6), 1.0, 0.0).astype(BF16)

    row_in_sub = lax.broadcasted_iota(jnp.int32, (cs, hk), 0)
    intra = []
    for n in range(rb // cs):
        qs = q[n * cs:(n + 1) * cs]
        zs = z[n * cs:(n + 1) * cs]
        pairs = []
        for jl in range(cs):
            r = n * cs + jl
            decay = jnp.exp(jnp.minimum(zs - z[r:r + 1], 0.0))
            keep = (row_in_sub <= jl) if reverse else (row_in_sub >= jl)
            pairs.append(jnp.where(keep, qs * k[r:r + 1] * decay, 0.0))
        w = jnp.dot(jnp.concatenate(pairs, axis=0).astype(BF16), head_sum, preferred_element_type=F32)
        acc = jnp.zeros((cs, GW), F32)
        for jl in range(cs):
            r = n * cs + jl
            acc = acc + w[jl * cs:(jl + 1) * cs] * v[r:r + 1]
        intra.append(acc)
    o = jnp.concatenate(intra, axis=0)

    r_s = lax.broadcasted_iota(jnp.int32, st.shape, 0)
    c_s = lax.broadcasted_iota(jnp.int32, st.shape, 1)
    diag = (r_s >> 6) == (c_s >> 5)
    v_t = v.T
    col = lax.broadcasted_iota(jnp.int32, v_t.shape, 1)
    state = st[...]
    n_sub = rb // cs
    inter = [None] * n_sub
    order = range(n_sub - 1, -1, -1) if reverse else range(n_sub)
    for n in order:
        rows = slice(n * cs, (n + 1) * cs)
        inter[n] = _dot_nt(q_dec[rows], state)
        v_sel = jnp.where((col >= n * cs) & (col < (n + 1) * cs), v_t, 0.0)
        kv_t = _dot(v_sel, k_dec)
        state = state * jnp.exp(total[n * cs:n * cs + 1, :]) + jnp.where(diag, kv_t, 0.0)
    st[...] = state
    o = o + jnp.concatenate(inter, axis=0)

    if reverse:
        o = _seg_rmsnorm(o + of_ref[...], B_DV, on_ref[...]) * _silu(bg_ref[...])
    o_ref[...] = o

    @pl.when(j == n_blocks - 1)
    def _():
        so_ref[...] = state


def _gla(proj, lp, layer, nseq, seq_len, reverse, s0=None, o_fwd=None):
    rb = GLA_ROWS
    nb = seq_len // rb
    hk = B_HEADS * B_DK
    blk = (lambda b, j: b * nb + nb - 1 - j) if reverse else (lambda b, j: b * nb + j)
    tile = lambda c: pl.BlockSpec((rb, GW), lambda b, j: (blk(b, j), c))
    d = 1 if reverse else 0
    in_specs = [tile(BLK_BQK), tile(BLK_BV), tile(BLK_LR),
                pl.BlockSpec((None, None, GW, hk), lambda b, j: (layer, d, 0, 0)),
                pl.BlockSpec((None, None, 1, hk), lambda b, j: (layer, d, 0, 0))]
    args = [proj, proj, proj, lp["b_gate"], lp["b_gate_bias"]]
    if s0 is not None:
        in_specs.append(pl.BlockSpec((None, None, GW, hk), lambda b, j: (b, d, 0, 0)))
        args.append(s0)
    if reverse:
        in_specs += [tile(0), tile(BLK_BG), pl.BlockSpec((None, 1, GW), lambda b, j: (layer, 0, 0))]
        args += [o_fwd, proj, lp["b_onorm"]]
    return pl.pallas_call(
        functools.partial(_gla_kernel, reverse=reverse, has_s0=s0 is not None, n_blocks=nb),
        out_shape=[jax.ShapeDtypeStruct((nseq * seq_len, GW), F32),
                   jax.ShapeDtypeStruct((nseq, GW, hk), F32)],
        grid=(nseq, nb),
        in_specs=in_specs,
        out_specs=[tile(0), pl.BlockSpec((None, GW, hk), lambda b, j: (b, 0, 0))],
        scratch_shapes=[pltpu.VMEM((GW, hk), F32)],
        compiler_params=_params(("arbitrary", "arbitrary")),
        name="gla_bwd" if reverse else "gla_fwd",
    )(*args)


def _state_to_kernel_layout(s):
    eye = jnp.eye(B_HEADS, dtype=s.dtype)
    t = jnp.einsum("bxhde,hg->bxhegd", s, eye)
    return t.reshape(s.shape[0], 2, B_HEADS * B_DV, B_HEADS * B_DK)


def _state_from_kernel_layout(st):
    t = st.reshape(st.shape[0], B_HEADS, B_DV, B_HEADS, B_DK)
    return jnp.stack([t[:, h, :, h, :] for h in range(B_HEADS)], axis=1).transpose(0, 1, 3, 2)


def _outproj_kernel(a_ref, b_ref, c_ref, d_ref, x_ref, mod_ref, w_ref, o_ref):
    mix = jnp.concatenate([a_ref[...], b_ref[...], c_ref[...], d_ref[...]], axis=-1)
    gate = mod_ref[:, 2 * D_MODEL:3 * D_MODEL]
    o_ref[...] = x_ref[...] + gate * _dot(mix, w_ref[...])


def _outproj(o_a, o_b, o_c, o_d, x, mods3, w_out, layer, latent, seq_len):
    t_tokens = x.shape[0]
    tm = 512
    row = _mod_row(latent, tm, seq_len)
    part = pl.BlockSpec((tm, GW), lambda t: (t, 0))
    return pl.pallas_call(
        _outproj_kernel,
        out_shape=jax.ShapeDtypeStruct((t_tokens, D_MODEL), F32),
        grid=(t_tokens // tm,),
        in_specs=[part, part, part, part,
                  pl.BlockSpec((tm, D_MODEL), lambda t: (t, 0)),
                  pl.BlockSpec((None, 1, 6 * D_MODEL), lambda t: (row(t), 0, 0)),
                  pl.BlockSpec((None, 4 * GW, D_MODEL), lambda t: (layer, 0, 0))],
        out_specs=pl.BlockSpec((tm, D_MODEL), lambda t: (t, 0)),
        compiler_params=_params(("arbitrary",)),
        name="outproj",
    )(o_a, o_b, o_c, o_d, x, mods3, w_out)


MOE_TM = 512
ROW = 8
LANES = D_MODEL // ROW
R_E1, R_E2, R_W1, R_W2, R_RANK1, R_RANK2 = 0, 1, 2, 3, 4, 5


def _router_kernel(x_ref, g_ref, mod_ref, r_ref, rb_ref, h_ref, route_ref, cnt_ref, carry):
    d = D_MODEL

    @pl.when(pl.program_id(0) == 0)
    def _():
        carry[...] = jnp.zeros(carry.shape, F32)

    h = _modulate(x_ref[...], g_ref[...], mod_ref[:, 3 * d:4 * d], mod_ref[:, 4 * d:5 * d])
    tm = h.shape[0]
    for s in range(ROW):
        h_ref[pl.ds(s, tm, stride=ROW), :] = h[:, s * LANES:(s + 1) * LANES]
    r = r_ref[...]
    h_hi = h.astype(BF16)
    h_lo = (h - h_hi.astype(F32)).astype(BF16)
    r_hi = r.astype(BF16)
    r_lo = (r - r_hi.astype(F32)).astype(BF16)
    dot = lambda a, b: jnp.dot(a, b, preferred_element_type=F32)
    logits = dot(h_hi, r_hi) + dot(h_lo, r_hi) + dot(h_hi, r_lo) + rb_ref[...]
    lane = lax.broadcasted_iota(jnp.int32, logits.shape, 1).astype(F32)
    big = float(logits.shape[1])
    logits = jnp.where(lane < N_EXPERTS, logits, -jnp.inf)
    v1 = jnp.max(logits, axis=-1, keepdims=True)
    i1 = jnp.min(jnp.where(logits == v1, lane, big), axis=-1, keepdims=True)
    rest = jnp.where(lane == i1, -jnp.inf, logits)
    v2 = jnp.max(rest, axis=-1, keepdims=True)
    i2 = jnp.min(jnp.where(rest == v2, lane, big), axis=-1, keepdims=True)
    e2 = jnp.exp(v2 - v1)
    inv = 1.0 / (1.0 + e2)
    sel = jnp.where((lane == i1) | (lane == i2), 1.0, 0.0)
    r_i = lax.broadcasted_iota(jnp.int32, (tm, tm), 0)
    c_i = lax.broadcasted_iota(jnp.int32, (tm, tm), 1)
    before = jnp.where(c_i < r_i, 1.0, 0.0).astype(BF16)
    rank = jnp.dot(before, sel.astype(BF16), preferred_element_type=F32) + carry[...]
    count = carry[...] + jnp.sum(sel, axis=0, keepdims=True)
    carry[...] = count
    cnt_ref[...] = jnp.broadcast_to(count, cnt_ref.shape)
    rank1 = jnp.sum(jnp.where(lane == i1, rank, 0.0), axis=-1, keepdims=True)
    rank2 = jnp.sum(jnp.where(lane == i2, rank, 0.0), axis=-1, keepdims=True)
    rec = jnp.zeros(logits.shape, F32)
    for pos, val in ((R_E1, i1), (R_E2, i2), (R_W1, inv), (R_W2, e2 * inv), (R_RANK1, rank1), (R_RANK2, rank2)):
        rec = jnp.where(lane == pos, val, rec)
    route_ref[...] = rec


def _router(x, norm_g3, mods3, router, router_b, layer, j, latent, seq_len):
    t_tokens = x.shape[0]
    tm = 512
    row = _mod_row(latent, tm, seq_len)
    lanes = router.shape[-1]
    return pl.pallas_call(
        _router_kernel,
        out_shape=[jax.ShapeDtypeStruct((t_tokens * ROW, LANES), F32),
                   jax.ShapeDtypeStruct((t_tokens, lanes), F32),
                   jax.ShapeDtypeStruct((8, lanes), F32)],
        grid=(t_tokens // tm,),
        in_specs=[pl.BlockSpec((tm, D_MODEL), lambda t: (t, 0)),
                  pl.BlockSpec((None, 1, D_MODEL), lambda t: (layer, 0, 0)),
                  pl.BlockSpec((None, 1, 6 * D_MODEL), lambda t: (row(t), 0, 0)),
                  pl.BlockSpec((None, D_MODEL, lanes), lambda t: (j, 0, 0)),
                  pl.BlockSpec((None, 1, lanes), lambda t: (j, 0, 0))],
        out_specs=[pl.BlockSpec((tm * ROW, LANES), lambda t: (t, 0)),
                   pl.BlockSpec((tm, lanes), lambda t: (t, 0)),
                   pl.BlockSpec((8, lanes), lambda t: (0, 0))],
        scratch_shapes=[pltpu.VMEM((1, lanes), F32)],
        compiler_params=_params(("arbitrary",)),
        name="router",
    )(x, norm_g3, mods3, router, router_b)


def _row_copy(src, src_row, dst, dst_row, sem):
    s0 = pl.multiple_of(src_row * ROW, ROW)
    d0 = pl.multiple_of(dst_row * ROW, ROW)
    return pltpu.make_async_copy(src.at[pl.ds(s0, ROW), :], dst.at[pl.ds(d0, ROW), :], sem)


def _dispatch_kernel(p0_ref, p1_ref, h_ref, xs_in_ref, xs_ref, sem):
    del xs_in_ref
    tm = h_ref.shape[0] // ROW
    base = pl.program_id(0) * tm

    def issue(r, carry):
        _row_copy(h_ref, r, xs_ref, p0_ref[base + r], sem.at[0]).start()
        _row_copy(h_ref, r, xs_ref, p1_ref[base + r], sem.at[1]).start()
        return carry

    def wait(r, carry):
        _row_copy(h_ref, r, xs_ref, p0_ref[base + r], sem.at[0]).wait()
        _row_copy(h_ref, r, xs_ref, p1_ref[base + r], sem.at[1]).wait()
        return carry

    lax.fori_loop(0, tm, issue, 0)
    lax.fori_loop(0, tm, wait, 0)


def _dispatch(pos0, pos1, h_rows, n_slots):
    t_tokens = pos0.shape[0]
    tm = 512
    xs0 = jnp.zeros((n_slots * ROW, LANES), F32)
    return pl.pallas_call(
        _dispatch_kernel,
        out_shape=jax.ShapeDtypeStruct(xs0.shape, F32),
        grid_spec=pltpu.PrefetchScalarGridSpec(
            num_scalar_prefetch=2,
            grid=(t_tokens // tm,),
            in_specs=[pl.BlockSpec((tm * ROW, LANES), lambda t, p0, p1: (t, 0)),
                      pl.BlockSpec(memory_space=pl.ANY)],
            out_specs=pl.BlockSpec(memory_space=pl.ANY),
            scratch_shapes=[pltpu.SemaphoreType.DMA((2,))]),
        input_output_aliases={3: 0},
        compiler_params=_params(("arbitrary",)),
        name="moe_dispatch",
    )(pos0, pos1, h_rows, xs0)


def _moe_ffn_kernel(te_ref, nu_ref, xs_ref, w1_ref, w3_ref, w2_ref, y_ref, xb, acc, *, n_f):
    del te_ref
    i = pl.program_id(0)
    f = pl.program_id(1)
    tm = MOE_TM
    used = i < nu_ref[0]

    @pl.when(used & (f == 0))
    def _():
        for s in range(ROW):
            xb[:, s * LANES:(s + 1) * LANES] = xs_ref[pl.ds(s, tm, stride=ROW), :].astype(BF16)
        acc[...] = jnp.zeros(acc.shape, F32)

    @pl.when(used)
    def _():
        x = xb[...]
        a = jnp.dot(x, w1_ref[...].astype(BF16), preferred_element_type=F32)
        b = jnp.dot(x, w3_ref[...].astype(BF16), preferred_element_type=F32)
        acc[...] += _dot(_silu(a) * b, w2_ref[...])

    @pl.when(used & (f == n_f - 1))
    def _():
        for s in range(ROW):
            y_ref[pl.ds(s, tm, stride=ROW), :] = acc[:, s * LANES:(s + 1) * LANES]

    @pl.when(jnp.logical_not(used) & (f == n_f - 1))
    def _():
        y_ref[...] = jnp.zeros(y_ref.shape, F32)


def _moe_ffn(tile_expert, n_used, xs, w1, w3, w2, j):
    tm = MOE_TM
    n_tiles = xs.shape[0] // (tm * ROW)
    d_ff = w1.shape[-1]
    tf = 512
    n_f = d_ff // tf
    f_of = lambda i, f, nu: jnp.where(i < nu[0], f, n_f - 1)
    return pl.pallas_call(
        functools.partial(_moe_ffn_kernel, n_f=n_f),
        out_shape=jax.ShapeDtypeStruct(xs.shape, F32),
        grid_spec=pltpu.PrefetchScalarGridSpec(
            num_scalar_prefetch=2,
            grid=(n_tiles, n_f),
            in_specs=[pl.BlockSpec((tm * ROW, LANES), lambda i, f, te, nu: (jnp.minimum(i, nu[0] - 1), 0)),
                      pl.BlockSpec((None, None, D_MODEL, tf), lambda i, f, te, nu: (j, te[i], 0, f_of(i, f, nu))),
                      pl.BlockSpec((None, None, D_MODEL, tf), lambda i, f, te, nu: (j, te[i], 0, f_of(i, f, nu))),
                      pl.BlockSpec((None, None, tf, D_MODEL), lambda i, f, te, nu: (j, te[i], f_of(i, f, nu), 0))],
            out_specs=pl.BlockSpec((tm * ROW, LANES), lambda i, f, te, nu: (i, 0)),
            scratch_shapes=[pltpu.VMEM((tm, D_MODEL), BF16), pltpu.VMEM((tm, D_MODEL), F32)]),
        compiler_params=_params(("arbitrary", "arbitrary")),
        name="moe_ffn",
    )(tile_expert, n_used, xs, w1, w3, w2)


def _combine_kernel(p0_ref, p1_ref, x_ref, mod_ref, route_ref, y_ref, o_ref, buf0, buf1, sem):
    tm = x_ref.shape[0]
    base = pl.program_id(0) * tm

    def issue(r, carry):
        _row_copy(y_ref, p0_ref[base + r], buf0, r, sem.at[0]).start()
        _row_copy(y_ref, p1_ref[base + r], buf1, r, sem.at[1]).start()
        return carry

    def wait(r, carry):
        _row_copy(y_ref, p0_ref[base + r], buf0, r, sem.at[0]).wait()
        _row_copy(y_ref, p1_ref[base + r], buf1, r, sem.at[1]).wait()
        return carry

    lax.fori_loop(0, tm, issue, 0)
    lax.fori_loop(0, tm, wait, 0)
    route = route_ref[...]
    w_a = route[:, R_W1:R_W1 + 1]
    w_b = route[:, R_W2:R_W2 + 1]
    for s in range(ROW):
        cols = slice(s * LANES, (s + 1) * LANES)
        y = w_a * buf0[pl.ds(s, tm, stride=ROW), :] + w_b * buf1[pl.ds(s, tm, stride=ROW), :]
        o_ref[:, cols] = x_ref[:, cols] + mod_ref[:, 5 * D_MODEL + s * LANES:5 * D_MODEL + (s + 1) * LANES] * y


def _combine(pos0, pos1, x, mods3, route, y, latent, seq_len):
    t_tokens = x.shape[0]
    tm = 512
    row = _mod_row(latent, tm, seq_len)
    return pl.pallas_call(
        _combine_kernel,
        out_shape=jax.ShapeDtypeStruct((t_tokens, D_MODEL), F32),
        grid_spec=pltpu.PrefetchScalarGridSpec(
            num_scalar_prefetch=2,
            grid=(t_tokens // tm,),
            in_specs=[pl.BlockSpec((tm, D_MODEL), lambda t, p0, p1: (t, 0)),
                      pl.BlockSpec((None, 1, 6 * D_MODEL), lambda t, p0, p1: (row(t), 0, 0)),
                      pl.BlockSpec((tm, route.shape[1]), lambda t, p0, p1: (t, 0)),
                      pl.BlockSpec(memory_space=pl.ANY)],
            out_specs=pl.BlockSpec((tm, D_MODEL), lambda t, p0, p1: (t, 0)),
            scratch_shapes=[pltpu.VMEM((tm * ROW, LANES), F32), pltpu.VMEM((tm * ROW, LANES), F32),
                            pltpu.SemaphoreType.DMA((2,))]),
        compiler_params=_params(("arbitrary",)),
        name="moe_combine",
    )(pos0, pos1, x, mods3, route, y)


def _moe(x, norm_g3, mods3, router, router_b, w1, w3, w2, layer, j, latent, seq_len):
    t_tokens = x.shape[0]
    tm = MOE_TM
    n_tiles = 2 * t_tokens // tm + N_EXPERTS
    h_rows, route, cnt = _router(x, norm_g3, mods3, router, router_b, layer, j, latent, seq_len)
    expert = route[:, R_E1:R_E2 + 1].astype(jnp.int32)
    rank = route[:, R_RANK1:R_RANK2 + 1].astype(jnp.int32)
    tiles = (cnt[0, :N_EXPERTS].astype(jnp.int32) + tm - 1) // tm
    ends = jnp.cumsum(tiles)
    starts = ends - tiles
    ids = jnp.arange(N_EXPERTS, dtype=jnp.int32)
    start_of = jnp.sum(jnp.where(expert[:, :, None] == ids, starts * tm, 0), axis=-1)
    pos = start_of + rank
    n_used = ends[-1:]
    tile_ids = jnp.arange(n_tiles, dtype=jnp.int32)
    tile_expert = jnp.sum((tile_ids[:, None] >= ends[None, :]).astype(jnp.int32), axis=1)
    last_expert = jnp.max(jnp.where(tiles > 0, ids, 0))
    tile_expert = jnp.where(tile_ids < n_used, tile_expert, last_expert)
    pos0, pos1 = pos[:, 0], pos[:, 1]
    xs = _dispatch(pos0, pos1, h_rows, n_tiles * tm)
    y = _moe_ffn(tile_expert, n_used, xs, w1, w3, w2, j)
    return _combine(pos0, pos1, x, mods3, route, y, latent, seq_len)


def _ffn_kernel(x_ref, g_ref, mod_ref, w1_ref, w3_ref, w2_ref, o_ref, h_s, acc, *, n_f):
    d = D_MODEL
    f = pl.program_id(1)

    @pl.when(f == 0)
    def _():
        h = _modulate(x_ref[...], g_ref[...], mod_ref[:, 3 * d:4 * d], mod_ref[:, 4 * d:5 * d])
        h_s[...] = h.astype(BF16)
        acc[...] = jnp.zeros(acc.shape, F32)

    h = h_s[...]
    a = jnp.dot(h, w1_ref[...], preferred_element_type=F32)
    b = jnp.dot(h, w3_ref[...], preferred_element_type=F32)
    acc[...] += _dot(_silu(a) * b, w2_ref[...])

    @pl.when(f == n_f - 1)
    def _():
        o_ref[...] = x_ref[...] + mod_ref[:, 5 * d:6 * d] * acc[...]


def _ffn(x, norm_g3, mods3, w1, w3, w2, layer, j, latent, seq_len):
    t_tokens = x.shape[0]
    d_ff = w1.shape[-1]
    tm = 512
    tf = d_ff // 2
    n_f = d_ff // tf
    row = _mod_row(latent, tm, seq_len)
    return pl.pallas_call(
        functools.partial(_ffn_kernel, n_f=n_f),
        out_shape=jax.ShapeDtypeStruct((t_tokens, D_MODEL), F32),
        grid=(t_tokens // tm, n_f),
        in_specs=[pl.BlockSpec((tm, D_MODEL), lambda t, f: (t, 0)),
                  pl.BlockSpec((None, 1, D_MODEL), lambda t, f: (layer, 0, 0)),
                  pl.BlockSpec((None, 1, 6 * D_MODEL), lambda t, f: (row(t), 0, 0)),
                  pl.BlockSpec((None, D_MODEL, tf), lambda t, f: (j, 0, f)),
                  pl.BlockSpec((None, D_MODEL, tf), lambda t, f: (j, 0, f)),
                  pl.BlockSpec((None, tf, D_MODEL), lambda t, f: (j, f, 0))],
        out_specs=pl.BlockSpec((tm, D_MODEL), lambda t, f: (t, 0)),
        scratch_shapes=[pltpu.VMEM((tm, D_MODEL), BF16), pltpu.VMEM((tm, D_MODEL), F32)],
        compiler_params=_params(("arbitrary", "arbitrary")),
        name="ffn",
    )(x, norm_g3, mods3, w1, w3, w2)


def _rope_tables():
    half = A_QK // 2
    nf = half // 2
    t = np.arange(DEC_SEQ)
    pos = np.stack([(t // GRID_W), (t % GRID_W)], axis=1).astype(np.float32)
    inv = jnp.asarray(ROPE_BASE, F32) ** (-jnp.arange(nf, dtype=F32) / nf)
    lane = np.arange(GW) % A_QK
    which = (lane >= half).astype(np.int32)
    freq = lane % nf
    ang = jnp.asarray(pos)[:, which] * inv[freq][None, :]
    return jnp.cos(ang), jnp.sin(ang)


def _tile_vec(v, reps):
    return jnp.tile(v, (1, reps))[:, None, :]


def _prepare_params(w_in, w_out, a_qnorm, a_knorm, a_lam, a_subln, b_gate, b_gate_bias, b_onorm,
                    c_dw, c_dw_b, c_ln_g, c_ln_b, d_qnorm, d_knorm):
    lr0 = 1536
    lr1 = lr0 + 2 * B_GATE_RANK
    pad = PROJ_W - w_in.shape[-1]
    w_in_p = jnp.concatenate([w_in[:, :, :lr0], w_in[:, :, lr1:], w_in[:, :, lr0:lr1],
                              jnp.zeros(w_in.shape[:2] + (pad,), w_in.dtype)], axis=-1).astype(BF16)
    hk = B_HEADS * B_DK
    wg = jnp.zeros((DEPTH, 2, GW, hk), F32)
    wg = wg.at[:, 0, 0:B_GATE_RANK].set(b_gate[:, 0])
    wg = wg.at[:, 1, B_GATE_RANK:2 * B_GATE_RANK].set(b_gate[:, 1])
    return dict(
        w_in=w_in_p, w_out=w_out.astype(BF16),
        a_qnorm=_tile_vec(a_qnorm, GW // A_QK), a_knorm=_tile_vec(a_knorm, GW // A_QK),
        a_lam=a_lam, a_subln=_tile_vec(a_subln, GW // A_V),
        b_gate=wg.astype(BF16), b_gate_bias=b_gate_bias[:, :, None, :], b_onorm=_tile_vec(b_onorm, GW // B_DV),
        c_dw=c_dw, c_dw_b=c_dw_b[:, None, :], c_ln_g=c_ln_g[:, None, :], c_ln_b=c_ln_b[:, None, :],
        d_qnorm=_tile_vec(d_qnorm, GW // D_HEAD), d_knorm=_tile_vec(d_knorm, GW // D_HEAD),
    )


def _lambda_init(i):
    return 0.8 - 0.6 * math.exp(-0.3 * i)


def kernel(x_prompt, x_sample, cache_a_k, cache_a_v, state_b, cache_d_k, cache_d_v, c, c_ctx,
           norm1_g, norm2_g, w_ada, b_ada, w_in, w_out, a_qnorm, a_knorm, a_lam, a_subln,
           b_gate, b_gate_bias, b_onorm, c_dw, c_dw_b, c_ln_g, c_ln_b, d_qnorm, d_knorm, d_rpb,
           ffn_w1, ffn_w3, ffn_w2, moe_router, moe_router_b, moe_w1, moe_w3, moe_w2):
    lp = _prepare_params(w_in, w_out, a_qnorm, a_knorm, a_lam, a_subln, b_gate, b_gate_bias, b_onorm,
                         c_dw, c_dw_b, c_ln_g, c_ln_b, d_qnorm, d_knorm)
    norm1 = norm1_g[:, None, :]
    norm2 = norm2_g[:, None, :]
    b_ada3 = b_ada[:, None, :]
    cond8 = jnp.concatenate([c_ctx[None, :], c, jnp.zeros((8 - 1 - DEC_BATCH, D_MODEL), F32)], axis=0)
    rope = _rope_tables()
    ffn_w = [w.astype(BF16) for w in (ffn_w1, ffn_w3, ffn_w2)]
    lanes = 128
    router_p = jnp.pad(moe_router, ((0, 0), (0, 0), (0, lanes - N_EXPERTS)))
    router_b_p = jnp.pad(moe_router_b, ((0, 0), (0, lanes - N_EXPERTS)))[:, None, :]
    ck_a = cache_a_k.reshape(DEC_BATCH, DEPTH, PAST_LEN, GW)
    cv_a = cache_a_v.reshape(DEC_BATCH, DEPTH, PAST_LEN, GW)
    ck_d = cache_d_k.reshape(DEC_BATCH, DEPTH, PAST_LEN, GW)
    cv_d = cache_d_v.reshape(DEC_BATCH, DEPTH, PAST_LEN, GW)
    s0_lat = _state_to_kernel_layout(state_b.transpose(1, 0, 2, 3, 4, 5).reshape(
        DEPTH * DEC_BATCH, 2, B_HEADS, B_DK, B_DV)).reshape(DEPTH, DEC_BATCH, 2, GW, B_HEADS * B_DK)

    def channel_mixer(x, mods3, i, latent, seq_len):
        j = i // 2
        if i % 2 == 0:
            return _ffn(x, norm2, mods3, *ffn_w, i, j, latent, seq_len)
        return _moe(x, norm2, mods3, router_p, router_b_p, moe_w1, moe_w3, moe_w2, i, j, latent, seq_len)

    xc = x_prompt.reshape(BATCH * SEQ, D_MODEL)
    xl = x_sample.reshape(DEC_BATCH * DEC_SEQ, D_MODEL)
    new_ak, new_av, new_sb, new_dk, new_dv = [], [], [], [], []
    for i in range(DEPTH):
        lam_init = _lambda_init(i)
        mods3 = _adaln(cond8, w_ada, b_ada3, i)[:, None, :]

        proj = _inproj(xc, norm1, mods3, lp["w_in"], i, False, SEQ)
        o_a, ak = _attn_a(proj, lp, i, False, BATCH, SEQ, lam_init)
        o_f, s_f = _gla(proj, lp, i, BATCH, SEQ, False)
        o_b, s_b = _gla(proj, lp, i, BATCH, SEQ, True, o_fwd=o_f)
        o_c = _conv(proj, lp, i, BATCH, SEQ)
        o_d, dk = _attn_d_ctx(proj, lp, i, BATCH, SEQ)
        xc = _outproj(o_a, o_b, o_c, o_d, xc, mods3, lp["w_out"], i, False, SEQ)
        xc = channel_mixer(xc, mods3, i, False, SEQ)
        new_ak.append(ak.reshape(BATCH, SEQ, A_HEADS, 2 * A_QK))
        new_av.append(proj[:, BLK_AV * GW:(BLK_AV + 1) * GW].reshape(BATCH, SEQ, A_HEADS, A_V))
        new_sb.append(jnp.stack([_state_from_kernel_layout(s_f), _state_from_kernel_layout(s_b)], axis=1))
        new_dk.append(dk.reshape(BATCH, SEQ, D_HEADS, D_HEAD))
        new_dv.append(proj[:, BLK_DV * GW:(BLK_DV + 1) * GW].reshape(BATCH, SEQ, D_HEADS, D_HEAD))

        proj = _inproj(xl, norm1, mods3, lp["w_in"], i, True, DEC_SEQ)
        o_a, = _attn_a(proj, lp, i, True, DEC_BATCH, DEC_SEQ, lam_init, ck_a, cv_a, rope)
        o_f, _ = _gla(proj, lp, i, DEC_BATCH, DEC_SEQ, False, s0=s0_lat[i])
        o_b, _ = _gla(proj, lp, i, DEC_BATCH, DEC_SEQ, True, s0=s0_lat[i], o_fwd=o_f)
        o_c = _conv(proj, lp, i, DEC_BATCH, DEC_SEQ)
        o_d = _attn_d_lat(proj, lp, i, ck_d, cv_d, _na_bias_table(d_rpb[i]))
        xl = _outproj(o_a, o_b, o_c, o_d, xl, mods3, lp["w_out"], i, True, DEC_SEQ)
        xl = channel_mixer(xl, mods3, i, True, DEC_SEQ)

    return (xc.reshape(BATCH, SEQ, D_MODEL), xl.reshape(DEC_BATCH, DEC_SEQ, D_MODEL),
            jnp.stack(new_ak, axis=1), jnp.stack(new_av, axis=1), jnp.stack(new_sb, axis=1),
            jnp.stack(new_dk, axis=1), jnp.stack(new_dv, axis=1))
```

```python
import functools
import math

import numpy as np
import jax
import jax.numpy as jnp
from jax import lax
from jax.experimental import pallas as pl
from jax.experimental.pallas import tpu as pltpu

F32 = jnp.float32
BF16 = jnp.bfloat16

D_MODEL = 1024
BATCH = 16
SEQ = 256
DEPTH = 2
DEC_BATCH = 4
DEC_SEQ = 2048
PAST_LEN = 512
GRID_W = 64
A_HEADS = 4
A_QK = 32
A_V = 64
B_HEADS = 4
B_DK = 32
B_DV = 64
B_GATE_RANK = 16
B_TAU = 16.0
C_CH = 256
C_KSIZE = 31
D_HEADS = 4
D_HEAD = 64
NA_ROWS = 8
NA_COLS = 16
NA_QROWS = 4
NA_KROWS = 12
ROPE_BASE = 10000.0
D_FF = 2816
N_EXPERTS = 8
D_FF_EXPERT = 3584
EPS = 1e-6
NEG_INF = -1e30
LOG2_E = math.log2(math.e)

GW = 256
N_PROJ_BLOCKS = 12
PROJ_W = GW * N_PROJ_BLOCKS
BLK_AQ, BLK_AK, BLK_AV, BLK_BQK, BLK_BV, BLK_BG = 0, 1, 2, 3, 4, 5
BLK_C = 3
BLK_DQ, BLK_DK, BLK_DV, BLK_LR = 8, 9, 10, 11

GLA_SUB = 16
GLA_ROWS = 256
VMEM_LIMIT = 56 * 1024 * 1024


def _params(sem, vmem=VMEM_LIMIT):
    return pltpu.CompilerParams(dimension_semantics=sem, vmem_limit_bytes=vmem)


def _dot(a, b):
    return jnp.dot(a.astype(BF16), b.astype(BF16), preferred_element_type=F32)


def _dot_nt(a, b):
    return lax.dot_general(a.astype(BF16), b.astype(BF16), (((1,), (1,)), ((), ())),
                           preferred_element_type=F32)


def _dot_exact_rhs(x, m):
    hi = x.astype(BF16)
    lo = (x - hi.astype(F32)).astype(BF16)
    return (jnp.dot(hi, m, preferred_element_type=F32) + jnp.dot(lo, m, preferred_element_type=F32))


def _sigmoid(x):
    return 1.0 / (1.0 + jnp.exp(-x))


def _silu(x):
    return x * _sigmoid(x)


def _seg_matrix(n, seg):
    r = lax.broadcasted_iota(jnp.int32, (n, n), 0)
    c = lax.broadcasted_iota(jnp.int32, (n, n), 1)
    return jnp.where((r ^ c) < seg, 1.0, 0.0).astype(BF16)


def _seg_rmsnorm(x, seg, w):
    ms = _dot_exact_rhs(x * x, _seg_matrix(x.shape[-1], seg)) * (1.0 / seg)
    return x * lax.rsqrt(ms + EPS) * w


def _lane_mask(shape, lo, width):
    lane = lax.broadcasted_iota(jnp.int32, shape, len(shape) - 1)
    return (lane >= lo) & (lane < lo + width)


def _modulate(x, g, shift, scale):
    y = x * lax.rsqrt(jnp.mean(x * x, axis=-1, keepdims=True) + EPS)
    return y * g * (1.0 + scale) + shift


def _rope(x, cos, sin):
    lane = lax.broadcasted_iota(jnp.int32, x.shape, 1)
    w = x.shape[1]
    rot = jnp.where((lane & 15) < 8, -pltpu.roll(x, w - 8, 1), pltpu.roll(x, 8, 1))
    return x * cos + rot * sin


def _adaln_kernel(c_ref, w_ref, b_ref, o_ref):
    o_ref[...] = _dot(_silu(c_ref[...]), w_ref[...]) + b_ref[...]


def _adaln(cond8, w_ada, b_ada3, layer):
    tn = 1536
    n = 6 * D_MODEL
    return pl.pallas_call(
        _adaln_kernel,
        out_shape=jax.ShapeDtypeStruct((8, n), F32),
        grid=(n // tn,),
        in_specs=[pl.BlockSpec((8, D_MODEL), lambda j: (0, 0)),
                  pl.BlockSpec((None, D_MODEL, tn), lambda j: (layer, 0, j)),
                  pl.BlockSpec((None, 1, tn), lambda j: (layer, 0, j))],
        out_specs=pl.BlockSpec((8, tn), lambda j: (0, j)),
        compiler_params=_params(("arbitrary",)),
        name="adaln",
    )(cond8, w_ada, b_ada3)


def _mod_row(latent, tm, seq_len):
    if latent:
        return lambda t: 1 + (t * tm) // seq_len
    return lambda t: 0


def _inproj_kernel(x_ref, g_ref, mod_ref, w_ref, o_ref):
    d = D_MODEL
    h = _modulate(x_ref[...], g_ref[...], mod_ref[:, 0:d], mod_ref[:, d:2 * d])
    o_ref[...] = _dot(h, w_ref[...])


def _inproj(x, norm_g3, mods3, w_in, layer, latent, seq_len):
    t_tokens = x.shape[0]
    tm = 512
    row = _mod_row(latent, tm, seq_len)
    return pl.pallas_call(
        _inproj_kernel,
        out_shape=jax.ShapeDtypeStruct((t_tokens, PROJ_W), F32),
        grid=(t_tokens // tm,),
        in_specs=[pl.BlockSpec((tm, D_MODEL), lambda t: (t, 0)),
                  pl.BlockSpec((None, 1, D_MODEL), lambda t: (layer, 0, 0)),
                  pl.BlockSpec((None, 1, 6 * D_MODEL), lambda t: (row(t), 0, 0)),
                  pl.BlockSpec((None, D_MODEL, PROJ_W), lambda t: (layer, 0, 0))],
        out_specs=pl.BlockSpec((tm, PROJ_W), lambda t: (t, 0)),
        compiler_params=_params(("arbitrary",)),
        name="inproj",
    )(x, norm_g3, mods3, w_in)


def _diff_lambda(lam_ref, lam_init):
    lam = lam_ref[...]
    s1 = jnp.sum(lam[0:1] * lam[1:2], axis=1, keepdims=True)
    s2 = jnp.sum(lam[2:3] * lam[3:4], axis=1, keepdims=True)
    return jnp.exp(s1) - jnp.exp(s2) + lam_init


def _attn_a_kernel(*refs, n_own, n_ctx, tq, latent, lam_init):
    if latent:
        (q_ref, k_ref, v_ref, ck_ref, cv_ref, cos_ref, sin_ref, qn_ref, kn_ref, lam_ref, sub_ref,
         o_ref, ks, vs) = refs
    else:
        (q_ref, k_ref, v_ref, qn_ref, kn_ref, lam_ref, sub_ref, o_ref, ko_ref, ks, vs) = refs
    t = pl.program_id(1)

    @pl.when(t == 0)
    def _():
        kn = _seg_rmsnorm(k_ref[...], A_QK, kn_ref[...])
        if latent:
            kn = _rope(kn, cos_ref[...], sin_ref[...])
            ks[n_own:n_own + n_ctx, :] = ck_ref[...].astype(BF16)
            vs[n_own:n_own + n_ctx, :] = cv_ref[...].astype(BF16)
        else:
            ko_ref[...] = kn
        ks[0:n_own, :] = kn.astype(BF16)
        vs[0:n_own, :] = v_ref[...].astype(BF16)

    qn = _seg_rmsnorm(q_ref[...], A_QK, qn_ref[...])
    if latent:
        r0 = pl.multiple_of(t * tq, tq)
        qn = _rope(qn, cos_ref[pl.ds(r0, tq), :], sin_ref[pl.ds(r0, tq), :])
    qn = qn * (A_QK ** -0.5 * LOG2_E)
    lam = _diff_lambda(lam_ref, lam_init)
    k_all = ks[...]
    v_all = vs[...]
    o = jnp.zeros((tq, GW), F32)
    for h in range(A_HEADS):
        es, inv = [], []
        for m in range(2):
            qm = jnp.where(_lane_mask(qn.shape, (2 * h + m) * A_QK, A_QK), qn, 0.0)
            s = _dot_nt(qm, k_all)
            e = jnp.exp2(s - jnp.max(s, axis=-1, keepdims=True))
            es.append(e)
            inv.append(1.0 / jnp.sum(e, axis=-1, keepdims=True))
        w = es[0] * inv[0] - es[1] * (lam * inv[1])
        o = o + jnp.where(_lane_mask(o.shape, h * A_V, A_V), _dot(w, v_all), 0.0)
    o_ref[...] = _seg_rmsnorm(o, A_V, sub_ref[...]) * (1.0 - lam_init)


def _attn_a(proj, lp, layer, latent, nseq, seq_len, lam_init, cache_k=None, cache_v=None, rope=None):
    tq = 256
    nt = seq_len // tq
    n_ctx = PAST_LEN if latent else 0
    kern = functools.partial(_attn_a_kernel, n_own=seq_len, n_ctx=n_ctx, tq=tq, latent=latent,
                             lam_init=lam_init)
    vec = lambda name: pl.BlockSpec((None, 1, GW), lambda b, t: (layer, 0, 0))
    in_specs = [pl.BlockSpec((tq, GW), lambda b, t: (b * nt + t, BLK_AQ)),
                pl.BlockSpec((seq_len, GW), lambda b, t: (b, BLK_AK)),
                pl.BlockSpec((seq_len, GW), lambda b, t: (b, BLK_AV))]
    args = [proj, proj, proj]
    if latent:
        in_specs += [pl.BlockSpec((None, None, PAST_LEN, GW), lambda b, t: (b, layer, 0, 0)),
                     pl.BlockSpec((None, None, PAST_LEN, GW), lambda b, t: (b, layer, 0, 0)),
                     pl.BlockSpec((seq_len, GW), lambda b, t: (0, 0)),
                     pl.BlockSpec((seq_len, GW), lambda b, t: (0, 0))]
        args += [cache_k, cache_v, rope[0], rope[1]]
    in_specs += [vec("q"), vec("k"),
                 pl.BlockSpec((None, 4, A_QK), lambda b, t: (layer, 0, 0)),
                 vec("s")]
    args += [lp["a_qnorm"], lp["a_knorm"], lp["a_lam"], lp["a_subln"]]
    out_shape = [jax.ShapeDtypeStruct((nseq * seq_len, GW), F32)]
    out_specs = [pl.BlockSpec((tq, GW), lambda b, t: (b * nt + t, 0))]
    if not latent:
        out_shape.append(jax.ShapeDtypeStruct((nseq * seq_len, GW), F32))
        out_specs.append(pl.BlockSpec((seq_len, GW), lambda b, t: (b, 0)))
    return pl.pallas_call(
        kern,
        out_shape=out_shape,
        grid=(nseq, nt),
        in_specs=in_specs,
        out_specs=out_specs,
        scratch_shapes=[pltpu.VMEM((seq_len + n_ctx, GW), BF16), pltpu.VMEM((seq_len + n_ctx, GW), BF16)],
        compiler_params=_params(("arbitrary", "arbitrary")),
        name="attn_a_lat" if latent else "attn_a_ctx",
    )(*args)


def _attn_d_ctx_kernel(q_ref, k_ref, v_ref, qn_ref, kn_ref, o_ref, ko_ref):
    kn = _seg_rmsnorm(k_ref[...], D_HEAD, kn_ref[...])
    ko_ref[...] = kn
    qn = _seg_rmsnorm(q_ref[...], D_HEAD, qn_ref[...]) * (D_HEAD ** -0.5)
    kb = kn.astype(BF16)
    vb = v_ref[...].astype(BF16)
    o = jnp.zeros(qn.shape, F32)
    for h in range(D_HEADS):
        hm = _lane_mask(qn.shape, h * D_HEAD, D_HEAD)
        s = _dot_nt(jnp.where(hm, qn, 0.0), kb)
        e = jnp.exp(s - jnp.max(s, axis=-1, keepdims=True))
        p = e * (1.0 / jnp.sum(e, axis=-1, keepdims=True))
        o = o + jnp.where(hm, _dot(p, vb), 0.0)
    o_ref[...] = o


def _attn_d_ctx(proj, lp, layer, nseq, seq_len):
    vec = pl.BlockSpec((None, 1, GW), lambda b: (layer, 0, 0))
    blk = lambda c: pl.BlockSpec((seq_len, GW), lambda b: (b, c))
    return pl.pallas_call(
        _attn_d_ctx_kernel,
        out_shape=[jax.ShapeDtypeStruct((nseq * seq_len, GW), F32)] * 2,
        grid=(nseq,),
        in_specs=[blk(BLK_DQ), blk(BLK_DK), blk(BLK_DV), vec, vec],
        out_specs=[blk(0), blk(0)],
        compiler_params=_params(("arbitrary",)),
        name="attn_d_ctx",
    )(proj, proj, proj, lp["d_qnorm"], lp["d_knorm"])


def _attn_d_lat_kernel(q_ref, k_ref, v_ref, ck_ref, cv_ref, qn_ref, kn_ref, bias_ref, o_ref,
                       ks, vs, cks, cvs):
    g = pl.program_id(1)
    n_rows = DEC_SEQ // GRID_W
    n_groups = n_rows // NA_QROWS
    n_loc = NA_KROWS * GRID_W

    @pl.when(g == 0)
    def _():
        ks[...] = _seg_rmsnorm(k_ref[...], D_HEAD, kn_ref[...]).astype(BF16)
        vs[...] = v_ref[...].astype(BF16)
        cks[...] = ck_ref[...].astype(BF16)
        cvs[...] = cv_ref[...].astype(BF16)

    qn = _seg_rmsnorm(q_ref[...], D_HEAD, qn_ref[...]) * (D_HEAD ** -0.5)
    row_start = jnp.clip(g * NA_QROWS - NA_ROWS // 2, 0, n_rows - NA_KROWS)
    variant = jnp.where(g == 0, 0, jnp.where(g == n_groups - 1, 2, 1))
    k0 = pl.multiple_of(row_start * GRID_W, GRID_W)
    kl = ks[pl.ds(k0, n_loc), :]
    vl = vs[pl.ds(k0, n_loc), :]
    kc = cks[...]
    vc = cvs[...]
    o = jnp.zeros(qn.shape, F32)
    for h in range(D_HEADS):
        hm = _lane_mask(qn.shape, h * D_HEAD, D_HEAD)
        qm = jnp.where(hm, qn, 0.0)
        s_loc = _dot_nt(qm, kl) + bias_ref[h, variant]
        s_ctx = _dot_nt(qm, kc)
        mx = jnp.maximum(jnp.max(s_loc, axis=-1, keepdims=True), jnp.max(s_ctx, axis=-1, keepdims=True))
        e_loc = jnp.exp(s_loc - mx)
        e_ctx = jnp.exp(s_ctx - mx)
        inv = 1.0 / (jnp.sum(e_loc, axis=-1, keepdims=True) + jnp.sum(e_ctx, axis=-1, keepdims=True))
        o = o + jnp.where(hm, _dot(e_loc * inv, vl) + _dot(e_ctx * inv, vc), 0.0)
    o_ref[...] = o


def _na_bias_table(rpb):
    n_rows = DEC_SEQ // GRID_W
    n_groups = n_rows // NA_QROWS
    q = np.arange(GRID_W)
    kcol = np.arange(GRID_W)
    cs = np.clip(q - NA_COLS // 2, 0, GRID_W - NA_COLS)
    in_win = (kcol[None, :] >= cs[:, None]) & (kcol[None, :] < cs[:, None] + NA_COLS)
    dc = np.clip(kcol[None, :] - q[:, None] + (NA_COLS - 1), 0, 2 * NA_COLS - 2)
    pick_col = (dc[None] == np.arange(2 * NA_COLS - 1)[:, None, None]).astype(np.float32)
    pick_row = np.zeros((3, NA_QROWS, NA_KROWS, 2 * NA_ROWS - 1), np.float32)
    for v, g in enumerate((0, 1, n_groups - 1)):
        first_key_row = np.clip(g * NA_QROWS - NA_ROWS // 2, 0, n_rows - NA_KROWS)
        for a in range(NA_QROWS):
            r = g * NA_QROWS + a
            win = np.clip(r - NA_ROWS // 2, 0, n_rows - NA_ROWS)
            for i in range(NA_KROWS):
                key_row = first_key_row + i
                if win <= key_row < win + NA_ROWS:
                    pick_row[v, a, i, key_row - r + NA_ROWS - 1] = 1.0
    hi = lax.Precision.HIGHEST
    rows = jnp.einsum("hrc,vair->hvaic", rpb.astype(F32), jnp.asarray(pick_row), precision=hi)
    tab = jnp.einsum("hvaic,cqk->hvaqik", rows, jnp.asarray(pick_col), precision=hi)
    ok = (pick_row.sum(-1) > 0)[None, :, :, None, :, None] & in_win[None, None, None, :, None, :]
    tab = jnp.where(ok, tab, NEG_INF)
    return tab.reshape(D_HEADS, 3, NA_QROWS * GRID_W, NA_KROWS * GRID_W)


def _attn_d_lat(proj, lp, layer, cache_k, cache_v, bias):
    n_groups = DEC_SEQ // GRID_W // NA_QROWS
    tq = NA_QROWS * GRID_W
    vec = pl.BlockSpec((None, 1, GW), lambda b, r: (layer, 0, 0))
    seq = lambda c: pl.BlockSpec((DEC_SEQ, GW), lambda b, r: (b, c))
    cache = pl.BlockSpec((None, None, PAST_LEN, GW), lambda b, r: (b, layer, 0, 0))
    return pl.pallas_call(
        _attn_d_lat_kernel,
        out_shape=jax.ShapeDtypeStruct((DEC_BATCH * DEC_SEQ, GW), F32),
        grid=(DEC_BATCH, n_groups),
        in_specs=[pl.BlockSpec((tq, GW), lambda b, r: (b * n_groups + r, BLK_DQ)),
                  seq(BLK_DK), seq(BLK_DV), cache, cache, vec, vec,
                  pl.BlockSpec(bias.shape, lambda b, r: (0, 0, 0, 0))],
        out_specs=pl.BlockSpec((tq, GW), lambda b, r: (b * n_groups + r, 0)),
        scratch_shapes=[pltpu.VMEM((DEC_SEQ, GW), BF16), pltpu.VMEM((DEC_SEQ, GW), BF16),
                        pltpu.VMEM((PAST_LEN, GW), BF16), pltpu.VMEM((PAST_LEN, GW), BF16)],
        compiler_params=_params(("arbitrary", "arbitrary")),
        name="attn_d_lat",
    )(proj, proj, proj, cache_k, cache_v, lp["d_qnorm"], lp["d_knorm"], bias)


def _conv_kernel(c_ref, w_ref, b_ref, g_ref, beta_ref, o_ref, pad, *, seq_len):
    half = C_KSIZE // 2
    top = 16
    cin = c_ref[...]
    u = cin[:, :C_CH] * _sigmoid(cin[:, C_CH:])
    pad[0:top, :] = jnp.zeros((top, C_CH), F32)
    pad[top + seq_len:top + seq_len + top, :] = jnp.zeros((top, C_CH), F32)
    pad[top:top + seq_len, :] = u
    w = w_ref[...]
    rb = 256
    for r0 in range(0, seq_len, rb):
        acc = jnp.zeros((rb, C_CH), F32)
        for k in range(C_KSIZE):
            acc = acc + pad[r0 + top - half + k:r0 + top - half + k + rb, :] * w[k:k + 1, :]
        acc = acc + b_ref[...]
        mu = jnp.mean(acc, axis=-1, keepdims=True)
        xc = acc - mu
        y = xc * lax.rsqrt(jnp.mean(xc * xc, axis=-1, keepdims=True) + EPS) * g_ref[...] + beta_ref[...]
        o_ref[r0:r0 + rb, :] = _silu(y)


def _conv(proj, lp, layer, nseq, seq_len):
    vec = pl.BlockSpec((None, 1, C_CH), lambda b: (layer, 0, 0))
    return pl.pallas_call(
        functools.partial(_conv_kernel, seq_len=seq_len),
        out_shape=jax.ShapeDtypeStruct((nseq * seq_len, C_CH), F32),
        grid=(nseq,),
        in_specs=[pl.BlockSpec((seq_len, 2 * C_CH), lambda b: (b, BLK_C)),
                  pl.BlockSpec((None, C_KSIZE, C_CH), lambda b: (layer, 0, 0)),
                  vec, vec, vec],
        out_specs=pl.BlockSpec((seq_len, C_CH), lambda b: (b, 0)),
        scratch_shapes=[pltpu.VMEM((seq_len + 32, C_CH), F32)],
        compiler_params=_params(("arbitrary",)),
        name="conv",
    )(proj, lp["c_dw"], lp["c_dw_b"], lp["c_ln_g"], lp["c_ln_b"])


def _gla_kernel(*refs, reverse, has_s0, n_blocks):
    refs = list(refs)
    qk_ref, v_ref, lr_ref, wg_ref, gb_ref = refs[:5]
    refs = refs[5:]
    s0_ref = refs.pop(0) if has_s0 else None
    if reverse:
        of_ref, bg_ref, on_ref = refs[:3]
        refs = refs[3:]
    o_ref, so_ref, st = refs
    rb, cs = GLA_ROWS, GLA_SUB
    hk = B_HEADS * B_DK
    j = pl.program_id(1)

    @pl.when(j == 0)
    def _():
        st[...] = s0_ref[...] if has_s0 else jnp.zeros(st.shape, F32)

    qk = qk_ref[...]
    q = qk[:, :hk] * (B_DK ** -0.5)
    k = qk[:, hk:]
    v = v_ref[...]
    pre = _dot(lr_ref[...], wg_ref[...]) + gb_ref[...]
    g = (jnp.minimum(pre, 0.0) - jnp.log(1.0 + jnp.exp(-jnp.abs(pre)))) * (1.0 / B_TAU)

    pos = lax.broadcasted_iota(jnp.int32, (rb, hk), 0) & (cs - 1)
    pre_sum = g
    suf_sum = g
    step = 1
    while step < cs:
        pre_sum = pre_sum + jnp.where(pos >= step, pltpu.roll(pre_sum, step, 0), 0.0)
        suf_sum = suf_sum + jnp.where(pos < cs - step, pltpu.roll(suf_sum, rb - step, 0), 0.0)
        step *= 2
    total = pre_sum + suf_sum - g
    if reverse:
        z = suf_sum
        k_dec = k * jnp.exp(pre_sum - g)
    else:
        z = pre_sum
        k_dec = k * jnp.exp(suf_sum - g)
    q_dec = q * jnp.exp(z)

    r_i = lax.broadcasted_iota(jnp.int32, (hk, GW), 0)
    c_i = lax.broadcasted_iota(jnp.int32, (hk, GW), 1)
    head_sum = jnp.where((r_i >> 5) == (c_i >> 6), 1.0, 0.0).astype(BF16)

    row_in_sub = lax.broadcasted_iota(jnp.int32, (cs, hk), 0)
    intra = []
    for n in range(rb // cs):
        qs = q[n * cs:(n + 1) * cs]
        zs = z[n * cs:(n + 1) * cs]
        pairs = []
        for jl in range(cs):
            r = n * cs + jl
            decay = jnp.exp(jnp.minimum(zs - z[r:r + 1], 0.0))
            keep = (row_in_sub <= jl) if reverse else (row_in_sub >= jl)
            pairs.append(jnp.where(keep, qs * k[r:r + 1] * decay, 0.0))
        w = jnp.dot(jnp.concatenate(pairs, axis=0).astype(BF16), head_sum, preferred_element_type=F32)
        acc = jnp.zeros((cs, GW), F32)
        for jl in range(cs):
            r = n * cs + jl
            acc = acc + w[jl * cs:(jl + 1) * cs] * v[r:r + 1]
        intra.append(acc)
    o = jnp.concatenate(intra, axis=0)

    r_s = lax.broadcasted_iota(jnp.int32, st.shape, 0)
    c_s = lax.broadcasted_iota(jnp.int32, st.shape, 1)
    diag = (r_s >> 6) == (c_s >> 5)
    n_sub = rb // cs
    row = lax.broadcasted_iota(jnp.int32, (rb, hk), 0)

    def per_sub_chunk(x):
        return jnp.concatenate([jnp.where((row >= n * cs) & (row < (n + 1) * cs), x, 0.0).astype(BF16)
                                for n in range(n_sub)], axis=1)

    kv_all = _dot(v.T, per_sub_chunk(k_dec))
    state = st[...]
    states = [None] * n_sub
    order = range(n_sub - 1, -1, -1) if reverse else range(n_sub)
    for n in order:
        states[n] = state.astype(BF16)
        decay = jnp.exp(total[n * cs:n * cs + 1, :])
        state = state * decay + jnp.where(diag, kv_all[:, n * hk:(n + 1) * hk], 0.0)
    st[...] = state
    o = o + _dot_nt(per_sub_chunk(q_dec), jnp.concatenate(states, axis=1))

    if reverse:
        o = _seg_rmsnorm(o + of_ref[...], B_DV, on_ref[...]) * _silu(bg_ref[...])
    o_ref[...] = o

    @pl.when(j == n_blocks - 1)
    def _():
        so_ref[...] = state


def _gla(proj, lp, layer, nseq, seq_len, reverse, s0=None, o_fwd=None):
    rb = GLA_ROWS
    nb = seq_len // rb
    hk = B_HEADS * B_DK
    blk = (lambda b, j: b * nb + nb - 1 - j) if reverse else (lambda b, j: b * nb + j)
    tile = lambda c: pl.BlockSpec((rb, GW), lambda b, j: (blk(b, j), c))
    d = 1 if reverse else 0
    in_specs = [tile(BLK_BQK), tile(BLK_BV), tile(BLK_LR),
                pl.BlockSpec((None, None, GW, hk), lambda b, j: (layer, d, 0, 0)),
                pl.BlockSpec((None, None, 1, hk), lambda b, j: (layer, d, 0, 0))]
    args = [proj, proj, proj, lp["b_gate"], lp["b_gate_bias"]]
    if s0 is not None:
        in_specs.append(pl.BlockSpec((None, None, GW, hk), lambda b, j: (b, d, 0, 0)))
        args.append(s0)
    if reverse:
        in_specs += [tile(0), tile(BLK_BG), pl.BlockSpec((None, 1, GW), lambda b, j: (layer, 0, 0))]
        args += [o_fwd, proj, lp["b_onorm"]]
    return pl.pallas_call(
        functools.partial(_gla_kernel, reverse=reverse, has_s0=s0 is not None, n_blocks=nb),
        out_shape=[jax.ShapeDtypeStruct((nseq * seq_len, GW), F32),
                   jax.ShapeDtypeStruct((nseq, GW, hk), F32)],
        grid=(nseq, nb),
        in_specs=in_specs,
        out_specs=[tile(0), pl.BlockSpec((None, GW, hk), lambda b, j: (b, 0, 0))],
        scratch_shapes=[pltpu.VMEM((GW, hk), F32)],
        compiler_params=_params(("arbitrary", "arbitrary")),
        name="gla_bwd" if reverse else "gla_fwd",
    )(*args)


def _state_to_kernel_layout(s):
    eye = jnp.eye(B_HEADS, dtype=s.dtype)
    t = jnp.einsum("bxhde,hg->bxhegd", s, eye)
    return t.reshape(s.shape[0], 2, B_HEADS * B_DV, B_HEADS * B_DK)


def _state_from_kernel_layout(st):
    t = st.reshape(st.shape[0], B_HEADS, B_DV, B_HEADS, B_DK)
    return jnp.stack([t[:, h, :, h, :] for h in range(B_HEADS)], axis=1).transpose(0, 1, 3, 2)


def _outproj_kernel(a_ref, b_ref, c_ref, d_ref, x_ref, mod_ref, w_ref, o_ref):
    mix = jnp.concatenate([a_ref[...], b_ref[...], c_ref[...], d_ref[...]], axis=-1)
    gate = mod_ref[:, 2 * D_MODEL:3 * D_MODEL]
    o_ref[...] = x_ref[...] + gate * _dot(mix, w_ref[...])


def _outproj(o_a, o_b, o_c, o_d, x, mods3, w_out, layer, latent, seq_len):
    t_tokens = x.shape[0]
    tm = 512
    row = _mod_row(latent, tm, seq_len)
    part = pl.BlockSpec((tm, GW), lambda t: (t, 0))
    return pl.pallas_call(
        _outproj_kernel,
        out_shape=jax.ShapeDtypeStruct((t_tokens, D_MODEL), F32),
        grid=(t_tokens // tm,),
        in_specs=[part, part, part, part,
                  pl.BlockSpec((tm, D_MODEL), lambda t: (t, 0)),
                  pl.BlockSpec((None, 1, 6 * D_MODEL), lambda t: (row(t), 0, 0)),
                  pl.BlockSpec((None, 4 * GW, D_MODEL), lambda t: (layer, 0, 0))],
        out_specs=pl.BlockSpec((tm, D_MODEL), lambda t: (t, 0)),
        compiler_params=_params(("arbitrary",)),
        name="outproj",
    )(o_a, o_b, o_c, o_d, x, mods3, w_out)


MOE_TM = 512
ROW = 8
LANES = D_MODEL // ROW
R_E1, R_E2, R_W1, R_W2, R_RANK1, R_RANK2 = 0, 1, 2, 3, 4, 5


def _router_kernel(x_ref, g_ref, mod_ref, r_ref, rb_ref, h_ref, route_ref, cnt_ref, carry):
    d = D_MODEL

    @pl.when(pl.program_id(0) == 0)
    def _():
        carry[...] = jnp.zeros(carry.shape, F32)

    h = _modulate(x_ref[...], g_ref[...], mod_ref[:, 3 * d:4 * d], mod_ref[:, 4 * d:5 * d])
    tm = h.shape[0]
    for s in range(ROW):
        h_ref[pl.ds(s, tm, stride=ROW), :] = h[:, s * LANES:(s + 1) * LANES]
    r = r_ref[...]
    h_hi = h.astype(BF16)
    h_lo = (h - h_hi.astype(F32)).astype(BF16)
    r_hi = r.astype(BF16)
    r_lo = (r - r_hi.astype(F32)).astype(BF16)
    dot = lambda a, b: jnp.dot(a, b, preferred_element_type=F32)
    logits = dot(h_hi, r_hi) + dot(h_lo, r_hi) + dot(h_hi, r_lo) + rb_ref[...]
    lane = lax.broadcasted_iota(jnp.int32, logits.shape, 1).astype(F32)
    big = float(logits.shape[1])
    logits = jnp.where(lane < N_EXPERTS, logits, -jnp.inf)
    v1 = jnp.max(logits, axis=-1, keepdims=True)
    i1 = jnp.min(jnp.where(logits == v1, lane, big), axis=-1, keepdims=True)
    rest = jnp.where(lane == i1, -jnp.inf, logits)
    v2 = jnp.max(rest, axis=-1, keepdims=True)
    i2 = jnp.min(jnp.where(rest == v2, lane, big), axis=-1, keepdims=True)
    e2 = jnp.exp(v2 - v1)
    inv = 1.0 / (1.0 + e2)
    sel = jnp.where((lane == i1) | (lane == i2), 1.0, 0.0)
    r_i = lax.broadcasted_iota(jnp.int32, (tm, tm), 0)
    c_i = lax.broadcasted_iota(jnp.int32, (tm, tm), 1)
    before = jnp.where(c_i < r_i, 1.0, 0.0).astype(BF16)
    rank = jnp.dot(before, sel.astype(BF16), preferred_element_type=F32) + carry[...]
    count = carry[...] + jnp.sum(sel, axis=0, keepdims=True)
    carry[...] = count
    cnt_ref[...] = jnp.broadcast_to(count, cnt_ref.shape)
    rank1 = jnp.sum(jnp.where(lane == i1, rank, 0.0), axis=-1, keepdims=True)
    rank2 = jnp.sum(jnp.where(lane == i2, rank, 0.0), axis=-1, keepdims=True)
    rec = jnp.zeros(logits.shape, F32)
    for pos, val in ((R_E1, i1), (R_E2, i2), (R_W1, inv), (R_W2, e2 * inv), (R_RANK1, rank1), (R_RANK2, rank2)):
        rec = jnp.where(lane == pos, val, rec)
    route_ref[...] = rec


def _router(x, norm_g3, mods3, router, router_b, layer, j, latent, seq_len):
    t_tokens = x.shape[0]
    tm = 512
    row = _mod_row(latent, tm, seq_len)
    lanes = router.shape[-1]
    return pl.pallas_call(
        _router_kernel,
        out_shape=[jax.ShapeDtypeStruct((t_tokens * ROW, LANES), F32),
                   jax.ShapeDtypeStruct((t_tokens, lanes), F32),
                   jax.ShapeDtypeStruct((8, lanes), F32)],
        grid=(t_tokens // tm,),
        in_specs=[pl.BlockSpec((tm, D_MODEL), lambda t: (t, 0)),
                  pl.BlockSpec((None, 1, D_MODEL), lambda t: (layer, 0, 0)),
                  pl.BlockSpec((None, 1, 6 * D_MODEL), lambda t: (row(t), 0, 0)),
                  pl.BlockSpec((None, D_MODEL, lanes), lambda t: (j, 0, 0)),
                  pl.BlockSpec((None, 1, lanes), lambda t: (j, 0, 0))],
        out_specs=[pl.BlockSpec((tm * ROW, LANES), lambda t: (t, 0)),
                   pl.BlockSpec((tm, lanes), lambda t: (t, 0)),
                   pl.BlockSpec((8, lanes), lambda t: (0, 0))],
        scratch_shapes=[pltpu.VMEM((1, lanes), F32)],
        compiler_params=_params(("arbitrary",)),
        name="router",
    )(x, norm_g3, mods3, router, router_b)


def _row_copy(src, src_row, dst, dst_row, sem):
    s0 = pl.multiple_of(src_row * ROW, ROW)
    d0 = pl.multiple_of(dst_row * ROW, ROW)
    return pltpu.make_async_copy(src.at[pl.ds(s0, ROW), :], dst.at[pl.ds(d0, ROW), :], sem)


def _dispatch_kernel(p0_ref, p1_ref, h_ref, xs_in_ref, xs_ref, sem):
    del xs_in_ref
    tm = h_ref.shape[0] // ROW
    base = pl.program_id(0) * tm

    def issue(r, carry):
        _row_copy(h_ref, r, xs_ref, p0_ref[base + r], sem.at[0]).start()
        _row_copy(h_ref, r, xs_ref, p1_ref[base + r], sem.at[1]).start()
        return carry

    def wait(r, carry):
        _row_copy(h_ref, r, xs_ref, p0_ref[base + r], sem.at[0]).wait()
        _row_copy(h_ref, r, xs_ref, p1_ref[base + r], sem.at[1]).wait()
        return carry

    lax.fori_loop(0, tm, issue, 0, unroll=8)
    lax.fori_loop(0, tm, wait, 0, unroll=8)


def _dispatch(pos0, pos1, h_rows, n_slots):
    t_tokens = pos0.shape[0]
    tm = 512
    xs0 = jnp.zeros((n_slots * ROW, LANES), F32)
    return pl.pallas_call(
        _dispatch_kernel,
        out_shape=jax.ShapeDtypeStruct(xs0.shape, F32),
        grid_spec=pltpu.PrefetchScalarGridSpec(
            num_scalar_prefetch=2,
            grid=(t_tokens // tm,),
            in_specs=[pl.BlockSpec((tm * ROW, LANES), lambda t, p0, p1: (t, 0)),
                      pl.BlockSpec(memory_space=pl.ANY)],
            out_specs=pl.BlockSpec(memory_space=pl.ANY),
            scratch_shapes=[pltpu.SemaphoreType.DMA((2,))]),
        input_output_aliases={3: 0},
        compiler_params=_params(("arbitrary",)),
        name="moe_dispatch",
    )(pos0, pos1, h_rows, xs0)


def _moe_ffn_kernel(te_ref, nu_ref, xs_ref, w1_ref, w3_ref, w2_ref, y_ref, xb, acc, *, n_f):
    del te_ref
    i = pl.program_id(0)
    f = pl.program_id(1)
    tm = MOE_TM
    used = i < nu_ref[0]

    @pl.when(used & (f == 0))
    def _():
        for s in range(ROW):
            xb[:, s * LANES:(s + 1) * LANES] = xs_ref[pl.ds(s, tm, stride=ROW), :].astype(BF16)
        acc[...] = jnp.zeros(acc.shape, F32)

    @pl.when(used)
    def _():
        x = xb[...]
        a = jnp.dot(x, w1_ref[...].astype(BF16), preferred_element_type=F32)
        b = jnp.dot(x, w3_ref[...].astype(BF16), preferred_element_type=F32)
        acc[...] += _dot(_silu(a) * b, w2_ref[...])

    @pl.when(used & (f == n_f - 1))
    def _():
        for s in range(ROW):
            y_ref[pl.ds(s, tm, stride=ROW), :] = acc[:, s * LANES:(s + 1) * LANES]

    @pl.when(jnp.logical_not(used) & (f == n_f - 1))
    def _():
        y_ref[...] = jnp.zeros(y_ref.shape, F32)


def _moe_ffn(tile_expert, n_used, xs, w1, w3, w2, j):
    tm = MOE_TM
    n_tiles = xs.shape[0] // (tm * ROW)
    d_ff = w1.shape[-1]
    tf = 512
    n_f = d_ff // tf
    f_of = lambda i, f, nu: jnp.where(i < nu[0], f, n_f - 1)
    return pl.pallas_call(
        functools.partial(_moe_ffn_kernel, n_f=n_f),
        out_shape=jax.ShapeDtypeStruct(xs.shape, F32),
        grid_spec=pltpu.PrefetchScalarGridSpec(
            num_scalar_prefetch=2,
            grid=(n_tiles, n_f),
            in_specs=[pl.BlockSpec((tm * ROW, LANES), lambda i, f, te, nu: (jnp.minimum(i, nu[0] - 1), 0)),
                      pl.BlockSpec((None, None, D_MODEL, tf), lambda i, f, te, nu: (j, te[i], 0, f_of(i, f, nu))),
                      pl.BlockSpec((None, None, D_MODEL, tf), lambda i, f, te, nu: (j, te[i], 0, f_of(i, f, nu))),
                      pl.BlockSpec((None, None, tf, D_MODEL), lambda i, f, te, nu: (j, te[i], f_of(i, f, nu), 0))],
            out_specs=pl.BlockSpec((tm * ROW, LANES), lambda i, f, te, nu: (i, 0)),
            scratch_shapes=[pltpu.VMEM((tm, D_MODEL), BF16), pltpu.VMEM((tm, D_MODEL), F32)]),
        compiler_params=_params(("arbitrary", "arbitrary")),
        name="moe_ffn",
    )(tile_expert, n_used, xs, w1, w3, w2)


def _combine_kernel(p0_ref, p1_ref, x_ref, mod_ref, route_ref, y_ref, o_ref, buf0, buf1, sem):
    tm = x_ref.shape[0]
    base = pl.program_id(0) * tm

    def issue(r, carry):
        _row_copy(y_ref, p0_ref[base + r], buf0, r, sem.at[0]).start()
        _row_copy(y_ref, p1_ref[base + r], buf1, r, sem.at[1]).start()
        return carry

    def wait(r, carry):
        _row_copy(y_ref, p0_ref[base + r], buf0, r, sem.at[0]).wait()
        _row_copy(y_ref, p1_ref[base + r], buf1, r, sem.at[1]).wait()
        return carry

    lax.fori_loop(0, tm, issue, 0, unroll=8)
    lax.fori_loop(0, tm, wait, 0, unroll=8)
    route = route_ref[...]
    w_a = route[:, R_W1:R_W1 + 1]
    w_b = route[:, R_W2:R_W2 + 1]
    for s in range(ROW):
        cols = slice(s * LANES, (s + 1) * LANES)
        y = w_a * buf0[pl.ds(s, tm, stride=ROW), :] + w_b * buf1[pl.ds(s, tm, stride=ROW), :]
        o_ref[:, cols] = x_ref[:, cols] + mod_ref[:, 5 * D_MODEL + s * LANES:5 * D_MODEL + (s + 1) * LANES] * y


def _combine(pos0, pos1, x, mods3, route, y, latent, seq_len):
    t_tokens = x.shape[0]
    tm = 512
    row = _mod_row(latent, tm, seq_len)
    return pl.pallas_call(
        _combine_kernel,
        out_shape=jax.ShapeDtypeStruct((t_tokens, D_MODEL), F32),
        grid_spec=pltpu.PrefetchScalarGridSpec(
            num_scalar_prefetch=2,
            grid=(t_tokens // tm,),
            in_specs=[pl.BlockSpec((tm, D_MODEL), lambda t, p0, p1: (t, 0)),
                      pl.BlockSpec((None, 1, 6 * D_MODEL), lambda t, p0, p1: (row(t), 0, 0)),
                      pl.BlockSpec((tm, route.shape[1]), lambda t, p0, p1: (t, 0)),
                      pl.BlockSpec(memory_space=pl.ANY)],
            out_specs=pl.BlockSpec((tm, D_MODEL), lambda t, p0, p1: (t, 0)),
            scratch_shapes=[pltpu.VMEM((tm * ROW, LANES), F32), pltpu.VMEM((tm * ROW, LANES), F32),
                            pltpu.SemaphoreType.DMA((2,))]),
        compiler_params=_params(("arbitrary",)),
        name="moe_combine",
    )(pos0, pos1, x, mods3, route, y)


def _moe(x, norm_g3, mods3, router, router_b, w1, w3, w2, layer, j, latent, seq_len):
    t_tokens = x.shape[0]
    tm = MOE_TM
    n_tiles = 2 * t_tokens // tm + N_EXPERTS
    h_rows, route, cnt = _router(x, norm_g3, mods3, router, router_b, layer, j, latent, seq_len)
    expert = route[:, R_E1:R_E2 + 1].astype(jnp.int32)
    rank = route[:, R_RANK1:R_RANK2 + 1].astype(jnp.int32)
    tiles = (cnt[0, :N_EXPERTS].astype(jnp.int32) + tm - 1) // tm
    ends = jnp.cumsum(tiles)
    starts = ends - tiles
    ids = jnp.arange(N_EXPERTS, dtype=jnp.int32)
    start_of = jnp.sum(jnp.where(expert[:, :, None] == ids, starts * tm, 0), axis=-1)
    pos = start_of + rank
    n_used = ends[-1:]
    tile_ids = jnp.arange(n_tiles, dtype=jnp.int32)
    tile_expert = jnp.sum((tile_ids[:, None] >= ends[None, :]).astype(jnp.int32), axis=1)
    last_expert = jnp.max(jnp.where(tiles > 0, ids, 0))
    tile_expert = jnp.where(tile_ids < n_used, tile_expert, last_expert)
    pos0, pos1 = pos[:, 0], pos[:, 1]
    xs = _dispatch(pos0, pos1, h_rows, n_tiles * tm)
    y = _moe_ffn(tile_expert, n_used, xs, w1, w3, w2, j)
    return _combine(pos0, pos1, x, mods3, route, y, latent, seq_len)


def _ffn_kernel(x_ref, g_ref, mod_ref, w1_ref, w3_ref, w2_ref, o_ref, h_s, acc, *, n_f):
    d = D_MODEL
    f = pl.program_id(1)

    @pl.when(f == 0)
    def _():
        h = _modulate(x_ref[...], g_ref[...], mod_ref[:, 3 * d:4 * d], mod_ref[:, 4 * d:5 * d])
        h_s[...] = h.astype(BF16)
        acc[...] = jnp.zeros(acc.shape, F32)

    h = h_s[...]
    a = jnp.dot(h, w1_ref[...], preferred_element_type=F32)
    b = jnp.dot(h, w3_ref[...], preferred_element_type=F32)
    acc[...] += _dot(_silu(a) * b, w2_ref[...])

    @pl.when(f == n_f - 1)
    def _():
        o_ref[...] = x_ref[...] + mod_ref[:, 5 * d:6 * d] * acc[...]


def _ffn(x, norm_g3, mods3, w1, w3, w2, layer, j, latent, seq_len):
    t_tokens = x.shape[0]
    d_ff = w1.shape[-1]
    tm = 512
    tf = d_ff // 2
    n_f = d_ff // tf
    row = _mod_row(latent, tm, seq_len)
    return pl.pallas_call(
        functools.partial(_ffn_kernel, n_f=n_f),
        out_shape=jax.ShapeDtypeStruct((t_tokens, D_MODEL), F32),
        grid=(t_tokens // tm, n_f),
        in_specs=[pl.BlockSpec((tm, D_MODEL), lambda t, f: (t, 0)),
                  pl.BlockSpec((None, 1, D_MODEL), lambda t, f: (layer, 0, 0)),
                  pl.BlockSpec((None, 1, 6 * D_MODEL), lambda t, f: (row(t), 0, 0)),
                  pl.BlockSpec((None, D_MODEL, tf), lambda t, f: (j, 0, f)),
                  pl.BlockSpec((None, D_MODEL, tf), lambda t, f: (j, 0, f)),
                  pl.BlockSpec((None, tf, D_MODEL), lambda t, f: (j, f, 0))],
        out_specs=pl.BlockSpec((tm, D_MODEL), lambda t, f: (t, 0)),
        scratch_shapes=[pltpu.VMEM((tm, D_MODEL), BF16), pltpu.VMEM((tm, D_MODEL), F32)],
        compiler_params=_params(("arbitrary", "arbitrary")),
        name="ffn",
    )(x, norm_g3, mods3, w1, w3, w2)


def _rope_tables():
    half = A_QK // 2
    nf = half // 2
    t = np.arange(DEC_SEQ)
    pos = np.stack([(t // GRID_W), (t % GRID_W)], axis=1).astype(np.float32)
    inv = jnp.asarray(ROPE_BASE, F32) ** (-jnp.arange(nf, dtype=F32) / nf)
    lane = np.arange(GW) % A_QK
    which = (lane >= half).astype(np.int32)
    freq = lane % nf
    ang = jnp.asarray(pos)[:, which] * inv[freq][None, :]
    return jnp.cos(ang), jnp.sin(ang)


def _tile_vec(v, reps):
    return jnp.tile(v, (1, reps))[:, None, :]


def _prepare_params(w_in, w_out, a_qnorm, a_knorm, a_lam, a_subln, b_gate, b_gate_bias, b_onorm,
                    c_dw, c_dw_b, c_ln_g, c_ln_b, d_qnorm, d_knorm):
    lr0 = 1536
    lr1 = lr0 + 2 * B_GATE_RANK
    pad = PROJ_W - w_in.shape[-1]
    w_in_p = jnp.concatenate([w_in[:, :, :lr0], w_in[:, :, lr1:], w_in[:, :, lr0:lr1],
                              jnp.zeros(w_in.shape[:2] + (pad,), w_in.dtype)], axis=-1).astype(BF16)
    hk = B_HEADS * B_DK
    wg = jnp.zeros((DEPTH, 2, GW, hk), F32)
    wg = wg.at[:, 0, 0:B_GATE_RANK].set(b_gate[:, 0])
    wg = wg.at[:, 1, B_GATE_RANK:2 * B_GATE_RANK].set(b_gate[:, 1])
    return dict(
        w_in=w_in_p, w_out=w_out.astype(BF16),
        a_qnorm=_tile_vec(a_qnorm, GW // A_QK), a_knorm=_tile_vec(a_knorm, GW // A_QK),
        a_lam=a_lam, a_subln=_tile_vec(a_subln, GW // A_V),
        b_gate=wg.astype(BF16), b_gate_bias=b_gate_bias[:, :, None, :], b_onorm=_tile_vec(b_onorm, GW // B_DV),
        c_dw=c_dw, c_dw_b=c_dw_b[:, None, :], c_ln_g=c_ln_g[:, None, :], c_ln_b=c_ln_b[:, None, :],
        d_qnorm=_tile_vec(d_qnorm, GW // D_HEAD), d_knorm=_tile_vec(d_knorm, GW // D_HEAD),
    )


def _lambda_init(i):
    return 0.8 - 0.6 * math.exp(-0.3 * i)


def kernel(x_prompt, x_sample, cache_a_k, cache_a_v, state_b, cache_d_k, cache_d_v, c, c_ctx,
           norm1_g, norm2_g, w_ada, b_ada, w_in, w_out, a_qnorm, a_knorm, a_lam, a_subln,
           b_gate, b_gate_bias, b_onorm, c_dw, c_dw_b, c_ln_g, c_ln_b, d_qnorm, d_knorm, d_rpb,
           ffn_w1, ffn_w3, ffn_w2, moe_router, moe_router_b, moe_w1, moe_w3, moe_w2):
    lp = _prepare_params(w_in, w_out, a_qnorm, a_knorm, a_lam, a_subln, b_gate, b_gate_bias, b_onorm,
                         c_dw, c_dw_b, c_ln_g, c_ln_b, d_qnorm, d_knorm)
    norm1 = norm1_g[:, None, :]
    norm2 = norm2_g[:, None, :]
    b_ada3 = b_ada[:, None, :]
    cond8 = jnp.concatenate([c_ctx[None, :], c, jnp.zeros((8 - 1 - DEC_BATCH, D_MODEL), F32)], axis=0)
    rope = _rope_tables()
    ffn_w = [w.astype(BF16) for w in (ffn_w1, ffn_w3, ffn_w2)]
    lanes = 128
    router_p = jnp.pad(moe_router, ((0, 0), (0, 0), (0, lanes - N_EXPERTS)))
    router_b_p = jnp.pad(moe_router_b, ((0, 0), (0, lanes - N_EXPERTS)))[:, None, :]
    ck_a = cache_a_k.reshape(DEC_BATCH, DEPTH, PAST_LEN, GW)
    cv_a = cache_a_v.reshape(DEC_BATCH, DEPTH, PAST_LEN, GW)
    ck_d = cache_d_k.reshape(DEC_BATCH, DEPTH, PAST_LEN, GW)
    cv_d = cache_d_v.reshape(DEC_BATCH, DEPTH, PAST_LEN, GW)
    s0_lat = _state_to_kernel_layout(state_b.transpose(1, 0, 2, 3, 4, 5).reshape(
        DEPTH * DEC_BATCH, 2, B_HEADS, B_DK, B_DV)).reshape(DEPTH, DEC_BATCH, 2, GW, B_HEADS * B_DK)

    def channel_mixer(x, mods3, i, latent, seq_len):
        j = i // 2
        if i % 2 == 0:
            return _ffn(x, norm2, mods3, *ffn_w, i, j, latent, seq_len)
        return _moe(x, norm2, mods3, router_p, router_b_p, moe_w1, moe_w3, moe_w2, i, j, latent, seq_len)

    xc = x_prompt.reshape(BATCH * SEQ, D_MODEL)
    xl = x_sample.reshape(DEC_BATCH * DEC_SEQ, D_MODEL)
    new_ak, new_av, new_sb, new_dk, new_dv = [], [], [], [], []
    for i in range(DEPTH):
        lam_init = _lambda_init(i)
        mods3 = _adaln(cond8, w_ada, b_ada3, i)[:, None, :]

        proj = _inproj(xc, norm1, mods3, lp["w_in"], i, False, SEQ)
        o_a, ak = _attn_a(proj, lp, i, False, BATCH, SEQ, lam_init)
        o_f, s_f = _gla(proj, lp, i, BATCH, SEQ, False)
        o_b, s_b = _gla(proj, lp, i, BATCH, SEQ, True, o_fwd=o_f)
        o_c = _conv(proj, lp, i, BATCH, SEQ)
        o_d, dk = _attn_d_ctx(proj, lp, i, BATCH, SEQ)
        xc = _outproj(o_a, o_b, o_c, o_d, xc, mods3, lp["w_out"], i, False, SEQ)
        xc = channel_mixer(xc, mods3, i, False, SEQ)
        new_ak.append(ak.reshape(BATCH, SEQ, A_HEADS, 2 * A_QK))
        new_av.append(proj[:, BLK_AV * GW:(BLK_AV + 1) * GW].reshape(BATCH, SEQ, A_HEADS, A_V))
        new_sb.append(jnp.stack([_state_from_kernel_layout(s_f), _state_from_kernel_layout(s_b)], axis=1))
        new_dk.append(dk.reshape(BATCH, SEQ, D_HEADS, D_HEAD))
        new_dv.append(proj[:, BLK_DV * GW:(BLK_DV + 1) * GW].reshape(BATCH, SEQ, D_HEADS, D_HEAD))

        proj = _inproj(xl, norm1, mods3, lp["w_in"], i, True, DEC_SEQ)
        o_a, = _attn_a(proj, lp, i, True, DEC_BATCH, DEC_SEQ, lam_init, ck_a, cv_a, rope)
        o_f, _ = _gla(proj, lp, i, DEC_BATCH, DEC_SEQ, False, s0=s0_lat[i])
        o_b, _ = _gla(proj, lp, i, DEC_BATCH, DEC_SEQ, True, s0=s0_lat[i], o_fwd=o_f)
        o_c = _conv(proj, lp, i, DEC_BATCH, DEC_SEQ)
        o_d = _attn_d_lat(proj, lp, i, ck_d, cv_d, _na_bias_table(d_rpb[i]))
        xl = _outproj(o_a, o_b, o_c, o_d, xl, mods3, lp["w_out"], i, True, DEC_SEQ)
        xl = channel_mixer(xl, mods3, i, True, DEC_SEQ)

    return (xc.reshape(BATCH, SEQ, D_MODEL), xl.reshape(DEC_BATCH, DEC_SEQ, D_MODEL),
            jnp.stack(new_ak, axis=1), jnp.stack(new_av, axis=1), jnp.stack(new_sb, axis=1),
            jnp.stack(new_dk, axis=1), jnp.stack(new_dv, axis=1))
```

```python
import functools
import math

import numpy as np
import jax
import jax.numpy as jnp
from jax import lax
from jax.experimental import pallas as pl
from jax.experimental.pallas import tpu as pltpu

F32 = jnp.float32
BF16 = jnp.bfloat16
MIX_DTYPE = BF16

D_MODEL = 1024
BATCH = 16
SEQ = 256
DEPTH = 2
DEC_BATCH = 4
DEC_SEQ = 2048
PAST_LEN = 512
GRID_W = 64
A_HEADS = 4
A_QK = 32
A_V = 64
B_HEADS = 4
B_DK = 32
B_DV = 64
B_GATE_RANK = 16
B_TAU = 16.0
C_CH = 256
C_KSIZE = 31
D_HEADS = 4
D_HEAD = 64
NA_ROWS = 8
NA_COLS = 16
NA_QROWS = 4
NA_KROWS = 12
ROPE_BASE = 10000.0
D_FF = 2816
N_EXPERTS = 8
D_FF_EXPERT = 3584
EPS = 1e-6
NEG_INF = -1e30
LOG2_E = math.log2(math.e)

GW = 256
N_PROJ_BLOCKS = 12
PROJ_W = GW * N_PROJ_BLOCKS
BLK_AQ, BLK_AK, BLK_AV, BLK_BQK, BLK_BV, BLK_BG = 0, 1, 2, 3, 4, 5
BLK_C = 3
BLK_DQ, BLK_DK, BLK_DV, BLK_LR = 8, 9, 10, 11

GLA_SUB = 16
GLA_ROWS = 256
VMEM_LIMIT = 56 * 1024 * 1024


def _params(sem, vmem=VMEM_LIMIT):
    return pltpu.CompilerParams(dimension_semantics=sem, vmem_limit_bytes=vmem)


def _dot(a, b):
    return jnp.dot(a.astype(BF16), b.astype(BF16), preferred_element_type=F32)


def _dot_nt(a, b):
    return lax.dot_general(a.astype(BF16), b.astype(BF16), (((1,), (1,)), ((), ())),
                           preferred_element_type=F32)


def _dot_exact_rhs(x, m):
    hi = x.astype(BF16)
    lo = (x - hi.astype(F32)).astype(BF16)
    return (jnp.dot(hi, m, preferred_element_type=F32) + jnp.dot(lo, m, preferred_element_type=F32))


def _sigmoid(x):
    return 1.0 / (1.0 + jnp.exp(-x))


def _silu(x):
    return x * _sigmoid(x)


def _seg_matrix(n, seg):
    r = lax.broadcasted_iota(jnp.int32, (n, n), 0)
    c = lax.broadcasted_iota(jnp.int32, (n, n), 1)
    return jnp.where((r ^ c) < seg, 1.0, 0.0).astype(BF16)


def _seg_rmsnorm(x, seg, w):
    ms = _dot_exact_rhs(x * x, _seg_matrix(x.shape[-1], seg)) * (1.0 / seg)
    return x * lax.rsqrt(ms + EPS) * w


def _lane_mask(shape, lo, width):
    lane = lax.broadcasted_iota(jnp.int32, shape, len(shape) - 1)
    return (lane >= lo) & (lane < lo + width)


def _modulate(x, g, shift, scale):
    y = x * lax.rsqrt(jnp.mean(x * x, axis=-1, keepdims=True) + EPS)
    return y * g * (1.0 + scale) + shift


def _rope(x, cos, sin):
    lane = lax.broadcasted_iota(jnp.int32, x.shape, 1)
    w = x.shape[1]
    rot = jnp.where((lane & 15) < 8, -pltpu.roll(x, w - 8, 1), pltpu.roll(x, 8, 1))
    return x * cos + rot * sin


def _adaln_kernel(c_ref, w_ref, b_ref, o_ref):
    o_ref[...] = _dot(_silu(c_ref[...]), w_ref[...]) + b_ref[...]


def _adaln(cond8, w_ada, b_ada3, layer):
    tn = 1536
    n = 6 * D_MODEL
    return pl.pallas_call(
        _adaln_kernel,
        out_shape=jax.ShapeDtypeStruct((8, n), F32),
        grid=(n // tn,),
        in_specs=[pl.BlockSpec((8, D_MODEL), lambda j: (0, 0)),
                  pl.BlockSpec((None, D_MODEL, tn), lambda j: (layer, 0, j)),
                  pl.BlockSpec((None, 1, tn), lambda j: (layer, 0, j))],
        out_specs=pl.BlockSpec((8, tn), lambda j: (0, j)),
        compiler_params=_params(("arbitrary",)),
        name="adaln",
    )(cond8, w_ada, b_ada3)


def _mod_row(latent, tm, seq_len):
    if latent:
        return lambda t: 1 + (t * tm) // seq_len
    return lambda t: 0


def _inproj_kernel(x_ref, g_ref, mod_ref, w_ref, o_ref):
    d = D_MODEL
    h = _modulate(x_ref[...], g_ref[...], mod_ref[:, 0:d], mod_ref[:, d:2 * d])
    o_ref[...] = _dot(h, w_ref[...])


def _inproj(x, norm_g3, mods3, w_in, layer, latent, seq_len):
    t_tokens = x.shape[0]
    tm = 512
    row = _mod_row(latent, tm, seq_len)
    return pl.pallas_call(
        _inproj_kernel,
        out_shape=jax.ShapeDtypeStruct((t_tokens, PROJ_W), F32),
        grid=(t_tokens // tm,),
        in_specs=[pl.BlockSpec((tm, D_MODEL), lambda t: (t, 0)),
                  pl.BlockSpec((None, 1, D_MODEL), lambda t: (layer, 0, 0)),
                  pl.BlockSpec((None, 1, 6 * D_MODEL), lambda t: (row(t), 0, 0)),
                  pl.BlockSpec((None, D_MODEL, PROJ_W), lambda t: (layer, 0, 0))],
        out_specs=pl.BlockSpec((tm, PROJ_W), lambda t: (t, 0)),
        compiler_params=_params(("arbitrary",)),
        name="inproj",
    )(x, norm_g3, mods3, w_in)


def _diff_lambda(lam_ref, lam_init):
    lam = lam_ref[...]
    s1 = jnp.sum(lam[0:1] * lam[1:2], axis=1, keepdims=True)
    s2 = jnp.sum(lam[2:3] * lam[3:4], axis=1, keepdims=True)
    return jnp.exp(s1) - jnp.exp(s2) + lam_init


def _values_and_ones(v):
    one = jnp.where(lax.broadcasted_iota(jnp.int32, (v.shape[0], A_V), 1) == 0, 1.0, 0.0)
    parts = []
    for h in range(A_HEADS):
        parts += [v[:, h * A_V:(h + 1) * A_V], one]
    return jnp.concatenate(parts, axis=1).astype(BF16)


def _attn_a_kernel(*refs, n_own, n_ctx, tq, latent, lam_init):
    if latent:
        (q_ref, k_ref, v_ref, ck_ref, cv_ref, cos_ref, sin_ref, qn_ref, kn_ref, lam_ref, sub_ref,
         o_ref, ks, vs) = refs
    else:
        (q_ref, k_ref, v_ref, qn_ref, kn_ref, lam_ref, sub_ref, o_ref, ko_ref, ks, vs) = refs
    t = pl.program_id(1)

    @pl.when(t == 0)
    def _():
        kn = _seg_rmsnorm(k_ref[...], A_QK, kn_ref[...])
        if latent:
            kn = _rope(kn, cos_ref[...], sin_ref[...])
            ks[n_own:n_own + n_ctx, :] = ck_ref[...].astype(BF16)
            vs[n_own:n_own + n_ctx, :] = _values_and_ones(cv_ref[...])
        else:
            ko_ref[...] = kn
        ks[0:n_own, :] = kn.astype(BF16)
        vs[0:n_own, :] = _values_and_ones(v_ref[...])

    qn = _seg_rmsnorm(q_ref[...], A_QK, qn_ref[...])
    if latent:
        r0 = pl.multiple_of(t * tq, tq)
        qn = _rope(qn, cos_ref[pl.ds(r0, tq), :], sin_ref[pl.ds(r0, tq), :])
    qn = qn * (A_QK ** -0.5 * LOG2_E)
    lam = _diff_lambda(lam_ref, lam_init)
    k_all = ks[...]
    heads = []
    for h in range(A_HEADS):
        v_h = vs[:, h * 2 * A_V:(h + 1) * 2 * A_V]
        acc = []
        for m in range(2):
            qm = jnp.where(_lane_mask(qn.shape, (2 * h + m) * A_QK, A_QK), qn, 0.0)
            s = _dot_nt(qm, k_all)
            e = jnp.exp2(s - jnp.max(s, axis=-1, keepdims=True))
            acc.append(_dot(e, v_h))
        inv0 = 1.0 / acc[0][:, A_V:A_V + 1]
        inv1 = lam / acc[1][:, A_V:A_V + 1]
        heads.append((acc[0] * inv0 - acc[1] * inv1)[:, :A_V])
    o = jnp.concatenate(heads, axis=1)
    o_ref[...] = (_seg_rmsnorm(o, A_V, sub_ref[...]) * (1.0 - lam_init)).astype(o_ref.dtype)


def _attn_a(proj, lp, layer, latent, nseq, seq_len, lam_init, cache_k=None, cache_v=None, rope=None):
    tq = 256
    nt = seq_len // tq
    n_ctx = PAST_LEN if latent else 0
    kern = functools.partial(_attn_a_kernel, n_own=seq_len, n_ctx=n_ctx, tq=tq, latent=latent,
                             lam_init=lam_init)
    vec = lambda name: pl.BlockSpec((None, 1, GW), lambda b, t: (layer, 0, 0))
    in_specs = [pl.BlockSpec((tq, GW), lambda b, t: (b * nt + t, BLK_AQ)),
                pl.BlockSpec((seq_len, GW), lambda b, t: (b, BLK_AK)),
                pl.BlockSpec((seq_len, GW), lambda b, t: (b, BLK_AV))]
    args = [proj, proj, proj]
    if latent:
        in_specs += [pl.BlockSpec((None, None, PAST_LEN, GW), lambda b, t: (b, layer, 0, 0)),
                     pl.BlockSpec((None, None, PAST_LEN, GW), lambda b, t: (b, layer, 0, 0)),
                     pl.BlockSpec((seq_len, GW), lambda b, t: (0, 0)),
                     pl.BlockSpec((seq_len, GW), lambda b, t: (0, 0))]
        args += [cache_k, cache_v, rope[0], rope[1]]
    in_specs += [vec("q"), vec("k"),
                 pl.BlockSpec((None, 4, A_QK), lambda b, t: (layer, 0, 0)),
                 vec("s")]
    args += [lp["a_qnorm"], lp["a_knorm"], lp["a_lam"], lp["a_subln"]]
    out_shape = [jax.ShapeDtypeStruct((nseq * seq_len, GW), MIX_DTYPE)]
    out_specs = [pl.BlockSpec((tq, GW), lambda b, t: (b * nt + t, 0))]
    if not latent:
        out_shape.append(jax.ShapeDtypeStruct((nseq * seq_len, GW), F32))
        out_specs.append(pl.BlockSpec((seq_len, GW), lambda b, t: (b, 0)))
    return pl.pallas_call(
        kern,
        out_shape=out_shape,
        grid=(nseq, nt),
        in_specs=in_specs,
        out_specs=out_specs,
        scratch_shapes=[pltpu.VMEM((seq_len + n_ctx, GW), BF16), pltpu.VMEM((seq_len + n_ctx, 2 * GW), BF16)],
        compiler_params=_params(("arbitrary", "arbitrary")),
        name="attn_a_lat" if latent else "attn_a_ctx",
    )(*args)


def _attn_d_ctx_kernel(q_ref, k_ref, v_ref, qn_ref, kn_ref, o_ref, ko_ref):
    kn = _seg_rmsnorm(k_ref[...], D_HEAD, kn_ref[...])
    ko_ref[...] = kn
    qn = _seg_rmsnorm(q_ref[...], D_HEAD, qn_ref[...]) * (D_HEAD ** -0.5)
    kb = kn.astype(BF16)
    vb = v_ref[...].astype(BF16)
    o = jnp.zeros(qn.shape, F32)
    for h in range(D_HEADS):
        hm = _lane_mask(qn.shape, h * D_HEAD, D_HEAD)
        s = _dot_nt(jnp.where(hm, qn, 0.0), kb)
        e = jnp.exp(s - jnp.max(s, axis=-1, keepdims=True))
        p = e * (1.0 / jnp.sum(e, axis=-1, keepdims=True))
        o = o + jnp.where(hm, _dot(p, vb), 0.0)
    o_ref[...] = o.astype(o_ref.dtype)


def _attn_d_ctx(proj, lp, layer, nseq, seq_len):
    vec = pl.BlockSpec((None, 1, GW), lambda b: (layer, 0, 0))
    blk = lambda c: pl.BlockSpec((seq_len, GW), lambda b: (b, c))
    return pl.pallas_call(
        _attn_d_ctx_kernel,
        out_shape=[jax.ShapeDtypeStruct((nseq * seq_len, GW), MIX_DTYPE),
                   jax.ShapeDtypeStruct((nseq * seq_len, GW), F32)],
        grid=(nseq,),
        in_specs=[blk(BLK_DQ), blk(BLK_DK), blk(BLK_DV), vec, vec],
        out_specs=[blk(0), blk(0)],
        compiler_params=_params(("arbitrary",)),
        name="attn_d_ctx",
    )(proj, proj, proj, lp["d_qnorm"], lp["d_knorm"])


def _attn_d_lat_kernel(q_ref, k_ref, v_ref, ck_ref, cv_ref, qn_ref, kn_ref, bias_ref, o_ref,
                       ks, vs, cks, cvs):
    g = pl.program_id(1)
    n_rows = DEC_SEQ // GRID_W
    n_groups = n_rows // NA_QROWS
    n_loc = NA_KROWS * GRID_W

    @pl.when(g == 0)
    def _():
        ks[...] = _seg_rmsnorm(k_ref[...], D_HEAD, kn_ref[...]).astype(BF16)
        vs[...] = v_ref[...].astype(BF16)
        cks[...] = ck_ref[...].astype(BF16)
        cvs[...] = cv_ref[...].astype(BF16)

    qn = _seg_rmsnorm(q_ref[...], D_HEAD, qn_ref[...]) * (D_HEAD ** -0.5)
    row_start = jnp.clip(g * NA_QROWS - NA_ROWS // 2, 0, n_rows - NA_KROWS)
    variant = jnp.where(g == 0, 0, jnp.where(g == n_groups - 1, 2, 1))
    k0 = pl.multiple_of(row_start * GRID_W, GRID_W)
    kl = ks[pl.ds(k0, n_loc), :]
    vl = vs[pl.ds(k0, n_loc), :]
    kc = cks[...]
    vc = cvs[...]
    o = jnp.zeros(qn.shape, F32)
    for h in range(D_HEADS):
        hm = _lane_mask(qn.shape, h * D_HEAD, D_HEAD)
        qm = jnp.where(hm, qn, 0.0)
        s_loc = _dot_nt(qm, kl) + bias_ref[h, variant]
        s_ctx = _dot_nt(qm, kc)
        mx = jnp.maximum(jnp.max(s_loc, axis=-1, keepdims=True), jnp.max(s_ctx, axis=-1, keepdims=True))
        e_loc = jnp.exp(s_loc - mx)
        e_ctx = jnp.exp(s_ctx - mx)
        inv = 1.0 / (jnp.sum(e_loc, axis=-1, keepdims=True) + jnp.sum(e_ctx, axis=-1, keepdims=True))
        o = o + jnp.where(hm, _dot(e_loc * inv, vl) + _dot(e_ctx * inv, vc), 0.0)
    o_ref[...] = o.astype(o_ref.dtype)


def _na_bias_table(rpb):
    n_rows = DEC_SEQ // GRID_W
    n_groups = n_rows // NA_QROWS
    q = np.arange(GRID_W)
    kcol = np.arange(GRID_W)
    cs = np.clip(q - NA_COLS // 2, 0, GRID_W - NA_COLS)
    in_win = (kcol[None, :] >= cs[:, None]) & (kcol[None, :] < cs[:, None] + NA_COLS)
    dc = np.clip(kcol[None, :] - q[:, None] + (NA_COLS - 1), 0, 2 * NA_COLS - 2)
    pick_col = (dc[None] == np.arange(2 * NA_COLS - 1)[:, None, None]).astype(np.float32)
    cols = jnp.einsum("hrc,cqk->hrqk", rpb.astype(F32), jnp.asarray(pick_col), precision=lax.Precision.HIGHEST)
    cols = jnp.where(in_win[None, None], cols, NEG_INF)
    masked = jnp.full((D_HEADS, GRID_W, GRID_W), NEG_INF, F32)
    slabs = []
    for g in (0, 1, n_groups - 1):
        first_key_row = np.clip(g * NA_QROWS - NA_ROWS // 2, 0, n_rows - NA_KROWS)
        for a in range(NA_QROWS):
            r = g * NA_QROWS + a
            win = np.clip(r - NA_ROWS // 2, 0, n_rows - NA_ROWS)
            pieces = []
            for i in range(NA_KROWS):
                key_row = first_key_row + i
                in_rows = win <= key_row < win + NA_ROWS
                pieces.append(cols[:, key_row - r + NA_ROWS - 1] if in_rows else masked)
            slabs.append(jnp.concatenate(pieces, axis=-1))
    return jnp.stack(slabs, axis=1).reshape(D_HEADS, 3, NA_QROWS * GRID_W, NA_KROWS * GRID_W)


def _attn_d_lat(proj, lp, layer, cache_k, cache_v, bias):
    n_groups = DEC_SEQ // GRID_W // NA_QROWS
    tq = NA_QROWS * GRID_W
    vec = pl.BlockSpec((None, 1, GW), lambda b, r: (layer, 0, 0))
    seq = lambda c: pl.BlockSpec((DEC_SEQ, GW), lambda b, r: (b, c))
    cache = pl.BlockSpec((None, None, PAST_LEN, GW), lambda b, r: (b, layer, 0, 0))
    return pl.pallas_call(
        _attn_d_lat_kernel,
        out_shape=jax.ShapeDtypeStruct((DEC_BATCH * DEC_SEQ, GW), MIX_DTYPE),
        grid=(DEC_BATCH, n_groups),
        in_specs=[pl.BlockSpec((tq, GW), lambda b, r: (b * n_groups + r, BLK_DQ)),
                  seq(BLK_DK), seq(BLK_DV), cache, cache, vec, vec,
                  pl.BlockSpec(bias.shape, lambda b, r: (0, 0, 0, 0))],
        out_specs=pl.BlockSpec((tq, GW), lambda b, r: (b * n_groups + r, 0)),
        scratch_shapes=[pltpu.VMEM((DEC_SEQ, GW), BF16), pltpu.VMEM((DEC_SEQ, GW), BF16),
                        pltpu.VMEM((PAST_LEN, GW), BF16), pltpu.VMEM((PAST_LEN, GW), BF16)],
        compiler_params=_params(("arbitrary", "arbitrary")),
        name="attn_d_lat",
    )(proj, proj, proj, cache_k, cache_v, lp["d_qnorm"], lp["d_knorm"], bias)


def _conv_kernel(c_ref, w_ref, b_ref, g_ref, beta_ref, o_ref, pad, *, seq_len):
    half = C_KSIZE // 2
    top = 16
    cin = c_ref[...]
    u = cin[:, :C_CH] * _sigmoid(cin[:, C_CH:])
    pad[0:top, :] = jnp.zeros((top, C_CH), F32)
    pad[top + seq_len:top + seq_len + top, :] = jnp.zeros((top, C_CH), F32)
    pad[top:top + seq_len, :] = u
    w = w_ref[...]
    rb = 256
    for r0 in range(0, seq_len, rb):
        acc = jnp.zeros((rb, C_CH), F32)
        n_hi = -(-C_KSIZE // 8)
        for lo in range(8):
            base = r0 + top - half + lo
            shifted = pad[base:base + rb + 8 * (n_hi - 1), :]
            for hi in range(n_hi):
                k = 8 * hi + lo
                if k < C_KSIZE:
                    acc = acc + shifted[8 * hi:8 * hi + rb, :] * w[k:k + 1, :]
        acc = acc + b_ref[...]
        mu = jnp.mean(acc, axis=-1, keepdims=True)
        xc = acc - mu
        y = xc * lax.rsqrt(jnp.mean(xc * xc, axis=-1, keepdims=True) + EPS) * g_ref[...] + beta_ref[...]
        o_ref[r0:r0 + rb, :] = _silu(y).astype(o_ref.dtype)


def _conv(proj, lp, layer, nseq, seq_len):
    vec = pl.BlockSpec((None, 1, C_CH), lambda b: (layer, 0, 0))
    return pl.pallas_call(
        functools.partial(_conv_kernel, seq_len=seq_len),
        out_shape=jax.ShapeDtypeStruct((nseq * seq_len, C_CH), MIX_DTYPE),
        grid=(nseq,),
        in_specs=[pl.BlockSpec((seq_len, 2 * C_CH), lambda b: (b, BLK_C)),
                  pl.BlockSpec((None, C_KSIZE, C_CH), lambda b: (layer, 0, 0)),
                  vec, vec, vec],
        out_specs=pl.BlockSpec((seq_len, C_CH), lambda b: (b, 0)),
        scratch_shapes=[pltpu.VMEM((seq_len + 32, C_CH), F32)],
        compiler_params=_params(("arbitrary",)),
        name="conv",
    )(proj, lp["c_dw"], lp["c_dw_b"], lp["c_ln_g"], lp["c_ln_b"])


def _gla_kernel(*refs, reverse, has_s0, n_blocks):
    refs = list(refs)
    qk_ref, v_ref, lr_ref, wg_ref, gb_ref = refs[:5]
    refs = refs[5:]
    s0_ref = refs.pop(0) if has_s0 else None
    if reverse:
        of_ref, bg_ref, on_ref = refs[:3]
        refs = refs[3:]
    o_ref, so_ref, st = refs
    rb, cs = GLA_ROWS, GLA_SUB
    hk = B_HEADS * B_DK
    j = pl.program_id(1)

    @pl.when(j == 0)
    def _():
        st[...] = s0_ref[...] if has_s0 else jnp.zeros(st.shape, F32)

    qk = qk_ref[...]
    q = qk[:, :hk] * (B_DK ** -0.5)
    k = qk[:, hk:]
    v = v_ref[...]
    pre = _dot(lr_ref[...], wg_ref[...]) + gb_ref[...]
    g = (jnp.minimum(pre, 0.0) - jnp.log(1.0 + jnp.exp(-jnp.abs(pre)))) * (1.0 / B_TAU)

    pos = lax.broadcasted_iota(jnp.int32, (rb, hk), 0) & (cs - 1)
    pre_sum = g
    suf_sum = g
    step = 1
    while step < cs:
        pre_sum = pre_sum + jnp.where(pos >= step, pltpu.roll(pre_sum, step, 0), 0.0)
        suf_sum = suf_sum + jnp.where(pos < cs - step, pltpu.roll(suf_sum, rb - step, 0), 0.0)
        step *= 2
    total = pre_sum + suf_sum - g
    if reverse:
        z = suf_sum
        k_dec = k * jnp.exp(pre_sum - g)
    else:
        z = pre_sum
        k_dec = k * jnp.exp(suf_sum - g)
    q_dec = q * jnp.exp(z)

    r_i = lax.broadcasted_iota(jnp.int32, (hk, GW), 0)
    c_i = lax.broadcasted_iota(jnp.int32, (hk, GW), 1)
    head_sum = jnp.where((r_i >> 5) == (c_i >> 6), 1.0, 0.0).astype(BF16)

    row_in_sub = lax.broadcasted_iota(jnp.int32, (cs, hk), 0)
    intra = []
    for n in range(rb // cs):
        qs = q[n * cs:(n + 1) * cs]
        zs = z[n * cs:(n + 1) * cs]
        pairs = []
        for jl in range(cs):
            r = n * cs + jl
            decay = jnp.exp(jnp.minimum(zs - z[r:r + 1], 0.0))
            keep = (row_in_sub <= jl) if reverse else (row_in_sub >= jl)
            pairs.append(jnp.where(keep, qs * k[r:r + 1] * decay, 0.0))
        w = jnp.dot(jnp.concatenate(pairs, axis=0).astype(BF16), head_sum, preferred_element_type=F32)
        acc = jnp.zeros((cs, GW), F32)
        for jl in range(cs):
            r = n * cs + jl
            acc = acc + w[jl * cs:(jl + 1) * cs] * v[r:r + 1]
        intra.append(acc)
    o = jnp.concatenate(intra, axis=0)

    r_s = lax.broadcasted_iota(jnp.int32, st.shape, 0)
    c_s = lax.broadcasted_iota(jnp.int32, st.shape, 1)
    diag = (r_s >> 6) == (c_s >> 5)
    n_sub = rb // cs
    row = lax.broadcasted_iota(jnp.int32, (rb, hk), 0)

    def per_sub_chunk(x):
        return jnp.concatenate([jnp.where((row >= n * cs) & (row < (n + 1) * cs), x, 0.0).astype(BF16)
                                for n in range(n_sub)], axis=1)

    kv_all = _dot(v.T, per_sub_chunk(k_dec))
    state = st[...]
    states = [None] * n_sub
    order = range(n_sub - 1, -1, -1) if reverse else range(n_sub)
    for n in order:
        states[n] = state.astype(BF16)
        decay = jnp.exp(total[n * cs:n * cs + 1, :])
        state = state * decay + jnp.where(diag, kv_all[:, n * hk:(n + 1) * hk], 0.0)
    st[...] = state
    o = o + _dot_nt(per_sub_chunk(q_dec), jnp.concatenate(states, axis=1))

    if reverse:
        o = _seg_rmsnorm(o + of_ref[...], B_DV, on_ref[...]) * _silu(bg_ref[...])
    o_ref[...] = o.astype(o_ref.dtype)

    @pl.when(j == n_blocks - 1)
    def _():
        so_ref[...] = state


def _gla(proj, lp, layer, nseq, seq_len, reverse, s0=None, o_fwd=None):
    rb = GLA_ROWS
    nb = seq_len // rb
    hk = B_HEADS * B_DK
    blk = (lambda b, j: b * nb + nb - 1 - j) if reverse else (lambda b, j: b * nb + j)
    tile = lambda c: pl.BlockSpec((rb, GW), lambda b, j: (blk(b, j), c))
    d = 1 if reverse else 0
    in_specs = [tile(BLK_BQK), tile(BLK_BV), tile(BLK_LR),
                pl.BlockSpec((None, None, GW, hk), lambda b, j: (layer, d, 0, 0)),
                pl.BlockSpec((None, None, 1, hk), lambda b, j: (layer, d, 0, 0))]
    args = [proj, proj, proj, lp["b_gate"], lp["b_gate_bias"]]
    if s0 is not None:
        in_specs.append(pl.BlockSpec((None, None, GW, hk), lambda b, j: (b, d, 0, 0)))
        args.append(s0)
    if reverse:
        in_specs += [tile(0), tile(BLK_BG), pl.BlockSpec((None, 1, GW), lambda b, j: (layer, 0, 0))]
        args += [o_fwd, proj, lp["b_onorm"]]
    return pl.pallas_call(
        functools.partial(_gla_kernel, reverse=reverse, has_s0=s0 is not None, n_blocks=nb),
        out_shape=[jax.ShapeDtypeStruct((nseq * seq_len, GW), MIX_DTYPE if reverse else F32),
                   jax.ShapeDtypeStruct((nseq, GW, hk), F32)],
        grid=(nseq, nb),
        in_specs=in_specs,
        out_specs=[tile(0), pl.BlockSpec((None, GW, hk), lambda b, j: (b, 0, 0))],
        scratch_shapes=[pltpu.VMEM((GW, hk), F32)],
        compiler_params=_params(("arbitrary", "arbitrary")),
        name="gla_bwd" if reverse else "gla_fwd",
    )(*args)


def _state_to_kernel_layout(s):
    eye = jnp.eye(B_HEADS, dtype=s.dtype)
    t = jnp.einsum("bxhde,hg->bxhegd", s, eye)
    return t.reshape(s.shape[0], 2, B_HEADS * B_DV, B_HEADS * B_DK)


def _state_from_kernel_layout(st):
    t = st.reshape(st.shape[0], B_HEADS, B_DV, B_HEADS, B_DK)
    return jnp.stack([t[:, h, :, h, :] for h in range(B_HEADS)], axis=1).transpose(0, 1, 3, 2)


def _outproj_kernel(a_ref, b_ref, c_ref, d_ref, x_ref, mod_ref, w_ref, o_ref):
    mix = jnp.concatenate([a_ref[...], b_ref[...], c_ref[...], d_ref[...]], axis=-1)
    gate = mod_ref[:, 2 * D_MODEL:3 * D_MODEL]
    o_ref[...] = x_ref[...] + gate * _dot(mix, w_ref[...])


def _outproj(o_a, o_b, o_c, o_d, x, mods3, w_out, layer, latent, seq_len):
    t_tokens = x.shape[0]
    tm = 512
    row = _mod_row(latent, tm, seq_len)
    part = pl.BlockSpec((tm, GW), lambda t: (t, 0))
    return pl.pallas_call(
        _outproj_kernel,
        out_shape=jax.ShapeDtypeStruct((t_tokens, D_MODEL), F32),
        grid=(t_tokens // tm,),
        in_specs=[part, part, part, part,
                  pl.BlockSpec((tm, D_MODEL), lambda t: (t, 0)),
                  pl.BlockSpec((None, 1, 6 * D_MODEL), lambda t: (row(t), 0, 0)),
                  pl.BlockSpec((None, 4 * GW, D_MODEL), lambda t: (layer, 0, 0))],
        out_specs=pl.BlockSpec((tm, D_MODEL), lambda t: (t, 0)),
        compiler_params=_params(("arbitrary",)),
        name="outproj",
    )(o_a, o_b, o_c, o_d, x, mods3, w_out)


MOE_TM = 512
ROW = 8
LANES = D_MODEL // ROW
R_E1, R_E2, R_W1, R_W2, R_RANK1, R_RANK2 = 0, 1, 2, 3, 4, 5


def _router_kernel(x_ref, g_ref, mod_ref, r_ref, rb_ref, h_ref, route_ref, cnt_ref, carry):
    d = D_MODEL

    @pl.when(pl.program_id(0) == 0)
    def _():
        carry[...] = jnp.zeros(carry.shape, F32)

    h = _modulate(x_ref[...], g_ref[...], mod_ref[:, 3 * d:4 * d], mod_ref[:, 4 * d:5 * d])
    tm = h.shape[0]
    for s in range(ROW):
        h_ref[pl.ds(s, tm, stride=ROW), :] = h[:, s * LANES:(s + 1) * LANES]
    r = r_ref[...]
    h_hi = h.astype(BF16)
    h_lo = (h - h_hi.astype(F32)).astype(BF16)
    r_hi = r.astype(BF16)
    r_lo = (r - r_hi.astype(F32)).astype(BF16)
    dot = lambda a, b: jnp.dot(a, b, preferred_element_type=F32)
    logits = dot(h_hi, r_hi) + dot(h_lo, r_hi) + dot(h_hi, r_lo) + rb_ref[...]
    lane = lax.broadcasted_iota(jnp.int32, logits.shape, 1).astype(F32)
    big = float(logits.shape[1])
    logits = jnp.where(lane < N_EXPERTS, logits, -jnp.inf)
    v1 = jnp.max(logits, axis=-1, keepdims=True)
    i1 = jnp.min(jnp.where(logits == v1, lane, big), axis=-1, keepdims=True)
    rest = jnp.where(lane == i1, -jnp.inf, logits)
    v2 = jnp.max(rest, axis=-1, keepdims=True)
    i2 = jnp.min(jnp.where(rest == v2, lane, big), axis=-1, keepdims=True)
    e2 = jnp.exp(v2 - v1)
    inv = 1.0 / (1.0 + e2)
    sel = jnp.where((lane == i1) | (lane == i2), 1.0, 0.0)
    r_i = lax.broadcasted_iota(jnp.int32, (tm, tm), 0)
    c_i = lax.broadcasted_iota(jnp.int32, (tm, tm), 1)
    before = jnp.where(c_i < r_i, 1.0, 0.0).astype(BF16)
    rank = jnp.dot(before, sel.astype(BF16), preferred_element_type=F32) + carry[...]
    count = carry[...] + jnp.sum(sel, axis=0, keepdims=True)
    carry[...] = count
    cnt_ref[...] = jnp.broadcast_to(count, cnt_ref.shape)
    rank1 = jnp.sum(jnp.where(lane == i1, rank, 0.0), axis=-1, keepdims=True)
    rank2 = jnp.sum(jnp.where(lane == i2, rank, 0.0), axis=-1, keepdims=True)
    rec = jnp.zeros(logits.shape, F32)
    for pos, val in ((R_E1, i1), (R_E2, i2), (R_W1, inv), (R_W2, e2 * inv), (R_RANK1, rank1), (R_RANK2, rank2)):
        rec = jnp.where(lane == pos, val, rec)
    route_ref[...] = rec


def _router(x, norm_g3, mods3, router, router_b, layer, j, latent, seq_len):
    t_tokens = x.shape[0]
    tm = 512
    row = _mod_row(latent, tm, seq_len)
    lanes = router.shape[-1]
    return pl.pallas_call(
        _router_kernel,
        out_shape=[jax.ShapeDtypeStruct((t_tokens * ROW, LANES), F32),
                   jax.ShapeDtypeStruct((t_tokens, lanes), F32),
                   jax.ShapeDtypeStruct((8, lanes), F32)],
        grid=(t_tokens // tm,),
        in_specs=[pl.BlockSpec((tm, D_MODEL), lambda t: (t, 0)),
                  pl.BlockSpec((None, 1, D_MODEL), lambda t: (layer, 0, 0)),
                  pl.BlockSpec((None, 1, 6 * D_MODEL), lambda t: (row(t), 0, 0)),
                  pl.BlockSpec((None, D_MODEL, lanes), lambda t: (j, 0, 0)),
                  pl.BlockSpec((None, 1, lanes), lambda t: (j, 0, 0))],
        out_specs=[pl.BlockSpec((tm * ROW, LANES), lambda t: (t, 0)),
                   pl.BlockSpec((tm, lanes), lambda t: (t, 0)),
                   pl.BlockSpec((8, lanes), lambda t: (0, 0))],
        scratch_shapes=[pltpu.VMEM((1, lanes), F32)],
        compiler_params=_params(("arbitrary",)),
        name="router",
    )(x, norm_g3, mods3, router, router_b)


def _row_copy(src, src_row, dst, dst_row, sem):
    s0 = pl.multiple_of(src_row * ROW, ROW)
    d0 = pl.multiple_of(dst_row * ROW, ROW)
    return pltpu.make_async_copy(src.at[pl.ds(s0, ROW), :], dst.at[pl.ds(d0, ROW), :], sem)


def _dispatch_kernel(p0_ref, p1_ref, h_ref, xs_in_ref, xs_ref, sem):
    del xs_in_ref
    tm = h_ref.shape[0] // ROW
    base = pl.program_id(0) * tm

    def issue(r, carry):
        _row_copy(h_ref, r, xs_ref, p0_ref[base + r], sem.at[0]).start()
        _row_copy(h_ref, r, xs_ref, p1_ref[base + r], sem.at[1]).start()
        return carry

    def wait(r, carry):
        _row_copy(h_ref, r, xs_ref, p0_ref[base + r], sem.at[0]).wait()
        _row_copy(h_ref, r, xs_ref, p1_ref[base + r], sem.at[1]).wait()
        return carry

    lax.fori_loop(0, tm, issue, 0, unroll=8)
    lax.fori_loop(0, tm, wait, 0, unroll=8)


def _dispatch(pos0, pos1, h_rows, n_slots):
    t_tokens = pos0.shape[0]
    tm = 512
    xs0 = jnp.zeros((n_slots * ROW, LANES), F32)
    return pl.pallas_call(
        _dispatch_kernel,
        out_shape=jax.ShapeDtypeStruct(xs0.shape, F32),
        grid_spec=pltpu.PrefetchScalarGridSpec(
            num_scalar_prefetch=2,
            grid=(t_tokens // tm,),
            in_specs=[pl.BlockSpec((tm * ROW, LANES), lambda t, p0, p1: (t, 0)),
                      pl.BlockSpec(memory_space=pl.ANY)],
            out_specs=pl.BlockSpec(memory_space=pl.ANY),
            scratch_shapes=[pltpu.SemaphoreType.DMA((2,))]),
        input_output_aliases={3: 0},
        compiler_params=_params(("arbitrary",)),
        name="moe_dispatch",
    )(pos0, pos1, h_rows, xs0)


def _moe_ffn_kernel(te_ref, nu_ref, xs_ref, w1_ref, w3_ref, w2_ref, y_ref, xb, acc, *, n_f):
    del te_ref
    i = pl.program_id(0)
    f = pl.program_id(1)
    tm = MOE_TM
    used = i < nu_ref[0]

    @pl.when(used & (f == 0))
    def _():
        for s in range(ROW):
            xb[:, s * LANES:(s + 1) * LANES] = xs_ref[pl.ds(s, tm, stride=ROW), :].astype(BF16)
        acc[...] = jnp.zeros(acc.shape, F32)

    @pl.when(used)
    def _():
        x = xb[...]
        a = jnp.dot(x, w1_ref[...].astype(BF16), preferred_element_type=F32)
        b = jnp.dot(x, w3_ref[...].astype(BF16), preferred_element_type=F32)
        acc[...] += _dot(_silu(a) * b, w2_ref[...])

    @pl.when(used & (f == n_f - 1))
    def _():
        for s in range(ROW):
            y_ref[pl.ds(s, tm, stride=ROW), :] = acc[:, s * LANES:(s + 1) * LANES]

    @pl.when(jnp.logical_not(used) & (f == n_f - 1))
    def _():
        y_ref[...] = jnp.zeros(y_ref.shape, F32)


def _moe_ffn(tile_expert, n_used, xs, w1, w3, w2, j):
    tm = MOE_TM
    n_tiles = xs.shape[0] // (tm * ROW)
    d_ff = w1.shape[-1]
    tf = 896
    n_f = d_ff // tf
    f_of = lambda i, f, nu: jnp.where(i < nu[0], f, n_f - 1)
    return pl.pallas_call(
        functools.partial(_moe_ffn_kernel, n_f=n_f),
        out_shape=jax.ShapeDtypeStruct(xs.shape, F32),
        grid_spec=pltpu.PrefetchScalarGridSpec(
            num_scalar_prefetch=2,
            grid=(n_tiles, n_f),
            in_specs=[pl.BlockSpec((tm * ROW, LANES), lambda i, f, te, nu: (jnp.minimum(i, nu[0] - 1), 0)),
                      pl.BlockSpec((None, None, D_MODEL, tf), lambda i, f, te, nu: (j, te[i], 0, f_of(i, f, nu))),
                      pl.BlockSpec((None, None, D_MODEL, tf), lambda i, f, te, nu: (j, te[i], 0, f_of(i, f, nu))),
                      pl.BlockSpec((None, None, tf, D_MODEL), lambda i, f, te, nu: (j, te[i], f_of(i, f, nu), 0))],
            out_specs=pl.BlockSpec((tm * ROW, LANES), lambda i, f, te, nu: (i, 0)),
            scratch_shapes=[pltpu.VMEM((tm, D_MODEL), BF16), pltpu.VMEM((tm, D_MODEL), F32)]),
        compiler_params=_params(("arbitrary", "arbitrary")),
        name="moe_ffn",
    )(tile_expert, n_used, xs, w1, w3, w2)


def _combine_kernel(p0_ref, p1_ref, x_ref, mod_ref, route_ref, y_ref, o_ref, buf0, buf1, sem):
    tm = x_ref.shape[0]
    base = pl.program_id(0) * tm

    def issue(r, carry):
        _row_copy(y_ref, p0_ref[base + r], buf0, r, sem.at[0]).start()
        _row_copy(y_ref, p1_ref[base + r], buf1, r, sem.at[1]).start()
        return carry

    def wait(r, carry):
        _row_copy(y_ref, p0_ref[base + r], buf0, r, sem.at[0]).wait()
        _row_copy(y_ref, p1_ref[base + r], buf1, r, sem.at[1]).wait()
        return carry

    lax.fori_loop(0, tm, issue, 0, unroll=8)
    lax.fori_loop(0, tm, wait, 0, unroll=8)
    route = route_ref[...]
    w_a = route[:, R_W1:R_W1 + 1]
    w_b = route[:, R_W2:R_W2 + 1]
    for s in range(ROW):
        cols = slice(s * LANES, (s + 1) * LANES)
        y = w_a * buf0[pl.ds(s, tm, stride=ROW), :] + w_b * buf1[pl.ds(s, tm, stride=ROW), :]
        o_ref[:, cols] = x_ref[:, cols] + mod_ref[:, 5 * D_MODEL + s * LANES:5 * D_MODEL + (s + 1) * LANES] * y


def _combine(pos0, pos1, x, mods3, route, y, latent, seq_len):
    t_tokens = x.shape[0]
    tm = 512
    row = _mod_row(latent, tm, seq_len)
    return pl.pallas_call(
        _combine_kernel,
        out_shape=jax.ShapeDtypeStruct((t_tokens, D_MODEL), F32),
        grid_spec=pltpu.PrefetchScalarGridSpec(
            num_scalar_prefetch=2,
            grid=(t_tokens // tm,),
            in_specs=[pl.BlockSpec((tm, D_MODEL), lambda t, p0, p1: (t, 0)),
                      pl.BlockSpec((None, 1, 6 * D_MODEL), lambda t, p0, p1: (row(t), 0, 0)),
                      pl.BlockSpec((tm, route.shape[1]), lambda t, p0, p1: (t, 0)),
                      pl.BlockSpec(memory_space=pl.ANY)],
            out_specs=pl.BlockSpec((tm, D_MODEL), lambda t, p0, p1: (t, 0)),
            scratch_shapes=[pltpu.VMEM((tm * ROW, LANES), F32), pltpu.VMEM((tm * ROW, LANES), F32),
                            pltpu.SemaphoreType.DMA((2,))]),
        compiler_params=_params(("arbitrary",)),
        name="moe_combine",
    )(pos0, pos1, x, mods3, route, y)


def _moe(x, norm_g3, mods3, router, router_b, w1, w3, w2, layer, j, latent, seq_len):
    t_tokens = x.shape[0]
    tm = MOE_TM
    n_tiles = 2 * t_tokens // tm + N_EXPERTS
    h_rows, route, cnt = _router(x, norm_g3, mods3, router, router_b, layer, j, latent, seq_len)
    expert = route[:, R_E1:R_E2 + 1].astype(jnp.int32)
    rank = route[:, R_RANK1:R_RANK2 + 1].astype(jnp.int32)
    tiles = (cnt[0, :N_EXPERTS].astype(jnp.int32) + tm - 1) // tm
    ends = jnp.cumsum(tiles)
    starts = ends - tiles
    ids = jnp.arange(N_EXPERTS, dtype=jnp.int32)
    start_of = jnp.sum(jnp.where(expert[:, :, None] == ids, starts * tm, 0), axis=-1)
    pos = start_of + rank
    n_used = ends[-1:]
    tile_ids = jnp.arange(n_tiles, dtype=jnp.int32)
    tile_expert = jnp.sum((tile_ids[:, None] >= ends[None, :]).astype(jnp.int32), axis=1)
    last_expert = jnp.max(jnp.where(tiles > 0, ids, 0))
    tile_expert = jnp.where(tile_ids < n_used, tile_expert, last_expert)
    pos0, pos1 = pos[:, 0], pos[:, 1]
    xs = _dispatch(pos0, pos1, h_rows, n_tiles * tm)
    y = _moe_ffn(tile_expert, n_used, xs, w1, w3, w2, j)
    return _combine(pos0, pos1, x, mods3, route, y, latent, seq_len)


def _ffn_kernel(x_ref, g_ref, mod_ref, w1_ref, w3_ref, w2_ref, o_ref, h_s, acc, *, n_f):
    d = D_MODEL
    f = pl.program_id(1)

    @pl.when(f == 0)
    def _():
        h = _modulate(x_ref[...], g_ref[...], mod_ref[:, 3 * d:4 * d], mod_ref[:, 4 * d:5 * d])
        h_s[...] = h.astype(BF16)
        acc[...] = jnp.zeros(acc.shape, F32)

    h = h_s[...]
    a = jnp.dot(h, w1_ref[...], preferred_element_type=F32)
    b = jnp.dot(h, w3_ref[...], preferred_element_type=F32)
    acc[...] += _dot(_silu(a) * b, w2_ref[...])

    @pl.when(f == n_f - 1)
    def _():
        o_ref[...] = x_ref[...] + mod_ref[:, 5 * d:6 * d] * acc[...]


def _ffn(x, norm_g3, mods3, w1, w3, w2, layer, j, latent, seq_len):
    t_tokens = x.shape[0]
    d_ff = w1.shape[-1]
    tm = 512
    tf = d_ff // 2
    n_f = d_ff // tf
    row = _mod_row(latent, tm, seq_len)
    return pl.pallas_call(
        functools.partial(_ffn_kernel, n_f=n_f),
        out_shape=jax.ShapeDtypeStruct((t_tokens, D_MODEL), F32),
        grid=(t_tokens // tm, n_f),
        in_specs=[pl.BlockSpec((tm, D_MODEL), lambda t, f: (t, 0)),
                  pl.BlockSpec((None, 1, D_MODEL), lambda t, f: (layer, 0, 0)),
                  pl.BlockSpec((None, 1, 6 * D_MODEL), lambda t, f: (row(t), 0, 0)),
                  pl.BlockSpec((None, D_MODEL, tf), lambda t, f: (j, 0, f)),
                  pl.BlockSpec((None, D_MODEL, tf), lambda t, f: (j, 0, f)),
                  pl.BlockSpec((None, tf, D_MODEL), lambda t, f: (j, f, 0))],
        out_specs=pl.BlockSpec((tm, D_MODEL), lambda t, f: (t, 0)),
        scratch_shapes=[pltpu.VMEM((tm, D_MODEL), BF16), pltpu.VMEM((tm, D_MODEL), F32)],
        compiler_params=_params(("arbitrary", "arbitrary")),
        name="ffn",
    )(x, norm_g3, mods3, w1, w3, w2)


def _rope_tables():
    half = A_QK // 2
    nf = half // 2
    t = np.arange(DEC_SEQ)
    pos = np.stack([(t // GRID_W), (t % GRID_W)], axis=1).astype(np.float32)
    inv = jnp.asarray(ROPE_BASE, F32) ** (-jnp.arange(nf, dtype=F32) / nf)
    lane = np.arange(GW) % A_QK
    which = (lane >= half).astype(np.int32)
    freq = lane % nf
    ang = jnp.asarray(pos)[:, which] * inv[freq][None, :]
    return jnp.cos(ang), jnp.sin(ang)


def _tile_vec(v, reps):
    return jnp.tile(v, (1, reps))[:, None, :]


def _prepare_params(w_in, w_out, a_qnorm, a_knorm, a_lam, a_subln, b_gate, b_gate_bias, b_onorm,
                    c_dw, c_dw_b, c_ln_g, c_ln_b, d_qnorm, d_knorm):
    lr0 = 1536
    lr1 = lr0 + 2 * B_GATE_RANK
    pad = PROJ_W - w_in.shape[-1]
    w_in_p = jnp.concatenate([w_in[:, :, :lr0], w_in[:, :, lr1:], w_in[:, :, lr0:lr1],
                              jnp.zeros(w_in.shape[:2] + (pad,), w_in.dtype)], axis=-1).astype(BF16)
    hk = B_HEADS * B_DK
    wg = jnp.zeros((DEPTH, 2, GW, hk), F32)
    wg = wg.at[:, 0, 0:B_GATE_RANK].set(b_gate[:, 0])
    wg = wg.at[:, 1, B_GATE_RANK:2 * B_GATE_RANK].set(b_gate[:, 1])
    return dict(
        w_in=w_in_p, w_out=w_out.astype(BF16),
        a_qnorm=_tile_vec(a_qnorm, GW // A_QK), a_knorm=_tile_vec(a_knorm, GW // A_QK),
        a_lam=a_lam, a_subln=_tile_vec(a_subln, GW // A_V),
        b_gate=wg.astype(BF16), b_gate_bias=b_gate_bias[:, :, None, :], b_onorm=_tile_vec(b_onorm, GW // B_DV),
        c_dw=c_dw, c_dw_b=c_dw_b[:, None, :], c_ln_g=c_ln_g[:, None, :], c_ln_b=c_ln_b[:, None, :],
        d_qnorm=_tile_vec(d_qnorm, GW // D_HEAD), d_knorm=_tile_vec(d_knorm, GW // D_HEAD),
    )


def _lambda_init(i):
    return 0.8 - 0.6 * math.exp(-0.3 * i)


def kernel(x_prompt, x_sample, cache_a_k, cache_a_v, state_b, cache_d_k, cache_d_v, c, c_ctx,
           norm1_g, norm2_g, w_ada, b_ada, w_in, w_out, a_qnorm, a_knorm, a_lam, a_subln,
           b_gate, b_gate_bias, b_onorm, c_dw, c_dw_b, c_ln_g, c_ln_b, d_qnorm, d_knorm, d_rpb,
           ffn_w1, ffn_w3, ffn_w2, moe_router, moe_router_b, moe_w1, moe_w3, moe_w2):
    lp = _prepare_params(w_in, w_out, a_qnorm, a_knorm, a_lam, a_subln, b_gate, b_gate_bias, b_onorm,
                         c_dw, c_dw_b, c_ln_g, c_ln_b, d_qnorm, d_knorm)
    norm1 = norm1_g[:, None, :]
    norm2 = norm2_g[:, None, :]
    b_ada3 = b_ada[:, None, :]
    cond8 = jnp.concatenate([c_ctx[None, :], c, jnp.zeros((8 - 1 - DEC_BATCH, D_MODEL), F32)], axis=0)
    rope = _rope_tables()
    ffn_w = [w.astype(BF16) for w in (ffn_w1, ffn_w3, ffn_w2)]
    lanes = 128
    router_p = jnp.pad(moe_router, ((0, 0), (0, 0), (0, lanes - N_EXPERTS)))
    router_b_p = jnp.pad(moe_router_b, ((0, 0), (0, lanes - N_EXPERTS)))[:, None, :]
    ck_a = cache_a_k.reshape(DEC_BATCH, DEPTH, PAST_LEN, GW)
    cv_a = cache_a_v.reshape(DEC_BATCH, DEPTH, PAST_LEN, GW)
    ck_d = cache_d_k.reshape(DEC_BATCH, DEPTH, PAST_LEN, GW)
    cv_d = cache_d_v.reshape(DEC_BATCH, DEPTH, PAST_LEN, GW)
    s0_lat = _state_to_kernel_layout(state_b.transpose(1, 0, 2, 3, 4, 5).reshape(
        DEPTH * DEC_BATCH, 2, B_HEADS, B_DK, B_DV)).reshape(DEPTH, DEC_BATCH, 2, GW, B_HEADS * B_DK)

    def channel_mixer(x, mods3, i, latent, seq_len):
        j = i // 2
        if i % 2 == 0:
            return _ffn(x, norm2, mods3, *ffn_w, i, j, latent, seq_len)
        return _moe(x, norm2, mods3, router_p, router_b_p, moe_w1, moe_w3, moe_w2, i, j, latent, seq_len)

    xc = x_prompt.reshape(BATCH * SEQ, D_MODEL)
    xl = x_sample.reshape(DEC_BATCH * DEC_SEQ, D_MODEL)
    new_ak, new_av, new_sb, new_dk, new_dv = [], [], [], [], []
    for i in range(DEPTH):
        lam_init = _lambda_init(i)
        mods3 = _adaln(cond8, w_ada, b_ada3, i)[:, None, :]

        proj = _inproj(xc, norm1, mods3, lp["w_in"], i, False, SEQ)
        o_a, ak = _attn_a(proj, lp, i, False, BATCH, SEQ, lam_init)
        o_f, s_f = _gla(proj, lp, i, BATCH, SEQ, False)
        o_b, s_b = _gla(proj, lp, i, BATCH, SEQ, True, o_fwd=o_f)
        o_c = _conv(proj, lp, i, BATCH, SEQ)
        o_d, dk = _attn_d_ctx(proj, lp, i, BATCH, SEQ)
        xc = _outproj(o_a, o_b, o_c, o_d, xc, mods3, lp["w_out"], i, False, SEQ)
        xc = channel_mixer(xc, mods3, i, False, SEQ)
        new_ak.append(ak.reshape(BATCH, SEQ, A_HEADS, 2 * A_QK))
        new_av.append(proj[:, BLK_AV * GW:(BLK_AV + 1) * GW].reshape(BATCH, SEQ, A_HEADS, A_V))
        new_sb.append(jnp.stack([_state_from_kernel_layout(s_f), _state_from_kernel_layout(s_b)], axis=1))
        new_dk.append(dk.reshape(BATCH, SEQ, D_HEADS, D_HEAD))
        new_dv.append(proj[:, BLK_DV * GW:(BLK_DV + 1) * GW].reshape(BATCH, SEQ, D_HEADS, D_HEAD))

        proj = _inproj(xl, norm1, mods3, lp["w_in"], i, True, DEC_SEQ)
        o_a, = _attn_a(proj, lp, i, True, DEC_BATCH, DEC_SEQ, lam_init, ck_a, cv_a, rope)
        o_f, _ = _gla(proj, lp, i, DEC_BATCH, DEC_SEQ, False, s0=s0_lat[i])
        o_b, _ = _gla(proj, lp, i, DEC_BATCH, DEC_SEQ, True, s0=s0_lat[i], o_fwd=o_f)
        o_c = _conv(proj, lp, i, DEC_BATCH, DEC_SEQ)
        o_d = _attn_d_lat(proj, lp, i, ck_d, cv_d, _na_bias_table(d_rpb[i]))
        xl = _outproj(o_a, o_b, o_c, o_d, xl, mods3, lp["w_out"], i, True, DEC_SEQ)
        xl = channel_mixer(xl, mods3, i, True, DEC_SEQ)

    return (xc.reshape(BATCH, SEQ, D_MODEL), xl.reshape(DEC_BATCH, DEC_SEQ, D_MODEL),
            jnp.stack(new_ak, axis=1), jnp.stack(new_av, axis=1), jnp.stack(new_sb, axis=1),
            jnp.stack(new_dk, axis=1), jnp.stack(new_dv, axis=1))
```

```python
import functools
import math

import numpy as np
import jax
import jax.numpy as jnp
from jax import lax
from jax.experimental import pallas as pl
from jax.experimental.pallas import tpu as pltpu

F32 = jnp.float32
BF16 = jnp.bfloat16
MIX_DTYPE = BF16

D_MODEL = 1024
BATCH = 16
SEQ = 256
DEPTH = 2
DEC_BATCH = 4
DEC_SEQ = 2048
PAST_LEN = 512
GRID_W = 64
A_HEADS = 4
A_QK = 32
A_V = 64
B_HEADS = 4
B_DK = 32
B_DV = 64
B_GATE_RANK = 16
B_TAU = 16.0
C_CH = 256
C_KSIZE = 31
D_HEADS = 4
D_HEAD = 64
NA_ROWS = 8
NA_COLS = 16
NA_QROWS = 4
NA_KROWS = 12
ROPE_BASE = 10000.0
D_FF = 2816
N_EXPERTS = 8
D_FF_EXPERT = 3584
EPS = 1e-6
NEG_INF = -1e30
LOG2_E = math.log2(math.e)

GW = 256
N_PROJ_BLOCKS = 12
PROJ_W = GW * N_PROJ_BLOCKS
BLK_AQ, BLK_AK, BLK_AV, BLK_BQK, BLK_BV, BLK_BG = 0, 1, 2, 3, 4, 5
BLK_C = 3
BLK_DQ, BLK_DK, BLK_DV, BLK_LR = 8, 9, 10, 11

GLA_SUB = 16
GLA_ROWS = 256
VMEM_LIMIT = 56 * 1024 * 1024


def _params(sem, vmem=VMEM_LIMIT):
    return pltpu.CompilerParams(dimension_semantics=sem, vmem_limit_bytes=vmem)


def _dot(a, b):
    return jnp.dot(a.astype(BF16), b.astype(BF16), preferred_element_type=F32)


def _dot_nt(a, b):
    return lax.dot_general(a.astype(BF16), b.astype(BF16), (((1,), (1,)), ((), ())),
                           preferred_element_type=F32)


def _dot_exact_rhs(x, m):
    hi = x.astype(BF16)
    lo = (x - hi.astype(F32)).astype(BF16)
    return (jnp.dot(hi, m, preferred_element_type=F32) + jnp.dot(lo, m, preferred_element_type=F32))


def _sigmoid(x):
    return 1.0 / (1.0 + jnp.exp(-x))


def _silu(x):
    return x * _sigmoid(x)


def _seg_matrix(n, seg):
    r = lax.broadcasted_iota(jnp.int32, (n, n), 0)
    c = lax.broadcasted_iota(jnp.int32, (n, n), 1)
    return jnp.where((r ^ c) < seg, 1.0, 0.0).astype(BF16)


def _seg_rmsnorm(x, seg, w):
    ms = _dot_exact_rhs(x * x, _seg_matrix(x.shape[-1], seg)) * (1.0 / seg)
    return x * lax.rsqrt(ms + EPS) * w


def _lane_mask(shape, lo, width):
    lane = lax.broadcasted_iota(jnp.int32, shape, len(shape) - 1)
    return (lane >= lo) & (lane < lo + width)


def _stack_heads(q, heads, width):
    return jnp.concatenate([jnp.where(_lane_mask(q.shape, h * width, width), q, 0.0) for h in range(heads)],
                           axis=0)


def _unstack_heads(o_all, heads, width):
    n = o_all.shape[0] // heads
    o = jnp.zeros((n, o_all.shape[1]), F32)
    for h in range(heads):
        blk = o_all[h * n:(h + 1) * n]
        o = jnp.where(_lane_mask(blk.shape, h * width, width), blk, o)
    return o


def _modulate(x, g, shift, scale):
    y = x * lax.rsqrt(jnp.mean(x * x, axis=-1, keepdims=True) + EPS)
    return y * g * (1.0 + scale) + shift


def _rope(x, cos, sin):
    lane = lax.broadcasted_iota(jnp.int32, x.shape, 1)
    w = x.shape[1]
    rot = jnp.where((lane & 15) < 8, -pltpu.roll(x, w - 8, 1), pltpu.roll(x, 8, 1))
    return x * cos + rot * sin


def _adaln_kernel(c_ref, w_ref, b_ref, o_ref):
    o_ref[...] = _dot(_silu(c_ref[...]), w_ref[...]) + b_ref[...]


def _adaln(cond8, w_ada, b_ada3, layer):
    tn = 1536
    n = 6 * D_MODEL
    return pl.pallas_call(
        _adaln_kernel,
        out_shape=jax.ShapeDtypeStruct((8, n), F32),
        grid=(n // tn,),
        in_specs=[pl.BlockSpec((8, D_MODEL), lambda j: (0, 0)),
                  pl.BlockSpec((None, D_MODEL, tn), lambda j: (layer, 0, j)),
                  pl.BlockSpec((None, 1, tn), lambda j: (layer, 0, j))],
        out_specs=pl.BlockSpec((8, tn), lambda j: (0, j)),
        compiler_params=_params(("arbitrary",)),
        name="adaln",
    )(cond8, w_ada, b_ada3)


def _mod_row(latent, tm, seq_len):
    if latent:
        return lambda t: 1 + (t * tm) // seq_len
    return lambda t: 0


def _inproj_kernel(x_ref, g_ref, mod_ref, w_ref, o_ref):
    d = D_MODEL
    h = _modulate(x_ref[...], g_ref[...], mod_ref[:, 0:d], mod_ref[:, d:2 * d])
    o_ref[...] = _dot(h, w_ref[...])


def _inproj(x, norm_g3, mods3, w_in, layer, latent, seq_len):
    t_tokens = x.shape[0]
    tm = 512
    row = _mod_row(latent, tm, seq_len)
    return pl.pallas_call(
        _inproj_kernel,
        out_shape=jax.ShapeDtypeStruct((t_tokens, PROJ_W), F32),
        grid=(t_tokens // tm,),
        in_specs=[pl.BlockSpec((tm, D_MODEL), lambda t: (t, 0)),
                  pl.BlockSpec((None, 1, D_MODEL), lambda t: (layer, 0, 0)),
                  pl.BlockSpec((None, 1, 6 * D_MODEL), lambda t: (row(t), 0, 0)),
                  pl.BlockSpec((None, D_MODEL, PROJ_W), lambda t: (layer, 0, 0))],
        out_specs=pl.BlockSpec((tm, PROJ_W), lambda t: (t, 0)),
        compiler_params=_params(("arbitrary",)),
        name="inproj",
    )(x, norm_g3, mods3, w_in)


def _diff_lambda(lam_ref, lam_init):
    lam = lam_ref[...]
    s1 = jnp.sum(lam[0:1] * lam[1:2], axis=1, keepdims=True)
    s2 = jnp.sum(lam[2:3] * lam[3:4], axis=1, keepdims=True)
    return jnp.exp(s1) - jnp.exp(s2) + lam_init


def _values_and_ones(v):
    one = jnp.where(lax.broadcasted_iota(jnp.int32, (v.shape[0], A_V), 1) == 0, 1.0, 0.0)
    parts = []
    for h in range(A_HEADS):
        parts += [v[:, h * A_V:(h + 1) * A_V], one]
    return jnp.concatenate(parts, axis=1).astype(BF16)


def _attn_a_kernel(*refs, n_own, n_ctx, tq, latent, lam_init):
    if latent:
        (q_ref, k_ref, v_ref, ck_ref, cv_ref, cos_ref, sin_ref, qn_ref, kn_ref, lam_ref, sub_ref,
         o_ref, ks, vs) = refs
    else:
        (q_ref, k_ref, v_ref, qn_ref, kn_ref, lam_ref, sub_ref, o_ref, ko_ref, ks, vs) = refs
    t = pl.program_id(1)

    @pl.when(t == 0)
    def _():
        kn = _seg_rmsnorm(k_ref[...], A_QK, kn_ref[...])
        if latent:
            kn = _rope(kn, cos_ref[...], sin_ref[...])
            ks[n_own:n_own + n_ctx, :] = ck_ref[...].astype(BF16)
            vs[n_own:n_own + n_ctx, :] = _values_and_ones(cv_ref[...])
        else:
            ko_ref[...] = kn
        ks[0:n_own, :] = kn.astype(BF16)
        vs[0:n_own, :] = _values_and_ones(v_ref[...])

    qn = _seg_rmsnorm(q_ref[...], A_QK, qn_ref[...])
    if latent:
        r0 = pl.multiple_of(t * tq, tq)
        qn = _rope(qn, cos_ref[pl.ds(r0, tq), :], sin_ref[pl.ds(r0, tq), :])
    qn = qn * (A_QK ** -0.5 * LOG2_E)
    lam = _diff_lambda(lam_ref, lam_init)
    k_all = ks[...]
    heads = []
    for h in range(A_HEADS):
        v_h = vs[:, h * 2 * A_V:(h + 1) * 2 * A_V]
        qm = jnp.concatenate([jnp.where(_lane_mask(qn.shape, (2 * h + m) * A_QK, A_QK), qn, 0.0)
                              for m in range(2)], axis=0)
        s = _dot_nt(qm, k_all)
        e = jnp.exp2(s - jnp.max(s, axis=-1, keepdims=True))
        acc = _dot(e, v_h)
        acc0, acc1 = acc[:tq], acc[tq:]
        inv0 = 1.0 / acc0[:, A_V:A_V + 1]
        inv1 = lam / acc1[:, A_V:A_V + 1]
        heads.append((acc0 * inv0 - acc1 * inv1)[:, :A_V])
    o = jnp.concatenate(heads, axis=1)
    o_ref[...] = (_seg_rmsnorm(o, A_V, sub_ref[...]) * (1.0 - lam_init)).astype(o_ref.dtype)


def _attn_a(proj, lp, layer, latent, nseq, seq_len, lam_init, cache_k=None, cache_v=None, rope=None):
    tq = min(512, seq_len)
    nt = seq_len // tq
    n_ctx = PAST_LEN if latent else 0
    kern = functools.partial(_attn_a_kernel, n_own=seq_len, n_ctx=n_ctx, tq=tq, latent=latent,
                             lam_init=lam_init)
    vec = lambda name: pl.BlockSpec((None, 1, GW), lambda b, t: (layer, 0, 0))
    in_specs = [pl.BlockSpec((tq, GW), lambda b, t: (b * nt + t, BLK_AQ)),
                pl.BlockSpec((seq_len, GW), lambda b, t: (b, BLK_AK)),
                pl.BlockSpec((seq_len, GW), lambda b, t: (b, BLK_AV))]
    args = [proj, proj, proj]
    if latent:
        in_specs += [pl.BlockSpec((None, None, PAST_LEN, GW), lambda b, t: (b, layer, 0, 0)),
                     pl.BlockSpec((None, None, PAST_LEN, GW), lambda b, t: (b, layer, 0, 0)),
                     pl.BlockSpec((seq_len, GW), lambda b, t: (0, 0)),
                     pl.BlockSpec((seq_len, GW), lambda b, t: (0, 0))]
        args += [cache_k, cache_v, rope[0], rope[1]]
    in_specs += [vec("q"), vec("k"),
                 pl.BlockSpec((None, 4, A_QK), lambda b, t: (layer, 0, 0)),
                 vec("s")]
    args += [lp["a_qnorm"], lp["a_knorm"], lp["a_lam"], lp["a_subln"]]
    out_shape = [jax.ShapeDtypeStruct((nseq * seq_len, GW), MIX_DTYPE)]
    out_specs = [pl.BlockSpec((tq, GW), lambda b, t: (b * nt + t, 0))]
    if not latent:
        out_shape.append(jax.ShapeDtypeStruct((nseq * seq_len, GW), F32))
        out_specs.append(pl.BlockSpec((seq_len, GW), lambda b, t: (b, 0)))
    return pl.pallas_call(
        kern,
        out_shape=out_shape,
        grid=(nseq, nt),
        in_specs=in_specs,
        out_specs=out_specs,
        scratch_shapes=[pltpu.VMEM((seq_len + n_ctx, GW), BF16), pltpu.VMEM((seq_len + n_ctx, 2 * GW), BF16)],
        compiler_params=_params(("arbitrary", "arbitrary")),
        name="attn_a_lat" if latent else "attn_a_ctx",
    )(*args)


def _attn_d_ctx_kernel(q_ref, k_ref, v_ref, qn_ref, kn_ref, o_ref, ko_ref):
    kn = _seg_rmsnorm(k_ref[...], D_HEAD, kn_ref[...])
    ko_ref[...] = kn
    qn = _seg_rmsnorm(q_ref[...], D_HEAD, qn_ref[...]) * (D_HEAD ** -0.5)
    s = _dot_nt(_stack_heads(qn, D_HEADS, D_HEAD), kn)
    e = jnp.exp(s - jnp.max(s, axis=-1, keepdims=True))
    o_all = _dot(e, v_ref[...]) * (1.0 / jnp.sum(e, axis=-1, keepdims=True))
    o_ref[...] = _unstack_heads(o_all, D_HEADS, D_HEAD).astype(o_ref.dtype)


def _attn_d_ctx(proj, lp, layer, nseq, seq_len):
    vec = pl.BlockSpec((None, 1, GW), lambda b: (layer, 0, 0))
    blk = lambda c: pl.BlockSpec((seq_len, GW), lambda b: (b, c))
    return pl.pallas_call(
        _attn_d_ctx_kernel,
        out_shape=[jax.ShapeDtypeStruct((nseq * seq_len, GW), MIX_DTYPE),
                   jax.ShapeDtypeStruct((nseq * seq_len, GW), F32)],
        grid=(nseq,),
        in_specs=[blk(BLK_DQ), blk(BLK_DK), blk(BLK_DV), vec, vec],
        out_specs=[blk(0), blk(0)],
        compiler_params=_params(("arbitrary",)),
        name="attn_d_ctx",
    )(proj, proj, proj, lp["d_qnorm"], lp["d_knorm"])


def _attn_d_lat_kernel(q_ref, k_ref, v_ref, ck_ref, cv_ref, qn_ref, kn_ref, cols_ref, o_ref,
                       ks, vs, cks, cvs, bias_ref):
    g = pl.program_id(1)
    n_rows = DEC_SEQ // GRID_W
    n_groups = n_rows // NA_QROWS
    n_loc = NA_KROWS * GRID_W

    @pl.when((pl.program_id(0) == 0) & (g == 0))
    def _():
        _fill_na_bias(cols_ref, bias_ref)

    @pl.when(g == 0)
    def _():
        ks[...] = _seg_rmsnorm(k_ref[...], D_HEAD, kn_ref[...]).astype(BF16)
        vs[...] = v_ref[...].astype(BF16)
        cks[...] = ck_ref[...].astype(BF16)
        cvs[...] = cv_ref[...].astype(BF16)

    qn = _seg_rmsnorm(q_ref[...], D_HEAD, qn_ref[...]) * (D_HEAD ** -0.5)
    row_start = jnp.clip(g * NA_QROWS - NA_ROWS // 2, 0, n_rows - NA_KROWS)
    variant = jnp.where(g == 0, 0, jnp.where(g == n_groups - 1, 2, 1))
    k0 = pl.multiple_of(row_start * GRID_W, GRID_W)
    kl = ks[pl.ds(k0, n_loc), :]
    vl = vs[pl.ds(k0, n_loc), :]
    kc = cks[...]
    vc = cvs[...]
    o = jnp.zeros(qn.shape, F32)
    for h in range(D_HEADS):
        hm = _lane_mask(qn.shape, h * D_HEAD, D_HEAD)
        qm = jnp.where(hm, qn, 0.0)
        s_loc = _dot_nt(qm, kl) + bias_ref[h, variant]
        s_ctx = _dot_nt(qm, kc)
        mx = jnp.maximum(jnp.max(s_loc, axis=-1, keepdims=True), jnp.max(s_ctx, axis=-1, keepdims=True))
        e_loc = jnp.exp(s_loc - mx)
        e_ctx = jnp.exp(s_ctx - mx)
        inv = 1.0 / (jnp.sum(e_loc, axis=-1, keepdims=True) + jnp.sum(e_ctx, axis=-1, keepdims=True))
        o = o + jnp.where(hm, (_dot(e_loc, vl) + _dot(e_ctx, vc)) * inv, 0.0)
    o_ref[...] = o.astype(o_ref.dtype)


def _na_bias_rows():
    n_rows = DEC_SEQ // GRID_W
    n_groups = n_rows // NA_QROWS
    plan = []
    for g in (0, 1, n_groups - 1):
        first_key_row = int(np.clip(g * NA_QROWS - NA_ROWS // 2, 0, n_rows - NA_KROWS))
        per_query_row = []
        for a in range(NA_QROWS):
            r = g * NA_QROWS + a
            win = int(np.clip(r - NA_ROWS // 2, 0, n_rows - NA_ROWS))
            per_query_row.append([first_key_row + i - r + NA_ROWS - 1
                                  if win <= first_key_row + i < win + NA_ROWS else None
                                  for i in range(NA_KROWS)])
        plan.append(per_query_row)
    return plan


def _fill_na_bias(cols_ref, bias_s):
    lane = lax.broadcasted_iota(jnp.int32, (GRID_W, 2 * GRID_W), 1)
    masked = jnp.full((GRID_W, 2 * GRID_W), NEG_INF, F32)
    for h in range(D_HEADS):
        for v, per_query_row in enumerate(_na_bias_rows()):
            for a, rows in enumerate(per_query_row):
                for p in range(NA_KROWS // 2):
                    left, right = rows[2 * p], rows[2 * p + 1]
                    lhs = masked if left is None else cols_ref[h, left]
                    rhs = masked if right is None else cols_ref[h, right]
                    bias_s[h, v, a * GRID_W:(a + 1) * GRID_W, 2 * p * GRID_W:2 * (p + 1) * GRID_W] = (
                        jnp.where(lane < GRID_W, lhs, rhs))


def _na_bias_columns(rpb):
    q = np.arange(GRID_W)
    kcol = np.arange(GRID_W)
    cs = np.clip(q - NA_COLS // 2, 0, GRID_W - NA_COLS)
    in_win = (kcol[None, :] >= cs[:, None]) & (kcol[None, :] < cs[:, None] + NA_COLS)
    dc = np.clip(kcol[None, :] - q[:, None] + (NA_COLS - 1), 0, 2 * NA_COLS - 2)
    pick_col = (dc[None] == np.arange(2 * NA_COLS - 1)[:, None, None]).astype(np.float32)
    cols = jnp.einsum("hrc,cqk->hrqk", rpb.astype(F32), jnp.asarray(pick_col), precision=lax.Precision.HIGHEST)
    cols = jnp.where(in_win[None, None], cols, NEG_INF)
    return jnp.concatenate([cols, cols], axis=-1)


def _attn_d_lat(proj, lp, layer, cache_k, cache_v, bias_cols):
    n_groups = DEC_SEQ // GRID_W // NA_QROWS
    tq = NA_QROWS * GRID_W
    vec = pl.BlockSpec((None, 1, GW), lambda b, r: (layer, 0, 0))
    seq = lambda c: pl.BlockSpec((DEC_SEQ, GW), lambda b, r: (b, c))
    cache = pl.BlockSpec((None, None, PAST_LEN, GW), lambda b, r: (b, layer, 0, 0))
    return pl.pallas_call(
        _attn_d_lat_kernel,
        out_shape=jax.ShapeDtypeStruct((DEC_BATCH * DEC_SEQ, GW), MIX_DTYPE),
        grid=(DEC_BATCH, n_groups),
        in_specs=[pl.BlockSpec((tq, GW), lambda b, r: (b * n_groups + r, BLK_DQ)),
                  seq(BLK_DK), seq(BLK_DV), cache, cache, vec, vec,
                  pl.BlockSpec(bias_cols.shape, lambda b, r: (0, 0, 0, 0))],
        out_specs=pl.BlockSpec((tq, GW), lambda b, r: (b * n_groups + r, 0)),
        scratch_shapes=[pltpu.VMEM((DEC_SEQ, GW), BF16), pltpu.VMEM((DEC_SEQ, GW), BF16),
                        pltpu.VMEM((PAST_LEN, GW), BF16), pltpu.VMEM((PAST_LEN, GW), BF16),
                        pltpu.VMEM((D_HEADS, 3, tq, NA_KROWS * GRID_W), F32)],
        compiler_params=_params(("arbitrary", "arbitrary")),
        name="attn_d_lat",
    )(proj, proj, proj, cache_k, cache_v, lp["d_qnorm"], lp["d_knorm"], bias_cols)


def _conv_kernel(c_ref, w_ref, b_ref, g_ref, beta_ref, o_ref, pad, *, seq_len):
    half = C_KSIZE // 2
    top = 16
    cin = c_ref[...]
    u = cin[:, :C_CH] * _sigmoid(cin[:, C_CH:])
    pad[0:top, :] = jnp.zeros((top, C_CH), F32)
    pad[top + seq_len:top + seq_len + top, :] = jnp.zeros((top, C_CH), F32)
    pad[top:top + seq_len, :] = u
    w = w_ref[...]
    rb = 256
    for r0 in range(0, seq_len, rb):
        acc = jnp.zeros((rb, C_CH), F32)
        n_hi = -(-C_KSIZE // 8)
        for lo in range(8):
            base = r0 + top - half + lo
            shifted = pad[base:base + rb + 8 * (n_hi - 1), :]
            for hi in range(n_hi):
                k = 8 * hi + lo
                if k < C_KSIZE:
                    acc = acc + shifted[8 * hi:8 * hi + rb, :] * w[k:k + 1, :]
        acc = acc + b_ref[...]
        mu = jnp.mean(acc, axis=-1, keepdims=True)
        xc = acc - mu
        y = xc * lax.rsqrt(jnp.mean(xc * xc, axis=-1, keepdims=True) + EPS) * g_ref[...] + beta_ref[...]
        o_ref[r0:r0 + rb, :] = _silu(y).astype(o_ref.dtype)


def _conv(proj, lp, layer, nseq, seq_len):
    vec = pl.BlockSpec((None, 1, C_CH), lambda b: (layer, 0, 0))
    return pl.pallas_call(
        functools.partial(_conv_kernel, seq_len=seq_len),
        out_shape=jax.ShapeDtypeStruct((nseq * seq_len, C_CH), MIX_DTYPE),
        grid=(nseq,),
        in_specs=[pl.BlockSpec((seq_len, 2 * C_CH), lambda b: (b, BLK_C)),
                  pl.BlockSpec((None, C_KSIZE, C_CH), lambda b: (layer, 0, 0)),
                  vec, vec, vec],
        out_specs=pl.BlockSpec((seq_len, C_CH), lambda b: (b, 0)),
        scratch_shapes=[pltpu.VMEM((seq_len + 32, C_CH), F32)],
        compiler_params=_params(("arbitrary",)),
        name="conv",
    )(proj, lp["c_dw"], lp["c_dw_b"], lp["c_ln_g"], lp["c_ln_b"])


def _gla_kernel(*refs, reverse, has_s0, n_blocks):
    refs = list(refs)
    qk_ref, v_ref, lr_ref, wg_ref, gb_ref = refs[:5]
    refs = refs[5:]
    s0_ref = refs.pop(0) if has_s0 else None
    if reverse:
        of_ref, bg_ref, on_ref = refs[:3]
        refs = refs[3:]
    o_ref, so_ref, st = refs
    rb, cs = GLA_ROWS, GLA_SUB
    hk = B_HEADS * B_DK
    j = pl.program_id(1)

    @pl.when(j == 0)
    def _():
        st[...] = s0_ref[...] if has_s0 else jnp.zeros(st.shape, F32)

    qk = qk_ref[...]
    q = qk[:, :hk] * (B_DK ** -0.5)
    k = qk[:, hk:]
    v = v_ref[...]
    pre = _dot(lr_ref[...], wg_ref[...]) + gb_ref[...]
    g = (jnp.minimum(pre, 0.0) - jnp.log(1.0 + jnp.exp(-jnp.abs(pre)))) * (1.0 / B_TAU)

    pos = lax.broadcasted_iota(jnp.int32, (rb, hk), 0) & (cs - 1)
    pre_sum = g
    suf_sum = g
    step = 1
    while step < cs:
        pre_sum = pre_sum + jnp.where(pos >= step, pltpu.roll(pre_sum, step, 0), 0.0)
        suf_sum = suf_sum + jnp.where(pos < cs - step, pltpu.roll(suf_sum, rb - step, 0), 0.0)
        step *= 2
    total = pre_sum + suf_sum - g
    if reverse:
        z = suf_sum
        k_dec = k * jnp.exp(pre_sum - g)
    else:
        z = pre_sum
        k_dec = k * jnp.exp(suf_sum - g)
    q_dec = q * jnp.exp(z)

    r_i = lax.broadcasted_iota(jnp.int32, (hk, GW), 0)
    c_i = lax.broadcasted_iota(jnp.int32, (hk, GW), 1)
    head_sum = jnp.where((r_i >> 5) == (c_i >> 6), 1.0, 0.0).astype(BF16)

    row_in_sub = lax.broadcasted_iota(jnp.int32, (cs, hk), 0)
    intra = []
    for n in range(rb // cs):
        qs = q[n * cs:(n + 1) * cs]
        zs = z[n * cs:(n + 1) * cs]
        pairs = []
        for jl in range(cs):
            r = n * cs + jl
            decay = jnp.exp(jnp.minimum(zs - z[r:r + 1], 0.0))
            keep = (row_in_sub <= jl) if reverse else (row_in_sub >= jl)
            pairs.append(jnp.where(keep, qs * k[r:r + 1] * decay, 0.0))
        w = jnp.dot(jnp.concatenate(pairs, axis=0).astype(BF16), head_sum, preferred_element_type=F32)
        acc = jnp.zeros((cs, GW), F32)
        for jl in range(cs):
            r = n * cs + jl
            acc = acc + w[jl * cs:(jl + 1) * cs] * v[r:r + 1]
        intra.append(acc)
    o = jnp.concatenate(intra, axis=0)

    r_s = lax.broadcasted_iota(jnp.int32, st.shape, 0)
    c_s = lax.broadcasted_iota(jnp.int32, st.shape, 1)
    diag = (r_s >> 6) == (c_s >> 5)
    n_sub = rb // cs
    row = lax.broadcasted_iota(jnp.int32, (rb, hk), 0)

    def per_sub_chunk(x):
        return jnp.concatenate([jnp.where((row >= n * cs) & (row < (n + 1) * cs), x, 0.0).astype(BF16)
                                for n in range(n_sub)], axis=1)

    kv_all = _dot(v.T, per_sub_chunk(k_dec))
    state = st[...]
    states = [None] * n_sub
    order = range(n_sub - 1, -1, -1) if reverse else range(n_sub)
    for n in order:
        states[n] = state.astype(BF16)
        decay = jnp.exp(total[n * cs:n * cs + 1, :])
        state = state * decay + jnp.where(diag, kv_all[:, n * hk:(n + 1) * hk], 0.0)
    st[...] = state
    o = o + _dot_nt(per_sub_chunk(q_dec), jnp.concatenate(states, axis=1))

    if reverse:
        o = _seg_rmsnorm(o + of_ref[...], B_DV, on_ref[...]) * _silu(bg_ref[...])
    o_ref[...] = o.astype(o_ref.dtype)

    @pl.when(j == n_blocks - 1)
    def _():
        so_ref[...] = state


def _gla(proj, lp, layer, nseq, seq_len, reverse, s0=None, o_fwd=None):
    rb = GLA_ROWS
    nb = seq_len // rb
    hk = B_HEADS * B_DK
    blk = (lambda b, j: b * nb + nb - 1 - j) if reverse else (lambda b, j: b * nb + j)
    tile = lambda c: pl.BlockSpec((rb, GW), lambda b, j: (blk(b, j), c))
    d = 1 if reverse else 0
    in_specs = [tile(BLK_BQK), tile(BLK_BV), tile(BLK_LR),
                pl.BlockSpec((None, None, GW, hk), lambda b, j: (layer, d, 0, 0)),
                pl.BlockSpec((None, None, 1, hk), lambda b, j: (layer, d, 0, 0))]
    args = [proj, proj, proj, lp["b_gate"], lp["b_gate_bias"]]
    if s0 is not None:
        in_specs.append(pl.BlockSpec((None, None, GW, hk), lambda b, j: (b, d, 0, 0)))
        args.append(s0)
    if reverse:
        in_specs += [tile(0), tile(BLK_BG), pl.BlockSpec((None, 1, GW), lambda b, j: (layer, 0, 0))]
        args += [o_fwd, proj, lp["b_onorm"]]
    return pl.pallas_call(
        functools.partial(_gla_kernel, reverse=reverse, has_s0=s0 is not None, n_blocks=nb),
        out_shape=[jax.ShapeDtypeStruct((nseq * seq_len, GW), MIX_DTYPE if reverse else F32),
                   jax.ShapeDtypeStruct((nseq, GW, hk), F32)],
        grid=(nseq, nb),
        in_specs=in_specs,
        out_specs=[tile(0), pl.BlockSpec((None, GW, hk), lambda b, j: (b, 0, 0))],
        scratch_shapes=[pltpu.VMEM((GW, hk), F32)],
        compiler_params=_params(("arbitrary", "arbitrary")),
        name="gla_bwd" if reverse else "gla_fwd",
    )(*args)


def _state_to_kernel_layout(s):
    eye = jnp.eye(B_HEADS, dtype=s.dtype)
    t = jnp.einsum("bxhde,hg->bxhegd", s, eye)
    return t.reshape(s.shape[0], 2, B_HEADS * B_DV, B_HEADS * B_DK)


def _state_from_kernel_layout(st):
    t = st.reshape(st.shape[0], B_HEADS, B_DV, B_HEADS, B_DK)
    return jnp.stack([t[:, h, :, h, :] for h in range(B_HEADS)], axis=1).transpose(0, 1, 3, 2)


def _outproj_kernel(a_ref, b_ref, c_ref, d_ref, x_ref, mod_ref, w_ref, o_ref):
    mix = jnp.concatenate([a_ref[...], b_ref[...], c_ref[...], d_ref[...]], axis=-1)
    gate = mod_ref[:, 2 * D_MODEL:3 * D_MODEL]
    o_ref[...] = x_ref[...] + gate * _dot(mix, w_ref[...])


def _outproj(o_a, o_b, o_c, o_d, x, mods3, w_out, layer, latent, seq_len):
    t_tokens = x.shape[0]
    tm = 512
    row = _mod_row(latent, tm, seq_len)
    part = pl.BlockSpec((tm, GW), lambda t: (t, 0))
    return pl.pallas_call(
        _outproj_kernel,
        out_shape=jax.ShapeDtypeStruct((t_tokens, D_MODEL), F32),
        grid=(t_tokens // tm,),
        in_specs=[part, part, part, part,
                  pl.BlockSpec((tm, D_MODEL), lambda t: (t, 0)),
                  pl.BlockSpec((None, 1, 6 * D_MODEL), lambda t: (row(t), 0, 0)),
                  pl.BlockSpec((None, 4 * GW, D_MODEL), lambda t: (layer, 0, 0))],
        out_specs=pl.BlockSpec((tm, D_MODEL), lambda t: (t, 0)),
        compiler_params=_params(("arbitrary",)),
        name="outproj",
    )(o_a, o_b, o_c, o_d, x, mods3, w_out)


MOE_TM = 512
ROW = 8
LANES = D_MODEL // ROW
R_E1, R_E2, R_W1, R_W2, R_RANK1, R_RANK2 = 0, 1, 2, 3, 4, 5


def _router_kernel(x_ref, g_ref, mod_ref, r_ref, rb_ref, h_ref, route_ref, cnt_ref, carry):
    d = D_MODEL

    @pl.when(pl.program_id(0) == 0)
    def _():
        carry[...] = jnp.zeros(carry.shape, F32)

    h = _modulate(x_ref[...], g_ref[...], mod_ref[:, 3 * d:4 * d], mod_ref[:, 4 * d:5 * d])
    tm = h.shape[0]
    for s in range(ROW):
        h_ref[pl.ds(s, tm, stride=ROW), :] = h[:, s * LANES:(s + 1) * LANES]
    r = r_ref[...]
    h_hi = h.astype(BF16)
    h_lo = (h - h_hi.astype(F32)).astype(BF16)
    r_hi = r.astype(BF16)
    r_lo = (r - r_hi.astype(F32)).astype(BF16)
    dot = lambda a, b: jnp.dot(a, b, preferred_element_type=F32)
    logits = dot(h_hi, r_hi) + dot(h_lo, r_hi) + dot(h_hi, r_lo) + rb_ref[...]
    lane = lax.broadcasted_iota(jnp.int32, logits.shape, 1).astype(F32)
    big = float(logits.shape[1])
    logits = jnp.where(lane < N_EXPERTS, logits, -jnp.inf)
    v1 = jnp.max(logits, axis=-1, keepdims=True)
    i1 = jnp.min(jnp.where(logits == v1, lane, big), axis=-1, keepdims=True)
    rest = jnp.where(lane == i1, -jnp.inf, logits)
    v2 = jnp.max(rest, axis=-1, keepdims=True)
    i2 = jnp.min(jnp.where(rest == v2, lane, big), axis=-1, keepdims=True)
    e2 = jnp.exp(v2 - v1)
    inv = 1.0 / (1.0 + e2)
    sel = jnp.where((lane == i1) | (lane == i2), 1.0, 0.0)
    r_i = lax.broadcasted_iota(jnp.int32, (tm, tm), 0)
    c_i = lax.broadcasted_iota(jnp.int32, (tm, tm), 1)
    before = jnp.where(c_i < r_i, 1.0, 0.0).astype(BF16)
    rank = jnp.dot(before, sel.astype(BF16), preferred_element_type=F32) + carry[...]
    count = carry[...] + jnp.sum(sel, axis=0, keepdims=True)
    carry[...] = count
    cnt_ref[...] = jnp.broadcast_to(count, cnt_ref.shape)
    rank1 = jnp.sum(jnp.where(lane == i1, rank, 0.0), axis=-1, keepdims=True)
    rank2 = jnp.sum(jnp.where(lane == i2, rank, 0.0), axis=-1, keepdims=True)
    rec = jnp.zeros(logits.shape, F32)
    for pos, val in ((R_E1, i1), (R_E2, i2), (R_W1, inv), (R_W2, e2 * inv), (R_RANK1, rank1), (R_RANK2, rank2)):
        rec = jnp.where(lane == pos, val, rec)
    route_ref[...] = rec


def _router(x, norm_g3, mods3, router, router_b, layer, j, latent, seq_len):
    t_tokens = x.shape[0]
    tm = 512
    row = _mod_row(latent, tm, seq_len)
    lanes = router.shape[-1]
    return pl.pallas_call(
        _router_kernel,
        out_shape=[jax.ShapeDtypeStruct((t_tokens * ROW, LANES), F32),
                   jax.ShapeDtypeStruct((t_tokens, lanes), F32),
                   jax.ShapeDtypeStruct((8, lanes), F32)],
        grid=(t_tokens // tm,),
        in_specs=[pl.BlockSpec((tm, D_MODEL), lambda t: (t, 0)),
                  pl.BlockSpec((None, 1, D_MODEL), lambda t: (layer, 0, 0)),
                  pl.BlockSpec((None, 1, 6 * D_MODEL), lambda t: (row(t), 0, 0)),
                  pl.BlockSpec((None, D_MODEL, lanes), lambda t: (j, 0, 0)),
                  pl.BlockSpec((None, 1, lanes), lambda t: (j, 0, 0))],
        out_specs=[pl.BlockSpec((tm * ROW, LANES), lambda t: (t, 0)),
                   pl.BlockSpec((tm, lanes), lambda t: (t, 0)),
                   pl.BlockSpec((8, lanes), lambda t: (0, 0))],
        scratch_shapes=[pltpu.VMEM((1, lanes), F32)],
        compiler_params=_params(("arbitrary",)),
        name="router",
    )(x, norm_g3, mods3, router, router_b)


def _row_copy(src, src_row, dst, dst_row, sem):
    s0 = pl.multiple_of(src_row * ROW, ROW)
    d0 = pl.multiple_of(dst_row * ROW, ROW)
    return pltpu.make_async_copy(src.at[pl.ds(s0, ROW), :], dst.at[pl.ds(d0, ROW), :], sem)


def _dispatch_kernel(p0_ref, p1_ref, h_ref, xs_in_ref, xs_ref, sem):
    del xs_in_ref
    tm = h_ref.shape[0] // ROW
    base = pl.program_id(0) * tm

    def issue(r, carry):
        _row_copy(h_ref, r, xs_ref, p0_ref[base + r], sem.at[0]).start()
        _row_copy(h_ref, r, xs_ref, p1_ref[base + r], sem.at[1]).start()
        return carry

    def wait(r, carry):
        _row_copy(h_ref, r, xs_ref, p0_ref[base + r], sem.at[0]).wait()
        _row_copy(h_ref, r, xs_ref, p1_ref[base + r], sem.at[1]).wait()
        return carry

    lax.fori_loop(0, tm, issue, 0, unroll=8)
    lax.fori_loop(0, tm, wait, 0, unroll=8)


def _dispatch(pos0, pos1, h_rows, n_slots):
    t_tokens = pos0.shape[0]
    tm = 512
    xs0 = jnp.zeros((n_slots * ROW, LANES), F32)
    return pl.pallas_call(
        _dispatch_kernel,
        out_shape=jax.ShapeDtypeStruct(xs0.shape, F32),
        grid_spec=pltpu.PrefetchScalarGridSpec(
            num_scalar_prefetch=2,
            grid=(t_tokens // tm,),
            in_specs=[pl.BlockSpec((tm * ROW, LANES), lambda t, p0, p1: (t, 0)),
                      pl.BlockSpec(memory_space=pl.ANY)],
            out_specs=pl.BlockSpec(memory_space=pl.ANY),
            scratch_shapes=[pltpu.SemaphoreType.DMA((2,))]),
        input_output_aliases={3: 0},
        compiler_params=_params(("arbitrary",)),
        name="moe_dispatch",
    )(pos0, pos1, h_rows, xs0)


def _moe_ffn_kernel(te_ref, nu_ref, xs_ref, w1_ref, w3_ref, w2_ref, y_ref, xb, acc, *, n_f):
    del te_ref
    i = pl.program_id(0)
    f = pl.program_id(1)
    tm = MOE_TM
    used = i < nu_ref[0]

    @pl.when(used & (f == 0))
    def _():
        for s in range(ROW):
            xb[:, s * LANES:(s + 1) * LANES] = xs_ref[pl.ds(s, tm, stride=ROW), :].astype(BF16)
        acc[...] = jnp.zeros(acc.shape, F32)

    @pl.when(used)
    def _():
        x = xb[...]
        a = jnp.dot(x, w1_ref[...].astype(BF16), preferred_element_type=F32)
        b = jnp.dot(x, w3_ref[...].astype(BF16), preferred_element_type=F32)
        acc[...] += _dot(_silu(a) * b, w2_ref[...])

    @pl.when(used & (f == n_f - 1))
    def _():
        for s in range(ROW):
            y_ref[pl.ds(s, tm, stride=ROW), :] = acc[:, s * LANES:(s + 1) * LANES]

    @pl.when(jnp.logical_not(used) & (f == n_f - 1))
    def _():
        y_ref[...] = jnp.zeros(y_ref.shape, F32)


def _moe_ffn(tile_expert, n_used, xs, w1, w3, w2, j):
    tm = MOE_TM
    n_tiles = xs.shape[0] // (tm * ROW)
    d_ff = w1.shape[-1]
    tf = 896
    n_f = d_ff // tf
    f_of = lambda i, f, nu: jnp.where(i < nu[0], f, n_f - 1)
    return pl.pallas_call(
        functools.partial(_moe_ffn_kernel, n_f=n_f),
        out_shape=jax.ShapeDtypeStruct(xs.shape, F32),
        grid_spec=pltpu.PrefetchScalarGridSpec(
            num_scalar_prefetch=2,
            grid=(n_tiles, n_f),
            in_specs=[pl.BlockSpec((tm * ROW, LANES), lambda i, f, te, nu: (jnp.minimum(i, nu[0] - 1), 0)),
                      pl.BlockSpec((None, None, D_MODEL, tf), lambda i, f, te, nu: (j, te[i], 0, f_of(i, f, nu))),
                      pl.BlockSpec((None, None, D_MODEL, tf), lambda i, f, te, nu: (j, te[i], 0, f_of(i, f, nu))),
                      pl.BlockSpec((None, None, tf, D_MODEL), lambda i, f, te, nu: (j, te[i], f_of(i, f, nu), 0))],
            out_specs=pl.BlockSpec((tm * ROW, LANES), lambda i, f, te, nu: (i, 0)),
            scratch_shapes=[pltpu.VMEM((tm, D_MODEL), BF16), pltpu.VMEM((tm, D_MODEL), F32)]),
        compiler_params=_params(("arbitrary", "arbitrary")),
        name="moe_ffn",
    )(tile_expert, n_used, xs, w1, w3, w2)


def _combine_kernel(p0_ref, p1_ref, x_ref, mod_ref, route_ref, y_ref, o_ref, buf0, buf1, sem):
    tm = x_ref.shape[0]
    base = pl.program_id(0) * tm

    def issue(r, carry):
        _row_copy(y_ref, p0_ref[base + r], buf0, r, sem.at[0]).start()
        _row_copy(y_ref, p1_ref[base + r], buf1, r, sem.at[1]).start()
        return carry

    def wait(r, carry):
        _row_copy(y_ref, p0_ref[base + r], buf0, r, sem.at[0]).wait()
        _row_copy(y_ref, p1_ref[base + r], buf1, r, sem.at[1]).wait()
        return carry

    lax.fori_loop(0, tm, issue, 0, unroll=8)
    lax.fori_loop(0, tm, wait, 0, unroll=8)
    route = route_ref[...]
    w_a = route[:, R_W1:R_W1 + 1]
    w_b = route[:, R_W2:R_W2 + 1]
    for s in range(ROW):
        cols = slice(s * LANES, (s + 1) * LANES)
        y = w_a * buf0[pl.ds(s, tm, stride=ROW), :] + w_b * buf1[pl.ds(s, tm, stride=ROW), :]
        o_ref[:, cols] = x_ref[:, cols] + mod_ref[:, 5 * D_MODEL + s * LANES:5 * D_MODEL + (s + 1) * LANES] * y


def _combine(pos0, pos1, x, mods3, route, y, latent, seq_len):
    t_tokens = x.shape[0]
    tm = 512
    row = _mod_row(latent, tm, seq_len)
    return pl.pallas_call(
        _combine_kernel,
        out_shape=jax.ShapeDtypeStruct((t_tokens, D_MODEL), F32),
        grid_spec=pltpu.PrefetchScalarGridSpec(
            num_scalar_prefetch=2,
            grid=(t_tokens // tm,),
            in_specs=[pl.BlockSpec((tm, D_MODEL), lambda t, p0, p1: (t, 0)),
                      pl.BlockSpec((None, 1, 6 * D_MODEL), lambda t, p0, p1: (row(t), 0, 0)),
                      pl.BlockSpec((tm, route.shape[1]), lambda t, p0, p1: (t, 0)),
                      pl.BlockSpec(memory_space=pl.ANY)],
            out_specs=pl.BlockSpec((tm, D_MODEL), lambda t, p0, p1: (t, 0)),
            scratch_shapes=[pltpu.VMEM((tm * ROW, LANES), F32), pltpu.VMEM((tm * ROW, LANES), F32),
                            pltpu.SemaphoreType.DMA((2,))]),
        compiler_params=_params(("arbitrary",)),
        name="moe_combine",
    )(pos0, pos1, x, mods3, route, y)


def _moe(x, norm_g3, mods3, router, router_b, w1, w3, w2, layer, j, latent, seq_len):
    t_tokens = x.shape[0]
    tm = MOE_TM
    n_tiles = 2 * t_tokens // tm + N_EXPERTS
    h_rows, route, cnt = _router(x, norm_g3, mods3, router, router_b, layer, j, latent, seq_len)
    expert = route[:, R_E1:R_E2 + 1].astype(jnp.int32)
    rank = route[:, R_RANK1:R_RANK2 + 1].astype(jnp.int32)
    tiles = (cnt[0, :N_EXPERTS].astype(jnp.int32) + tm - 1) // tm
    ends = jnp.cumsum(tiles)
    starts = ends - tiles
    ids = jnp.arange(N_EXPERTS, dtype=jnp.int32)
    start_of = jnp.sum(jnp.where(expert[:, :, None] == ids, starts * tm, 0), axis=-1)
    pos = start_of + rank
    n_used = ends[-1:]
    tile_ids = jnp.arange(n_tiles, dtype=jnp.int32)
    tile_expert = jnp.sum((tile_ids[:, None] >= ends[None, :]).astype(jnp.int32), axis=1)
    last_expert = jnp.max(jnp.where(tiles > 0, ids, 0))
    tile_expert = jnp.where(tile_ids < n_used, tile_expert, last_expert)
    pos0, pos1 = pos[:, 0], pos[:, 1]
    xs = _dispatch(pos0, pos1, h_rows, n_tiles * tm)
    y = _moe_ffn(tile_expert, n_used, xs, w1, w3, w2, j)
    return _combine(pos0, pos1, x, mods3, route, y, latent, seq_len)


def _ffn_kernel(x_ref, g_ref, mod_ref, w1_ref, w3_ref, w2_ref, o_ref, h_s, acc, *, n_f):
    d = D_MODEL
    f = pl.program_id(1)

    @pl.when(f == 0)
    def _():
        h = _modulate(x_ref[...], g_ref[...], mod_ref[:, 3 * d:4 * d], mod_ref[:, 4 * d:5 * d])
        h_s[...] = h.astype(BF16)
        acc[...] = jnp.zeros(acc.shape, F32)

    h = h_s[...]
    a = jnp.dot(h, w1_ref[...], preferred_element_type=F32)
    b = jnp.dot(h, w3_ref[...], preferred_element_type=F32)
    acc[...] += _dot(_silu(a) * b, w2_ref[...])

    @pl.when(f == n_f - 1)
    def _():
        o_ref[...] = x_ref[...] + mod_ref[:, 5 * d:6 * d] * acc[...]


def _ffn(x, norm_g3, mods3, w1, w3, w2, layer, j, latent, seq_len):
    t_tokens = x.shape[0]
    d_ff = w1.shape[-1]
    tm = 512
    tf = d_ff // 2
    n_f = d_ff // tf
    row = _mod_row(latent, tm, seq_len)
    return pl.pallas_call(
        functools.partial(_ffn_kernel, n_f=n_f),
        out_shape=jax.ShapeDtypeStruct((t_tokens, D_MODEL), F32),
        grid=(t_tokens // tm, n_f),
        in_specs=[pl.BlockSpec((tm, D_MODEL), lambda t, f: (t, 0)),
                  pl.BlockSpec((None, 1, D_MODEL), lambda t, f: (layer, 0, 0)),
                  pl.BlockSpec((None, 1, 6 * D_MODEL), lambda t, f: (row(t), 0, 0)),
                  pl.BlockSpec((None, D_MODEL, tf), lambda t, f: (j, 0, f)),
                  pl.BlockSpec((None, D_MODEL, tf), lambda t, f: (j, 0, f)),
                  pl.BlockSpec((None, tf, D_MODEL), lambda t, f: (j, f, 0))],
        out_specs=pl.BlockSpec((tm, D_MODEL), lambda t, f: (t, 0)),
        scratch_shapes=[pltpu.VMEM((tm, D_MODEL), BF16), pltpu.VMEM((tm, D_MODEL), F32)],
        compiler_params=_params(("arbitrary", "arbitrary")),
        name="ffn",
    )(x, norm_g3, mods3, w1, w3, w2)


def _rope_tables():
    half = A_QK // 2
    nf = half // 2
    t = np.arange(DEC_SEQ)
    pos = np.stack([(t // GRID_W), (t % GRID_W)], axis=1).astype(np.float32)
    inv = jnp.asarray(ROPE_BASE, F32) ** (-jnp.arange(nf, dtype=F32) / nf)
    lane = np.arange(GW) % A_QK
    which = (lane >= half).astype(np.int32)
    freq = lane % nf
    ang = jnp.asarray(pos)[:, which] * inv[freq][None, :]
    return jnp.cos(ang), jnp.sin(ang)


def _tile_vec(v, reps):
    return jnp.tile(v, (1, reps))[:, None, :]


def _prepare_params(w_in, w_out, a_qnorm, a_knorm, a_lam, a_subln, b_gate, b_gate_bias, b_onorm,
                    c_dw, c_dw_b, c_ln_g, c_ln_b, d_qnorm, d_knorm):
    lr0 = 1536
    lr1 = lr0 + 2 * B_GATE_RANK
    pad = PROJ_W - w_in.shape[-1]
    w_in_p = jnp.concatenate([w_in[:, :, :lr0], w_in[:, :, lr1:], w_in[:, :, lr0:lr1],
                              jnp.zeros(w_in.shape[:2] + (pad,), w_in.dtype)], axis=-1).astype(BF16)
    hk = B_HEADS * B_DK
    wg = jnp.zeros((DEPTH, 2, GW, hk), F32)
    wg = wg.at[:, 0, 0:B_GATE_RANK].set(b_gate[:, 0])
    wg = wg.at[:, 1, B_GATE_RANK:2 * B_GATE_RANK].set(b_gate[:, 1])
    return dict(
        w_in=w_in_p, w_out=w_out.astype(BF16),
        a_qnorm=_tile_vec(a_qnorm, GW // A_QK), a_knorm=_tile_vec(a_knorm, GW // A_QK),
        a_lam=a_lam, a_subln=_tile_vec(a_subln, GW // A_V),
        b_gate=wg.astype(BF16), b_gate_bias=b_gate_bias[:, :, None, :], b_onorm=_tile_vec(b_onorm, GW // B_DV),
        c_dw=c_dw, c_dw_b=c_dw_b[:, None, :], c_ln_g=c_ln_g[:, None, :], c_ln_b=c_ln_b[:, None, :],
        d_qnorm=_tile_vec(d_qnorm, GW // D_HEAD), d_knorm=_tile_vec(d_knorm, GW // D_HEAD),
    )


def _lambda_init(i):
    return 0.8 - 0.6 * math.exp(-0.3 * i)


def kernel(x_prompt, x_sample, cache_a_k, cache_a_v, state_b, cache_d_k, cache_d_v, c, c_ctx,
           norm1_g, norm2_g, w_ada, b_ada, w_in, w_out, a_qnorm, a_knorm, a_lam, a_subln,
           b_gate, b_gate_bias, b_onorm, c_dw, c_dw_b, c_ln_g, c_ln_b, d_qnorm, d_knorm, d_rpb,
           ffn_w1, ffn_w3, ffn_w2, moe_router, moe_router_b, moe_w1, moe_w3, moe_w2):
    lp = _prepare_params(w_in, w_out, a_qnorm, a_knorm, a_lam, a_subln, b_gate, b_gate_bias, b_onorm,
                         c_dw, c_dw_b, c_ln_g, c_ln_b, d_qnorm, d_knorm)
    norm1 = norm1_g[:, None, :]
    norm2 = norm2_g[:, None, :]
    b_ada3 = b_ada[:, None, :]
    cond8 = jnp.concatenate([c_ctx[None, :], c, jnp.zeros((8 - 1 - DEC_BATCH, D_MODEL), F32)], axis=0)
    rope = _rope_tables()
    ffn_w = [w.astype(BF16) for w in (ffn_w1, ffn_w3, ffn_w2)]
    lanes = 128
    router_p = jnp.pad(moe_router, ((0, 0), (0, 0), (0, lanes - N_EXPERTS)))
    router_b_p = jnp.pad(moe_router_b, ((0, 0), (0, lanes - N_EXPERTS)))[:, None, :]
    ck_a = cache_a_k.reshape(DEC_BATCH, DEPTH, PAST_LEN, GW)
    cv_a = cache_a_v.reshape(DEC_BATCH, DEPTH, PAST_LEN, GW)
    ck_d = cache_d_k.reshape(DEC_BATCH, DEPTH, PAST_LEN, GW)
    cv_d = cache_d_v.reshape(DEC_BATCH, DEPTH, PAST_LEN, GW)
    s0_lat = _state_to_kernel_layout(state_b.transpose(1, 0, 2, 3, 4, 5).reshape(
        DEPTH * DEC_BATCH, 2, B_HEADS, B_DK, B_DV)).reshape(DEPTH, DEC_BATCH, 2, GW, B_HEADS * B_DK)

    def channel_mixer(x, mods3, i, latent, seq_len):
        j = i // 2
        if i % 2 == 0:
            return _ffn(x, norm2, mods3, *ffn_w, i, j, latent, seq_len)
        return _moe(x, norm2, mods3, router_p, router_b_p, moe_w1, moe_w3, moe_w2, i, j, latent, seq_len)

    xc = x_prompt.reshape(BATCH * SEQ, D_MODEL)
    xl = x_sample.reshape(DEC_BATCH * DEC_SEQ, D_MODEL)
    new_ak, new_av, new_sb, new_dk, new_dv = [], [], [], [], []
    for i in range(DEPTH):
        lam_init = _lambda_init(i)
        mods3 = _adaln(cond8, w_ada, b_ada3, i)[:, None, :]

        proj = _inproj(xc, norm1, mods3, lp["w_in"], i, False, SEQ)
        o_a, ak = _attn_a(proj, lp, i, False, BATCH, SEQ, lam_init)
        o_f, s_f = _gla(proj, lp, i, BATCH, SEQ, False)
        o_b, s_b = _gla(proj, lp, i, BATCH, SEQ, True, o_fwd=o_f)
        o_c = _conv(proj, lp, i, BATCH, SEQ)
        o_d, dk = _attn_d_ctx(proj, lp, i, BATCH, SEQ)
        xc = _outproj(o_a, o_b, o_c, o_d, xc, mods3, lp["w_out"], i, False, SEQ)
        xc = channel_mixer(xc, mods3, i, False, SEQ)
        new_ak.append(ak.reshape(BATCH, SEQ, A_HEADS, 2 * A_QK))
        new_av.append(proj[:, BLK_AV * GW:(BLK_AV + 1) * GW].reshape(BATCH, SEQ, A_HEADS, A_V))
        new_sb.append(jnp.stack([_state_from_kernel_layout(s_f), _state_from_kernel_layout(s_b)], axis=1))
        new_dk.append(dk.reshape(BATCH, SEQ, D_HEADS, D_HEAD))
        new_dv.append(proj[:, BLK_DV * GW:(BLK_DV + 1) * GW].reshape(BATCH, SEQ, D_HEADS, D_HEAD))

        proj = _inproj(xl, norm1, mods3, lp["w_in"], i, True, DEC_SEQ)
        o_a, = _attn_a(proj, lp, i, True, DEC_BATCH, DEC_SEQ, lam_init, ck_a, cv_a, rope)
        o_f, _ = _gla(proj, lp, i, DEC_BATCH, DEC_SEQ, False, s0=s0_lat[i])
        o_b, _ = _gla(proj, lp, i, DEC_BATCH, DEC_SEQ, True, s0=s0_lat[i], o_fwd=o_f)
        o_c = _conv(proj, lp, i, DEC_BATCH, DEC_SEQ)
        o_d = _attn_d_lat(proj, lp, i, ck_d, cv_d, _na_bias_columns(d_rpb[i]))
        xl = _outproj(o_a, o_b, o_c, o_d, xl, mods3, lp["w_out"], i, True, DEC_SEQ)
        xl = channel_mixer(xl, mods3, i, True, DEC_SEQ)

    return (xc.reshape(BATCH, SEQ, D_MODEL), xl.reshape(DEC_BATCH, DEC_SEQ, D_MODEL),
            jnp.stack(new_ak, axis=1), jnp.stack(new_av, axis=1), jnp.stack(new_sb, axis=1),
            jnp.stack(new_dk, axis=1), jnp.stack(new_dv, axis=1))
```

```python
import functools
import math

import numpy as np
import jax
import jax.numpy as jnp
from jax import lax
from jax.experimental import pallas as pl
from jax.experimental.pallas import tpu as pltpu

F32 = jnp.float32
BF16 = jnp.bfloat16
MIX_DTYPE = BF16

D_MODEL = 1024
BATCH = 16
SEQ = 256
DEPTH = 2
DEC_BATCH = 4
DEC_SEQ = 2048
PAST_LEN = 512
GRID_W = 64
A_HEADS = 4
A_QK = 32
A_V = 64
B_HEADS = 4
B_DK = 32
B_DV = 64
B_GATE_RANK = 16
B_TAU = 16.0
C_CH = 256
C_KSIZE = 31
D_HEADS = 4
D_HEAD = 64
NA_ROWS = 8
NA_COLS = 16
NA_QROWS = 4
NA_KROWS = 12
ROPE_BASE = 10000.0
D_FF = 2816
N_EXPERTS = 8
D_FF_EXPERT = 3584
EPS = 1e-6
NEG_INF = -1e30
LOG2_E = math.log2(math.e)

GW = 256
N_PROJ_BLOCKS = 12
PROJ_W = GW * N_PROJ_BLOCKS
BLK_AQ, BLK_AK, BLK_AV, BLK_BQK, BLK_BV, BLK_BG = 0, 1, 2, 3, 4, 5
BLK_C = 3
BLK_DQ, BLK_DK, BLK_DV, BLK_LR = 8, 9, 10, 11

GLA_SUB = 16
GLA_ROWS = 256
VMEM_LIMIT = 56 * 1024 * 1024


def _params(sem, vmem=VMEM_LIMIT):
    return pltpu.CompilerParams(dimension_semantics=sem, vmem_limit_bytes=vmem)


def _dot(a, b):
    return jnp.dot(a.astype(BF16), b.astype(BF16), preferred_element_type=F32)


def _dot_nt(a, b):
    return lax.dot_general(a.astype(BF16), b.astype(BF16), (((1,), (1,)), ((), ())),
                           preferred_element_type=F32)


def _dot_exact_rhs(x, m):
    hi = x.astype(BF16)
    lo = (x - hi.astype(F32)).astype(BF16)
    return (jnp.dot(hi, m, preferred_element_type=F32) + jnp.dot(lo, m, preferred_element_type=F32))


def _sigmoid(x):
    return 1.0 / (1.0 + jnp.exp(-x))


def _silu(x):
    return x * _sigmoid(x)


def _seg_matrix(n, seg):
    r = lax.broadcasted_iota(jnp.int32, (n, n), 0)
    c = lax.broadcasted_iota(jnp.int32, (n, n), 1)
    return jnp.where((r ^ c) < seg, 1.0, 0.0).astype(BF16)


def _seg_rmsnorm(x, seg, w):
    ms = _dot_exact_rhs(x * x, _seg_matrix(x.shape[-1], seg)) * (1.0 / seg)
    return x * lax.rsqrt(ms + EPS) * w


def _lane_mask(shape, lo, width):
    lane = lax.broadcasted_iota(jnp.int32, shape, len(shape) - 1)
    return (lane >= lo) & (lane < lo + width)


def _stack_heads(q, heads, width):
    return jnp.concatenate([jnp.where(_lane_mask(q.shape, h * width, width), q, 0.0) for h in range(heads)],
                           axis=0)


def _unstack_heads(o_all, heads, width):
    n = o_all.shape[0] // heads
    o = jnp.zeros((n, o_all.shape[1]), F32)
    for h in range(heads):
        blk = o_all[h * n:(h + 1) * n]
        o = jnp.where(_lane_mask(blk.shape, h * width, width), blk, o)
    return o


def _modulate(x, g, shift, scale):
    y = x * lax.rsqrt(jnp.mean(x * x, axis=-1, keepdims=True) + EPS)
    return y * g * (1.0 + scale) + shift


def _rope(x, cos, sin):
    lane = lax.broadcasted_iota(jnp.int32, x.shape, 1)
    w = x.shape[1]
    rot = jnp.where((lane & 15) < 8, -pltpu.roll(x, w - 8, 1), pltpu.roll(x, 8, 1))
    return x * cos + rot * sin


def _adaln_kernel(c_ref, w_ref, b_ref, o_ref):
    o_ref[...] = _dot(_silu(c_ref[...]), w_ref[...]) + b_ref[...]


def _adaln(cond8, w_ada, b_ada3, layer):
    tn = 1536
    n = 6 * D_MODEL
    return pl.pallas_call(
        _adaln_kernel,
        out_shape=jax.ShapeDtypeStruct((8, n), F32),
        grid=(n // tn,),
        in_specs=[pl.BlockSpec((8, D_MODEL), lambda j: (0, 0)),
                  pl.BlockSpec((None, D_MODEL, tn), lambda j: (layer, 0, j)),
                  pl.BlockSpec((None, 1, tn), lambda j: (layer, 0, j))],
        out_specs=pl.BlockSpec((8, tn), lambda j: (0, j)),
        compiler_params=_params(("arbitrary",)),
        name="adaln",
    )(cond8, w_ada, b_ada3)


def _mod_row(latent, tm, seq_len):
    if latent:
        return lambda t: 1 + (t * tm) // seq_len
    return lambda t: 0


def _inproj_kernel(x_ref, g_ref, mod_ref, w_ref, o_ref):
    d = D_MODEL
    h = _modulate(x_ref[...], g_ref[...], mod_ref[:, 0:d], mod_ref[:, d:2 * d])
    o_ref[...] = _dot(h, w_ref[...])


def _inproj(x, norm_g3, mods3, w_in, layer, latent, seq_len):
    t_tokens = x.shape[0]
    tm = 512
    row = _mod_row(latent, tm, seq_len)
    return pl.pallas_call(
        _inproj_kernel,
        out_shape=jax.ShapeDtypeStruct((t_tokens, PROJ_W), F32),
        grid=(t_tokens // tm,),
        in_specs=[pl.BlockSpec((tm, D_MODEL), lambda t: (t, 0)),
                  pl.BlockSpec((None, 1, D_MODEL), lambda t: (layer, 0, 0)),
                  pl.BlockSpec((None, 1, 6 * D_MODEL), lambda t: (row(t), 0, 0)),
                  pl.BlockSpec((None, D_MODEL, PROJ_W), lambda t: (layer, 0, 0))],
        out_specs=pl.BlockSpec((tm, PROJ_W), lambda t: (t, 0)),
        compiler_params=_params(("arbitrary",)),
        name="inproj",
    )(x, norm_g3, mods3, w_in)


def _diff_lambda(lam_ref, lam_init):
    lam = lam_ref[...]
    s1 = jnp.sum(lam[0:1] * lam[1:2], axis=1, keepdims=True)
    s2 = jnp.sum(lam[2:3] * lam[3:4], axis=1, keepdims=True)
    return jnp.exp(s1) - jnp.exp(s2) + lam_init


def _values_and_ones(v):
    one = jnp.where(lax.broadcasted_iota(jnp.int32, (v.shape[0], A_V), 1) == 0, 1.0, 0.0)
    parts = []
    for h in range(A_HEADS):
        parts += [v[:, h * A_V:(h + 1) * A_V], one]
    return jnp.concatenate(parts, axis=1).astype(BF16)


def _attn_a_kernel(*refs, n_own, n_ctx, tq, latent, lam_init):
    if latent:
        (q_ref, k_ref, v_ref, ck_ref, cv_ref, cos_ref, sin_ref, qn_ref, kn_ref, lam_ref, sub_ref,
         o_ref, ks, vs) = refs
    else:
        (q_ref, k_ref, v_ref, qn_ref, kn_ref, lam_ref, sub_ref, o_ref, ko_ref, ks, vs) = refs
    t = pl.program_id(1)

    @pl.when(t == 0)
    def _():
        kn = _seg_rmsnorm(k_ref[...], A_QK, kn_ref[...])
        if latent:
            kn = _rope(kn, cos_ref[...], sin_ref[...])
            ks[n_own:n_own + n_ctx, :] = ck_ref[...].astype(BF16)
            vs[n_own:n_own + n_ctx, :] = _values_and_ones(cv_ref[...])
        else:
            ko_ref[...] = kn
        ks[0:n_own, :] = kn.astype(BF16)
        vs[0:n_own, :] = _values_and_ones(v_ref[...])

    qn = _seg_rmsnorm(q_ref[...], A_QK, qn_ref[...])
    if latent:
        r0 = pl.multiple_of(t * tq, tq)
        qn = _rope(qn, cos_ref[pl.ds(r0, tq), :], sin_ref[pl.ds(r0, tq), :])
    qn = qn * (A_QK ** -0.5 * LOG2_E)
    lam = _diff_lambda(lam_ref, lam_init)
    k_all = ks[...]
    heads = []
    for h in range(A_HEADS):
        v_h = vs[:, h * 2 * A_V:(h + 1) * 2 * A_V]
        qm = jnp.concatenate([jnp.where(_lane_mask(qn.shape, (2 * h + m) * A_QK, A_QK), qn, 0.0)
                              for m in range(2)], axis=0)
        s = _dot_nt(qm, k_all)
        e = jnp.exp2(s - jnp.max(s, axis=-1, keepdims=True))
        acc = _dot(e, v_h)
        acc0, acc1 = acc[:tq], acc[tq:]
        inv0 = 1.0 / acc0[:, A_V:A_V + 1]
        inv1 = lam / acc1[:, A_V:A_V + 1]
        heads.append((acc0 * inv0 - acc1 * inv1)[:, :A_V])
    o = jnp.concatenate(heads, axis=1)
    o_ref[...] = (_seg_rmsnorm(o, A_V, sub_ref[...]) * (1.0 - lam_init)).astype(o_ref.dtype)


def _attn_a(proj, lp, layer, latent, nseq, seq_len, lam_init, cache_k=None, cache_v=None, rope=None):
    tq = min(512, seq_len)
    nt = seq_len // tq
    n_ctx = PAST_LEN if latent else 0
    kern = functools.partial(_attn_a_kernel, n_own=seq_len, n_ctx=n_ctx, tq=tq, latent=latent,
                             lam_init=lam_init)
    vec = lambda name: pl.BlockSpec((None, 1, GW), lambda b, t: (layer, 0, 0))
    in_specs = [pl.BlockSpec((tq, GW), lambda b, t: (b * nt + t, BLK_AQ)),
                pl.BlockSpec((seq_len, GW), lambda b, t: (b, BLK_AK)),
                pl.BlockSpec((seq_len, GW), lambda b, t: (b, BLK_AV))]
    args = [proj, proj, proj]
    if latent:
        in_specs += [pl.BlockSpec((None, None, PAST_LEN, GW), lambda b, t: (b, layer, 0, 0)),
                     pl.BlockSpec((None, None, PAST_LEN, GW), lambda b, t: (b, layer, 0, 0)),
                     pl.BlockSpec((seq_len, GW), lambda b, t: (0, 0)),
                     pl.BlockSpec((seq_len, GW), lambda b, t: (0, 0))]
        args += [cache_k, cache_v, rope[0], rope[1]]
    in_specs += [vec("q"), vec("k"),
                 pl.BlockSpec((None, 4, A_QK), lambda b, t: (layer, 0, 0)),
                 vec("s")]
    args += [lp["a_qnorm"], lp["a_knorm"], lp["a_lam"], lp["a_subln"]]
    out_shape = [jax.ShapeDtypeStruct((nseq * seq_len, GW), MIX_DTYPE)]
    out_specs = [pl.BlockSpec((tq, GW), lambda b, t: (b * nt + t, 0))]
    if not latent:
        out_shape.append(jax.ShapeDtypeStruct((nseq * seq_len, GW), F32))
        out_specs.append(pl.BlockSpec((seq_len, GW), lambda b, t: (b, 0)))
    return pl.pallas_call(
        kern,
        out_shape=out_shape,
        grid=(nseq, nt),
        in_specs=in_specs,
        out_specs=out_specs,
        scratch_shapes=[pltpu.VMEM((seq_len + n_ctx, GW), BF16), pltpu.VMEM((seq_len + n_ctx, 2 * GW), BF16)],
        compiler_params=_params(("arbitrary", "arbitrary")),
        name="attn_a_lat" if latent else "attn_a_ctx",
    )(*args)


def _attn_d_ctx_kernel(q_ref, k_ref, v_ref, qn_ref, kn_ref, o_ref, ko_ref):
    kn = _seg_rmsnorm(k_ref[...], D_HEAD, kn_ref[...])
    ko_ref[...] = kn
    qn = _seg_rmsnorm(q_ref[...], D_HEAD, qn_ref[...]) * (D_HEAD ** -0.5)
    s = _dot_nt(_stack_heads(qn, D_HEADS, D_HEAD), kn)
    e = jnp.exp(s - jnp.max(s, axis=-1, keepdims=True))
    o_all = _dot(e, v_ref[...]) * (1.0 / jnp.sum(e, axis=-1, keepdims=True))
    o_ref[...] = _unstack_heads(o_all, D_HEADS, D_HEAD).astype(o_ref.dtype)


def _attn_d_ctx(proj, lp, layer, nseq, seq_len):
    vec = pl.BlockSpec((None, 1, GW), lambda b: (layer, 0, 0))
    blk = lambda c: pl.BlockSpec((seq_len, GW), lambda b: (b, c))
    return pl.pallas_call(
        _attn_d_ctx_kernel,
        out_shape=[jax.ShapeDtypeStruct((nseq * seq_len, GW), MIX_DTYPE),
                   jax.ShapeDtypeStruct((nseq * seq_len, GW), F32)],
        grid=(nseq,),
        in_specs=[blk(BLK_DQ), blk(BLK_DK), blk(BLK_DV), vec, vec],
        out_specs=[blk(0), blk(0)],
        compiler_params=_params(("arbitrary",)),
        name="attn_d_ctx",
    )(proj, proj, proj, lp["d_qnorm"], lp["d_knorm"])


def _attn_d_lat_kernel(q_ref, k_ref, v_ref, ck_ref, cv_ref, qn_ref, kn_ref, cols_ref, o_ref,
                       ks, vs, cks, cvs, bias_ref):
    g = pl.program_id(1)
    n_rows = DEC_SEQ // GRID_W
    n_groups = n_rows // NA_QROWS
    n_loc = NA_KROWS * GRID_W

    @pl.when((pl.program_id(0) == 0) & (g == 0))
    def _():
        _fill_na_bias(cols_ref, bias_ref)

    @pl.when(g == 0)
    def _():
        ks[...] = _seg_rmsnorm(k_ref[...], D_HEAD, kn_ref[...]).astype(BF16)
        vs[...] = v_ref[...].astype(BF16)
        cks[...] = ck_ref[...].astype(BF16)
        cvs[...] = cv_ref[...].astype(BF16)

    qn = _seg_rmsnorm(q_ref[...], D_HEAD, qn_ref[...]) * (D_HEAD ** -0.5)
    row_start = jnp.clip(g * NA_QROWS - NA_ROWS // 2, 0, n_rows - NA_KROWS)
    variant = jnp.where(g == 0, 0, jnp.where(g == n_groups - 1, 2, 1))
    k0 = pl.multiple_of(row_start * GRID_W, GRID_W)
    kl = ks[pl.ds(k0, n_loc), :]
    vl = vs[pl.ds(k0, n_loc), :]
    kc = cks[...]
    vc = cvs[...]
    o = jnp.zeros(qn.shape, F32)
    for h in range(D_HEADS):
        hm = _lane_mask(qn.shape, h * D_HEAD, D_HEAD)
        qm = jnp.where(hm, qn, 0.0)
        s_loc = _dot_nt(qm, kl) + bias_ref[h, variant]
        s_ctx = _dot_nt(qm, kc)
        mx = jnp.maximum(jnp.max(s_loc, axis=-1, keepdims=True), jnp.max(s_ctx, axis=-1, keepdims=True))
        e_loc = jnp.exp(s_loc - mx)
        e_ctx = jnp.exp(s_ctx - mx)
        inv = 1.0 / (jnp.sum(e_loc, axis=-1, keepdims=True) + jnp.sum(e_ctx, axis=-1, keepdims=True))
        o = o + jnp.where(hm, (_dot(e_loc, vl) + _dot(e_ctx, vc)) * inv, 0.0)
    o_ref[...] = o.astype(o_ref.dtype)


def _na_bias_rows():
    n_rows = DEC_SEQ // GRID_W
    n_groups = n_rows // NA_QROWS
    plan = []
    for g in (0, 1, n_groups - 1):
        first_key_row = int(np.clip(g * NA_QROWS - NA_ROWS // 2, 0, n_rows - NA_KROWS))
        per_query_row = []
        for a in range(NA_QROWS):
            r = g * NA_QROWS + a
            win = int(np.clip(r - NA_ROWS // 2, 0, n_rows - NA_ROWS))
            per_query_row.append([first_key_row + i - r + NA_ROWS - 1
                                  if win <= first_key_row + i < win + NA_ROWS else None
                                  for i in range(NA_KROWS)])
        plan.append(per_query_row)
    return plan


def _fill_na_bias(cols_ref, bias_s):
    lane = lax.broadcasted_iota(jnp.int32, (GRID_W, 2 * GRID_W), 1)
    masked = jnp.full((GRID_W, 2 * GRID_W), NEG_INF, F32)
    for h in range(D_HEADS):
        for v, per_query_row in enumerate(_na_bias_rows()):
            for a, rows in enumerate(per_query_row):
                for p in range(NA_KROWS // 2):
                    left, right = rows[2 * p], rows[2 * p + 1]
                    lhs = masked if left is None else cols_ref[h, left]
                    rhs = masked if right is None else cols_ref[h, right]
                    bias_s[h, v, a * GRID_W:(a + 1) * GRID_W, 2 * p * GRID_W:2 * (p + 1) * GRID_W] = (
                        jnp.where(lane < GRID_W, lhs, rhs))


def _na_bias_columns(rpb):
    q = np.arange(GRID_W)
    kcol = np.arange(GRID_W)
    cs = np.clip(q - NA_COLS // 2, 0, GRID_W - NA_COLS)
    in_win = (kcol[None, :] >= cs[:, None]) & (kcol[None, :] < cs[:, None] + NA_COLS)
    dc = np.clip(kcol[None, :] - q[:, None] + (NA_COLS - 1), 0, 2 * NA_COLS - 2)
    pick_col = (dc[None] == np.arange(2 * NA_COLS - 1)[:, None, None]).astype(np.float32)
    cols = jnp.einsum("hrc,cqk->hrqk", rpb.astype(F32), jnp.asarray(pick_col), precision=lax.Precision.HIGHEST)
    cols = jnp.where(in_win[None, None], cols, NEG_INF)
    return jnp.concatenate([cols, cols], axis=-1)


def _attn_d_lat(proj, lp, layer, cache_k, cache_v, bias_cols):
    n_groups = DEC_SEQ // GRID_W // NA_QROWS
    tq = NA_QROWS * GRID_W
    vec = pl.BlockSpec((None, 1, GW), lambda b, r: (layer, 0, 0))
    seq = lambda c: pl.BlockSpec((DEC_SEQ, GW), lambda b, r: (b, c))
    cache = pl.BlockSpec((None, None, PAST_LEN, GW), lambda b, r: (b, layer, 0, 0))
    return pl.pallas_call(
        _attn_d_lat_kernel,
        out_shape=jax.ShapeDtypeStruct((DEC_BATCH * DEC_SEQ, GW), MIX_DTYPE),
        grid=(DEC_BATCH, n_groups),
        in_specs=[pl.BlockSpec((tq, GW), lambda b, r: (b * n_groups + r, BLK_DQ)),
                  seq(BLK_DK), seq(BLK_DV), cache, cache, vec, vec,
                  pl.BlockSpec(bias_cols.shape, lambda b, r: (0, 0, 0, 0))],
        out_specs=pl.BlockSpec((tq, GW), lambda b, r: (b * n_groups + r, 0)),
        scratch_shapes=[pltpu.VMEM((DEC_SEQ, GW), BF16), pltpu.VMEM((DEC_SEQ, GW), BF16),
                        pltpu.VMEM((PAST_LEN, GW), BF16), pltpu.VMEM((PAST_LEN, GW), BF16),
                        pltpu.VMEM((D_HEADS, 3, tq, NA_KROWS * GRID_W), F32)],
        compiler_params=_params(("arbitrary", "arbitrary")),
        name="attn_d_lat",
    )(proj, proj, proj, cache_k, cache_v, lp["d_qnorm"], lp["d_knorm"], bias_cols)


def _conv_kernel(c_ref, w_ref, b_ref, g_ref, beta_ref, o_ref, pad, *, seq_len):
    half = C_KSIZE // 2
    top = 16
    cin = c_ref[...]
    u = cin[:, :C_CH] * _sigmoid(cin[:, C_CH:])
    pad[0:top, :] = jnp.zeros((top, C_CH), F32)
    pad[top + seq_len:top + seq_len + top, :] = jnp.zeros((top, C_CH), F32)
    pad[top:top + seq_len, :] = u
    w = w_ref[...]
    rb = 256
    for r0 in range(0, seq_len, rb):
        acc = jnp.zeros((rb, C_CH), F32)
        n_hi = -(-C_KSIZE // 8)
        for lo in range(8):
            base = r0 + top - half + lo
            shifted = pad[base:base + rb + 8 * (n_hi - 1), :]
            for hi in range(n_hi):
                k = 8 * hi + lo
                if k < C_KSIZE:
                    acc = acc + shifted[8 * hi:8 * hi + rb, :] * w[k:k + 1, :]
        acc = acc + b_ref[...]
        mu = jnp.mean(acc, axis=-1, keepdims=True)
        xc = acc - mu
        y = xc * lax.rsqrt(jnp.mean(xc * xc, axis=-1, keepdims=True) + EPS) * g_ref[...] + beta_ref[...]
        o_ref[r0:r0 + rb, :] = _silu(y).astype(o_ref.dtype)


def _conv(proj, lp, layer, nseq, seq_len):
    vec = pl.BlockSpec((None, 1, C_CH), lambda b: (layer, 0, 0))
    return pl.pallas_call(
        functools.partial(_conv_kernel, seq_len=seq_len),
        out_shape=jax.ShapeDtypeStruct((nseq * seq_len, C_CH), MIX_DTYPE),
        grid=(nseq,),
        in_specs=[pl.BlockSpec((seq_len, 2 * C_CH), lambda b: (b, BLK_C)),
                  pl.BlockSpec((None, C_KSIZE, C_CH), lambda b: (layer, 0, 0)),
                  vec, vec, vec],
        out_specs=pl.BlockSpec((seq_len, C_CH), lambda b: (b, 0)),
        scratch_shapes=[pltpu.VMEM((seq_len + 32, C_CH), F32)],
        compiler_params=_params(("arbitrary",)),
        name="conv",
    )(proj, lp["c_dw"], lp["c_dw_b"], lp["c_ln_g"], lp["c_ln_b"])


def _gla_kernel(*refs, reverse, has_s0, n_blocks):
    refs = list(refs)
    qk_ref, v_ref, lr_ref, wg_ref, gb_ref = refs[:5]
    refs = refs[5:]
    s0_ref = refs.pop(0) if has_s0 else None
    if reverse:
        of_ref, bg_ref, on_ref = refs[:3]
        refs = refs[3:]
    o_ref, so_ref, st = refs
    rb, cs = GLA_ROWS, GLA_SUB
    hk = B_HEADS * B_DK
    j = pl.program_id(1)

    @pl.when(j == 0)
    def _():
        st[...] = s0_ref[...] if has_s0 else jnp.zeros(st.shape, F32)

    qk = qk_ref[...]
    q = qk[:, :hk] * (B_DK ** -0.5)
    k = qk[:, hk:]
    v = v_ref[...]
    pre = _dot(lr_ref[...], wg_ref[...]) + gb_ref[...]
    g = (jnp.minimum(pre, 0.0) - jnp.log(1.0 + jnp.exp(-jnp.abs(pre)))) * (1.0 / B_TAU)

    pos = lax.broadcasted_iota(jnp.int32, (rb, hk), 0) & (cs - 1)
    pre_sum = g
    suf_sum = g
    step = 1
    while step < cs:
        pre_sum = pre_sum + jnp.where(pos >= step, pltpu.roll(pre_sum, step, 0), 0.0)
        suf_sum = suf_sum + jnp.where(pos < cs - step, pltpu.roll(suf_sum, rb - step, 0), 0.0)
        step *= 2
    total = pre_sum + suf_sum - g
    if reverse:
        z = suf_sum
        k_dec = k * jnp.exp(pre_sum - g)
    else:
        z = pre_sum
        k_dec = k * jnp.exp(suf_sum - g)
    q_dec = q * jnp.exp(z)

    r_i = lax.broadcasted_iota(jnp.int32, (hk, GW), 0)
    c_i = lax.broadcasted_iota(jnp.int32, (hk, GW), 1)
    head_sum = jnp.where((r_i >> 5) == (c_i >> 6), 1.0, 0.0).astype(BF16)

    row_in_sub = lax.broadcasted_iota(jnp.int32, (cs, hk), 0)
    intra = []
    for n in range(rb // cs):
        qs = q[n * cs:(n + 1) * cs]
        zs = z[n * cs:(n + 1) * cs]
        pairs = []
        for jl in range(cs):
            r = n * cs + jl
            decay = jnp.exp(jnp.minimum(zs - z[r:r + 1], 0.0))
            keep = (row_in_sub <= jl) if reverse else (row_in_sub >= jl)
            pairs.append(jnp.where(keep, qs * k[r:r + 1] * decay, 0.0))
        w = jnp.dot(jnp.concatenate(pairs, axis=0).astype(BF16), head_sum, preferred_element_type=F32)
        acc = jnp.zeros((cs, GW), F32)
        for jl in range(cs):
            r = n * cs + jl
            acc = acc + w[jl * cs:(jl + 1) * cs] * v[r:r + 1]
        intra.append(acc)
    o = jnp.concatenate(intra, axis=0)

    r_s = lax.broadcasted_iota(jnp.int32, st.shape, 0)
    c_s = lax.broadcasted_iota(jnp.int32, st.shape, 1)
    diag = (r_s >> 6) == (c_s >> 5)
    n_sub = rb // cs
    row = lax.broadcasted_iota(jnp.int32, (rb, hk), 0)

    def per_sub_chunk(x):
        return jnp.concatenate([jnp.where((row >= n * cs) & (row < (n + 1) * cs), x, 0.0).astype(BF16)
                                for n in range(n_sub)], axis=1)

    kv_all = _dot(v.T, per_sub_chunk(k_dec))
    state = st[...]
    states = [None] * n_sub
    order = range(n_sub - 1, -1, -1) if reverse else range(n_sub)
    for n in order:
        states[n] = state.astype(BF16)
        decay = jnp.exp(total[n * cs:n * cs + 1, :])
        state = state * decay + jnp.where(diag, kv_all[:, n * hk:(n + 1) * hk], 0.0)
    st[...] = state
    o = o + _dot_nt(per_sub_chunk(q_dec), jnp.concatenate(states, axis=1))

    if reverse:
        o = _seg_rmsnorm(o + of_ref[...], B_DV, on_ref[...]) * _silu(bg_ref[...])
    o_ref[...] = o.astype(o_ref.dtype)

    @pl.when(j == n_blocks - 1)
    def _():
        so_ref[...] = state


def _gla(proj, lp, layer, nseq, seq_len, reverse, s0=None, o_fwd=None):
    rb = GLA_ROWS
    nb = seq_len // rb
    hk = B_HEADS * B_DK
    blk = (lambda b, j: b * nb + nb - 1 - j) if reverse else (lambda b, j: b * nb + j)
    tile = lambda c: pl.BlockSpec((rb, GW), lambda b, j: (blk(b, j), c))
    d = 1 if reverse else 0
    in_specs = [tile(BLK_BQK), tile(BLK_BV), tile(BLK_LR),
                pl.BlockSpec((None, None, GW, hk), lambda b, j: (layer, d, 0, 0)),
                pl.BlockSpec((None, None, 1, hk), lambda b, j: (layer, d, 0, 0))]
    args = [proj, proj, proj, lp["b_gate"], lp["b_gate_bias"]]
    if s0 is not None:
        in_specs.append(pl.BlockSpec((None, None, GW, hk), lambda b, j: (b, d, 0, 0)))
        args.append(s0)
    if reverse:
        in_specs += [tile(0), tile(BLK_BG), pl.BlockSpec((None, 1, GW), lambda b, j: (layer, 0, 0))]
        args += [o_fwd, proj, lp["b_onorm"]]
    return pl.pallas_call(
        functools.partial(_gla_kernel, reverse=reverse, has_s0=s0 is not None, n_blocks=nb),
        out_shape=[jax.ShapeDtypeStruct((nseq * seq_len, GW), MIX_DTYPE if reverse else F32),
                   jax.ShapeDtypeStruct((nseq, GW, hk), F32)],
        grid=(nseq, nb),
        in_specs=in_specs,
        out_specs=[tile(0), pl.BlockSpec((None, GW, hk), lambda b, j: (b, 0, 0))],
        scratch_shapes=[pltpu.VMEM((GW, hk), F32)],
        compiler_params=_params(("arbitrary", "arbitrary")),
        name="gla_bwd" if reverse else "gla_fwd",
    )(*args)


def _state_to_kernel_layout(s):
    eye = jnp.eye(B_HEADS, dtype=s.dtype)
    t = jnp.einsum("bxhde,hg->bxhegd", s, eye)
    return t.reshape(s.shape[0], 2, B_HEADS * B_DV, B_HEADS * B_DK)


def _state_from_kernel_layout(st):
    t = st.reshape(st.shape[0], B_HEADS, B_DV, B_HEADS, B_DK)
    return jnp.stack([t[:, h, :, h, :] for h in range(B_HEADS)], axis=1).transpose(0, 1, 3, 2)


def _outproj_kernel(a_ref, b_ref, c_ref, d_ref, x_ref, mod_ref, w_ref, o_ref):
    mix = jnp.concatenate([a_ref[...], b_ref[...], c_ref[...], d_ref[...]], axis=-1)
    gate = mod_ref[:, 2 * D_MODEL:3 * D_MODEL]
    o_ref[...] = x_ref[...] + gate * _dot(mix, w_ref[...])


def _outproj(o_a, o_b, o_c, o_d, x, mods3, w_out, layer, latent, seq_len):
    t_tokens = x.shape[0]
    tm = 512
    row = _mod_row(latent, tm, seq_len)
    part = pl.BlockSpec((tm, GW), lambda t: (t, 0))
    return pl.pallas_call(
        _outproj_kernel,
        out_shape=jax.ShapeDtypeStruct((t_tokens, D_MODEL), F32),
        grid=(t_tokens // tm,),
        in_specs=[part, part, part, part,
                  pl.BlockSpec((tm, D_MODEL), lambda t: (t, 0)),
                  pl.BlockSpec((None, 1, 6 * D_MODEL), lambda t: (row(t), 0, 0)),
                  pl.BlockSpec((None, 4 * GW, D_MODEL), lambda t: (layer, 0, 0))],
        out_specs=pl.BlockSpec((tm, D_MODEL), lambda t: (t, 0)),
        compiler_params=_params(("arbitrary",)),
        name="outproj",
    )(o_a, o_b, o_c, o_d, x, mods3, w_out)


MOE_TM = 512
ROW = 8
LANES = D_MODEL // ROW
R_E1, R_E2, R_W1, R_W2, R_RANK1, R_RANK2 = 0, 1, 2, 3, 4, 5


def _router_kernel(x_ref, g_ref, mod_ref, r_ref, rb_ref, h_ref, route_ref, cnt_ref, carry):
    d = D_MODEL

    @pl.when(pl.program_id(0) == 0)
    def _():
        carry[...] = jnp.zeros(carry.shape, F32)

    h = _modulate(x_ref[...], g_ref[...], mod_ref[:, 3 * d:4 * d], mod_ref[:, 4 * d:5 * d])
    tm = h.shape[0]
    for s in range(ROW):
        h_ref[pl.ds(s, tm, stride=ROW), :] = h[:, s * LANES:(s + 1) * LANES]
    r = r_ref[...]
    h_hi = h.astype(BF16)
    h_lo = (h - h_hi.astype(F32)).astype(BF16)
    r_hi = r.astype(BF16)
    r_lo = (r - r_hi.astype(F32)).astype(BF16)
    dot = lambda a, b: jnp.dot(a, b, preferred_element_type=F32)
    logits = dot(h_hi, r_hi) + dot(h_lo, r_hi) + dot(h_hi, r_lo) + rb_ref[...]
    lane = lax.broadcasted_iota(jnp.int32, logits.shape, 1).astype(F32)
    big = float(logits.shape[1])
    logits = jnp.where(lane < N_EXPERTS, logits, -jnp.inf)
    v1 = jnp.max(logits, axis=-1, keepdims=True)
    i1 = jnp.min(jnp.where(logits == v1, lane, big), axis=-1, keepdims=True)
    rest = jnp.where(lane == i1, -jnp.inf, logits)
    v2 = jnp.max(rest, axis=-1, keepdims=True)
    i2 = jnp.min(jnp.where(rest == v2, lane, big), axis=-1, keepdims=True)
    e2 = jnp.exp(v2 - v1)
    inv = 1.0 / (1.0 + e2)
    sel = jnp.where((lane == i1) | (lane == i2), 1.0, 0.0)
    r_i = lax.broadcasted_iota(jnp.int32, (tm, tm), 0)
    c_i = lax.broadcasted_iota(jnp.int32, (tm, tm), 1)
    before = jnp.where(c_i < r_i, 1.0, 0.0).astype(BF16)
    rank = jnp.dot(before, sel.astype(BF16), preferred_element_type=F32) + carry[...]
    count = carry[...] + jnp.sum(sel, axis=0, keepdims=True)
    carry[...] = count
    cnt_ref[...] = jnp.broadcast_to(count, cnt_ref.shape)
    rank1 = jnp.sum(jnp.where(lane == i1, rank, 0.0), axis=-1, keepdims=True)
    rank2 = jnp.sum(jnp.where(lane == i2, rank, 0.0), axis=-1, keepdims=True)
    rec = jnp.zeros(logits.shape, F32)
    for pos, val in ((R_E1, i1), (R_E2, i2), (R_W1, inv), (R_W2, e2 * inv), (R_RANK1, rank1), (R_RANK2, rank2)):
        rec = jnp.where(lane == pos, val, rec)
    route_ref[...] = rec


def _router(x, norm_g3, mods3, router, router_b, layer, j, latent, seq_len):
    t_tokens = x.shape[0]
    tm = 512
    row = _mod_row(latent, tm, seq_len)
    lanes = router.shape[-1]
    return pl.pallas_call(
        _router_kernel,
        out_shape=[jax.ShapeDtypeStruct((t_tokens * ROW, LANES), F32),
                   jax.ShapeDtypeStruct((t_tokens, lanes), F32),
                   jax.ShapeDtypeStruct((8, lanes), F32)],
        grid=(t_tokens // tm,),
        in_specs=[pl.BlockSpec((tm, D_MODEL), lambda t: (t, 0)),
                  pl.BlockSpec((None, 1, D_MODEL), lambda t: (layer, 0, 0)),
                  pl.BlockSpec((None, 1, 6 * D_MODEL), lambda t: (row(t), 0, 0)),
                  pl.BlockSpec((None, D_MODEL, lanes), lambda t: (j, 0, 0)),
                  pl.BlockSpec((None, 1, lanes), lambda t: (j, 0, 0))],
        out_specs=[pl.BlockSpec((tm * ROW, LANES), lambda t: (t, 0)),
                   pl.BlockSpec((tm, lanes), lambda t: (t, 0)),
                   pl.BlockSpec((8, lanes), lambda t: (0, 0))],
        scratch_shapes=[pltpu.VMEM((1, lanes), F32)],
        compiler_params=_params(("arbitrary",)),
        name="router",
    )(x, norm_g3, mods3, router, router_b)


def _row_copy(src, src_row, dst, dst_row, sem):
    s0 = pl.multiple_of(src_row * ROW, ROW)
    d0 = pl.multiple_of(dst_row * ROW, ROW)
    return pltpu.make_async_copy(src.at[pl.ds(s0, ROW), :], dst.at[pl.ds(d0, ROW), :], sem)


def _dispatch_kernel(p0_ref, p1_ref, h_ref, xs_in_ref, xs_ref, sem):
    del xs_in_ref
    tm = h_ref.shape[0] // ROW
    base = pl.program_id(0) * tm

    def issue(r, carry):
        _row_copy(h_ref, r, xs_ref, p0_ref[base + r], sem.at[0]).start()
        _row_copy(h_ref, r, xs_ref, p1_ref[base + r], sem.at[1]).start()
        return carry

    def wait(r, carry):
        _row_copy(h_ref, r, xs_ref, p0_ref[base + r], sem.at[0]).wait()
        _row_copy(h_ref, r, xs_ref, p1_ref[base + r], sem.at[1]).wait()
        return carry

    lax.fori_loop(0, tm, issue, 0, unroll=8)
    lax.fori_loop(0, tm, wait, 0, unroll=8)


def _dispatch(pos0, pos1, h_rows, n_slots):
    t_tokens = pos0.shape[0]
    tm = 512
    xs0 = jnp.zeros((n_slots * ROW, LANES), F32)
    return pl.pallas_call(
        _dispatch_kernel,
        out_shape=jax.ShapeDtypeStruct(xs0.shape, F32),
        grid_spec=pltpu.PrefetchScalarGridSpec(
            num_scalar_prefetch=2,
            grid=(t_tokens // tm,),
            in_specs=[pl.BlockSpec((tm * ROW, LANES), lambda t, p0, p1: (t, 0)),
                      pl.BlockSpec(memory_space=pl.ANY)],
            out_specs=pl.BlockSpec(memory_space=pl.ANY),
            scratch_shapes=[pltpu.SemaphoreType.DMA((2,))]),
        input_output_aliases={3: 0},
        compiler_params=_params(("arbitrary",)),
        name="moe_dispatch",
    )(pos0, pos1, h_rows, xs0)


def _moe_ffn_kernel(*refs, first):
    if first:
        te_ref, nu_ref, xs_ref, w1_ref, w3_ref, w2_ref, y_ref, w1b, w3b, w2b = refs
    else:
        te_ref, nu_ref, xs_ref, yin_ref, w1_ref, w3_ref, w2_ref, y_ref, w1b, w3b, w2b = refs
    i = pl.program_id(0)
    tm = MOE_TM
    used = i < nu_ref[0]
    new_weights = (i == 0) | (te_ref[i] != te_ref[jnp.maximum(i - 1, 0)])

    @pl.when(used & new_weights)
    def _():
        w1b[...] = w1_ref[...].astype(BF16)
        w3b[...] = w3_ref[...].astype(BF16)
        w2b[...] = w2_ref[...].astype(BF16)

    @pl.when(used)
    def _():
        x = jnp.concatenate([xs_ref[pl.ds(s, tm, stride=ROW), :] for s in range(ROW)], axis=1).astype(BF16)
        a = jnp.dot(x, w1b[...], preferred_element_type=F32)
        b = jnp.dot(x, w3b[...], preferred_element_type=F32)
        part = _dot(_silu(a) * b, w2b[...])
        for s in range(ROW):
            rows = pl.ds(s, tm, stride=ROW)
            piece = part[:, s * LANES:(s + 1) * LANES]
            y_ref[rows, :] = piece if first else yin_ref[rows, :] + piece

    @pl.when(jnp.logical_not(used))
    def _():
        y_ref[...] = jnp.zeros(y_ref.shape, F32)


def _moe_ffn(tile_expert, n_used, xs, w1, w3, w2, j):
    tm = MOE_TM
    n_tiles = xs.shape[0] // (tm * ROW)
    d_ff = w1.shape[-1]
    tf = 896
    rows = pl.BlockSpec((tm * ROW, LANES), lambda i, te, nu: (jnp.minimum(i, nu[0] - 1), 0))
    y = None
    for f in range(d_ff // tf):
        first = y is None
        in_specs = [rows] + ([] if first else [rows]) + [
            pl.BlockSpec((None, None, D_MODEL, tf), lambda i, te, nu, f=f: (j, te[i], 0, f)),
            pl.BlockSpec((None, None, D_MODEL, tf), lambda i, te, nu, f=f: (j, te[i], 0, f)),
            pl.BlockSpec((None, None, tf, D_MODEL), lambda i, te, nu, f=f: (j, te[i], f, 0))]
        args = [tile_expert, n_used, xs] + ([] if first else [y]) + [w1, w3, w2]
        y = pl.pallas_call(
            functools.partial(_moe_ffn_kernel, first=first),
            out_shape=jax.ShapeDtypeStruct(xs.shape, F32),
            grid_spec=pltpu.PrefetchScalarGridSpec(
                num_scalar_prefetch=2,
                grid=(n_tiles,),
                in_specs=in_specs,
                out_specs=pl.BlockSpec((tm * ROW, LANES), lambda i, te, nu: (i, 0)),
                scratch_shapes=[pltpu.VMEM((D_MODEL, tf), BF16), pltpu.VMEM((D_MODEL, tf), BF16),
                                pltpu.VMEM((tf, D_MODEL), BF16)]),
            compiler_params=_params(("arbitrary",)),
            name="moe_ffn",
        )(*args)
    return y


def _combine_kernel(p0_ref, p1_ref, x_ref, mod_ref, route_ref, y_ref, o_ref, buf0, buf1, sem):
    tm = x_ref.shape[0]
    base = pl.program_id(0) * tm

    def issue(r, carry):
        _row_copy(y_ref, p0_ref[base + r], buf0, r, sem.at[0]).start()
        _row_copy(y_ref, p1_ref[base + r], buf1, r, sem.at[1]).start()
        return carry

    def wait(r, carry):
        _row_copy(y_ref, p0_ref[base + r], buf0, r, sem.at[0]).wait()
        _row_copy(y_ref, p1_ref[base + r], buf1, r, sem.at[1]).wait()
        return carry

    lax.fori_loop(0, tm, issue, 0, unroll=8)
    lax.fori_loop(0, tm, wait, 0, unroll=8)
    route = route_ref[...]
    w_a = route[:, R_W1:R_W1 + 1]
    w_b = route[:, R_W2:R_W2 + 1]
    for s in range(ROW):
        cols = slice(s * LANES, (s + 1) * LANES)
        y = w_a * buf0[pl.ds(s, tm, stride=ROW), :] + w_b * buf1[pl.ds(s, tm, stride=ROW), :]
        o_ref[:, cols] = x_ref[:, cols] + mod_ref[:, 5 * D_MODEL + s * LANES:5 * D_MODEL + (s + 1) * LANES] * y


def _combine(pos0, pos1, x, mods3, route, y, latent, seq_len):
    t_tokens = x.shape[0]
    tm = 512
    row = _mod_row(latent, tm, seq_len)
    return pl.pallas_call(
        _combine_kernel,
        out_shape=jax.ShapeDtypeStruct((t_tokens, D_MODEL), F32),
        grid_spec=pltpu.PrefetchScalarGridSpec(
            num_scalar_prefetch=2,
            grid=(t_tokens // tm,),
            in_specs=[pl.BlockSpec((tm, D_MODEL), lambda t, p0, p1: (t, 0)),
                      pl.BlockSpec((None, 1, 6 * D_MODEL), lambda t, p0, p1: (row(t), 0, 0)),
                      pl.BlockSpec((tm, route.shape[1]), lambda t, p0, p1: (t, 0)),
                      pl.BlockSpec(memory_space=pl.ANY)],
            out_specs=pl.BlockSpec((tm, D_MODEL), lambda t, p0, p1: (t, 0)),
            scratch_shapes=[pltpu.VMEM((tm * ROW, LANES), F32), pltpu.VMEM((tm * ROW, LANES), F32),
                            pltpu.SemaphoreType.DMA((2,))]),
        compiler_params=_params(("arbitrary",)),
        name="moe_combine",
    )(pos0, pos1, x, mods3, route, y)


def _moe(x, norm_g3, mods3, router, router_b, w1, w3, w2, layer, j, latent, seq_len):
    t_tokens = x.shape[0]
    tm = MOE_TM
    n_tiles = 2 * t_tokens // tm + N_EXPERTS
    h_rows, route, cnt = _router(x, norm_g3, mods3, router, router_b, layer, j, latent, seq_len)
    expert = route[:, R_E1:R_E2 + 1].astype(jnp.int32)
    rank = route[:, R_RANK1:R_RANK2 + 1].astype(jnp.int32)
    tiles = (cnt[0, :N_EXPERTS].astype(jnp.int32) + tm - 1) // tm
    ends = jnp.cumsum(tiles)
    starts = ends - tiles
    ids = jnp.arange(N_EXPERTS, dtype=jnp.int32)
    start_of = jnp.sum(jnp.where(expert[:, :, None] == ids, starts * tm, 0), axis=-1)
    pos = start_of + rank
    n_used = ends[-1:]
    tile_ids = jnp.arange(n_tiles, dtype=jnp.int32)
    tile_expert = jnp.sum((tile_ids[:, None] >= ends[None, :]).astype(jnp.int32), axis=1)
    last_expert = jnp.max(jnp.where(tiles > 0, ids, 0))
    tile_expert = jnp.where(tile_ids < n_used, tile_expert, last_expert)
    pos0, pos1 = pos[:, 0], pos[:, 1]
    xs = _dispatch(pos0, pos1, h_rows, n_tiles * tm)
    y = _moe_ffn(tile_expert, n_used, xs, w1, w3, w2, j)
    return _combine(pos0, pos1, x, mods3, route, y, latent, seq_len)


def _ffn_kernel(x_ref, g_ref, mod_ref, w1_ref, w3_ref, w2_ref, o_ref, h_s, acc, *, n_f):
    d = D_MODEL
    f = pl.program_id(1)

    @pl.when(f == 0)
    def _():
        h = _modulate(x_ref[...], g_ref[...], mod_ref[:, 3 * d:4 * d], mod_ref[:, 4 * d:5 * d])
        h_s[...] = h.astype(BF16)
        acc[...] = jnp.zeros(acc.shape, F32)

    h = h_s[...]
    a = jnp.dot(h, w1_ref[...], preferred_element_type=F32)
    b = jnp.dot(h, w3_ref[...], preferred_element_type=F32)
    acc[...] += _dot(_silu(a) * b, w2_ref[...])

    @pl.when(f == n_f - 1)
    def _():
        o_ref[...] = x_ref[...] + mod_ref[:, 5 * d:6 * d] * acc[...]


def _ffn(x, norm_g3, mods3, w1, w3, w2, layer, j, latent, seq_len):
    t_tokens = x.shape[0]
    d_ff = w1.shape[-1]
    tm = 512
    tf = d_ff // 2
    n_f = d_ff // tf
    row = _mod_row(latent, tm, seq_len)
    return pl.pallas_call(
        functools.partial(_ffn_kernel, n_f=n_f),
        out_shape=jax.ShapeDtypeStruct((t_tokens, D_MODEL), F32),
        grid=(t_tokens // tm, n_f),
        in_specs=[pl.BlockSpec((tm, D_MODEL), lambda t, f: (t, 0)),
                  pl.BlockSpec((None, 1, D_MODEL), lambda t, f: (layer, 0, 0)),
                  pl.BlockSpec((None, 1, 6 * D_MODEL), lambda t, f: (row(t), 0, 0)),
                  pl.BlockSpec((None, D_MODEL, tf), lambda t, f: (j, 0, f)),
                  pl.BlockSpec((None, D_MODEL, tf), lambda t, f: (j, 0, f)),
                  pl.BlockSpec((None, tf, D_MODEL), lambda t, f: (j, f, 0))],
        out_specs=pl.BlockSpec((tm, D_MODEL), lambda t, f: (t, 0)),
        scratch_shapes=[pltpu.VMEM((tm, D_MODEL), BF16), pltpu.VMEM((tm, D_MODEL), F32)],
        compiler_params=_params(("arbitrary", "arbitrary")),
        name="ffn",
    )(x, norm_g3, mods3, w1, w3, w2)


def _rope_tables():
    half = A_QK // 2
    nf = half // 2
    t = np.arange(DEC_SEQ)
    pos = np.stack([(t // GRID_W), (t % GRID_W)], axis=1).astype(np.float32)
    inv = jnp.asarray(ROPE_BASE, F32) ** (-jnp.arange(nf, dtype=F32) / nf)
    lane = np.arange(GW) % A_QK
    which = (lane >= half).astype(np.int32)
    freq = lane % nf
    ang = jnp.asarray(pos)[:, which] * inv[freq][None, :]
    return jnp.cos(ang), jnp.sin(ang)


def _tile_vec(v, reps):
    return jnp.tile(v, (1, reps))[:, None, :]


def _prepare_params(w_in, w_out, a_qnorm, a_knorm, a_lam, a_subln, b_gate, b_gate_bias, b_onorm,
                    c_dw, c_dw_b, c_ln_g, c_ln_b, d_qnorm, d_knorm):
    lr0 = 1536
    lr1 = lr0 + 2 * B_GATE_RANK
    pad = PROJ_W - w_in.shape[-1]
    w_in_p = jnp.concatenate([w_in[:, :, :lr0], w_in[:, :, lr1:], w_in[:, :, lr0:lr1],
                              jnp.zeros(w_in.shape[:2] + (pad,), w_in.dtype)], axis=-1).astype(BF16)
    hk = B_HEADS * B_DK
    wg = jnp.zeros((DEPTH, 2, GW, hk), F32)
    wg = wg.at[:, 0, 0:B_GATE_RANK].set(b_gate[:, 0])
    wg = wg.at[:, 1, B_GATE_RANK:2 * B_GATE_RANK].set(b_gate[:, 1])
    return dict(
        w_in=w_in_p, w_out=w_out.astype(BF16),
        a_qnorm=_tile_vec(a_qnorm, GW // A_QK), a_knorm=_tile_vec(a_knorm, GW // A_QK),
        a_lam=a_lam, a_subln=_tile_vec(a_subln, GW // A_V),
        b_gate=wg.astype(BF16), b_gate_bias=b_gate_bias[:, :, None, :], b_onorm=_tile_vec(b_onorm, GW // B_DV),
        c_dw=c_dw, c_dw_b=c_dw_b[:, None, :], c_ln_g=c_ln_g[:, None, :], c_ln_b=c_ln_b[:, None, :],
        d_qnorm=_tile_vec(d_qnorm, GW // D_HEAD), d_knorm=_tile_vec(d_knorm, GW // D_HEAD),
    )


def _lambda_init(i):
    return 0.8 - 0.6 * math.exp(-0.3 * i)


def kernel(x_prompt, x_sample, cache_a_k, cache_a_v, state_b, cache_d_k, cache_d_v, c, c_ctx,
           norm1_g, norm2_g, w_ada, b_ada, w_in, w_out, a_qnorm, a_knorm, a_lam, a_subln,
           b_gate, b_gate_bias, b_onorm, c_dw, c_dw_b, c_ln_g, c_ln_b, d_qnorm, d_knorm, d_rpb,
           ffn_w1, ffn_w3, ffn_w2, moe_router, moe_router_b, moe_w1, moe_w3, moe_w2):
    lp = _prepare_params(w_in, w_out, a_qnorm, a_knorm, a_lam, a_subln, b_gate, b_gate_bias, b_onorm,
                         c_dw, c_dw_b, c_ln_g, c_ln_b, d_qnorm, d_knorm)
    norm1 = norm1_g[:, None, :]
    norm2 = norm2_g[:, None, :]
    b_ada3 = b_ada[:, None, :]
    cond8 = jnp.concatenate([c_ctx[None, :], c, jnp.zeros((8 - 1 - DEC_BATCH, D_MODEL), F32)], axis=0)
    rope = _rope_tables()
    ffn_w = [w.astype(BF16) for w in (ffn_w1, ffn_w3, ffn_w2)]
    lanes = 128
    router_p = jnp.pad(moe_router, ((0, 0), (0, 0), (0, lanes - N_EXPERTS)))
    router_b_p = jnp.pad(moe_router_b, ((0, 0), (0, lanes - N_EXPERTS)))[:, None, :]
    ck_a = cache_a_k.reshape(DEC_BATCH, DEPTH, PAST_LEN, GW)
    cv_a = cache_a_v.reshape(DEC_BATCH, DEPTH, PAST_LEN, GW)
    ck_d = cache_d_k.reshape(DEC_BATCH, DEPTH, PAST_LEN, GW)
    cv_d = cache_d_v.reshape(DEC_BATCH, DEPTH, PAST_LEN, GW)
    s0_lat = _state_to_kernel_layout(state_b.transpose(1, 0, 2, 3, 4, 5).reshape(
        DEPTH * DEC_BATCH, 2, B_HEADS, B_DK, B_DV)).reshape(DEPTH, DEC_BATCH, 2, GW, B_HEADS * B_DK)

    def channel_mixer(x, mods3, i, latent, seq_len):
        j = i // 2
        if i % 2 == 0:
            return _ffn(x, norm2, mods3, *ffn_w, i, j, latent, seq_len)
        return _moe(x, norm2, mods3, router_p, router_b_p, moe_w1, moe_w3, moe_w2, i, j, latent, seq_len)

    xc = x_prompt.reshape(BATCH * SEQ, D_MODEL)
    xl = x_sample.reshape(DEC_BATCH * DEC_SEQ, D_MODEL)
    new_ak, new_av, new_sb, new_dk, new_dv = [], [], [], [], []
    for i in range(DEPTH):
        lam_init = _lambda_init(i)
        mods3 = _adaln(cond8, w_ada, b_ada3, i)[:, None, :]

        proj = _inproj(xc, norm1, mods3, lp["w_in"], i, False, SEQ)
        o_a, ak = _attn_a(proj, lp, i, False, BATCH, SEQ, lam_init)
        o_f, s_f = _gla(proj, lp, i, BATCH, SEQ, False)
        o_b, s_b = _gla(proj, lp, i, BATCH, SEQ, True, o_fwd=o_f)
        o_c = _conv(proj, lp, i, BATCH, SEQ)
        o_d, dk = _attn_d_ctx(proj, lp, i, BATCH, SEQ)
        xc = _outproj(o_a, o_b, o_c, o_d, xc, mods3, lp["w_out"], i, False, SEQ)
        xc = channel_mixer(xc, mods3, i, False, SEQ)
        new_ak.append(ak.reshape(BATCH, SEQ, A_HEADS, 2 * A_QK))
        new_av.append(proj[:, BLK_AV * GW:(BLK_AV + 1) * GW].reshape(BATCH, SEQ, A_HEADS, A_V))
        new_sb.append(jnp.stack([_state_from_kernel_layout(s_f), _state_from_kernel_layout(s_b)], axis=1))
        new_dk.append(dk.reshape(BATCH, SEQ, D_HEADS, D_HEAD))
        new_dv.append(proj[:, BLK_DV * GW:(BLK_DV + 1) * GW].reshape(BATCH, SEQ, D_HEADS, D_HEAD))

        proj = _inproj(xl, norm1, mods3, lp["w_in"], i, True, DEC_SEQ)
        o_a, = _attn_a(proj, lp, i, True, DEC_BATCH, DEC_SEQ, lam_init, ck_a, cv_a, rope)
        o_f, _ = _gla(proj, lp, i, DEC_BATCH, DEC_SEQ, False, s0=s0_lat[i])
        o_b, _ = _gla(proj, lp, i, DEC_BATCH, DEC_SEQ, True, s0=s0_lat[i], o_fwd=o_f)
        o_c = _conv(proj, lp, i, DEC_BATCH, DEC_SEQ)
        o_d = _attn_d_lat(proj, lp, i, ck_d, cv_d, _na_bias_columns(d_rpb[i]))
        xl = _outproj(o_a, o_b, o_c, o_d, xl, mods3, lp["w_out"], i, True, DEC_SEQ)
        xl = channel_mixer(xl, mods3, i, True, DEC_SEQ)

    return (xc.reshape(BATCH, SEQ, D_MODEL), xl.reshape(DEC_BATCH, DEC_SEQ, D_MODEL),
            jnp.stack(new_ak, axis=1), jnp.stack(new_av, axis=1), jnp.stack(new_sb, axis=1),
            jnp.stack(new_dk, axis=1), jnp.stack(new_dv, axis=1))
```

```python
import functools
import math

import numpy as np
import jax
import jax.numpy as jnp
from jax import lax
from jax.experimental import pallas as pl
from jax.experimental.pallas import tpu as pltpu

F32 = jnp.float32
BF16 = jnp.bfloat16
MIX_DTYPE = BF16

D_MODEL = 1024
BATCH = 16
SEQ = 256
DEPTH = 2
DEC_BATCH = 4
DEC_SEQ = 2048
PAST_LEN = 512
GRID_W = 64
A_HEADS = 4
A_QK = 32
A_V = 64
B_HEADS = 4
B_DK = 32
B_DV = 64
B_GATE_RANK = 16
B_TAU = 16.0
C_CH = 256
C_KSIZE = 31
D_HEADS = 4
D_HEAD = 64
NA_ROWS = 8
NA_COLS = 16
NA_QROWS = 4
NA_KROWS = 12
ROPE_BASE = 10000.0
D_FF = 2816
N_EXPERTS = 8
D_FF_EXPERT = 3584
EPS = 1e-6
NEG_INF = -1e30
LOG2_E = math.log2(math.e)

GW = 256
N_PROJ_BLOCKS = 12
PROJ_W = GW * N_PROJ_BLOCKS
BLK_AQ, BLK_AK, BLK_AV, BLK_BQK, BLK_BV, BLK_BG = 0, 1, 2, 3, 4, 5
BLK_C = 3
BLK_DQ, BLK_DK, BLK_DV, BLK_LR = 8, 9, 10, 11

GLA_SUB = 16
GLA_ROWS = 256
VMEM_LIMIT = 56 * 1024 * 1024


def _params(sem, vmem=VMEM_LIMIT):
    return pltpu.CompilerParams(dimension_semantics=sem, vmem_limit_bytes=vmem)


def _dot(a, b):
    return jnp.dot(a.astype(BF16), b.astype(BF16), preferred_element_type=F32)


def _dot_nt(a, b):
    return lax.dot_general(a.astype(BF16), b.astype(BF16), (((1,), (1,)), ((), ())),
                           preferred_element_type=F32)


def _dot_exact_rhs(x, m):
    hi = x.astype(BF16)
    lo = (x - hi.astype(F32)).astype(BF16)
    return (jnp.dot(hi, m, preferred_element_type=F32) + jnp.dot(lo, m, preferred_element_type=F32))


def _sigmoid(x):
    return 1.0 / (1.0 + jnp.exp(-x))


def _silu(x):
    return x * _sigmoid(x)


def _seg_matrix(n, seg):
    r = lax.broadcasted_iota(jnp.int32, (n, n), 0)
    c = lax.broadcasted_iota(jnp.int32, (n, n), 1)
    return jnp.where((r ^ c) < seg, 1.0, 0.0).astype(BF16)


def _seg_rmsnorm(x, seg, w):
    ms = _dot_exact_rhs(x * x, _seg_matrix(x.shape[-1], seg)) * (1.0 / seg)
    return x * lax.rsqrt(ms + EPS) * w


def _lane_mask(shape, lo, width):
    lane = lax.broadcasted_iota(jnp.int32, shape, len(shape) - 1)
    return (lane >= lo) & (lane < lo + width)


def _stack_heads(q, heads, width):
    return jnp.concatenate([jnp.where(_lane_mask(q.shape, h * width, width), q, 0.0) for h in range(heads)],
                           axis=0)


def _unstack_heads(o_all, heads, width):
    n = o_all.shape[0] // heads
    o = jnp.zeros((n, o_all.shape[1]), F32)
    for h in range(heads):
        blk = o_all[h * n:(h + 1) * n]
        o = jnp.where(_lane_mask(blk.shape, h * width, width), blk, o)
    return o


def _modulate(x, g, shift, scale):
    y = x * lax.rsqrt(jnp.mean(x * x, axis=-1, keepdims=True) + EPS)
    return y * g * (1.0 + scale) + shift


def _rope(x, cos, sin):
    lane = lax.broadcasted_iota(jnp.int32, x.shape, 1)
    w = x.shape[1]
    rot = jnp.where((lane & 15) < 8, -pltpu.roll(x, w - 8, 1), pltpu.roll(x, 8, 1))
    return x * cos + rot * sin


def _adaln_kernel(c_ref, w_ref, b_ref, o_ref):
    o_ref[...] = _dot(_silu(c_ref[...]), w_ref[...]) + b_ref[...]


def _adaln(cond8, w_ada, b_ada3, layer):
    tn = 1536
    n = 6 * D_MODEL
    return pl.pallas_call(
        _adaln_kernel,
        out_shape=jax.ShapeDtypeStruct((8, n), F32),
        grid=(n // tn,),
        in_specs=[pl.BlockSpec((8, D_MODEL), lambda j: (0, 0)),
                  pl.BlockSpec((None, D_MODEL, tn), lambda j: (layer, 0, j)),
                  pl.BlockSpec((None, 1, tn), lambda j: (layer, 0, j))],
        out_specs=pl.BlockSpec((8, tn), lambda j: (0, j)),
        compiler_params=_params(("arbitrary",)),
        name="adaln",
    )(cond8, w_ada, b_ada3)


def _mod_row(latent, tm, seq_len):
    if latent:
        return lambda t: 1 + (t * tm) // seq_len
    return lambda t: 0


def _inproj_kernel(x_ref, g_ref, mod_ref, w_ref, o_ref):
    d = D_MODEL
    h = _modulate(x_ref[...], g_ref[...], mod_ref[:, 0:d], mod_ref[:, d:2 * d])
    o_ref[...] = _dot(h, w_ref[...])


def _inproj(x, norm_g3, mods3, w_in, layer, latent, seq_len):
    t_tokens = x.shape[0]
    tm = 512
    row = _mod_row(latent, tm, seq_len)
    return pl.pallas_call(
        _inproj_kernel,
        out_shape=jax.ShapeDtypeStruct((t_tokens, PROJ_W), F32),
        grid=(t_tokens // tm,),
        in_specs=[pl.BlockSpec((tm, D_MODEL), lambda t: (t, 0)),
                  pl.BlockSpec((None, 1, D_MODEL), lambda t: (layer, 0, 0)),
                  pl.BlockSpec((None, 1, 6 * D_MODEL), lambda t: (row(t), 0, 0)),
                  pl.BlockSpec((None, D_MODEL, PROJ_W), lambda t: (layer, 0, 0))],
        out_specs=pl.BlockSpec((tm, PROJ_W), lambda t: (t, 0)),
        compiler_params=_params(("arbitrary",)),
        name="inproj",
    )(x, norm_g3, mods3, w_in)


def _diff_lambda(lam_ref, lam_init):
    lam = lam_ref[...]
    s1 = jnp.sum(lam[0:1] * lam[1:2], axis=1, keepdims=True)
    s2 = jnp.sum(lam[2:3] * lam[3:4], axis=1, keepdims=True)
    return jnp.exp(s1) - jnp.exp(s2) + lam_init


def _values_and_ones(v):
    one = jnp.where(lax.broadcasted_iota(jnp.int32, (v.shape[0], A_V), 1) == 0, 1.0, 0.0)
    parts = []
    for h in range(A_HEADS):
        parts += [v[:, h * A_V:(h + 1) * A_V], one]
    return jnp.concatenate(parts, axis=1).astype(BF16)


def _attn_a_kernel(*refs, n_own, n_ctx, tq, latent, lam_init):
    if latent:
        (q_ref, k_ref, v_ref, ck_ref, cv_ref, cos_ref, sin_ref, qn_ref, kn_ref, lam_ref, sub_ref,
         o_ref, ks, vs) = refs
    else:
        (q_ref, k_ref, v_ref, qn_ref, kn_ref, lam_ref, sub_ref, o_ref, ko_ref, ks, vs) = refs
    t = pl.program_id(1)

    @pl.when(t == 0)
    def _():
        kn = _seg_rmsnorm(k_ref[...], A_QK, kn_ref[...])
        if latent:
            kn = _rope(kn, cos_ref[...], sin_ref[...])
            ks[n_own:n_own + n_ctx, :] = ck_ref[...].astype(BF16)
            vs[n_own:n_own + n_ctx, :] = _values_and_ones(cv_ref[...])
        else:
            ko_ref[...] = kn
        ks[0:n_own, :] = kn.astype(BF16)
        vs[0:n_own, :] = _values_and_ones(v_ref[...])

    qn = _seg_rmsnorm(q_ref[...], A_QK, qn_ref[...])
    if latent:
        r0 = pl.multiple_of(t * tq, tq)
        qn = _rope(qn, cos_ref[pl.ds(r0, tq), :], sin_ref[pl.ds(r0, tq), :])
    qn = qn * (A_QK ** -0.5 * LOG2_E)
    lam = _diff_lambda(lam_ref, lam_init)
    k_all = ks[...]
    heads = []
    for h in range(A_HEADS):
        v_h = vs[:, h * 2 * A_V:(h + 1) * 2 * A_V]
        qm = jnp.concatenate([jnp.where(_lane_mask(qn.shape, (2 * h + m) * A_QK, A_QK), qn, 0.0)
                              for m in range(2)], axis=0)
        s = _dot_nt(qm, k_all)
        e = jnp.exp2(s - jnp.max(s, axis=-1, keepdims=True))
        acc = _dot(e, v_h)
        acc0, acc1 = acc[:tq], acc[tq:]
        inv0 = 1.0 / acc0[:, A_V:A_V + 1]
        inv1 = lam / acc1[:, A_V:A_V + 1]
        heads.append((acc0 * inv0 - acc1 * inv1)[:, :A_V])
    o = jnp.concatenate(heads, axis=1)
    o_ref[...] = (_seg_rmsnorm(o, A_V, sub_ref[...]) * (1.0 - lam_init)).astype(o_ref.dtype)


def _attn_a(proj, lp, layer, latent, nseq, seq_len, lam_init, cache_k=None, cache_v=None, rope=None):
    tq = min(512, seq_len)
    nt = seq_len // tq
    n_ctx = PAST_LEN if latent else 0
    kern = functools.partial(_attn_a_kernel, n_own=seq_len, n_ctx=n_ctx, tq=tq, latent=latent,
                             lam_init=lam_init)
    vec = lambda name: pl.BlockSpec((None, 1, GW), lambda b, t: (layer, 0, 0))
    in_specs = [pl.BlockSpec((tq, GW), lambda b, t: (b * nt + t, BLK_AQ)),
                pl.BlockSpec((seq_len, GW), lambda b, t: (b, BLK_AK)),
                pl.BlockSpec((seq_len, GW), lambda b, t: (b, BLK_AV))]
    args = [proj, proj, proj]
    if latent:
        in_specs += [pl.BlockSpec((None, None, PAST_LEN, GW), lambda b, t: (b, layer, 0, 0)),
                     pl.BlockSpec((None, None, PAST_LEN, GW), lambda b, t: (b, layer, 0, 0)),
                     pl.BlockSpec((seq_len, GW), lambda b, t: (0, 0)),
                     pl.BlockSpec((seq_len, GW), lambda b, t: (0, 0))]
        args += [cache_k, cache_v, rope[0], rope[1]]
    in_specs += [vec("q"), vec("k"),
                 pl.BlockSpec((None, 4, A_QK), lambda b, t: (layer, 0, 0)),
                 vec("s")]
    args += [lp["a_qnorm"], lp["a_knorm"], lp["a_lam"], lp["a_subln"]]
    out_shape = [jax.ShapeDtypeStruct((nseq * seq_len, GW), MIX_DTYPE)]
    out_specs = [pl.BlockSpec((tq, GW), lambda b, t: (b * nt + t, 0))]
    if not latent:
        out_shape.append(jax.ShapeDtypeStruct((nseq * seq_len, GW), F32))
        out_specs.append(pl.BlockSpec((seq_len, GW), lambda b, t: (b, 0)))
    return pl.pallas_call(
        kern,
        out_shape=out_shape,
        grid=(nseq, nt),
        in_specs=in_specs,
        out_specs=out_specs,
        scratch_shapes=[pltpu.VMEM((seq_len + n_ctx, GW), BF16), pltpu.VMEM((seq_len + n_ctx, 2 * GW), BF16)],
        compiler_params=_params(("arbitrary", "arbitrary")),
        name="attn_a_lat" if latent else "attn_a_ctx",
    )(*args)


def _attn_d_ctx_kernel(q_ref, k_ref, v_ref, qn_ref, kn_ref, o_ref, ko_ref):
    kn = _seg_rmsnorm(k_ref[...], D_HEAD, kn_ref[...])
    ko_ref[...] = kn
    qn = _seg_rmsnorm(q_ref[...], D_HEAD, qn_ref[...]) * (D_HEAD ** -0.5)
    s = _dot_nt(_stack_heads(qn, D_HEADS, D_HEAD), kn)
    e = jnp.exp(s - jnp.max(s, axis=-1, keepdims=True))
    o_all = _dot(e, v_ref[...]) * (1.0 / jnp.sum(e, axis=-1, keepdims=True))
    o_ref[...] = _unstack_heads(o_all, D_HEADS, D_HEAD).astype(o_ref.dtype)


def _attn_d_ctx(proj, lp, layer, nseq, seq_len):
    vec = pl.BlockSpec((None, 1, GW), lambda b: (layer, 0, 0))
    blk = lambda c: pl.BlockSpec((seq_len, GW), lambda b: (b, c))
    return pl.pallas_call(
        _attn_d_ctx_kernel,
        out_shape=[jax.ShapeDtypeStruct((nseq * seq_len, GW), MIX_DTYPE),
                   jax.ShapeDtypeStruct((nseq * seq_len, GW), F32)],
        grid=(nseq,),
        in_specs=[blk(BLK_DQ), blk(BLK_DK), blk(BLK_DV), vec, vec],
        out_specs=[blk(0), blk(0)],
        compiler_params=_params(("arbitrary",)),
        name="attn_d_ctx",
    )(proj, proj, proj, lp["d_qnorm"], lp["d_knorm"])


def _attn_d_lat_kernel(q_ref, k_ref, v_ref, ck_ref, cv_ref, qn_ref, kn_ref, cols_ref, o_ref,
                       ks, vs, cks, cvs, bias_ref):
    g = pl.program_id(1)
    n_rows = DEC_SEQ // GRID_W
    n_groups = n_rows // NA_QROWS
    n_loc = NA_KROWS * GRID_W

    @pl.when((pl.program_id(0) == 0) & (g == 0))
    def _():
        _fill_na_bias(cols_ref, bias_ref)

    @pl.when(g == 0)
    def _():
        ks[...] = _seg_rmsnorm(k_ref[...], D_HEAD, kn_ref[...]).astype(BF16)
        vs[...] = v_ref[...].astype(BF16)
        cks[...] = ck_ref[...].astype(BF16)
        cvs[...] = cv_ref[...].astype(BF16)

    qn = _seg_rmsnorm(q_ref[...], D_HEAD, qn_ref[...]) * (D_HEAD ** -0.5)
    row_start = jnp.clip(g * NA_QROWS - NA_ROWS // 2, 0, n_rows - NA_KROWS)
    variant = jnp.where(g == 0, 0, jnp.where(g == n_groups - 1, 2, 1))
    k0 = pl.multiple_of(row_start * GRID_W, GRID_W)
    kl = ks[pl.ds(k0, n_loc), :]
    vl = vs[pl.ds(k0, n_loc), :]
    kc = cks[...]
    vc = cvs[...]
    o = jnp.zeros(qn.shape, F32)
    for h in range(D_HEADS):
        hm = _lane_mask(qn.shape, h * D_HEAD, D_HEAD)
        qm = jnp.where(hm, qn, 0.0)
        s_loc = _dot_nt(qm, kl) + bias_ref[h, variant]
        s_ctx = _dot_nt(qm, kc)
        mx = jnp.maximum(jnp.max(s_loc, axis=-1, keepdims=True), jnp.max(s_ctx, axis=-1, keepdims=True))
        e_loc = jnp.exp(s_loc - mx)
        e_ctx = jnp.exp(s_ctx - mx)
        inv = 1.0 / (jnp.sum(e_loc, axis=-1, keepdims=True) + jnp.sum(e_ctx, axis=-1, keepdims=True))
        o = o + jnp.where(hm, (_dot(e_loc, vl) + _dot(e_ctx, vc)) * inv, 0.0)
    o_ref[...] = o.astype(o_ref.dtype)


def _na_bias_rows():
    n_rows = DEC_SEQ // GRID_W
    n_groups = n_rows // NA_QROWS
    plan = []
    for g in (0, 1, n_groups - 1):
        first_key_row = int(np.clip(g * NA_QROWS - NA_ROWS // 2, 0, n_rows - NA_KROWS))
        per_query_row = []
        for a in range(NA_QROWS):
            r = g * NA_QROWS + a
            win = int(np.clip(r - NA_ROWS // 2, 0, n_rows - NA_ROWS))
            per_query_row.append([first_key_row + i - r + NA_ROWS - 1
                                  if win <= first_key_row + i < win + NA_ROWS else None
                                  for i in range(NA_KROWS)])
        plan.append(per_query_row)
    return plan


def _fill_na_bias(cols_ref, bias_s):
    lane = lax.broadcasted_iota(jnp.int32, (GRID_W, 2 * GRID_W), 1)
    masked = jnp.full((GRID_W, 2 * GRID_W), NEG_INF, F32)
    for h in range(D_HEADS):
        for v, per_query_row in enumerate(_na_bias_rows()):
            for a, rows in enumerate(per_query_row):
                for p in range(NA_KROWS // 2):
                    left, right = rows[2 * p], rows[2 * p + 1]
                    lhs = masked if left is None else cols_ref[h, left]
                    rhs = masked if right is None else cols_ref[h, right]
                    bias_s[h, v, a * GRID_W:(a + 1) * GRID_W, 2 * p * GRID_W:2 * (p + 1) * GRID_W] = (
                        jnp.where(lane < GRID_W, lhs, rhs))


def _na_bias_columns(rpb):
    q = np.arange(GRID_W)
    kcol = np.arange(GRID_W)
    cs = np.clip(q - NA_COLS // 2, 0, GRID_W - NA_COLS)
    in_win = (kcol[None, :] >= cs[:, None]) & (kcol[None, :] < cs[:, None] + NA_COLS)
    dc = np.clip(kcol[None, :] - q[:, None] + (NA_COLS - 1), 0, 2 * NA_COLS - 2)
    pick_col = (dc[None] == np.arange(2 * NA_COLS - 1)[:, None, None]).astype(np.float32)
    cols = jnp.einsum("hrc,cqk->hrqk", rpb.astype(F32), jnp.asarray(pick_col), precision=lax.Precision.HIGHEST)
    cols = jnp.where(in_win[None, None], cols, NEG_INF)
    return jnp.concatenate([cols, cols], axis=-1)


def _attn_d_lat(proj, lp, layer, cache_k, cache_v, bias_cols):
    n_groups = DEC_SEQ // GRID_W // NA_QROWS
    tq = NA_QROWS * GRID_W
    vec = pl.BlockSpec((None, 1, GW), lambda b, r: (layer, 0, 0))
    seq = lambda c: pl.BlockSpec((DEC_SEQ, GW), lambda b, r: (b, c))
    cache = pl.BlockSpec((None, None, PAST_LEN, GW), lambda b, r: (b, layer, 0, 0))
    return pl.pallas_call(
        _attn_d_lat_kernel,
        out_shape=jax.ShapeDtypeStruct((DEC_BATCH * DEC_SEQ, GW), MIX_DTYPE),
        grid=(DEC_BATCH, n_groups),
        in_specs=[pl.BlockSpec((tq, GW), lambda b, r: (b * n_groups + r, BLK_DQ)),
                  seq(BLK_DK), seq(BLK_DV), cache, cache, vec, vec,
                  pl.BlockSpec(bias_cols.shape, lambda b, r: (0, 0, 0, 0))],
        out_specs=pl.BlockSpec((tq, GW), lambda b, r: (b * n_groups + r, 0)),
        scratch_shapes=[pltpu.VMEM((DEC_SEQ, GW), BF16), pltpu.VMEM((DEC_SEQ, GW), BF16),
                        pltpu.VMEM((PAST_LEN, GW), BF16), pltpu.VMEM((PAST_LEN, GW), BF16),
                        pltpu.VMEM((D_HEADS, 3, tq, NA_KROWS * GRID_W), F32)],
        compiler_params=_params(("arbitrary", "arbitrary")),
        name="attn_d_lat",
    )(proj, proj, proj, cache_k, cache_v, lp["d_qnorm"], lp["d_knorm"], bias_cols)


def _conv_kernel(c_ref, w_ref, b_ref, g_ref, beta_ref, o_ref, pad, *, seq_len):
    half = C_KSIZE // 2
    top = 16
    cin = c_ref[...]
    u = cin[:, :C_CH] * _sigmoid(cin[:, C_CH:])
    pad[0:top, :] = jnp.zeros((top, C_CH), F32)
    pad[top + seq_len:top + seq_len + top, :] = jnp.zeros((top, C_CH), F32)
    pad[top:top + seq_len, :] = u
    w = w_ref[...]
    rb = 256
    for r0 in range(0, seq_len, rb):
        acc = jnp.zeros((rb, C_CH), F32)
        n_hi = -(-C_KSIZE // 8)
        for lo in range(8):
            base = r0 + top - half + lo
            shifted = pad[base:base + rb + 8 * (n_hi - 1), :]
            for hi in range(n_hi):
                k = 8 * hi + lo
                if k < C_KSIZE:
                    acc = acc + shifted[8 * hi:8 * hi + rb, :] * w[k:k + 1, :]
        acc = acc + b_ref[...]
        mu = jnp.mean(acc, axis=-1, keepdims=True)
        xc = acc - mu
        y = xc * lax.rsqrt(jnp.mean(xc * xc, axis=-1, keepdims=True) + EPS) * g_ref[...] + beta_ref[...]
        o_ref[r0:r0 + rb, :] = _silu(y).astype(o_ref.dtype)


def _conv(proj, lp, layer, nseq, seq_len):
    vec = pl.BlockSpec((None, 1, C_CH), lambda b: (layer, 0, 0))
    return pl.pallas_call(
        functools.partial(_conv_kernel, seq_len=seq_len),
        out_shape=jax.ShapeDtypeStruct((nseq * seq_len, C_CH), MIX_DTYPE),
        grid=(nseq,),
        in_specs=[pl.BlockSpec((seq_len, 2 * C_CH), lambda b: (b, BLK_C)),
                  pl.BlockSpec((None, C_KSIZE, C_CH), lambda b: (layer, 0, 0)),
                  vec, vec, vec],
        out_specs=pl.BlockSpec((seq_len, C_CH), lambda b: (b, 0)),
        scratch_shapes=[pltpu.VMEM((seq_len + 32, C_CH), F32)],
        compiler_params=_params(("arbitrary",)),
        name="conv",
    )(proj, lp["c_dw"], lp["c_dw_b"], lp["c_ln_g"], lp["c_ln_b"])


def _gla_kernel(*refs, reverse, has_s0, n_blocks):
    refs = list(refs)
    qk_ref, v_ref, lr_ref, wg_ref, gb_ref = refs[:5]
    refs = refs[5:]
    s0_ref = refs.pop(0) if has_s0 else None
    if reverse:
        of_ref, bg_ref, on_ref = refs[:3]
        refs = refs[3:]
    o_ref, so_ref, st = refs
    rb, cs = GLA_ROWS, GLA_SUB
    hk = B_HEADS * B_DK
    j = pl.program_id(1)

    @pl.when(j == 0)
    def _():
        st[...] = s0_ref[...] if has_s0 else jnp.zeros(st.shape, F32)

    qk = qk_ref[...]
    q = qk[:, :hk] * (B_DK ** -0.5)
    k = qk[:, hk:]
    v = v_ref[...]
    pre = _dot(lr_ref[...], wg_ref[...]) + gb_ref[...]
    g = (jnp.minimum(pre, 0.0) - jnp.log(1.0 + jnp.exp(-jnp.abs(pre)))) * (1.0 / B_TAU)

    pos = lax.broadcasted_iota(jnp.int32, (rb, hk), 0) & (cs - 1)
    pre_sum = g
    suf_sum = g
    step = 1
    while step < cs:
        pre_sum = pre_sum + jnp.where(pos >= step, pltpu.roll(pre_sum, step, 0), 0.0)
        suf_sum = suf_sum + jnp.where(pos < cs - step, pltpu.roll(suf_sum, rb - step, 0), 0.0)
        step *= 2
    total = pre_sum + suf_sum - g
    if reverse:
        z = suf_sum
        k_dec = k * jnp.exp(pre_sum - g)
    else:
        z = pre_sum
        k_dec = k * jnp.exp(suf_sum - g)
    q_dec = q * jnp.exp(z)

    r_i = lax.broadcasted_iota(jnp.int32, (hk, GW), 0)
    c_i = lax.broadcasted_iota(jnp.int32, (hk, GW), 1)
    head_sum = jnp.where((r_i >> 5) == (c_i >> 6), 1.0, 0.0).astype(BF16)

    row_in_sub = lax.broadcasted_iota(jnp.int32, (cs, hk), 0)
    intra = []
    for n in range(rb // cs):
        qs = q[n * cs:(n + 1) * cs]
        zs = z[n * cs:(n + 1) * cs]
        pairs = []
        for jl in range(cs):
            r = n * cs + jl
            decay = jnp.exp(jnp.minimum(zs - z[r:r + 1], 0.0))
            keep = (row_in_sub <= jl) if reverse else (row_in_sub >= jl)
            pairs.append(jnp.where(keep, qs * k[r:r + 1] * decay, 0.0))
        w = jnp.dot(jnp.concatenate(pairs, axis=0).astype(BF16), head_sum, preferred_element_type=F32)
        acc = jnp.zeros((cs, GW), F32)
        for jl in range(cs):
            r = n * cs + jl
            acc = acc + w[jl * cs:(jl + 1) * cs] * v[r:r + 1]
        intra.append(acc)
    o = jnp.concatenate(intra, axis=0)

    r_s = lax.broadcasted_iota(jnp.int32, st.shape, 0)
    c_s = lax.broadcasted_iota(jnp.int32, st.shape, 1)
    diag = (r_s >> 6) == (c_s >> 5)
    n_sub = rb // cs
    row = lax.broadcasted_iota(jnp.int32, (rb, hk), 0)

    def per_sub_chunk(x):
        return jnp.concatenate([jnp.where((row >= n * cs) & (row < (n + 1) * cs), x, 0.0).astype(BF16)
                                for n in range(n_sub)], axis=1)

    kv_all = _dot(v.T, per_sub_chunk(k_dec))
    state = st[...]
    states = [None] * n_sub
    order = range(n_sub - 1, -1, -1) if reverse else range(n_sub)
    for n in order:
        states[n] = state.astype(BF16)
        decay = jnp.exp(total[n * cs:n * cs + 1, :])
        state = state * decay + jnp.where(diag, kv_all[:, n * hk:(n + 1) * hk], 0.0)
    st[...] = state
    o = o + _dot_nt(per_sub_chunk(q_dec), jnp.concatenate(states, axis=1))

    if reverse:
        o = _seg_rmsnorm(o + of_ref[...], B_DV, on_ref[...]) * _silu(bg_ref[...])
    o_ref[...] = o.astype(o_ref.dtype)

    @pl.when(j == n_blocks - 1)
    def _():
        so_ref[...] = state


def _gla(proj, lp, layer, nseq, seq_len, reverse, s0=None, o_fwd=None):
    rb = GLA_ROWS
    nb = seq_len // rb
    hk = B_HEADS * B_DK
    blk = (lambda b, j: b * nb + nb - 1 - j) if reverse else (lambda b, j: b * nb + j)
    tile = lambda c: pl.BlockSpec((rb, GW), lambda b, j: (blk(b, j), c))
    d = 1 if reverse else 0
    in_specs = [tile(BLK_BQK), tile(BLK_BV), tile(BLK_LR),
                pl.BlockSpec((None, None, GW, hk), lambda b, j: (layer, d, 0, 0)),
                pl.BlockSpec((None, None, 1, hk), lambda b, j: (layer, d, 0, 0))]
    args = [proj, proj, proj, lp["b_gate"], lp["b_gate_bias"]]
    if s0 is not None:
        in_specs.append(pl.BlockSpec((None, None, GW, hk), lambda b, j: (b, d, 0, 0)))
        args.append(s0)
    if reverse:
        in_specs += [tile(0), tile(BLK_BG), pl.BlockSpec((None, 1, GW), lambda b, j: (layer, 0, 0))]
        args += [o_fwd, proj, lp["b_onorm"]]
    return pl.pallas_call(
        functools.partial(_gla_kernel, reverse=reverse, has_s0=s0 is not None, n_blocks=nb),
        out_shape=[jax.ShapeDtypeStruct((nseq * seq_len, GW), MIX_DTYPE if reverse else F32),
                   jax.ShapeDtypeStruct((nseq, GW, hk), F32)],
        grid=(nseq, nb),
        in_specs=in_specs,
        out_specs=[tile(0), pl.BlockSpec((None, GW, hk), lambda b, j: (b, 0, 0))],
        scratch_shapes=[pltpu.VMEM((GW, hk), F32)],
        compiler_params=_params(("arbitrary", "arbitrary")),
        name="gla_bwd" if reverse else "gla_fwd",
    )(*args)


def _state_to_kernel_layout(s):
    eye = jnp.eye(B_HEADS, dtype=s.dtype)
    t = jnp.einsum("bxhde,hg->bxhegd", s, eye)
    return t.reshape(s.shape[0], 2, B_HEADS * B_DV, B_HEADS * B_DK)


def _state_from_kernel_layout(st):
    t = st.reshape(st.shape[0], B_HEADS, B_DV, B_HEADS, B_DK)
    return jnp.stack([t[:, h, :, h, :] for h in range(B_HEADS)], axis=1).transpose(0, 1, 3, 2)


def _outproj_kernel(a_ref, b_ref, c_ref, d_ref, x_ref, mod_ref, w_ref, o_ref):
    mix = jnp.concatenate([a_ref[...], b_ref[...], c_ref[...], d_ref[...]], axis=-1)
    gate = mod_ref[:, 2 * D_MODEL:3 * D_MODEL]
    o_ref[...] = x_ref[...] + gate * _dot(mix, w_ref[...])


def _outproj(o_a, o_b, o_c, o_d, x, mods3, w_out, layer, latent, seq_len):
    t_tokens = x.shape[0]
    tm = 512
    row = _mod_row(latent, tm, seq_len)
    part = pl.BlockSpec((tm, GW), lambda t: (t, 0))
    return pl.pallas_call(
        _outproj_kernel,
        out_shape=jax.ShapeDtypeStruct((t_tokens, D_MODEL), F32),
        grid=(t_tokens // tm,),
        in_specs=[part, part, part, part,
                  pl.BlockSpec((tm, D_MODEL), lambda t: (t, 0)),
                  pl.BlockSpec((None, 1, 6 * D_MODEL), lambda t: (row(t), 0, 0)),
                  pl.BlockSpec((None, 4 * GW, D_MODEL), lambda t: (layer, 0, 0))],
        out_specs=pl.BlockSpec((tm, D_MODEL), lambda t: (t, 0)),
        compiler_params=_params(("arbitrary",)),
        name="outproj",
    )(o_a, o_b, o_c, o_d, x, mods3, w_out)


MOE_TM = 512
R_E1, R_E2, R_W1, R_W2, R_RANK1, R_RANK2 = 0, 1, 2, 3, 4, 5


def _router_kernel(x_ref, g_ref, mod_ref, r_ref, rb_ref, h_ref, route_ref, cnt_ref, carry):
    d = D_MODEL

    @pl.when(pl.program_id(0) == 0)
    def _():
        carry[...] = jnp.zeros(carry.shape, F32)

    h = _modulate(x_ref[...], g_ref[...], mod_ref[:, 3 * d:4 * d], mod_ref[:, 4 * d:5 * d])
    tm = h.shape[0]
    h_ref[...] = h
    r = r_ref[...]
    h_hi = h.astype(BF16)
    h_lo = (h - h_hi.astype(F32)).astype(BF16)
    r_hi = r.astype(BF16)
    r_lo = (r - r_hi.astype(F32)).astype(BF16)
    dot = lambda a, b: jnp.dot(a, b, preferred_element_type=F32)
    logits = dot(h_hi, r_hi) + dot(h_lo, r_hi) + dot(h_hi, r_lo) + rb_ref[...]
    lane = lax.broadcasted_iota(jnp.int32, logits.shape, 1).astype(F32)
    big = float(logits.shape[1])
    logits = jnp.where(lane < N_EXPERTS, logits, -jnp.inf)
    v1 = jnp.max(logits, axis=-1, keepdims=True)
    i1 = jnp.min(jnp.where(logits == v1, lane, big), axis=-1, keepdims=True)
    rest = jnp.where(lane == i1, -jnp.inf, logits)
    v2 = jnp.max(rest, axis=-1, keepdims=True)
    i2 = jnp.min(jnp.where(rest == v2, lane, big), axis=-1, keepdims=True)
    e2 = jnp.exp(v2 - v1)
    inv = 1.0 / (1.0 + e2)
    sel = jnp.where((lane == i1) | (lane == i2), 1.0, 0.0)
    r_i = lax.broadcasted_iota(jnp.int32, (tm, tm), 0)
    c_i = lax.broadcasted_iota(jnp.int32, (tm, tm), 1)
    before = jnp.where(c_i < r_i, 1.0, 0.0).astype(BF16)
    rank = jnp.dot(before, sel.astype(BF16), preferred_element_type=F32) + carry[...]
    count = carry[...] + jnp.sum(sel, axis=0, keepdims=True)
    carry[...] = count
    cnt_ref[...] = jnp.broadcast_to(count, cnt_ref.shape)
    rank1 = jnp.sum(jnp.where(lane == i1, rank, 0.0), axis=-1, keepdims=True)
    rank2 = jnp.sum(jnp.where(lane == i2, rank, 0.0), axis=-1, keepdims=True)
    rec = jnp.zeros(logits.shape, F32)
    for pos, val in ((R_E1, i1), (R_E2, i2), (R_W1, inv), (R_W2, e2 * inv), (R_RANK1, rank1), (R_RANK2, rank2)):
        rec = jnp.where(lane == pos, val, rec)
    route_ref[...] = rec


def _router(x, norm_g3, mods3, router, router_b, layer, j, latent, seq_len):
    t_tokens = x.shape[0]
    tm = 512
    row = _mod_row(latent, tm, seq_len)
    lanes = router.shape[-1]
    return pl.pallas_call(
        _router_kernel,
        out_shape=[jax.ShapeDtypeStruct((t_tokens, D_MODEL), F32),
                   jax.ShapeDtypeStruct((t_tokens, lanes), F32),
                   jax.ShapeDtypeStruct((8, lanes), F32)],
        grid=(t_tokens // tm,),
        in_specs=[pl.BlockSpec((tm, D_MODEL), lambda t: (t, 0)),
                  pl.BlockSpec((None, 1, D_MODEL), lambda t: (layer, 0, 0)),
                  pl.BlockSpec((None, 1, 6 * D_MODEL), lambda t: (row(t), 0, 0)),
                  pl.BlockSpec((None, D_MODEL, lanes), lambda t: (j, 0, 0)),
                  pl.BlockSpec((None, 1, lanes), lambda t: (j, 0, 0))],
        out_specs=[pl.BlockSpec((tm, D_MODEL), lambda t: (t, 0)),
                   pl.BlockSpec((tm, lanes), lambda t: (t, 0)),
                   pl.BlockSpec((8, lanes), lambda t: (0, 0))],
        scratch_shapes=[pltpu.VMEM((1, lanes), F32)],
        compiler_params=_params(("arbitrary",)),
        name="router",
    )(x, norm_g3, mods3, router, router_b)


def _row_copy(src, src_row, dst, dst_row, sem):
    return pltpu.make_async_copy(src.at[pl.ds(src_row, 1), :], dst.at[pl.ds(dst_row, 1), :], sem)


def _dispatch_kernel(p0_ref, p1_ref, h_ref, xs_in_ref, xs_ref, sem):
    del xs_in_ref
    tm = h_ref.shape[0]
    base = pl.program_id(0) * tm

    def issue(r, carry):
        _row_copy(h_ref, r, xs_ref, p0_ref[base + r], sem.at[0]).start()
        _row_copy(h_ref, r, xs_ref, p1_ref[base + r], sem.at[1]).start()
        return carry

    def wait(r, carry):
        _row_copy(h_ref, r, xs_ref, p0_ref[base + r], sem.at[0]).wait()
        _row_copy(h_ref, r, xs_ref, p1_ref[base + r], sem.at[1]).wait()
        return carry

    lax.fori_loop(0, tm, issue, 0, unroll=8)
    lax.fori_loop(0, tm, wait, 0, unroll=8)


def _dispatch(pos0, pos1, h_rows, n_slots):
    t_tokens = pos0.shape[0]
    tm = 512
    xs0 = jnp.zeros((n_slots, D_MODEL), F32)
    return pl.pallas_call(
        _dispatch_kernel,
        out_shape=jax.ShapeDtypeStruct(xs0.shape, F32),
        grid_spec=pltpu.PrefetchScalarGridSpec(
            num_scalar_prefetch=2,
            grid=(t_tokens // tm,),
            in_specs=[pl.BlockSpec((tm, D_MODEL), lambda t, p0, p1: (t, 0)),
                      pl.BlockSpec(memory_space=pl.ANY)],
            out_specs=pl.BlockSpec(memory_space=pl.ANY),
            scratch_shapes=[pltpu.SemaphoreType.DMA((2,))]),
        input_output_aliases={3: 0},
        compiler_params=_params(("arbitrary",)),
        name="moe_dispatch",
    )(pos0, pos1, h_rows, xs0)


def _moe_ffn_kernel(te_ref, nu_ref, xs_ref, w1_ref, w3_ref, w2_ref, y_ref, xb, acc, *, n_f):
    del te_ref
    i = pl.program_id(0)
    f = pl.program_id(1)
    used = i < nu_ref[0]

    @pl.when(used & (f == 0))
    def _():
        xb[...] = xs_ref[...].astype(BF16)
        acc[...] = jnp.zeros(acc.shape, F32)

    @pl.when(used)
    def _():
        x = xb[...]
        a = jnp.dot(x, w1_ref[...].astype(BF16), preferred_element_type=F32)
        b = jnp.dot(x, w3_ref[...].astype(BF16), preferred_element_type=F32)
        acc[...] += _dot(_silu(a) * b, w2_ref[...])

    @pl.when(used & (f == n_f - 1))
    def _():
        y_ref[...] = acc[...]

    @pl.when(jnp.logical_not(used) & (f == n_f - 1))
    def _():
        y_ref[...] = jnp.zeros(y_ref.shape, F32)


def _moe_ffn(tile_expert, n_used, xs, w1, w3, w2, j):
    tm = MOE_TM
    n_tiles = xs.shape[0] // tm
    d_ff = w1.shape[-1]
    tf = 896
    n_f = d_ff // tf
    f_of = lambda i, f, nu: jnp.where(i < nu[0], f, n_f - 1)
    return pl.pallas_call(
        functools.partial(_moe_ffn_kernel, n_f=n_f),
        out_shape=jax.ShapeDtypeStruct(xs.shape, F32),
        grid_spec=pltpu.PrefetchScalarGridSpec(
            num_scalar_prefetch=2,
            grid=(n_tiles, n_f),
            in_specs=[pl.BlockSpec((tm, D_MODEL), lambda i, f, te, nu: (jnp.minimum(i, nu[0] - 1), 0)),
                      pl.BlockSpec((None, None, D_MODEL, tf), lambda i, f, te, nu: (j, te[i], 0, f_of(i, f, nu))),
                      pl.BlockSpec((None, None, D_MODEL, tf), lambda i, f, te, nu: (j, te[i], 0, f_of(i, f, nu))),
                      pl.BlockSpec((None, None, tf, D_MODEL), lambda i, f, te, nu: (j, te[i], f_of(i, f, nu), 0))],
            out_specs=pl.BlockSpec((tm, D_MODEL), lambda i, f, te, nu: (i, 0)),
            scratch_shapes=[pltpu.VMEM((tm, D_MODEL), BF16), pltpu.VMEM((tm, D_MODEL), F32)]),
        compiler_params=_params(("arbitrary", "arbitrary")),
        name="moe_ffn",
    )(tile_expert, n_used, xs, w1, w3, w2)


def _combine_kernel(p0_ref, p1_ref, x_ref, mod_ref, route_ref, y_ref, o_ref, buf0, buf1, sem):
    tm = x_ref.shape[0]
    base = pl.program_id(0) * tm

    def issue(r, carry):
        _row_copy(y_ref, p0_ref[base + r], buf0, r, sem.at[0]).start()
        _row_copy(y_ref, p1_ref[base + r], buf1, r, sem.at[1]).start()
        return carry

    def wait(r, carry):
        _row_copy(y_ref, p0_ref[base + r], buf0, r, sem.at[0]).wait()
        _row_copy(y_ref, p1_ref[base + r], buf1, r, sem.at[1]).wait()
        return carry

    lax.fori_loop(0, tm, issue, 0, unroll=8)
    lax.fori_loop(0, tm, wait, 0, unroll=8)
    route = route_ref[...]
    w_a = route[:, R_W1:R_W1 + 1]
    w_b = route[:, R_W2:R_W2 + 1]
    y = w_a * buf0[...] + w_b * buf1[...]
    o_ref[...] = x_ref[...] + mod_ref[:, 5 * D_MODEL:6 * D_MODEL] * y


def _combine(pos0, pos1, x, mods3, route, y, latent, seq_len):
    t_tokens = x.shape[0]
    tm = 512
    row = _mod_row(latent, tm, seq_len)
    return pl.pallas_call(
        _combine_kernel,
        out_shape=jax.ShapeDtypeStruct((t_tokens, D_MODEL), F32),
        grid_spec=pltpu.PrefetchScalarGridSpec(
            num_scalar_prefetch=2,
            grid=(t_tokens // tm,),
            in_specs=[pl.BlockSpec((tm, D_MODEL), lambda t, p0, p1: (t, 0)),
                      pl.BlockSpec((None, 1, 6 * D_MODEL), lambda t, p0, p1: (row(t), 0, 0)),
                      pl.BlockSpec((tm, route.shape[1]), lambda t, p0, p1: (t, 0)),
                      pl.BlockSpec(memory_space=pl.ANY)],
            out_specs=pl.BlockSpec((tm, D_MODEL), lambda t, p0, p1: (t, 0)),
            scratch_shapes=[pltpu.VMEM((tm, D_MODEL), F32), pltpu.VMEM((tm, D_MODEL), F32),
                            pltpu.SemaphoreType.DMA((2,))]),
        compiler_params=_params(("arbitrary",)),
        name="moe_combine",
    )(pos0, pos1, x, mods3, route, y)


def _moe(x, norm_g3, mods3, router, router_b, w1, w3, w2, layer, j, latent, seq_len):
    t_tokens = x.shape[0]
    tm = MOE_TM
    n_tiles = 2 * t_tokens // tm + N_EXPERTS
    h_rows, route, cnt = _router(x, norm_g3, mods3, router, router_b, layer, j, latent, seq_len)
    expert = route[:, R_E1:R_E2 + 1].astype(jnp.int32)
    rank = route[:, R_RANK1:R_RANK2 + 1].astype(jnp.int32)
    tiles = (cnt[0, :N_EXPERTS].astype(jnp.int32) + tm - 1) // tm
    ends = jnp.cumsum(tiles)
    starts = ends - tiles
    ids = jnp.arange(N_EXPERTS, dtype=jnp.int32)
    start_of = jnp.sum(jnp.where(expert[:, :, None] == ids, starts * tm, 0), axis=-1)
    pos = start_of + rank
    n_used = ends[-1:]
    tile_ids = jnp.arange(n_tiles, dtype=jnp.int32)
    tile_expert = jnp.sum((tile_ids[:, None] >= ends[None, :]).astype(jnp.int32), axis=1)
    last_expert = jnp.max(jnp.where(tiles > 0, ids, 0))
    tile_expert = jnp.where(tile_ids < n_used, tile_expert, last_expert)
    pos0, pos1 = pos[:, 0], pos[:, 1]
    xs = _dispatch(pos0, pos1, h_rows, n_tiles * tm)
    y = _moe_ffn(tile_expert, n_used, xs, w1, w3, w2, j)
    return _combine(pos0, pos1, x, mods3, route, y, latent, seq_len)


def _ffn_kernel(x_ref, g_ref, mod_ref, w1_ref, w3_ref, w2_ref, o_ref, h_s, acc, *, n_f):
    d = D_MODEL
    f = pl.program_id(1)

    @pl.when(f == 0)
    def _():
        h = _modulate(x_ref[...], g_ref[...], mod_ref[:, 3 * d:4 * d], mod_ref[:, 4 * d:5 * d])
        h_s[...] = h.astype(BF16)
        acc[...] = jnp.zeros(acc.shape, F32)

    h = h_s[...]
    a = jnp.dot(h, w1_ref[...], preferred_element_type=F32)
    b = jnp.dot(h, w3_ref[...], preferred_element_type=F32)
    acc[...] += _dot(_silu(a) * b, w2_ref[...])

    @pl.when(f == n_f - 1)
    def _():
        o_ref[...] = x_ref[...] + mod_ref[:, 5 * d:6 * d] * acc[...]


def _ffn(x, norm_g3, mods3, w1, w3, w2, layer, j, latent, seq_len):
    t_tokens = x.shape[0]
    d_ff = w1.shape[-1]
    tm = 512
    tf = d_ff // 2
    n_f = d_ff // tf
    row = _mod_row(latent, tm, seq_len)
    return pl.pallas_call(
        functools.partial(_ffn_kernel, n_f=n_f),
        out_shape=jax.ShapeDtypeStruct((t_tokens, D_MODEL), F32),
        grid=(t_tokens // tm, n_f),
        in_specs=[pl.BlockSpec((tm, D_MODEL), lambda t, f: (t, 0)),
                  pl.BlockSpec((None, 1, D_MODEL), lambda t, f: (layer, 0, 0)),
                  pl.BlockSpec((None, 1, 6 * D_MODEL), lambda t, f: (row(t), 0, 0)),
                  pl.BlockSpec((None, D_MODEL, tf), lambda t, f: (j, 0, f)),
                  pl.BlockSpec((None, D_MODEL, tf), lambda t, f: (j, 0, f)),
                  pl.BlockSpec((None, tf, D_MODEL), lambda t, f: (j, f, 0))],
        out_specs=pl.BlockSpec((tm, D_MODEL), lambda t, f: (t, 0)),
        scratch_shapes=[pltpu.VMEM((tm, D_MODEL), BF16), pltpu.VMEM((tm, D_MODEL), F32)],
        compiler_params=_params(("arbitrary", "arbitrary")),
        name="ffn",
    )(x, norm_g3, mods3, w1, w3, w2)


def _rope_tables():
    half = A_QK // 2
    nf = half // 2
    t = np.arange(DEC_SEQ)
    pos = np.stack([(t // GRID_W), (t % GRID_W)], axis=1).astype(np.float32)
    inv = jnp.asarray(ROPE_BASE, F32) ** (-jnp.arange(nf, dtype=F32) / nf)
    lane = np.arange(GW) % A_QK
    which = (lane >= half).astype(np.int32)
    freq = lane % nf
    ang = jnp.asarray(pos)[:, which] * inv[freq][None, :]
    return jnp.cos(ang), jnp.sin(ang)


def _tile_vec(v, reps):
    return jnp.tile(v, (1, reps))[:, None, :]


def _prepare_params(w_in, w_out, a_qnorm, a_knorm, a_lam, a_subln, b_gate, b_gate_bias, b_onorm,
                    c_dw, c_dw_b, c_ln_g, c_ln_b, d_qnorm, d_knorm):
    lr0 = 1536
    lr1 = lr0 + 2 * B_GATE_RANK
    pad = PROJ_W - w_in.shape[-1]
    w_in_p = jnp.concatenate([w_in[:, :, :lr0], w_in[:, :, lr1:], w_in[:, :, lr0:lr1],
                              jnp.zeros(w_in.shape[:2] + (pad,), w_in.dtype)], axis=-1).astype(BF16)
    hk = B_HEADS * B_DK
    wg = jnp.zeros((DEPTH, 2, GW, hk), F32)
    wg = wg.at[:, 0, 0:B_GATE_RANK].set(b_gate[:, 0])
    wg = wg.at[:, 1, B_GATE_RANK:2 * B_GATE_RANK].set(b_gate[:, 1])
    return dict(
        w_in=w_in_p, w_out=w_out.astype(BF16),
        a_qnorm=_tile_vec(a_qnorm, GW // A_QK), a_knorm=_tile_vec(a_knorm, GW // A_QK),
        a_lam=a_lam, a_subln=_tile_vec(a_subln, GW // A_V),
        b_gate=wg.astype(BF16), b_gate_bias=b_gate_bias[:, :, None, :], b_onorm=_tile_vec(b_onorm, GW // B_DV),
        c_dw=c_dw, c_dw_b=c_dw_b[:, None, :], c_ln_g=c_ln_g[:, None, :], c_ln_b=c_ln_b[:, None, :],
        d_qnorm=_tile_vec(d_qnorm, GW // D_HEAD), d_knorm=_tile_vec(d_knorm, GW // D_HEAD),
    )


def _lambda_init(i):
    return 0.8 - 0.6 * math.exp(-0.3 * i)


def kernel(x_prompt, x_sample, cache_a_k, cache_a_v, state_b, cache_d_k, cache_d_v, c, c_ctx,
           norm1_g, norm2_g, w_ada, b_ada, w_in, w_out, a_qnorm, a_knorm, a_lam, a_subln,
           b_gate, b_gate_bias, b_onorm, c_dw, c_dw_b, c_ln_g, c_ln_b, d_qnorm, d_knorm, d_rpb,
           ffn_w1, ffn_w3, ffn_w2, moe_router, moe_router_b, moe_w1, moe_w3, moe_w2):
    lp = _prepare_params(w_in, w_out, a_qnorm, a_knorm, a_lam, a_subln, b_gate, b_gate_bias, b_onorm,
                         c_dw, c_dw_b, c_ln_g, c_ln_b, d_qnorm, d_knorm)
    norm1 = norm1_g[:, None, :]
    norm2 = norm2_g[:, None, :]
    b_ada3 = b_ada[:, None, :]
    cond8 = jnp.concatenate([c_ctx[None, :], c, jnp.zeros((8 - 1 - DEC_BATCH, D_MODEL), F32)], axis=0)
    rope = _rope_tables()
    ffn_w = [w.astype(BF16) for w in (ffn_w1, ffn_w3, ffn_w2)]
    lanes = 128
    router_p = jnp.pad(moe_router, ((0, 0), (0, 0), (0, lanes - N_EXPERTS)))
    router_b_p = jnp.pad(moe_router_b, ((0, 0), (0, lanes - N_EXPERTS)))[:, None, :]
    ck_a = cache_a_k.reshape(DEC_BATCH, DEPTH, PAST_LEN, GW)
    cv_a = cache_a_v.reshape(DEC_BATCH, DEPTH, PAST_LEN, GW)
    ck_d = cache_d_k.reshape(DEC_BATCH, DEPTH, PAST_LEN, GW)
    cv_d = cache_d_v.reshape(DEC_BATCH, DEPTH, PAST_LEN, GW)
    s0_lat = _state_to_kernel_layout(state_b.transpose(1, 0, 2, 3, 4, 5).reshape(
        DEPTH * DEC_BATCH, 2, B_HEADS, B_DK, B_DV)).reshape(DEPTH, DEC_BATCH, 2, GW, B_HEADS * B_DK)

    def channel_mixer(x, mods3, i, latent, seq_len):
        j = i // 2
        if i % 2 == 0:
            return _ffn(x, norm2, mods3, *ffn_w, i, j, latent, seq_len)
        return _moe(x, norm2, mods3, router_p, router_b_p, moe_w1, moe_w3, moe_w2, i, j, latent, seq_len)

    xc = x_prompt.reshape(BATCH * SEQ, D_MODEL)
    xl = x_sample.reshape(DEC_BATCH * DEC_SEQ, D_MODEL)
    new_ak, new_av, new_sb, new_dk, new_dv = [], [], [], [], []
    for i in range(DEPTH):
        lam_init = _lambda_init(i)
        mods3 = _adaln(cond8, w_ada, b_ada3, i)[:, None, :]

        proj = _inproj(xc, norm1, mods3, lp["w_in"], i, False, SEQ)
        o_a, ak = _attn_a(proj, lp, i, False, BATCH, SEQ, lam_init)
        o_f, s_f = _gla(proj, lp, i, BATCH, SEQ, False)
        o_b, s_b = _gla(proj, lp, i, BATCH, SEQ, True, o_fwd=o_f)
        o_c = _conv(proj, lp, i, BATCH, SEQ)
        o_d, dk = _attn_d_ctx(proj, lp, i, BATCH, SEQ)
        xc = _outproj(o_a, o_b, o_c, o_d, xc, mods3, lp["w_out"], i, False, SEQ)
        xc = channel_mixer(xc, mods3, i, False, SEQ)
        new_ak.append(ak.reshape(BATCH, SEQ, A_HEADS, 2 * A_QK))
        new_av.append(proj[:, BLK_AV * GW:(BLK_AV + 1) * GW].reshape(BATCH, SEQ, A_HEADS, A_V))
        new_sb.append(jnp.stack([_state_from_kernel_layout(s_f), _state_from_kernel_layout(s_b)], axis=1))
        new_dk.append(dk.reshape(BATCH, SEQ, D_HEADS, D_HEAD))
        new_dv.append(proj[:, BLK_DV * GW:(BLK_DV + 1) * GW].reshape(BATCH, SEQ, D_HEADS, D_HEAD))

        proj = _inproj(xl, norm1, mods3, lp["w_in"], i, True, DEC_SEQ)
        o_a, = _attn_a(proj, lp, i, True, DEC_BATCH, DEC_SEQ, lam_init, ck_a, cv_a, rope)
        o_f, _ = _gla(proj, lp, i, DEC_BATCH, DEC_SEQ, False, s0=s0_lat[i])
        o_b, _ = _gla(proj, lp, i, DEC_BATCH, DEC_SEQ, True, s0=s0_lat[i], o_fwd=o_f)
        o_c = _conv(proj, lp, i, DEC_BATCH, DEC_SEQ)
        o_d = _attn_d_lat(proj, lp, i, ck_d, cv_d, _na_bias_columns(d_rpb[i]))
        xl = _outproj(o_a, o_b, o_c, o_d, xl, mods3, lp["w_out"], i, True, DEC_SEQ)
        xl = channel_mixer(xl, mods3, i, True, DEC_SEQ)

    return (xc.reshape(BATCH, SEQ, D_MODEL), xl.reshape(DEC_BATCH, DEC_SEQ, D_MODEL),
            jnp.stack(new_ak, axis=1), jnp.stack(new_av, axis=1), jnp.stack(new_sb, axis=1),
            jnp.stack(new_dk, axis=1), jnp.stack(new_dv, axis=1))
```

```python
import functools
import math

import numpy as np
import jax
import jax.numpy as jnp
from jax import lax
from jax.experimental import pallas as pl
from jax.experimental.pallas import tpu as pltpu

F32 = jnp.float32
BF16 = jnp.bfloat16
MIX_DTYPE = BF16

D_MODEL = 1024
BATCH = 16
SEQ = 256
DEPTH = 2
DEC_BATCH = 4
DEC_SEQ = 2048
PAST_LEN = 512
GRID_W = 64
A_HEADS = 4
A_QK = 32
A_V = 64
B_HEADS = 4
B_DK = 32
B_DV = 64
B_GATE_RANK = 16
B_TAU = 16.0
C_CH = 256
C_KSIZE = 31
D_HEADS = 4
D_HEAD = 64
NA_ROWS = 8
NA_COLS = 16
NA_QROWS = 4
NA_KROWS = 12
ROPE_BASE = 10000.0
D_FF = 2816
N_EXPERTS = 8
D_FF_EXPERT = 3584
EPS = 1e-6
NEG_INF = -1e30
LOG2_E = math.log2(math.e)

GW = 256
N_PROJ_BLOCKS = 12
PROJ_W = GW * N_PROJ_BLOCKS
BLK_AQ, BLK_AK, BLK_AV, BLK_BQK, BLK_BV, BLK_BG = 0, 1, 2, 3, 4, 5
BLK_C = 3
BLK_DQ, BLK_DK, BLK_DV, BLK_LR = 8, 9, 10, 11

GLA_SUB = 16
GLA_ROWS = 256
VMEM_LIMIT = 56 * 1024 * 1024


def _params(sem, vmem=VMEM_LIMIT):
    return pltpu.CompilerParams(dimension_semantics=sem, vmem_limit_bytes=vmem)


def _dot(a, b):
    return jnp.dot(a.astype(BF16), b.astype(BF16), preferred_element_type=F32)


def _dot_nt(a, b):
    return lax.dot_general(a.astype(BF16), b.astype(BF16), (((1,), (1,)), ((), ())),
                           preferred_element_type=F32)


def _dot_exact_rhs(x, m):
    hi = x.astype(BF16)
    lo = (x - hi.astype(F32)).astype(BF16)
    return (jnp.dot(hi, m, preferred_element_type=F32) + jnp.dot(lo, m, preferred_element_type=F32))


def _sigmoid(x):
    return 1.0 / (1.0 + jnp.exp(-x))


def _silu(x):
    return x * _sigmoid(x)


def _seg_matrix(n, seg):
    r = lax.broadcasted_iota(jnp.int32, (n, n), 0)
    c = lax.broadcasted_iota(jnp.int32, (n, n), 1)
    return jnp.where((r ^ c) < seg, 1.0, 0.0).astype(BF16)


def _seg_rmsnorm(x, seg, w):
    ms = _dot_exact_rhs(x * x, _seg_matrix(x.shape[-1], seg)) * (1.0 / seg)
    return x * lax.rsqrt(ms + EPS) * w


def _lane_mask(shape, lo, width):
    lane = lax.broadcasted_iota(jnp.int32, shape, len(shape) - 1)
    return (lane >= lo) & (lane < lo + width)


def _stack_heads(q, heads, width):
    return jnp.concatenate([jnp.where(_lane_mask(q.shape, h * width, width), q, 0.0) for h in range(heads)],
                           axis=0)


def _unstack_heads(o_all, heads, width):
    n = o_all.shape[0] // heads
    o = jnp.zeros((n, o_all.shape[1]), F32)
    for h in range(heads):
        blk = o_all[h * n:(h + 1) * n]
        o = jnp.where(_lane_mask(blk.shape, h * width, width), blk, o)
    return o


def _modulate(x, g, shift, scale):
    y = x * lax.rsqrt(jnp.mean(x * x, axis=-1, keepdims=True) + EPS)
    return y * g * (1.0 + scale) + shift


def _rope(x, cos, sin):
    lane = lax.broadcasted_iota(jnp.int32, x.shape, 1)
    w = x.shape[1]
    rot = jnp.where((lane & 15) < 8, -pltpu.roll(x, w - 8, 1), pltpu.roll(x, 8, 1))
    return x * cos + rot * sin


def _adaln_kernel(c_ref, w_ref, b_ref, o_ref):
    o_ref[...] = _dot(_silu(c_ref[...]), w_ref[...]) + b_ref[...]


def _adaln(cond8, w_ada, b_ada3, layer):
    tn = 1536
    n = 6 * D_MODEL
    return pl.pallas_call(
        _adaln_kernel,
        out_shape=jax.ShapeDtypeStruct((8, n), F32),
        grid=(n // tn,),
        in_specs=[pl.BlockSpec((8, D_MODEL), lambda j: (0, 0)),
                  pl.BlockSpec((None, D_MODEL, tn), lambda j: (layer, 0, j)),
                  pl.BlockSpec((None, 1, tn), lambda j: (layer, 0, j))],
        out_specs=pl.BlockSpec((8, tn), lambda j: (0, j)),
        compiler_params=_params(("arbitrary",)),
        name="adaln",
    )(cond8, w_ada, b_ada3)


def _mod_row(latent, tm, seq_len):
    if latent == "both":
        n_ctx = BATCH * SEQ
        return lambda t: jnp.where(t * tm < n_ctx, 0, 1 + jnp.maximum(t * tm - n_ctx, 0) // DEC_SEQ)
    if latent:
        return lambda t: 1 + (t * tm) // seq_len
    return lambda t: 0


def _inproj_kernel(x_ref, g_ref, mod_ref, w_ref, o_ref):
    d = D_MODEL
    h = _modulate(x_ref[...], g_ref[...], mod_ref[:, 0:d], mod_ref[:, d:2 * d])
    o_ref[...] = _dot(h, w_ref[...])


def _inproj(x, norm_g3, mods3, w_in, layer, latent, seq_len):
    t_tokens = x.shape[0]
    tm = 512
    row = _mod_row(latent, tm, seq_len)
    return pl.pallas_call(
        _inproj_kernel,
        out_shape=jax.ShapeDtypeStruct((t_tokens, PROJ_W), F32),
        grid=(t_tokens // tm,),
        in_specs=[pl.BlockSpec((tm, D_MODEL), lambda t: (t, 0)),
                  pl.BlockSpec((None, 1, D_MODEL), lambda t: (layer, 0, 0)),
                  pl.BlockSpec((None, 1, 6 * D_MODEL), lambda t: (row(t), 0, 0)),
                  pl.BlockSpec((None, D_MODEL, PROJ_W), lambda t: (layer, 0, 0))],
        out_specs=pl.BlockSpec((tm, PROJ_W), lambda t: (t, 0)),
        compiler_params=_params(("arbitrary",)),
        name="inproj",
    )(x, norm_g3, mods3, w_in)


def _diff_lambda(lam_ref, lam_init):
    lam = lam_ref[...]
    s1 = jnp.sum(lam[0:1] * lam[1:2], axis=1, keepdims=True)
    s2 = jnp.sum(lam[2:3] * lam[3:4], axis=1, keepdims=True)
    return jnp.exp(s1) - jnp.exp(s2) + lam_init


def _values_and_ones(v):
    one = jnp.where(lax.broadcasted_iota(jnp.int32, (v.shape[0], A_V), 1) == 0, 1.0, 0.0)
    parts = []
    for h in range(A_HEADS):
        parts += [v[:, h * A_V:(h + 1) * A_V], one]
    return jnp.concatenate(parts, axis=1).astype(BF16)


def _attn_a_kernel(*refs, n_own, n_ctx, tq, latent, lam_init):
    if latent:
        (q_ref, k_ref, v_ref, ck_ref, cv_ref, cos_ref, sin_ref, qn_ref, kn_ref, lam_ref, sub_ref,
         o_ref, ks, vs) = refs
    else:
        (q_ref, k_ref, v_ref, qn_ref, kn_ref, lam_ref, sub_ref, o_ref, ko_ref, ks, vs) = refs
    t = pl.program_id(1)

    @pl.when(t == 0)
    def _():
        kn = _seg_rmsnorm(k_ref[...], A_QK, kn_ref[...])
        if latent:
            kn = _rope(kn, cos_ref[...], sin_ref[...])
            ks[n_own:n_own + n_ctx, :] = ck_ref[...].astype(BF16)
            vs[n_own:n_own + n_ctx, :] = _values_and_ones(cv_ref[...])
        else:
            ko_ref[...] = kn
        ks[0:n_own, :] = kn.astype(BF16)
        vs[0:n_own, :] = _values_and_ones(v_ref[...])

    qn = _seg_rmsnorm(q_ref[...], A_QK, qn_ref[...])
    if latent:
        r0 = pl.multiple_of(t * tq, tq)
        qn = _rope(qn, cos_ref[pl.ds(r0, tq), :], sin_ref[pl.ds(r0, tq), :])
    qn = qn * (A_QK ** -0.5 * LOG2_E)
    lam = _diff_lambda(lam_ref, lam_init)
    k_all = ks[...]
    heads = []
    for h in range(A_HEADS):
        v_h = vs[:, h * 2 * A_V:(h + 1) * 2 * A_V]
        qm = jnp.concatenate([jnp.where(_lane_mask(qn.shape, (2 * h + m) * A_QK, A_QK), qn, 0.0)
                              for m in range(2)], axis=0)
        s = _dot_nt(qm, k_all)
        e = jnp.exp2(s - jnp.max(s, axis=-1, keepdims=True))
        acc = _dot(e, v_h)
        acc0, acc1 = acc[:tq], acc[tq:]
        inv0 = 1.0 / acc0[:, A_V:A_V + 1]
        inv1 = lam / acc1[:, A_V:A_V + 1]
        heads.append((acc0 * inv0 - acc1 * inv1)[:, :A_V])
    o = jnp.concatenate(heads, axis=1)
    o_ref[...] = (_seg_rmsnorm(o, A_V, sub_ref[...]) * (1.0 - lam_init)).astype(o_ref.dtype)


def _attn_a(proj, lp, layer, latent, nseq, seq_len, lam_init, cache_k=None, cache_v=None, rope=None):
    tq = min(512, seq_len)
    nt = seq_len // tq
    n_ctx = PAST_LEN if latent else 0
    kern = functools.partial(_attn_a_kernel, n_own=seq_len, n_ctx=n_ctx, tq=tq, latent=latent,
                             lam_init=lam_init)
    vec = lambda name: pl.BlockSpec((None, 1, GW), lambda b, t: (layer, 0, 0))
    in_specs = [pl.BlockSpec((tq, GW), lambda b, t: (b * nt + t, BLK_AQ)),
                pl.BlockSpec((seq_len, GW), lambda b, t: (b, BLK_AK)),
                pl.BlockSpec((seq_len, GW), lambda b, t: (b, BLK_AV))]
    args = [proj, proj, proj]
    if latent:
        in_specs += [pl.BlockSpec((None, None, PAST_LEN, GW), lambda b, t: (b, layer, 0, 0)),
                     pl.BlockSpec((None, None, PAST_LEN, GW), lambda b, t: (b, layer, 0, 0)),
                     pl.BlockSpec((seq_len, GW), lambda b, t: (0, 0)),
                     pl.BlockSpec((seq_len, GW), lambda b, t: (0, 0))]
        args += [cache_k, cache_v, rope[0], rope[1]]
    in_specs += [vec("q"), vec("k"),
                 pl.BlockSpec((None, 4, A_QK), lambda b, t: (layer, 0, 0)),
                 vec("s")]
    args += [lp["a_qnorm"], lp["a_knorm"], lp["a_lam"], lp["a_subln"]]
    out_shape = [jax.ShapeDtypeStruct((nseq * seq_len, GW), MIX_DTYPE)]
    out_specs = [pl.BlockSpec((tq, GW), lambda b, t: (b * nt + t, 0))]
    if not latent:
        out_shape.append(jax.ShapeDtypeStruct((nseq * seq_len, GW), F32))
        out_specs.append(pl.BlockSpec((seq_len, GW), lambda b, t: (b, 0)))
    return pl.pallas_call(
        kern,
        out_shape=out_shape,
        grid=(nseq, nt),
        in_specs=in_specs,
        out_specs=out_specs,
        scratch_shapes=[pltpu.VMEM((seq_len + n_ctx, GW), BF16), pltpu.VMEM((seq_len + n_ctx, 2 * GW), BF16)],
        compiler_params=_params(("arbitrary", "arbitrary")),
        name="attn_a_lat" if latent else "attn_a_ctx",
    )(*args)


def _attn_d_ctx_kernel(q_ref, k_ref, v_ref, qn_ref, kn_ref, o_ref, ko_ref):
    kn = _seg_rmsnorm(k_ref[...], D_HEAD, kn_ref[...])
    ko_ref[...] = kn
    qn = _seg_rmsnorm(q_ref[...], D_HEAD, qn_ref[...]) * (D_HEAD ** -0.5)
    s = _dot_nt(_stack_heads(qn, D_HEADS, D_HEAD), kn)
    e = jnp.exp(s - jnp.max(s, axis=-1, keepdims=True))
    o_all = _dot(e, v_ref[...]) * (1.0 / jnp.sum(e, axis=-1, keepdims=True))
    o_ref[...] = _unstack_heads(o_all, D_HEADS, D_HEAD).astype(o_ref.dtype)


def _attn_d_ctx(proj, lp, layer, nseq, seq_len):
    vec = pl.BlockSpec((None, 1, GW), lambda b: (layer, 0, 0))
    blk = lambda c: pl.BlockSpec((seq_len, GW), lambda b: (b, c))
    return pl.pallas_call(
        _attn_d_ctx_kernel,
        out_shape=[jax.ShapeDtypeStruct((nseq * seq_len, GW), MIX_DTYPE),
                   jax.ShapeDtypeStruct((nseq * seq_len, GW), F32)],
        grid=(nseq,),
        in_specs=[blk(BLK_DQ), blk(BLK_DK), blk(BLK_DV), vec, vec],
        out_specs=[blk(0), blk(0)],
        compiler_params=_params(("arbitrary",)),
        name="attn_d_ctx",
    )(proj, proj, proj, lp["d_qnorm"], lp["d_knorm"])


def _attn_d_lat_kernel(q_ref, k_ref, v_ref, ck_ref, cv_ref, qn_ref, kn_ref, cols_ref, o_ref,
                       ks, vs, cks, cvs, bias_ref):
    g = pl.program_id(1)
    n_rows = DEC_SEQ // GRID_W
    n_groups = n_rows // NA_QROWS
    n_loc = NA_KROWS * GRID_W

    @pl.when((pl.program_id(0) == 0) & (g == 0))
    def _():
        _fill_na_bias(cols_ref, bias_ref)

    @pl.when(g == 0)
    def _():
        ks[...] = _seg_rmsnorm(k_ref[...], D_HEAD, kn_ref[...]).astype(BF16)
        vs[...] = v_ref[...].astype(BF16)
        cks[...] = ck_ref[...].astype(BF16)
        cvs[...] = cv_ref[...].astype(BF16)

    qn = _seg_rmsnorm(q_ref[...], D_HEAD, qn_ref[...]) * (D_HEAD ** -0.5)
    row_start = jnp.clip(g * NA_QROWS - NA_ROWS // 2, 0, n_rows - NA_KROWS)
    variant = jnp.where(g == 0, 0, jnp.where(g == n_groups - 1, 2, 1))
    k0 = pl.multiple_of(row_start * GRID_W, GRID_W)
    kl = ks[pl.ds(k0, n_loc), :]
    vl = vs[pl.ds(k0, n_loc), :]
    kc = cks[...]
    vc = cvs[...]
    o = jnp.zeros(qn.shape, F32)
    for h in range(D_HEADS):
        hm = _lane_mask(qn.shape, h * D_HEAD, D_HEAD)
        qm = jnp.where(hm, qn, 0.0)
        s_loc = _dot_nt(qm, kl) + bias_ref[h, variant]
        s_ctx = _dot_nt(qm, kc)
        mx = jnp.maximum(jnp.max(s_loc, axis=-1, keepdims=True), jnp.max(s_ctx, axis=-1, keepdims=True))
        e_loc = jnp.exp(s_loc - mx)
        e_ctx = jnp.exp(s_ctx - mx)
        inv = 1.0 / (jnp.sum(e_loc, axis=-1, keepdims=True) + jnp.sum(e_ctx, axis=-1, keepdims=True))
        o = o + jnp.where(hm, (_dot(e_loc, vl) + _dot(e_ctx, vc)) * inv, 0.0)
    o_ref[...] = o.astype(o_ref.dtype)


def _na_bias_rows():
    n_rows = DEC_SEQ // GRID_W
    n_groups = n_rows // NA_QROWS
    plan = []
    for g in (0, 1, n_groups - 1):
        first_key_row = int(np.clip(g * NA_QROWS - NA_ROWS // 2, 0, n_rows - NA_KROWS))
        per_query_row = []
        for a in range(NA_QROWS):
            r = g * NA_QROWS + a
            win = int(np.clip(r - NA_ROWS // 2, 0, n_rows - NA_ROWS))
            per_query_row.append([first_key_row + i - r + NA_ROWS - 1
                                  if win <= first_key_row + i < win + NA_ROWS else None
                                  for i in range(NA_KROWS)])
        plan.append(per_query_row)
    return plan


def _fill_na_bias(cols_ref, bias_s):
    lane = lax.broadcasted_iota(jnp.int32, (GRID_W, 2 * GRID_W), 1)
    masked = jnp.full((GRID_W, 2 * GRID_W), NEG_INF, F32)
    for h in range(D_HEADS):
        for v, per_query_row in enumerate(_na_bias_rows()):
            for a, rows in enumerate(per_query_row):
                for p in range(NA_KROWS // 2):
                    left, right = rows[2 * p], rows[2 * p + 1]
                    lhs = masked if left is None else cols_ref[h, left]
                    rhs = masked if right is None else cols_ref[h, right]
                    bias_s[h, v, a * GRID_W:(a + 1) * GRID_W, 2 * p * GRID_W:2 * (p + 1) * GRID_W] = (
                        jnp.where(lane < GRID_W, lhs, rhs))


def _na_bias_columns(rpb):
    q = np.arange(GRID_W)
    kcol = np.arange(GRID_W)
    cs = np.clip(q - NA_COLS // 2, 0, GRID_W - NA_COLS)
    in_win = (kcol[None, :] >= cs[:, None]) & (kcol[None, :] < cs[:, None] + NA_COLS)
    dc = np.clip(kcol[None, :] - q[:, None] + (NA_COLS - 1), 0, 2 * NA_COLS - 2)
    pick_col = (dc[None] == np.arange(2 * NA_COLS - 1)[:, None, None]).astype(np.float32)
    cols = jnp.einsum("hrc,cqk->hrqk", rpb.astype(F32), jnp.asarray(pick_col), precision=lax.Precision.HIGHEST)
    cols = jnp.where(in_win[None, None], cols, NEG_INF)
    return jnp.concatenate([cols, cols], axis=-1)


def _attn_d_lat(proj, lp, layer, cache_k, cache_v, bias_cols):
    n_groups = DEC_SEQ // GRID_W // NA_QROWS
    tq = NA_QROWS * GRID_W
    vec = pl.BlockSpec((None, 1, GW), lambda b, r: (layer, 0, 0))
    seq = lambda c: pl.BlockSpec((DEC_SEQ, GW), lambda b, r: (b, c))
    cache = pl.BlockSpec((None, None, PAST_LEN, GW), lambda b, r: (b, layer, 0, 0))
    return pl.pallas_call(
        _attn_d_lat_kernel,
        out_shape=jax.ShapeDtypeStruct((DEC_BATCH * DEC_SEQ, GW), MIX_DTYPE),
        grid=(DEC_BATCH, n_groups),
        in_specs=[pl.BlockSpec((tq, GW), lambda b, r: (b * n_groups + r, BLK_DQ)),
                  seq(BLK_DK), seq(BLK_DV), cache, cache, vec, vec,
                  pl.BlockSpec(bias_cols.shape, lambda b, r: (0, 0, 0, 0))],
        out_specs=pl.BlockSpec((tq, GW), lambda b, r: (b * n_groups + r, 0)),
        scratch_shapes=[pltpu.VMEM((DEC_SEQ, GW), BF16), pltpu.VMEM((DEC_SEQ, GW), BF16),
                        pltpu.VMEM((PAST_LEN, GW), BF16), pltpu.VMEM((PAST_LEN, GW), BF16),
                        pltpu.VMEM((D_HEADS, 3, tq, NA_KROWS * GRID_W), F32)],
        compiler_params=_params(("arbitrary", "arbitrary")),
        name="attn_d_lat",
    )(proj, proj, proj, cache_k, cache_v, lp["d_qnorm"], lp["d_knorm"], bias_cols)


def _conv_kernel(c_ref, w_ref, b_ref, g_ref, beta_ref, o_ref, pad, *, seq_len):
    half = C_KSIZE // 2
    top = 16
    cin = c_ref[...]
    u = cin[:, :C_CH] * _sigmoid(cin[:, C_CH:])
    pad[0:top, :] = jnp.zeros((top, C_CH), F32)
    pad[top + seq_len:top + seq_len + top, :] = jnp.zeros((top, C_CH), F32)
    pad[top:top + seq_len, :] = u
    w = w_ref[...]
    rb = 256
    for r0 in range(0, seq_len, rb):
        acc = jnp.zeros((rb, C_CH), F32)
        n_hi = -(-C_KSIZE // 8)
        for lo in range(8):
            base = r0 + top - half + lo
            shifted = pad[base:base + rb + 8 * (n_hi - 1), :]
            for hi in range(n_hi):
                k = 8 * hi + lo
                if k < C_KSIZE:
                    acc = acc + shifted[8 * hi:8 * hi + rb, :] * w[k:k + 1, :]
        acc = acc + b_ref[...]
        mu = jnp.mean(acc, axis=-1, keepdims=True)
        xc = acc - mu
        y = xc * lax.rsqrt(jnp.mean(xc * xc, axis=-1, keepdims=True) + EPS) * g_ref[...] + beta_ref[...]
        o_ref[r0:r0 + rb, :] = _silu(y).astype(o_ref.dtype)


def _conv(proj, lp, layer, nseq, seq_len):
    vec = pl.BlockSpec((None, 1, C_CH), lambda b: (layer, 0, 0))
    return pl.pallas_call(
        functools.partial(_conv_kernel, seq_len=seq_len),
        out_shape=jax.ShapeDtypeStruct((nseq * seq_len, C_CH), MIX_DTYPE),
        grid=(nseq,),
        in_specs=[pl.BlockSpec((seq_len, 2 * C_CH), lambda b: (b, BLK_C)),
                  pl.BlockSpec((None, C_KSIZE, C_CH), lambda b: (layer, 0, 0)),
                  vec, vec, vec],
        out_specs=pl.BlockSpec((seq_len, C_CH), lambda b: (b, 0)),
        scratch_shapes=[pltpu.VMEM((seq_len + 32, C_CH), F32)],
        compiler_params=_params(("arbitrary",)),
        name="conv",
    )(proj, lp["c_dw"], lp["c_dw_b"], lp["c_ln_g"], lp["c_ln_b"])


def _gla_kernel(*refs, reverse, has_s0, n_blocks):
    refs = list(refs)
    qk_ref, v_ref, lr_ref, wg_ref, gb_ref = refs[:5]
    refs = refs[5:]
    s0_ref = refs.pop(0) if has_s0 else None
    if reverse:
        of_ref, bg_ref, on_ref = refs[:3]
        refs = refs[3:]
    o_ref, so_ref, st = refs
    rb, cs = GLA_ROWS, GLA_SUB
    hk = B_HEADS * B_DK
    j = pl.program_id(1)

    @pl.when(j == 0)
    def _():
        st[...] = s0_ref[...] if has_s0 else jnp.zeros(st.shape, F32)

    qk = qk_ref[...]
    q = qk[:, :hk] * (B_DK ** -0.5)
    k = qk[:, hk:]
    v = v_ref[...]
    pre = _dot(lr_ref[...], wg_ref[...]) + gb_ref[...]
    g = (jnp.minimum(pre, 0.0) - jnp.log(1.0 + jnp.exp(-jnp.abs(pre)))) * (1.0 / B_TAU)

    pos = lax.broadcasted_iota(jnp.int32, (rb, hk), 0) & (cs - 1)
    pre_sum = g
    suf_sum = g
    step = 1
    while step < cs:
        pre_sum = pre_sum + jnp.where(pos >= step, pltpu.roll(pre_sum, step, 0), 0.0)
        suf_sum = suf_sum + jnp.where(pos < cs - step, pltpu.roll(suf_sum, rb - step, 0), 0.0)
        step *= 2
    total = pre_sum + suf_sum - g
    if reverse:
        z = suf_sum
        k_dec = k * jnp.exp(pre_sum - g)
    else:
        z = pre_sum
        k_dec = k * jnp.exp(suf_sum - g)
    q_dec = q * jnp.exp(z)

    r_i = lax.broadcasted_iota(jnp.int32, (hk, GW), 0)
    c_i = lax.broadcasted_iota(jnp.int32, (hk, GW), 1)
    head_sum = jnp.where((r_i >> 5) == (c_i >> 6), 1.0, 0.0).astype(BF16)

    row_in_sub = lax.broadcasted_iota(jnp.int32, (cs, hk), 0)
    intra = []
    for n in range(rb // cs):
        qs = q[n * cs:(n + 1) * cs]
        zs = z[n * cs:(n + 1) * cs]
        pairs = []
        for jl in range(cs):
            r = n * cs + jl
            decay = jnp.exp(jnp.minimum(zs - z[r:r + 1], 0.0))
            keep = (row_in_sub <= jl) if reverse else (row_in_sub >= jl)
            pairs.append(jnp.where(keep, qs * k[r:r + 1] * decay, 0.0))
        w = jnp.dot(jnp.concatenate(pairs, axis=0).astype(BF16), head_sum, preferred_element_type=F32)
        acc = jnp.zeros((cs, GW), F32)
        for jl in range(cs):
            r = n * cs + jl
            acc = acc + w[jl * cs:(jl + 1) * cs] * v[r:r + 1]
        intra.append(acc)
    o = jnp.concatenate(intra, axis=0)

    r_s = lax.broadcasted_iota(jnp.int32, st.shape, 0)
    c_s = lax.broadcasted_iota(jnp.int32, st.shape, 1)
    diag = (r_s >> 6) == (c_s >> 5)
    n_sub = rb // cs
    row = lax.broadcasted_iota(jnp.int32, (rb, hk), 0)

    def per_sub_chunk(x):
        return jnp.concatenate([jnp.where((row >= n * cs) & (row < (n + 1) * cs), x, 0.0).astype(BF16)
                                for n in range(n_sub)], axis=1)

    kv_all = _dot(v.T, per_sub_chunk(k_dec))
    state = st[...]
    states = [None] * n_sub
    order = range(n_sub - 1, -1, -1) if reverse else range(n_sub)
    for n in order:
        states[n] = state.astype(BF16)
        decay = jnp.exp(total[n * cs:n * cs + 1, :])
        state = state * decay + jnp.where(diag, kv_all[:, n * hk:(n + 1) * hk], 0.0)
    st[...] = state
    o = o + _dot_nt(per_sub_chunk(q_dec), jnp.concatenate(states, axis=1))

    if reverse:
        o = _seg_rmsnorm(o + of_ref[...], B_DV, on_ref[...]) * _silu(bg_ref[...])
    o_ref[...] = o.astype(o_ref.dtype)

    @pl.when(j == n_blocks - 1)
    def _():
        so_ref[...] = state


def _gla(proj, lp, layer, nseq, seq_len, reverse, s0=None, o_fwd=None):
    rb = GLA_ROWS
    nb = seq_len // rb
    hk = B_HEADS * B_DK
    blk = (lambda b, j: b * nb + nb - 1 - j) if reverse else (lambda b, j: b * nb + j)
    tile = lambda c: pl.BlockSpec((rb, GW), lambda b, j: (blk(b, j), c))
    d = 1 if reverse else 0
    in_specs = [tile(BLK_BQK), tile(BLK_BV), tile(BLK_LR),
                pl.BlockSpec((None, None, GW, hk), lambda b, j: (layer, d, 0, 0)),
                pl.BlockSpec((None, None, 1, hk), lambda b, j: (layer, d, 0, 0))]
    args = [proj, proj, proj, lp["b_gate"], lp["b_gate_bias"]]
    if s0 is not None:
        in_specs.append(pl.BlockSpec((None, None, GW, hk), lambda b, j: (b, d, 0, 0)))
        args.append(s0)
    if reverse:
        in_specs += [tile(0), tile(BLK_BG), pl.BlockSpec((None, 1, GW), lambda b, j: (layer, 0, 0))]
        args += [o_fwd, proj, lp["b_onorm"]]
    return pl.pallas_call(
        functools.partial(_gla_kernel, reverse=reverse, has_s0=s0 is not None, n_blocks=nb),
        out_shape=[jax.ShapeDtypeStruct((nseq * seq_len, GW), MIX_DTYPE if reverse else F32),
                   jax.ShapeDtypeStruct((nseq, GW, hk), F32)],
        grid=(nseq, nb),
        in_specs=in_specs,
        out_specs=[tile(0), pl.BlockSpec((None, GW, hk), lambda b, j: (b, 0, 0))],
        scratch_shapes=[pltpu.VMEM((GW, hk), F32)],
        compiler_params=_params(("arbitrary", "arbitrary")),
        name="gla_bwd" if reverse else "gla_fwd",
    )(*args)


def _state_to_kernel_layout(s):
    eye = jnp.eye(B_HEADS, dtype=s.dtype)
    t = jnp.einsum("bxhde,hg->bxhegd", s, eye)
    return t.reshape(s.shape[0], 2, B_HEADS * B_DV, B_HEADS * B_DK)


def _state_from_kernel_layout(st):
    t = st.reshape(st.shape[0], B_HEADS, B_DV, B_HEADS, B_DK)
    return jnp.stack([t[:, h, :, h, :] for h in range(B_HEADS)], axis=1).transpose(0, 1, 3, 2)


def _outproj_kernel(a_ref, b_ref, c_ref, d_ref, x_ref, mod_ref, w_ref, o_ref):
    mix = jnp.concatenate([a_ref[...], b_ref[...], c_ref[...], d_ref[...]], axis=-1)
    gate = mod_ref[:, 2 * D_MODEL:3 * D_MODEL]
    o_ref[...] = x_ref[...] + gate * _dot(mix, w_ref[...])


def _outproj(o_a, o_b, o_c, o_d, x, mods3, w_out, layer, latent, seq_len):
    t_tokens = x.shape[0]
    tm = 512
    row = _mod_row(latent, tm, seq_len)
    part = pl.BlockSpec((tm, GW), lambda t: (t, 0))
    return pl.pallas_call(
        _outproj_kernel,
        out_shape=jax.ShapeDtypeStruct((t_tokens, D_MODEL), F32),
        grid=(t_tokens // tm,),
        in_specs=[part, part, part, part,
                  pl.BlockSpec((tm, D_MODEL), lambda t: (t, 0)),
                  pl.BlockSpec((None, 1, 6 * D_MODEL), lambda t: (row(t), 0, 0)),
                  pl.BlockSpec((None, 4 * GW, D_MODEL), lambda t: (layer, 0, 0))],
        out_specs=pl.BlockSpec((tm, D_MODEL), lambda t: (t, 0)),
        compiler_params=_params(("arbitrary",)),
        name="outproj",
    )(o_a, o_b, o_c, o_d, x, mods3, w_out)


MOE_TM = 1024
R_E1, R_E2, R_W1, R_W2, R_RANK1, R_RANK2 = 0, 1, 2, 3, 4, 5


def _router_kernel(x_ref, g_ref, mod_ref, r_ref, rb_ref, h_ref, route_ref, cnt_ref, carry):
    d = D_MODEL

    @pl.when(pl.program_id(0) == 0)
    def _():
        carry[...] = jnp.zeros(carry.shape, F32)

    h = _modulate(x_ref[...], g_ref[...], mod_ref[:, 3 * d:4 * d], mod_ref[:, 4 * d:5 * d])
    tm = h.shape[0]
    h_ref[...] = h
    r = r_ref[...]
    h_hi = h.astype(BF16)
    h_lo = (h - h_hi.astype(F32)).astype(BF16)
    r_hi = r.astype(BF16)
    r_lo = (r - r_hi.astype(F32)).astype(BF16)
    dot = lambda a, b: jnp.dot(a, b, preferred_element_type=F32)
    logits = dot(h_hi, r_hi) + dot(h_lo, r_hi) + dot(h_hi, r_lo) + rb_ref[...]
    lane = lax.broadcasted_iota(jnp.int32, logits.shape, 1).astype(F32)
    big = float(logits.shape[1])
    logits = jnp.where(lane < N_EXPERTS, logits, -jnp.inf)
    v1 = jnp.max(logits, axis=-1, keepdims=True)
    i1 = jnp.min(jnp.where(logits == v1, lane, big), axis=-1, keepdims=True)
    rest = jnp.where(lane == i1, -jnp.inf, logits)
    v2 = jnp.max(rest, axis=-1, keepdims=True)
    i2 = jnp.min(jnp.where(rest == v2, lane, big), axis=-1, keepdims=True)
    e2 = jnp.exp(v2 - v1)
    inv = 1.0 / (1.0 + e2)
    sel = jnp.where((lane == i1) | (lane == i2), 1.0, 0.0)
    r_i = lax.broadcasted_iota(jnp.int32, (tm, tm), 0)
    c_i = lax.broadcasted_iota(jnp.int32, (tm, tm), 1)
    before = jnp.where(c_i < r_i, 1.0, 0.0).astype(BF16)
    rank = jnp.dot(before, sel.astype(BF16), preferred_element_type=F32) + carry[...]
    count = carry[...] + jnp.sum(sel, axis=0, keepdims=True)
    carry[...] = count
    cnt_ref[...] = jnp.broadcast_to(count, cnt_ref.shape)
    rank1 = jnp.sum(jnp.where(lane == i1, rank, 0.0), axis=-1, keepdims=True)
    rank2 = jnp.sum(jnp.where(lane == i2, rank, 0.0), axis=-1, keepdims=True)
    rec = jnp.zeros(logits.shape, F32)
    for pos, val in ((R_E1, i1), (R_E2, i2), (R_W1, inv), (R_W2, e2 * inv), (R_RANK1, rank1), (R_RANK2, rank2)):
        rec = jnp.where(lane == pos, val, rec)
    route_ref[...] = rec


def _router(x, norm_g3, mods3, router, router_b, layer, j, latent, seq_len):
    t_tokens = x.shape[0]
    tm = 512
    row = _mod_row(latent, tm, seq_len)
    lanes = router.shape[-1]
    return pl.pallas_call(
        _router_kernel,
        out_shape=[jax.ShapeDtypeStruct((t_tokens, D_MODEL), F32),
                   jax.ShapeDtypeStruct((t_tokens, lanes), F32),
                   jax.ShapeDtypeStruct((8, lanes), F32)],
        grid=(t_tokens // tm,),
        in_specs=[pl.BlockSpec((tm, D_MODEL), lambda t: (t, 0)),
                  pl.BlockSpec((None, 1, D_MODEL), lambda t: (layer, 0, 0)),
                  pl.BlockSpec((None, 1, 6 * D_MODEL), lambda t: (row(t), 0, 0)),
                  pl.BlockSpec((None, D_MODEL, lanes), lambda t: (j, 0, 0)),
                  pl.BlockSpec((None, 1, lanes), lambda t: (j, 0, 0))],
        out_specs=[pl.BlockSpec((tm, D_MODEL), lambda t: (t, 0)),
                   pl.BlockSpec((tm, lanes), lambda t: (t, 0)),
                   pl.BlockSpec((8, lanes), lambda t: (0, 0))],
        scratch_shapes=[pltpu.VMEM((1, lanes), F32)],
        compiler_params=_params(("arbitrary",)),
        name="router",
    )(x, norm_g3, mods3, router, router_b)


def _row_copy(src, src_row, dst, dst_row, sem):
    return pltpu.make_async_copy(src.at[pl.ds(src_row, 1), :], dst.at[pl.ds(dst_row, 1), :], sem)


def _dispatch_kernel(p0_ref, p1_ref, h_ref, xs_in_ref, xs_ref, sem):
    del xs_in_ref
    tm = h_ref.shape[0]
    base = pl.program_id(0) * tm

    def issue(r, carry):
        _row_copy(h_ref, r, xs_ref, p0_ref[base + r], sem.at[0]).start()
        _row_copy(h_ref, r, xs_ref, p1_ref[base + r], sem.at[1]).start()
        return carry

    def wait(r, carry):
        _row_copy(h_ref, r, xs_ref, p0_ref[base + r], sem.at[0]).wait()
        _row_copy(h_ref, r, xs_ref, p1_ref[base + r], sem.at[1]).wait()
        return carry

    lax.fori_loop(0, tm, issue, 0, unroll=8)
    lax.fori_loop(0, tm, wait, 0, unroll=8)


def _dispatch(pos0, pos1, h_rows, n_slots):
    t_tokens = pos0.shape[0]
    tm = 512
    xs0 = jnp.zeros((n_slots, D_MODEL), F32)
    return pl.pallas_call(
        _dispatch_kernel,
        out_shape=jax.ShapeDtypeStruct(xs0.shape, F32),
        grid_spec=pltpu.PrefetchScalarGridSpec(
            num_scalar_prefetch=2,
            grid=(t_tokens // tm,),
            in_specs=[pl.BlockSpec((tm, D_MODEL), lambda t, p0, p1: (t, 0)),
                      pl.BlockSpec(memory_space=pl.ANY)],
            out_specs=pl.BlockSpec(memory_space=pl.ANY),
            scratch_shapes=[pltpu.SemaphoreType.DMA((2,))]),
        input_output_aliases={3: 0},
        compiler_params=_params(("arbitrary",)),
        name="moe_dispatch",
    )(pos0, pos1, h_rows, xs0)


def _moe_ffn_kernel(te_ref, nu_ref, xs_ref, w1_ref, w3_ref, w2_ref, y_ref, xb, acc, *, n_f):
    del te_ref
    i = pl.program_id(0)
    f = pl.program_id(1)
    used = i < nu_ref[0]

    @pl.when(used & (f == 0))
    def _():
        xb[...] = xs_ref[...].astype(BF16)
        acc[...] = jnp.zeros(acc.shape, F32)

    @pl.when(used)
    def _():
        x = xb[...]
        a = jnp.dot(x, w1_ref[...].astype(BF16), preferred_element_type=F32)
        b = jnp.dot(x, w3_ref[...].astype(BF16), preferred_element_type=F32)
        acc[...] += _dot(_silu(a) * b, w2_ref[...])

    @pl.when(used & (f == n_f - 1))
    def _():
        y_ref[...] = acc[...]

    @pl.when(jnp.logical_not(used) & (f == n_f - 1))
    def _():
        y_ref[...] = jnp.zeros(y_ref.shape, F32)


def _moe_ffn(tile_expert, n_used, xs, w1, w3, w2, j):
    tm = MOE_TM
    n_tiles = xs.shape[0] // tm
    d_ff = w1.shape[-1]
    tf = 512
    n_f = d_ff // tf
    f_of = lambda i, f, nu: jnp.where(i < nu[0], f, n_f - 1)
    return pl.pallas_call(
        functools.partial(_moe_ffn_kernel, n_f=n_f),
        out_shape=jax.ShapeDtypeStruct(xs.shape, F32),
        grid_spec=pltpu.PrefetchScalarGridSpec(
            num_scalar_prefetch=2,
            grid=(n_tiles, n_f),
            in_specs=[pl.BlockSpec((tm, D_MODEL), lambda i, f, te, nu: (jnp.minimum(i, nu[0] - 1), 0)),
                      pl.BlockSpec((None, None, D_MODEL, tf), lambda i, f, te, nu: (j, te[i], 0, f_of(i, f, nu))),
                      pl.BlockSpec((None, None, D_MODEL, tf), lambda i, f, te, nu: (j, te[i], 0, f_of(i, f, nu))),
                      pl.BlockSpec((None, None, tf, D_MODEL), lambda i, f, te, nu: (j, te[i], f_of(i, f, nu), 0))],
            out_specs=pl.BlockSpec((tm, D_MODEL), lambda i, f, te, nu: (i, 0)),
            scratch_shapes=[pltpu.VMEM((tm, D_MODEL), BF16), pltpu.VMEM((tm, D_MODEL), F32)]),
        compiler_params=_params(("arbitrary", "arbitrary")),
        name="moe_ffn",
    )(tile_expert, n_used, xs, w1, w3, w2)


def _combine_kernel(p0_ref, p1_ref, x_ref, mod_ref, route_ref, y_ref, o_ref, buf0, buf1, sem):
    tm = x_ref.shape[0]
    base = pl.program_id(0) * tm

    def issue(r, carry):
        _row_copy(y_ref, p0_ref[base + r], buf0, r, sem.at[0]).start()
        _row_copy(y_ref, p1_ref[base + r], buf1, r, sem.at[1]).start()
        return carry

    def wait(r, carry):
        _row_copy(y_ref, p0_ref[base + r], buf0, r, sem.at[0]).wait()
        _row_copy(y_ref, p1_ref[base + r], buf1, r, sem.at[1]).wait()
        return carry

    lax.fori_loop(0, tm, issue, 0, unroll=8)
    lax.fori_loop(0, tm, wait, 0, unroll=8)
    route = route_ref[...]
    w_a = route[:, R_W1:R_W1 + 1]
    w_b = route[:, R_W2:R_W2 + 1]
    y = w_a * buf0[...] + w_b * buf1[...]
    o_ref[...] = x_ref[...] + mod_ref[:, 5 * D_MODEL:6 * D_MODEL] * y


def _combine(pos0, pos1, x, mods3, route, y, latent, seq_len):
    t_tokens = x.shape[0]
    tm = 512
    row = _mod_row(latent, tm, seq_len)
    return pl.pallas_call(
        _combine_kernel,
        out_shape=jax.ShapeDtypeStruct((t_tokens, D_MODEL), F32),
        grid_spec=pltpu.PrefetchScalarGridSpec(
            num_scalar_prefetch=2,
            grid=(t_tokens // tm,),
            in_specs=[pl.BlockSpec((tm, D_MODEL), lambda t, p0, p1: (t, 0)),
                      pl.BlockSpec((None, 1, 6 * D_MODEL), lambda t, p0, p1: (row(t), 0, 0)),
                      pl.BlockSpec((tm, route.shape[1]), lambda t, p0, p1: (t, 0)),
                      pl.BlockSpec(memory_space=pl.ANY)],
            out_specs=pl.BlockSpec((tm, D_MODEL), lambda t, p0, p1: (t, 0)),
            scratch_shapes=[pltpu.VMEM((tm, D_MODEL), F32), pltpu.VMEM((tm, D_MODEL), F32),
                            pltpu.SemaphoreType.DMA((2,))]),
        compiler_params=_params(("arbitrary",)),
        name="moe_combine",
    )(pos0, pos1, x, mods3, route, y)


def _moe(x, norm_g3, mods3, router, router_b, w1, w3, w2, layer, j, latent, seq_len):
    t_tokens = x.shape[0]
    tm = MOE_TM
    n_tiles = 2 * t_tokens // tm + N_EXPERTS
    h_rows, route, cnt = _router(x, norm_g3, mods3, router, router_b, layer, j, latent, seq_len)
    expert = route[:, R_E1:R_E2 + 1].astype(jnp.int32)
    rank = route[:, R_RANK1:R_RANK2 + 1].astype(jnp.int32)
    tiles = (cnt[0, :N_EXPERTS].astype(jnp.int32) + tm - 1) // tm
    ends = jnp.cumsum(tiles)
    starts = ends - tiles
    ids = jnp.arange(N_EXPERTS, dtype=jnp.int32)
    start_of = jnp.sum(jnp.where(expert[:, :, None] == ids, starts * tm, 0), axis=-1)
    pos = start_of + rank
    n_used = ends[-1:]
    tile_ids = jnp.arange(n_tiles, dtype=jnp.int32)
    tile_expert = jnp.sum((tile_ids[:, None] >= ends[None, :]).astype(jnp.int32), axis=1)
    last_expert = jnp.max(jnp.where(tiles > 0, ids, 0))
    tile_expert = jnp.where(tile_ids < n_used, tile_expert, last_expert)
    pos0, pos1 = pos[:, 0], pos[:, 1]
    xs = _dispatch(pos0, pos1, h_rows, n_tiles * tm)
    y = _moe_ffn(tile_expert, n_used, xs, w1, w3, w2, j)
    return _combine(pos0, pos1, x, mods3, route, y, latent, seq_len)


def _ffn_kernel(x_ref, g_ref, mod_ref, w1_ref, w3_ref, w2_ref, o_ref, h_s, acc, *, n_f):
    d = D_MODEL
    f = pl.program_id(1)

    @pl.when(f == 0)
    def _():
        h = _modulate(x_ref[...], g_ref[...], mod_ref[:, 3 * d:4 * d], mod_ref[:, 4 * d:5 * d])
        h_s[...] = h.astype(BF16)
        acc[...] = jnp.zeros(acc.shape, F32)

    h = h_s[...]
    a = jnp.dot(h, w1_ref[...], preferred_element_type=F32)
    b = jnp.dot(h, w3_ref[...], preferred_element_type=F32)
    acc[...] += _dot(_silu(a) * b, w2_ref[...])

    @pl.when(f == n_f - 1)
    def _():
        o_ref[...] = x_ref[...] + mod_ref[:, 5 * d:6 * d] * acc[...]


def _ffn(x, norm_g3, mods3, w1, w3, w2, layer, j, latent, seq_len):
    t_tokens = x.shape[0]
    d_ff = w1.shape[-1]
    tm = 512
    tf = d_ff // 2
    n_f = d_ff // tf
    row = _mod_row(latent, tm, seq_len)
    return pl.pallas_call(
        functools.partial(_ffn_kernel, n_f=n_f),
        out_shape=jax.ShapeDtypeStruct((t_tokens, D_MODEL), F32),
        grid=(t_tokens // tm, n_f),
        in_specs=[pl.BlockSpec((tm, D_MODEL), lambda t, f: (t, 0)),
                  pl.BlockSpec((None, 1, D_MODEL), lambda t, f: (layer, 0, 0)),
                  pl.BlockSpec((None, 1, 6 * D_MODEL), lambda t, f: (row(t), 0, 0)),
                  pl.BlockSpec((None, D_MODEL, tf), lambda t, f: (j, 0, f)),
                  pl.BlockSpec((None, D_MODEL, tf), lambda t, f: (j, 0, f)),
                  pl.BlockSpec((None, tf, D_MODEL), lambda t, f: (j, f, 0))],
        out_specs=pl.BlockSpec((tm, D_MODEL), lambda t, f: (t, 0)),
        scratch_shapes=[pltpu.VMEM((tm, D_MODEL), BF16), pltpu.VMEM((tm, D_MODEL), F32)],
        compiler_params=_params(("arbitrary", "arbitrary")),
        name="ffn",
    )(x, norm_g3, mods3, w1, w3, w2)


def _rope_tables():
    half = A_QK // 2
    nf = half // 2
    t = np.arange(DEC_SEQ)
    pos = np.stack([(t // GRID_W), (t % GRID_W)], axis=1).astype(np.float32)
    inv = jnp.asarray(ROPE_BASE, F32) ** (-jnp.arange(nf, dtype=F32) / nf)
    lane = np.arange(GW) % A_QK
    which = (lane >= half).astype(np.int32)
    freq = lane % nf
    ang = jnp.asarray(pos)[:, which] * inv[freq][None, :]
    return jnp.cos(ang), jnp.sin(ang)


def _tile_vec(v, reps):
    return jnp.tile(v, (1, reps))[:, None, :]


def _prepare_params(w_in, w_out, a_qnorm, a_knorm, a_lam, a_subln, b_gate, b_gate_bias, b_onorm,
                    c_dw, c_dw_b, c_ln_g, c_ln_b, d_qnorm, d_knorm):
    lr0 = 1536
    lr1 = lr0 + 2 * B_GATE_RANK
    pad = PROJ_W - w_in.shape[-1]
    w_in_p = jnp.concatenate([w_in[:, :, :lr0], w_in[:, :, lr1:], w_in[:, :, lr0:lr1],
                              jnp.zeros(w_in.shape[:2] + (pad,), w_in.dtype)], axis=-1).astype(BF16)
    hk = B_HEADS * B_DK
    wg = jnp.zeros((DEPTH, 2, GW, hk), F32)
    wg = wg.at[:, 0, 0:B_GATE_RANK].set(b_gate[:, 0])
    wg = wg.at[:, 1, B_GATE_RANK:2 * B_GATE_RANK].set(b_gate[:, 1])
    return dict(
        w_in=w_in_p, w_out=w_out.astype(BF16),
        a_qnorm=_tile_vec(a_qnorm, GW // A_QK), a_knorm=_tile_vec(a_knorm, GW // A_QK),
        a_lam=a_lam, a_subln=_tile_vec(a_subln, GW // A_V),
        b_gate=wg.astype(BF16), b_gate_bias=b_gate_bias[:, :, None, :], b_onorm=_tile_vec(b_onorm, GW // B_DV),
        c_dw=c_dw, c_dw_b=c_dw_b[:, None, :], c_ln_g=c_ln_g[:, None, :], c_ln_b=c_ln_b[:, None, :],
        d_qnorm=_tile_vec(d_qnorm, GW // D_HEAD), d_knorm=_tile_vec(d_knorm, GW // D_HEAD),
    )


def _lambda_init(i):
    return 0.8 - 0.6 * math.exp(-0.3 * i)


def kernel(x_prompt, x_sample, cache_a_k, cache_a_v, state_b, cache_d_k, cache_d_v, c, c_ctx,
           norm1_g, norm2_g, w_ada, b_ada, w_in, w_out, a_qnorm, a_knorm, a_lam, a_subln,
           b_gate, b_gate_bias, b_onorm, c_dw, c_dw_b, c_ln_g, c_ln_b, d_qnorm, d_knorm, d_rpb,
           ffn_w1, ffn_w3, ffn_w2, moe_router, moe_router_b, moe_w1, moe_w3, moe_w2):
    lp = _prepare_params(w_in, w_out, a_qnorm, a_knorm, a_lam, a_subln, b_gate, b_gate_bias, b_onorm,
                         c_dw, c_dw_b, c_ln_g, c_ln_b, d_qnorm, d_knorm)
    norm1 = norm1_g[:, None, :]
    norm2 = norm2_g[:, None, :]
    b_ada3 = b_ada[:, None, :]
    cond8 = jnp.concatenate([c_ctx[None, :], c, jnp.zeros((8 - 1 - DEC_BATCH, D_MODEL), F32)], axis=0)
    rope = _rope_tables()
    ffn_w = [w.astype(BF16) for w in (ffn_w1, ffn_w3, ffn_w2)]
    lanes = 128
    router_p = jnp.pad(moe_router, ((0, 0), (0, 0), (0, lanes - N_EXPERTS)))
    router_b_p = jnp.pad(moe_router_b, ((0, 0), (0, lanes - N_EXPERTS)))[:, None, :]
    ck_a = cache_a_k.reshape(DEC_BATCH, DEPTH, PAST_LEN, GW)
    cv_a = cache_a_v.reshape(DEC_BATCH, DEPTH, PAST_LEN, GW)
    ck_d = cache_d_k.reshape(DEC_BATCH, DEPTH, PAST_LEN, GW)
    cv_d = cache_d_v.reshape(DEC_BATCH, DEPTH, PAST_LEN, GW)
    s0_lat = _state_to_kernel_layout(state_b.transpose(1, 0, 2, 3, 4, 5).reshape(
        DEPTH * DEC_BATCH, 2, B_HEADS, B_DK, B_DV)).reshape(DEPTH, DEC_BATCH, 2, GW, B_HEADS * B_DK)

    xc = x_prompt.reshape(BATCH * SEQ, D_MODEL)
    xl = x_sample.reshape(DEC_BATCH * DEC_SEQ, D_MODEL)
    new_ak, new_av, new_sb, new_dk, new_dv = [], [], [], [], []
    for i in range(DEPTH):
        lam_init = _lambda_init(i)
        dense_ffn = i % 2 == 0
        mods3 =_adaln(cond8, w_ada, b_ada3, i)[:, None, :]

        proj = _inproj(xc, norm1, mods3, lp["w_in"], i, False, SEQ)
        o_a, ak = _attn_a(proj, lp, i, False, BATCH, SEQ, lam_init)
        o_f, s_f = _gla(proj, lp, i, BATCH, SEQ, False)
        o_b, s_b = _gla(proj, lp, i, BATCH, SEQ, True, o_fwd=o_f)
        o_c = _conv(proj, lp, i, BATCH, SEQ)
        o_d, dk = _attn_d_ctx(proj, lp, i, BATCH, SEQ)
        xc = _outproj(o_a, o_b, o_c, o_d, xc, mods3, lp["w_out"], i, False, SEQ)
        if dense_ffn:
            xc = _ffn(xc, norm2, mods3, *ffn_w, i, i // 2, False, SEQ)
        new_ak.append(ak.reshape(BATCH, SEQ, A_HEADS, 2 * A_QK))
        new_av.append(proj[:, BLK_AV * GW:(BLK_AV + 1) * GW].reshape(BATCH, SEQ, A_HEADS, A_V))
        new_sb.append(jnp.stack([_state_from_kernel_layout(s_f), _state_from_kernel_layout(s_b)], axis=1))
        new_dk.append(dk.reshape(BATCH, SEQ, D_HEADS, D_HEAD))
        new_dv.append(proj[:, BLK_DV * GW:(BLK_DV + 1) * GW].reshape(BATCH, SEQ, D_HEADS, D_HEAD))

        proj = _inproj(xl, norm1, mods3, lp["w_in"], i, True, DEC_SEQ)
        o_a, = _attn_a(proj, lp, i, True, DEC_BATCH, DEC_SEQ, lam_init, ck_a, cv_a, rope)
        o_f, _ = _gla(proj, lp, i, DEC_BATCH, DEC_SEQ, False, s0=s0_lat[i])
        o_b, _ = _gla(proj, lp, i, DEC_BATCH, DEC_SEQ, True, s0=s0_lat[i], o_fwd=o_f)
        o_c = _conv(proj, lp, i, DEC_BATCH, DEC_SEQ)
        o_d = _attn_d_lat(proj, lp, i, ck_d, cv_d, _na_bias_columns(d_rpb[i]))
        xl = _outproj(o_a, o_b, o_c, o_d, xl, mods3, lp["w_out"], i, True, DEC_SEQ)
        if dense_ffn:
            xl = _ffn(xl, norm2, mods3, *ffn_w, i, i // 2, True, DEC_SEQ)
        else:
            x_all = _moe(jnp.concatenate([xc, xl], axis=0), norm2, mods3, router_p, router_b_p,
                         moe_w1, moe_w3, moe_w2, i, i // 2, "both", None)
            xc, xl = x_all[:BATCH * SEQ], x_all[BATCH * SEQ:]

    return (xc.reshape(BATCH, SEQ, D_MODEL), xl.reshape(DEC_BATCH, DEC_SEQ, D_MODEL),
            jnp.stack(new_ak, axis=1), jnp.stack(new_av, axis=1), jnp.stack(new_sb, axis=1),
            jnp.stack(new_dk, axis=1), jnp.stack(new_dv, axis=1))
```

```python
import functools
import math

import numpy as np
import jax
import jax.numpy as jnp
from jax import lax
from jax.experimental import pallas as pl
from jax.experimental.pallas import tpu as pltpu

F32 = jnp.float32
BF16 = jnp.bfloat16
MIX_DTYPE = BF16

D_MODEL = 1024
BATCH = 16
SEQ = 256
DEPTH = 2
DEC_BATCH = 4
DEC_SEQ = 2048
PAST_LEN = 512
GRID_W = 64
A_HEADS = 4
A_QK = 32
A_V = 64
B_HEADS = 4
B_DK = 32
B_DV = 64
B_GATE_RANK = 16
B_TAU = 16.0
C_CH = 256
C_KSIZE = 31
D_HEADS = 4
D_HEAD = 64
NA_ROWS = 8
NA_COLS = 16
NA_QROWS = 4
NA_KROWS = 12
ROPE_BASE = 10000.0
D_FF = 2816
N_EXPERTS = 8
D_FF_EXPERT = 3584
EPS = 1e-6
NEG_INF = -1e30
LOG2_E = math.log2(math.e)

GW = 256
N_PROJ_BLOCKS = 12
PROJ_W = GW * N_PROJ_BLOCKS
BLK_AQ, BLK_AK, BLK_AV, BLK_BQK, BLK_BV, BLK_BG = 0, 1, 2, 3, 4, 5
BLK_C = 3
BLK_DQ, BLK_DK, BLK_DV, BLK_LR = 8, 9, 10, 11

GLA_SUB = 16
GLA_ROWS = 256
VMEM_LIMIT = 56 * 1024 * 1024


def _params(sem, vmem=VMEM_LIMIT):
    return pltpu.CompilerParams(dimension_semantics=sem, vmem_limit_bytes=vmem)


def _dot(a, b):
    return jnp.dot(a.astype(BF16), b.astype(BF16), preferred_element_type=F32)


def _dot_nt(a, b):
    return lax.dot_general(a.astype(BF16), b.astype(BF16), (((1,), (1,)), ((), ())),
                           preferred_element_type=F32)


def _dot_exact_rhs(x, m):
    hi = x.astype(BF16)
    lo = (x - hi.astype(F32)).astype(BF16)
    return (jnp.dot(hi, m, preferred_element_type=F32) + jnp.dot(lo, m, preferred_element_type=F32))


def _sigmoid(x):
    return 1.0 / (1.0 + jnp.exp(-x))


def _silu(x):
    return x * _sigmoid(x)


def _seg_matrix(n, seg):
    r = lax.broadcasted_iota(jnp.int32, (n, n), 0)
    c = lax.broadcasted_iota(jnp.int32, (n, n), 1)
    return jnp.where((r ^ c) < seg, 1.0, 0.0).astype(BF16)


def _seg_rmsnorm(x, seg, w):
    ms = _dot_exact_rhs(x * x, _seg_matrix(x.shape[-1], seg)) * (1.0 / seg)
    return x * lax.rsqrt(ms + EPS) * w


def _lane_mask(shape, lo, width):
    lane = lax.broadcasted_iota(jnp.int32, shape, len(shape) - 1)
    return (lane >= lo) & (lane < lo + width)


def _stack_heads(q, heads, width):
    return jnp.concatenate([jnp.where(_lane_mask(q.shape, h * width, width), q, 0.0) for h in range(heads)],
                           axis=0)


def _unstack_heads(o_all, heads, width):
    n = o_all.shape[0] // heads
    o = jnp.zeros((n, o_all.shape[1]), F32)
    for h in range(heads):
        blk = o_all[h * n:(h + 1) * n]
        o = jnp.where(_lane_mask(blk.shape, h * width, width), blk, o)
    return o


def _modulate(x, g, shift, scale):
    y = x * lax.rsqrt(jnp.mean(x * x, axis=-1, keepdims=True) + EPS)
    return y * g * (1.0 + scale) + shift


def _rope(x, cos, sin):
    lane = lax.broadcasted_iota(jnp.int32, x.shape, 1)
    w = x.shape[1]
    rot = jnp.where((lane & 15) < 8, -pltpu.roll(x, w - 8, 1), pltpu.roll(x, 8, 1))
    return x * cos + rot * sin


def _adaln_kernel(c_ref, w_ref, b_ref, o_ref):
    o_ref[...] = _dot(_silu(c_ref[...]), w_ref[...]) + b_ref[...]


def _adaln(cond8, w_ada, b_ada3, layer):
    tn = 1536
    n = 6 * D_MODEL
    return pl.pallas_call(
        _adaln_kernel,
        out_shape=jax.ShapeDtypeStruct((8, n), F32),
        grid=(n // tn,),
        in_specs=[pl.BlockSpec((8, D_MODEL), lambda j: (0, 0)),
                  pl.BlockSpec((None, D_MODEL, tn), lambda j: (layer, 0, j)),
                  pl.BlockSpec((None, 1, tn), lambda j: (layer, 0, j))],
        out_specs=pl.BlockSpec((8, tn), lambda j: (0, j)),
        compiler_params=_params(("arbitrary",)),
        name="adaln",
    )(cond8, w_ada, b_ada3)


def _mod_row(latent, tm, seq_len):
    if latent == "both":
        n_ctx = BATCH * SEQ
        return lambda t: jnp.where(t * tm < n_ctx, 0, 1 + jnp.maximum(t * tm - n_ctx, 0) // DEC_SEQ)
    if latent:
        return lambda t: 1 + (t * tm) // seq_len
    return lambda t: 0


def _inproj_kernel(x_ref, g_ref, mod_ref, w_ref, o_ref):
    d = D_MODEL
    h = _modulate(x_ref[...], g_ref[...], mod_ref[:, 0:d], mod_ref[:, d:2 * d])
    o_ref[...] = _dot(h, w_ref[...])


def _inproj(x, norm_g3, mods3, w_in, layer, latent, seq_len):
    t_tokens = x.shape[0]
    tm = 512
    row = _mod_row(latent, tm, seq_len)
    return pl.pallas_call(
        _inproj_kernel,
        out_shape=jax.ShapeDtypeStruct((t_tokens, PROJ_W), F32),
        grid=(t_tokens // tm,),
        in_specs=[pl.BlockSpec((tm, D_MODEL), lambda t: (t, 0)),
                  pl.BlockSpec((None, 1, D_MODEL), lambda t: (layer, 0, 0)),
                  pl.BlockSpec((None, 1, 6 * D_MODEL), lambda t: (row(t), 0, 0)),
                  pl.BlockSpec((None, D_MODEL, PROJ_W), lambda t: (layer, 0, 0))],
        out_specs=pl.BlockSpec((tm, PROJ_W), lambda t: (t, 0)),
        compiler_params=_params(("arbitrary",)),
        name="inproj",
    )(x, norm_g3, mods3, w_in)


def _diff_lambda(lam_ref, lam_init):
    lam = lam_ref[...]
    s1 = jnp.sum(lam[0:1] * lam[1:2], axis=1, keepdims=True)
    s2 = jnp.sum(lam[2:3] * lam[3:4], axis=1, keepdims=True)
    return jnp.exp(s1) - jnp.exp(s2) + lam_init


def _values_and_ones(v):
    one = jnp.where(lax.broadcasted_iota(jnp.int32, (v.shape[0], A_V), 1) == 0, 1.0, 0.0)
    parts = []
    for h in range(A_HEADS):
        parts += [v[:, h * A_V:(h + 1) * A_V], one]
    return jnp.concatenate(parts, axis=1).astype(BF16)


def _attn_a_kernel(*refs, n_own, n_ctx, tq, latent, lam_init):
    if latent:
        (q_ref, k_ref, v_ref, ck_ref, cv_ref, cos_ref, sin_ref, qn_ref, kn_ref, lam_ref, sub_ref,
         o_ref, ks, vs) = refs
    else:
        (q_ref, k_ref, v_ref, qn_ref, kn_ref, lam_ref, sub_ref, o_ref, ko_ref, ks, vs) = refs
    t = pl.program_id(1)

    @pl.when(t == 0)
    def _():
        kn = _seg_rmsnorm(k_ref[...], A_QK, kn_ref[...])
        if latent:
            kn = _rope(kn, cos_ref[...], sin_ref[...])
            ks[n_own:n_own + n_ctx, :] = ck_ref[...].astype(BF16)
            vs[n_own:n_own + n_ctx, :] = _values_and_ones(cv_ref[...])
        else:
            ko_ref[...] = kn
        ks[0:n_own, :] = kn.astype(BF16)
        vs[0:n_own, :] = _values_and_ones(v_ref[...])

    qn = _seg_rmsnorm(q_ref[...], A_QK, qn_ref[...])
    if latent:
        r0 = pl.multiple_of(t * tq, tq)
        qn = _rope(qn, cos_ref[pl.ds(r0, tq), :], sin_ref[pl.ds(r0, tq), :])
    qn = qn * (A_QK ** -0.5 * LOG2_E)
    lam = _diff_lambda(lam_ref, lam_init)
    k_all = ks[...]
    heads = []
    for h in range(A_HEADS):
        v_h = vs[:, h * 2 * A_V:(h + 1) * 2 * A_V]
        qm = jnp.concatenate([jnp.where(_lane_mask(qn.shape, (2 * h + m) * A_QK, A_QK), qn, 0.0)
                              for m in range(2)], axis=0)
        s = _dot_nt(qm, k_all)
        e = jnp.exp2(s - jnp.max(s, axis=-1, keepdims=True))
        acc = _dot(e, v_h)
        acc0, acc1 = acc[:tq], acc[tq:]
        inv0 = 1.0 / acc0[:, A_V:A_V + 1]
        inv1 = lam / acc1[:, A_V:A_V + 1]
        heads.append((acc0 * inv0 - acc1 * inv1)[:, :A_V])
    o = jnp.concatenate(heads, axis=1)
    o_ref[...] = (_seg_rmsnorm(o, A_V, sub_ref[...]) * (1.0 - lam_init)).astype(o_ref.dtype)


def _attn_a(proj, lp, layer, latent, nseq, seq_len, lam_init, cache_k=None, cache_v=None, rope=None):
    tq = min(512, seq_len)
    nt = seq_len // tq
    n_ctx = PAST_LEN if latent else 0
    kern = functools.partial(_attn_a_kernel, n_own=seq_len, n_ctx=n_ctx, tq=tq, latent=latent,
                             lam_init=lam_init)
    vec = lambda name: pl.BlockSpec((None, 1, GW), lambda b, t: (layer, 0, 0))
    in_specs = [pl.BlockSpec((tq, GW), lambda b, t: (b * nt + t, BLK_AQ)),
                pl.BlockSpec((seq_len, GW), lambda b, t: (b, BLK_AK)),
                pl.BlockSpec((seq_len, GW), lambda b, t: (b, BLK_AV))]
    args = [proj, proj, proj]
    if latent:
        in_specs += [pl.BlockSpec((None, None, PAST_LEN, GW), lambda b, t: (b, layer, 0, 0)),
                     pl.BlockSpec((None, None, PAST_LEN, GW), lambda b, t: (b, layer, 0, 0)),
                     pl.BlockSpec((seq_len, GW), lambda b, t: (0, 0)),
                     pl.BlockSpec((seq_len, GW), lambda b, t: (0, 0))]
        args += [cache_k, cache_v, rope[0], rope[1]]
    in_specs += [vec("q"), vec("k"),
                 pl.BlockSpec((None, 4, A_QK), lambda b, t: (layer, 0, 0)),
                 vec("s")]
    args += [lp["a_qnorm"], lp["a_knorm"], lp["a_lam"], lp["a_subln"]]
    out_shape = [jax.ShapeDtypeStruct((nseq * seq_len, GW), MIX_DTYPE)]
    out_specs = [pl.BlockSpec((tq, GW), lambda b, t: (b * nt + t, 0))]
    if not latent:
        out_shape.append(jax.ShapeDtypeStruct((nseq * seq_len, GW), F32))
        out_specs.append(pl.BlockSpec((seq_len, GW), lambda b, t: (b, 0)))
    return pl.pallas_call(
        kern,
        out_shape=out_shape,
        grid=(nseq, nt),
        in_specs=in_specs,
        out_specs=out_specs,
        scratch_shapes=[pltpu.VMEM((seq_len + n_ctx, GW), BF16), pltpu.VMEM((seq_len + n_ctx, 2 * GW), BF16)],
        compiler_params=_params(("arbitrary", "arbitrary")),
        name="attn_a_lat" if latent else "attn_a_ctx",
    )(*args)


def _attn_d_ctx_kernel(q_ref, k_ref, v_ref, qn_ref, kn_ref, o_ref, ko_ref):
    kn = _seg_rmsnorm(k_ref[...], D_HEAD, kn_ref[...])
    ko_ref[...] = kn
    qn = _seg_rmsnorm(q_ref[...], D_HEAD, qn_ref[...]) * (D_HEAD ** -0.5)
    s = _dot_nt(_stack_heads(qn, D_HEADS, D_HEAD), kn)
    e = jnp.exp(s - jnp.max(s, axis=-1, keepdims=True))
    o_all = _dot(e, v_ref[...]) * (1.0 / jnp.sum(e, axis=-1, keepdims=True))
    o_ref[...] = _unstack_heads(o_all, D_HEADS, D_HEAD).astype(o_ref.dtype)


def _attn_d_ctx(proj, lp, layer, nseq, seq_len):
    vec = pl.BlockSpec((None, 1, GW), lambda b: (layer, 0, 0))
    blk = lambda c: pl.BlockSpec((seq_len, GW), lambda b: (b, c))
    return pl.pallas_call(
        _attn_d_ctx_kernel,
        out_shape=[jax.ShapeDtypeStruct((nseq * seq_len, GW), MIX_DTYPE),
                   jax.ShapeDtypeStruct((nseq * seq_len, GW), F32)],
        grid=(nseq,),
        in_specs=[blk(BLK_DQ), blk(BLK_DK), blk(BLK_DV), vec, vec],
        out_specs=[blk(0), blk(0)],
        compiler_params=_params(("arbitrary",)),
        name="attn_d_ctx",
    )(proj, proj, proj, lp["d_qnorm"], lp["d_knorm"])


def _attn_d_lat_kernel(q_ref, k_ref, v_ref, ck_ref, cv_ref, qn_ref, kn_ref, cols_ref, o_ref,
                       ks, vs, cks, cvs, bias_ref):
    g = pl.program_id(1)
    n_rows = DEC_SEQ // GRID_W
    n_groups = n_rows // NA_QROWS
    n_loc = NA_KROWS * GRID_W

    @pl.when((pl.program_id(0) == 0) & (g == 0))
    def _():
        _fill_na_bias(cols_ref, bias_ref)

    @pl.when(g == 0)
    def _():
        ks[...] = _seg_rmsnorm(k_ref[...], D_HEAD, kn_ref[...]).astype(BF16)
        vs[...] = v_ref[...].astype(BF16)
        cks[...] = ck_ref[...].astype(BF16)
        cvs[...] = cv_ref[...].astype(BF16)

    qn = _seg_rmsnorm(q_ref[...], D_HEAD, qn_ref[...]) * (D_HEAD ** -0.5)
    row_start = jnp.clip(g * NA_QROWS - NA_ROWS // 2, 0, n_rows - NA_KROWS)
    variant = jnp.where(g == 0, 0, jnp.where(g == n_groups - 1, 2, 1))
    k0 = pl.multiple_of(row_start * GRID_W, GRID_W)
    kl = ks[pl.ds(k0, n_loc), :]
    vl = vs[pl.ds(k0, n_loc), :]
    kc = cks[...]
    vc = cvs[...]
    o = jnp.zeros(qn.shape, F32)
    for h in range(D_HEADS):
        hm = _lane_mask(qn.shape, h * D_HEAD, D_HEAD)
        qm = jnp.where(hm, qn, 0.0)
        s_loc = _dot_nt(qm, kl) + bias_ref[h, variant]
        s_ctx = _dot_nt(qm, kc)
        mx = jnp.maximum(jnp.max(s_loc, axis=-1, keepdims=True), jnp.max(s_ctx, axis=-1, keepdims=True))
        e_loc = jnp.exp(s_loc - mx)
        e_ctx = jnp.exp(s_ctx - mx)
        inv = 1.0 / (jnp.sum(e_loc, axis=-1, keepdims=True) + jnp.sum(e_ctx, axis=-1, keepdims=True))
        o = o + jnp.where(hm, (_dot(e_loc, vl) + _dot(e_ctx, vc)) * inv, 0.0)
    o_ref[...] = o.astype(o_ref.dtype)


def _na_bias_rows():
    n_rows = DEC_SEQ // GRID_W
    n_groups = n_rows // NA_QROWS
    plan = []
    for g in (0, 1, n_groups - 1):
        first_key_row = int(np.clip(g * NA_QROWS - NA_ROWS // 2, 0, n_rows - NA_KROWS))
        per_query_row = []
        for a in range(NA_QROWS):
            r = g * NA_QROWS + a
            win = int(np.clip(r - NA_ROWS // 2, 0, n_rows - NA_ROWS))
            per_query_row.append([first_key_row + i - r + NA_ROWS - 1
                                  if win <= first_key_row + i < win + NA_ROWS else None
                                  for i in range(NA_KROWS)])
        plan.append(per_query_row)
    return plan


def _fill_na_bias(cols_ref, bias_s):
    lane = lax.broadcasted_iota(jnp.int32, (GRID_W, 2 * GRID_W), 1)
    masked = jnp.full((GRID_W, 2 * GRID_W), NEG_INF, F32)
    for h in range(D_HEADS):
        for v, per_query_row in enumerate(_na_bias_rows()):
            for a, rows in enumerate(per_query_row):
                for p in range(NA_KROWS // 2):
                    left, right = rows[2 * p], rows[2 * p + 1]
                    lhs = masked if left is None else cols_ref[h, left]
                    rhs = masked if right is None else cols_ref[h, right]
                    bias_s[h, v, a * GRID_W:(a + 1) * GRID_W, 2 * p * GRID_W:2 * (p + 1) * GRID_W] = (
                        jnp.where(lane < GRID_W, lhs, rhs))


def _na_bias_columns(rpb):
    q = np.arange(GRID_W)
    kcol = np.arange(GRID_W)
    cs = np.clip(q - NA_COLS // 2, 0, GRID_W - NA_COLS)
    in_win = (kcol[None, :] >= cs[:, None]) & (kcol[None, :] < cs[:, None] + NA_COLS)
    dc = np.clip(kcol[None, :] - q[:, None] + (NA_COLS - 1), 0, 2 * NA_COLS - 2)
    pick_col = (dc[None] == np.arange(2 * NA_COLS - 1)[:, None, None]).astype(np.float32)
    cols = jnp.einsum("hrc,cqk->hrqk", rpb.astype(F32), jnp.asarray(pick_col), precision=lax.Precision.HIGHEST)
    cols = jnp.where(in_win[None, None], cols, NEG_INF)
    return jnp.concatenate([cols, cols], axis=-1)


def _attn_d_lat(proj, lp, layer, cache_k, cache_v, bias_cols):
    n_groups = DEC_SEQ // GRID_W // NA_QROWS
    tq = NA_QROWS * GRID_W
    vec = pl.BlockSpec((None, 1, GW), lambda b, r: (layer, 0, 0))
    seq = lambda c: pl.BlockSpec((DEC_SEQ, GW), lambda b, r: (b, c))
    cache = pl.BlockSpec((None, None, PAST_LEN, GW), lambda b, r: (b, layer, 0, 0))
    return pl.pallas_call(
        _attn_d_lat_kernel,
        out_shape=jax.ShapeDtypeStruct((DEC_BATCH * DEC_SEQ, GW), MIX_DTYPE),
        grid=(DEC_BATCH, n_groups),
        in_specs=[pl.BlockSpec((tq, GW), lambda b, r: (b * n_groups + r, BLK_DQ)),
                  seq(BLK_DK), seq(BLK_DV), cache, cache, vec, vec,
                  pl.BlockSpec(bias_cols.shape, lambda b, r: (0, 0, 0, 0))],
        out_specs=pl.BlockSpec((tq, GW), lambda b, r: (b * n_groups + r, 0)),
        scratch_shapes=[pltpu.VMEM((DEC_SEQ, GW), BF16), pltpu.VMEM((DEC_SEQ, GW), BF16),
                        pltpu.VMEM((PAST_LEN, GW), BF16), pltpu.VMEM((PAST_LEN, GW), BF16),
                        pltpu.VMEM((D_HEADS, 3, tq, NA_KROWS * GRID_W), F32)],
        compiler_params=_params(("arbitrary", "arbitrary")),
        name="attn_d_lat",
    )(proj, proj, proj, cache_k, cache_v, lp["d_qnorm"], lp["d_knorm"], bias_cols)


def _conv_kernel(c_ref, w_ref, b_ref, g_ref, beta_ref, o_ref, pad, *, seq_len):
    half = C_KSIZE // 2
    top = 16
    cin = c_ref[...]
    u = cin[:, :C_CH] * _sigmoid(cin[:, C_CH:])
    pad[0:top, :] = jnp.zeros((top, C_CH), F32)
    pad[top + seq_len:top + seq_len + top, :] = jnp.zeros((top, C_CH), F32)
    pad[top:top + seq_len, :] = u
    w = w_ref[...]
    rb = 256
    for r0 in range(0, seq_len, rb):
        acc = jnp.zeros((rb, C_CH), F32)
        n_hi = -(-C_KSIZE // 8)
        for lo in range(8):
            base = r0 + top - half + lo
            shifted = pad[base:base + rb + 8 * (n_hi - 1), :]
            for hi in range(n_hi):
                k = 8 * hi + lo
                if k < C_KSIZE:
                    acc = acc + shifted[8 * hi:8 * hi + rb, :] * w[k:k + 1, :]
        acc = acc + b_ref[...]
        mu = jnp.mean(acc, axis=-1, keepdims=True)
        xc = acc - mu
        y = xc * lax.rsqrt(jnp.mean(xc * xc, axis=-1, keepdims=True) + EPS) * g_ref[...] + beta_ref[...]
        o_ref[r0:r0 + rb, :] = _silu(y).astype(o_ref.dtype)


def _conv(proj, lp, layer, nseq, seq_len):
    vec = pl.BlockSpec((None, 1, C_CH), lambda b: (layer, 0, 0))
    return pl.pallas_call(
        functools.partial(_conv_kernel, seq_len=seq_len),
        out_shape=jax.ShapeDtypeStruct((nseq * seq_len, C_CH), MIX_DTYPE),
        grid=(nseq,),
        in_specs=[pl.BlockSpec((seq_len, 2 * C_CH), lambda b: (b, BLK_C)),
                  pl.BlockSpec((None, C_KSIZE, C_CH), lambda b: (layer, 0, 0)),
                  vec, vec, vec],
        out_specs=pl.BlockSpec((seq_len, C_CH), lambda b: (b, 0)),
        scratch_shapes=[pltpu.VMEM((seq_len + 32, C_CH), F32)],
        compiler_params=_params(("arbitrary",)),
        name="conv",
    )(proj, lp["c_dw"], lp["c_dw_b"], lp["c_ln_g"], lp["c_ln_b"])


def _gla_kernel(*refs, reverse, has_s0, n_blocks):
    refs = list(refs)
    qk_ref, v_ref, lr_ref, wg_ref, gb_ref = refs[:5]
    refs = refs[5:]
    s0_ref = refs.pop(0) if has_s0 else None
    if reverse:
        of_ref, bg_ref, on_ref = refs[:3]
        refs = refs[3:]
    o_ref, so_ref, st = refs
    rb, cs = GLA_ROWS, GLA_SUB
    hk = B_HEADS * B_DK
    j = pl.program_id(1)

    @pl.when(j == 0)
    def _():
        st[...] = s0_ref[...] if has_s0 else jnp.zeros(st.shape, F32)

    qk = qk_ref[...]
    q = qk[:, :hk] * (B_DK ** -0.5)
    k = qk[:, hk:]
    v = v_ref[...]
    pre = _dot(lr_ref[...], wg_ref[...]) + gb_ref[...]
    g = (jnp.minimum(pre, 0.0) - jnp.log(1.0 + jnp.exp(-jnp.abs(pre)))) * (1.0 / B_TAU)

    pos = lax.broadcasted_iota(jnp.int32, (rb, hk), 0) & (cs - 1)
    pre_sum = g
    suf_sum = g
    step = 1
    while step < cs:
        pre_sum = pre_sum + jnp.where(pos >= step, pltpu.roll(pre_sum, step, 0), 0.0)
        suf_sum = suf_sum + jnp.where(pos < cs - step, pltpu.roll(suf_sum, rb - step, 0), 0.0)
        step *= 2
    total = pre_sum + suf_sum - g
    if reverse:
        z = suf_sum
        k_dec = k * jnp.exp(pre_sum - g)
    else:
        z = pre_sum
        k_dec = k * jnp.exp(suf_sum - g)
    q_dec = q * jnp.exp(z)

    r_i = lax.broadcasted_iota(jnp.int32, (hk, GW), 0)
    c_i = lax.broadcasted_iota(jnp.int32, (hk, GW), 1)
    head_sum = jnp.where((r_i >> 5) == (c_i >> 6), 1.0, 0.0).astype(BF16)

    row_in_sub = lax.broadcasted_iota(jnp.int32, (cs, hk), 0)
    intra = []
    for n in range(rb // cs):
        qs = q[n * cs:(n + 1) * cs]
        zs = z[n * cs:(n + 1) * cs]
        pairs = []
        for jl in range(cs):
            r = n * cs + jl
            decay = jnp.exp(jnp.minimum(zs - z[r:r + 1], 0.0))
            keep = (row_in_sub <= jl) if reverse else (row_in_sub >= jl)
            pairs.append(jnp.where(keep, qs * k[r:r + 1] * decay, 0.0))
        w = jnp.dot(jnp.concatenate(pairs, axis=0).astype(BF16), head_sum, preferred_element_type=F32)
        acc = jnp.zeros((cs, GW), F32)
        for jl in range(cs):
            r = n * cs + jl
            acc = acc + w[jl * cs:(jl + 1) * cs] * v[r:r + 1]
        intra.append(acc)
    o = jnp.concatenate(intra, axis=0)

    r_s = lax.broadcasted_iota(jnp.int32, st.shape, 0)
    c_s = lax.broadcasted_iota(jnp.int32, st.shape, 1)
    diag = (r_s >> 6) == (c_s >> 5)
    n_sub = rb // cs
    row = lax.broadcasted_iota(jnp.int32, (rb, hk), 0)

    def per_sub_chunk(x):
        return jnp.concatenate([jnp.where((row >= n * cs) & (row < (n + 1) * cs), x, 0.0).astype(BF16)
                                for n in range(n_sub)], axis=1)

    kv_all = _dot(v.T, per_sub_chunk(k_dec))
    state = st[...]
    states = [None] * n_sub
    order = range(n_sub - 1, -1, -1) if reverse else range(n_sub)
    for n in order:
        states[n] = state.astype(BF16)
        decay = jnp.exp(total[n * cs:n * cs + 1, :])
        state = state * decay + jnp.where(diag, kv_all[:, n * hk:(n + 1) * hk], 0.0)
    st[...] = state
    o = o + _dot_nt(per_sub_chunk(q_dec), jnp.concatenate(states, axis=1))

    if reverse:
        o = _seg_rmsnorm(o + of_ref[...], B_DV, on_ref[...]) * _silu(bg_ref[...])
    o_ref[...] = o.astype(o_ref.dtype)

    @pl.when(j == n_blocks - 1)
    def _():
        so_ref[...] = state


def _gla(proj, lp, layer, nseq, seq_len, reverse, s0=None, o_fwd=None):
    rb = GLA_ROWS
    nb = seq_len // rb
    hk = B_HEADS * B_DK
    blk = (lambda b, j: b * nb + nb - 1 - j) if reverse else (lambda b, j: b * nb + j)
    tile = lambda c: pl.BlockSpec((rb, GW), lambda b, j: (blk(b, j), c))
    d = 1 if reverse else 0
    in_specs = [tile(BLK_BQK), tile(BLK_BV), tile(BLK_LR),
                pl.BlockSpec((None, None, GW, hk), lambda b, j: (layer, d, 0, 0)),
                pl.BlockSpec((None, None, 1, hk), lambda b, j: (layer, d, 0, 0))]
    args = [proj, proj, proj, lp["b_gate"], lp["b_gate_bias"]]
    if s0 is not None:
        in_specs.append(pl.BlockSpec((None, None, GW, hk), lambda b, j: (b, d, 0, 0)))
        args.append(s0)
    if reverse:
        in_specs += [tile(0), tile(BLK_BG), pl.BlockSpec((None, 1, GW), lambda b, j: (layer, 0, 0))]
        args += [o_fwd, proj, lp["b_onorm"]]
    return pl.pallas_call(
        functools.partial(_gla_kernel, reverse=reverse, has_s0=s0 is not None, n_blocks=nb),
        out_shape=[jax.ShapeDtypeStruct((nseq * seq_len, GW), MIX_DTYPE if reverse else F32),
                   jax.ShapeDtypeStruct((nseq, GW, hk), F32)],
        grid=(nseq, nb),
        in_specs=in_specs,
        out_specs=[tile(0), pl.BlockSpec((None, GW, hk), lambda b, j: (b, 0, 0))],
        scratch_shapes=[pltpu.VMEM((GW, hk), F32)],
        compiler_params=_params(("arbitrary", "arbitrary")),
        name="gla_bwd" if reverse else "gla_fwd",
    )(*args)


def _state_to_kernel_layout(s):
    eye = jnp.eye(B_HEADS, dtype=s.dtype)
    t = jnp.einsum("bxhde,hg->bxhegd", s, eye)
    return t.reshape(s.shape[0], 2, B_HEADS * B_DV, B_HEADS * B_DK)


def _state_from_kernel_layout(st):
    t = st.reshape(st.shape[0], B_HEADS, B_DV, B_HEADS, B_DK)
    return jnp.stack([t[:, h, :, h, :] for h in range(B_HEADS)], axis=1).transpose(0, 1, 3, 2)


def _outproj_kernel(a_ref, b_ref, c_ref, d_ref, x_ref, mod_ref, w_ref, o_ref):
    mix = jnp.concatenate([a_ref[...], b_ref[...], c_ref[...], d_ref[...]], axis=-1)
    gate = mod_ref[:, 2 * D_MODEL:3 * D_MODEL]
    o_ref[...] = x_ref[...] + gate * _dot(mix, w_ref[...])


def _outproj(o_a, o_b, o_c, o_d, x, mods3, w_out, layer, latent, seq_len):
    t_tokens = x.shape[0]
    tm = 512
    row = _mod_row(latent, tm, seq_len)
    part = pl.BlockSpec((tm, GW), lambda t: (t, 0))
    return pl.pallas_call(
        _outproj_kernel,
        out_shape=jax.ShapeDtypeStruct((t_tokens, D_MODEL), F32),
        grid=(t_tokens // tm,),
        in_specs=[part, part, part, part,
                  pl.BlockSpec((tm, D_MODEL), lambda t: (t, 0)),
                  pl.BlockSpec((None, 1, 6 * D_MODEL), lambda t: (row(t), 0, 0)),
                  pl.BlockSpec((None, 4 * GW, D_MODEL), lambda t: (layer, 0, 0))],
        out_specs=pl.BlockSpec((tm, D_MODEL), lambda t: (t, 0)),
        compiler_params=_params(("arbitrary",)),
        name="outproj",
    )(o_a, o_b, o_c, o_d, x, mods3, w_out)


MOE_TM = 1024
R_E1, R_E2, R_W1, R_W2, R_RANK1, R_RANK2 = 0, 1, 2, 3, 4, 5


def _router_kernel(xc_ref, xl_ref, g_ref, mod_ref, r_ref, rb_ref, h_ref, route_ref, cnt_ref, carry, *,
                   n_ctx_tiles):
    d = D_MODEL

    @pl.when(pl.program_id(0) == 0)
    def _():
        carry[...] = jnp.zeros(carry.shape, F32)

    x = jnp.where(pl.program_id(0) < n_ctx_tiles, xc_ref[...], xl_ref[...])
    h = _modulate(x, g_ref[...], mod_ref[:, 3 * d:4 * d], mod_ref[:, 4 * d:5 * d])
    tm = h.shape[0]
    h_ref[...] = h
    r = r_ref[...]
    h_hi = h.astype(BF16)
    h_lo = (h - h_hi.astype(F32)).astype(BF16)
    r_hi = r.astype(BF16)
    r_lo = (r - r_hi.astype(F32)).astype(BF16)
    dot = lambda a, b: jnp.dot(a, b, preferred_element_type=F32)
    logits = dot(h_hi, r_hi) + dot(h_lo, r_hi) + dot(h_hi, r_lo) + rb_ref[...]
    lane = lax.broadcasted_iota(jnp.int32, logits.shape, 1).astype(F32)
    big = float(logits.shape[1])
    logits = jnp.where(lane < N_EXPERTS, logits, -jnp.inf)
    v1 = jnp.max(logits, axis=-1, keepdims=True)
    i1 = jnp.min(jnp.where(logits == v1, lane, big), axis=-1, keepdims=True)
    rest = jnp.where(lane == i1, -jnp.inf, logits)
    v2 = jnp.max(rest, axis=-1, keepdims=True)
    i2 = jnp.min(jnp.where(rest == v2, lane, big), axis=-1, keepdims=True)
    e2 = jnp.exp(v2 - v1)
    inv = 1.0 / (1.0 + e2)
    sel = jnp.where((lane == i1) | (lane == i2), 1.0, 0.0)
    r_i = lax.broadcasted_iota(jnp.int32, (tm, tm), 0)
    c_i = lax.broadcasted_iota(jnp.int32, (tm, tm), 1)
    before = jnp.where(c_i < r_i, 1.0, 0.0).astype(BF16)
    rank = jnp.dot(before, sel.astype(BF16), preferred_element_type=F32) + carry[...]
    count = carry[...] + jnp.sum(sel, axis=0, keepdims=True)
    carry[...] = count
    cnt_ref[...] = jnp.broadcast_to(count, cnt_ref.shape)
    rank1 = jnp.sum(jnp.where(lane == i1, rank, 0.0), axis=-1, keepdims=True)
    rank2 = jnp.sum(jnp.where(lane == i2, rank, 0.0), axis=-1, keepdims=True)
    rec = jnp.zeros(logits.shape, F32)
    for pos, val in ((R_E1, i1), (R_E2, i2), (R_W1, inv), (R_W2, e2 * inv), (R_RANK1, rank1), (R_RANK2, rank2)):
        rec = jnp.where(lane == pos, val, rec)
    route_ref[...] = rec


def _two_pass_specs(tm):
    nc = BATCH * SEQ // tm
    ctx = pl.BlockSpec((tm, D_MODEL), lambda t, *_: (jnp.minimum(t, nc - 1), 0))
    lat = pl.BlockSpec((tm, D_MODEL), lambda t, *_: (jnp.maximum(t - nc, 0), 0))
    return nc, ctx, lat


def _router(xc, xl, norm_g3, mods3, router, router_b, layer, j):
    t_tokens = xc.shape[0] + xl.shape[0]
    tm = 512
    row = _mod_row("both", tm, None)
    lanes = router.shape[-1]
    nc, ctx_spec, lat_spec = _two_pass_specs(tm)
    return pl.pallas_call(
        functools.partial(_router_kernel, n_ctx_tiles=nc),
        out_shape=[jax.ShapeDtypeStruct((t_tokens, D_MODEL), F32),
                   jax.ShapeDtypeStruct((t_tokens, lanes), F32),
                   jax.ShapeDtypeStruct((8, lanes), F32)],
        grid=(t_tokens // tm,),
        in_specs=[ctx_spec, lat_spec,
                  pl.BlockSpec((None, 1, D_MODEL), lambda t: (layer, 0, 0)),
                  pl.BlockSpec((None, 1, 6 * D_MODEL), lambda t: (row(t), 0, 0)),
                  pl.BlockSpec((None, D_MODEL, lanes), lambda t: (j, 0, 0)),
                  pl.BlockSpec((None, 1, lanes), lambda t: (j, 0, 0))],
        out_specs=[pl.BlockSpec((tm, D_MODEL), lambda t: (t, 0)),
                   pl.BlockSpec((tm, lanes), lambda t: (t, 0)),
                   pl.BlockSpec((8, lanes), lambda t: (0, 0))],
        scratch_shapes=[pltpu.VMEM((1, lanes), F32)],
        compiler_params=_params(("arbitrary",)),
        name="router",
    )(xc, xl, norm_g3, mods3, router, router_b)


def _row_copy(src, src_row, dst, dst_row, sem):
    return pltpu.make_async_copy(src.at[pl.ds(src_row, 1), :], dst.at[pl.ds(dst_row, 1), :], sem)


def _dispatch_kernel(p0_ref, p1_ref, h_ref, xs_in_ref, xs_ref, sem):
    del xs_in_ref
    tm = h_ref.shape[0]
    base = pl.program_id(0) * tm

    def issue(r, carry):
        _row_copy(h_ref, r, xs_ref, p0_ref[base + r], sem.at[0]).start()
        _row_copy(h_ref, r, xs_ref, p1_ref[base + r], sem.at[1]).start()
        return carry

    def wait(r, carry):
        _row_copy(h_ref, r, xs_ref, p0_ref[base + r], sem.at[0]).wait()
        _row_copy(h_ref, r, xs_ref, p1_ref[base + r], sem.at[1]).wait()
        return carry

    lax.fori_loop(0, tm, issue, 0, unroll=8)
    lax.fori_loop(0, tm, wait, 0, unroll=8)


def _dispatch(pos0, pos1, h_rows, n_slots):
    t_tokens = pos0.shape[0]
    tm = 512
    xs0 = jnp.zeros((n_slots, D_MODEL), F32)
    return pl.pallas_call(
        _dispatch_kernel,
        out_shape=jax.ShapeDtypeStruct(xs0.shape, F32),
        grid_spec=pltpu.PrefetchScalarGridSpec(
            num_scalar_prefetch=2,
            grid=(t_tokens // tm,),
            in_specs=[pl.BlockSpec((tm, D_MODEL), lambda t, p0, p1: (t, 0)),
                      pl.BlockSpec(memory_space=pl.ANY)],
            out_specs=pl.BlockSpec(memory_space=pl.ANY),
            scratch_shapes=[pltpu.SemaphoreType.DMA((2,))]),
        input_output_aliases={3: 0},
        compiler_params=_params(("arbitrary",)),
        name="moe_dispatch",
    )(pos0, pos1, h_rows, xs0)


def _moe_ffn_kernel(te_ref, nu_ref, xs_ref, w1_ref, w3_ref, w2_ref, y_ref, xb, acc, *, n_f):
    del te_ref
    i = pl.program_id(0)
    f = pl.program_id(1)
    used = i < nu_ref[0]

    @pl.when(used & (f == 0))
    def _():
        xb[...] = xs_ref[...].astype(BF16)
        acc[...] = jnp.zeros(acc.shape, F32)

    @pl.when(used)
    def _():
        x = xb[...]
        a = jnp.dot(x, w1_ref[...].astype(BF16), preferred_element_type=F32)
        b = jnp.dot(x, w3_ref[...].astype(BF16), preferred_element_type=F32)
        acc[...] += _dot(_silu(a) * b, w2_ref[...])

    @pl.when(used & (f == n_f - 1))
    def _():
        y_ref[...] = acc[...]

    @pl.when(jnp.logical_not(used) & (f == n_f - 1))
    def _():
        y_ref[...] = jnp.zeros(y_ref.shape, F32)


def _moe_ffn(tile_expert, n_used, xs, w1, w3, w2, j):
    tm = MOE_TM
    n_tiles = xs.shape[0] // tm
    d_ff = w1.shape[-1]
    tf = 512
    n_f = d_ff // tf
    f_of = lambda i, f, nu: jnp.where(i < nu[0], f, n_f - 1)
    return pl.pallas_call(
        functools.partial(_moe_ffn_kernel, n_f=n_f),
        out_shape=jax.ShapeDtypeStruct(xs.shape, F32),
        grid_spec=pltpu.PrefetchScalarGridSpec(
            num_scalar_prefetch=2,
            grid=(n_tiles, n_f),
            in_specs=[pl.BlockSpec((tm, D_MODEL), lambda i, f, te, nu: (jnp.minimum(i, nu[0] - 1), 0)),
                      pl.BlockSpec((None, None, D_MODEL, tf), lambda i, f, te, nu: (j, te[i], 0, f_of(i, f, nu))),
                      pl.BlockSpec((None, None, D_MODEL, tf), lambda i, f, te, nu: (j, te[i], 0, f_of(i, f, nu))),
                      pl.BlockSpec((None, None, tf, D_MODEL), lambda i, f, te, nu: (j, te[i], f_of(i, f, nu), 0))],
            out_specs=pl.BlockSpec((tm, D_MODEL), lambda i, f, te, nu: (i, 0)),
            scratch_shapes=[pltpu.VMEM((tm, D_MODEL), BF16), pltpu.VMEM((tm, D_MODEL), F32)]),
        compiler_params=_params(("arbitrary", "arbitrary")),
        name="moe_ffn",
    )(tile_expert, n_used, xs, w1, w3, w2)


def _combine_kernel(p0_ref, p1_ref, xc_ref, xl_ref, mod_ref, route_ref, y_ref, oc_ref, ol_ref, buf0, buf1, sem, *,
                    n_ctx_tiles):
    tm = xc_ref.shape[0]
    t = pl.program_id(0)
    base = t * tm

    def issue(r, carry):
        _row_copy(y_ref, p0_ref[base + r], buf0, r, sem.at[0]).start()
        _row_copy(y_ref, p1_ref[base + r], buf1, r, sem.at[1]).start()
        return carry

    def wait(r, carry):
        _row_copy(y_ref, p0_ref[base + r], buf0, r, sem.at[0]).wait()
        _row_copy(y_ref, p1_ref[base + r], buf1, r, sem.at[1]).wait()
        return carry

    lax.fori_loop(0, tm, issue, 0, unroll=8)
    lax.fori_loop(0, tm, wait, 0, unroll=8)
    route = route_ref[...]
    w_a = route[:, R_W1:R_W1 + 1]
    w_b = route[:, R_W2:R_W2 + 1]
    y = w_a * buf0[...] + w_b * buf1[...]
    gated = mod_ref[:, 5 * D_MODEL:6 * D_MODEL] * y

    @pl.when(t < n_ctx_tiles)
    def _():
        oc_ref[...] = xc_ref[...] + gated

    @pl.when(t >= n_ctx_tiles)
    def _():
        ol_ref[...] = xl_ref[...] + gated


def _combine(pos0, pos1, xc, xl, mods3, route, y):
    tm = 512
    row = _mod_row("both", tm, None)
    nc, ctx_spec, lat_spec = _two_pass_specs(tm)
    return pl.pallas_call(
        functools.partial(_combine_kernel, n_ctx_tiles=nc),
        out_shape=[jax.ShapeDtypeStruct(xc.shape, F32), jax.ShapeDtypeStruct(xl.shape, F32)],
        grid_spec=pltpu.PrefetchScalarGridSpec(
            num_scalar_prefetch=2,
            grid=((xc.shape[0] + xl.shape[0]) // tm,),
            in_specs=[ctx_spec, lat_spec,
                      pl.BlockSpec((None, 1, 6 * D_MODEL), lambda t, p0, p1: (row(t), 0, 0)),
                      pl.BlockSpec((tm, route.shape[1]), lambda t, p0, p1: (t, 0)),
                      pl.BlockSpec(memory_space=pl.ANY)],
            out_specs=[ctx_spec, lat_spec],
            scratch_shapes=[pltpu.VMEM((tm, D_MODEL), F32), pltpu.VMEM((tm, D_MODEL), F32),
                            pltpu.SemaphoreType.DMA((2,))]),
        compiler_params=_params(("arbitrary",)),
        name="moe_combine",
    )(pos0, pos1, xc, xl, mods3, route, y)


def _moe(xc, xl, norm_g3, mods3, router, router_b, w1, w3, w2, layer, j):
    t_tokens = xc.shape[0] + xl.shape[0]
    tm = MOE_TM
    n_tiles = 2 * t_tokens // tm + N_EXPERTS
    h_rows, route, cnt = _router(xc, xl, norm_g3, mods3, router, router_b, layer, j)
    expert = route[:, R_E1:R_E2 + 1].astype(jnp.int32)
    rank = route[:, R_RANK1:R_RANK2 + 1].astype(jnp.int32)
    tiles = (cnt[0, :N_EXPERTS].astype(jnp.int32) + tm - 1) // tm
    ends = jnp.cumsum(tiles)
    starts = ends - tiles
    ids = jnp.arange(N_EXPERTS, dtype=jnp.int32)
    start_of = jnp.sum(jnp.where(expert[:, :, None] == ids, starts * tm, 0), axis=-1)
    pos = start_of + rank
    n_used = ends[-1:]
    tile_ids = jnp.arange(n_tiles, dtype=jnp.int32)
    tile_expert = jnp.sum((tile_ids[:, None] >= ends[None, :]).astype(jnp.int32), axis=1)
    last_expert = jnp.max(jnp.where(tiles > 0, ids, 0))
    tile_expert = jnp.where(tile_ids < n_used, tile_expert, last_expert)
    pos0, pos1 = pos[:, 0], pos[:, 1]
    xs = _dispatch(pos0, pos1, h_rows, n_tiles * tm)
    y = _moe_ffn(tile_expert, n_used, xs, w1, w3, w2, j)
    return _combine(pos0, pos1, xc, xl, mods3, route, y)


def _ffn_kernel(x_ref, g_ref, mod_ref, w1_ref, w3_ref, w2_ref, o_ref, h_s, acc, *, n_f):
    d = D_MODEL
    f = pl.program_id(1)

    @pl.when(f == 0)
    def _():
        h = _modulate(x_ref[...], g_ref[...], mod_ref[:, 3 * d:4 * d], mod_ref[:, 4 * d:5 * d])
        h_s[...] = h.astype(BF16)
        acc[...] = jnp.zeros(acc.shape, F32)

    h = h_s[...]
    a = jnp.dot(h, w1_ref[...], preferred_element_type=F32)
    b = jnp.dot(h, w3_ref[...], preferred_element_type=F32)
    acc[...] += _dot(_silu(a) * b, w2_ref[...])

    @pl.when(f == n_f - 1)
    def _():
        o_ref[...] = x_ref[...] + mod_ref[:, 5 * d:6 * d] * acc[...]


def _ffn(x, norm_g3, mods3, w1, w3, w2, layer, j, latent, seq_len):
    t_tokens = x.shape[0]
    d_ff = w1.shape[-1]
    tm = 512
    tf = d_ff // 2
    n_f = d_ff // tf
    row = _mod_row(latent, tm, seq_len)
    return pl.pallas_call(
        functools.partial(_ffn_kernel, n_f=n_f),
        out_shape=jax.ShapeDtypeStruct((t_tokens, D_MODEL), F32),
        grid=(t_tokens // tm, n_f),
        in_specs=[pl.BlockSpec((tm, D_MODEL), lambda t, f: (t, 0)),
                  pl.BlockSpec((None, 1, D_MODEL), lambda t, f: (layer, 0, 0)),
                  pl.BlockSpec((None, 1, 6 * D_MODEL), lambda t, f: (row(t), 0, 0)),
                  pl.BlockSpec((None, D_MODEL, tf), lambda t, f: (j, 0, f)),
                  pl.BlockSpec((None, D_MODEL, tf), lambda t, f: (j, 0, f)),
                  pl.BlockSpec((None, tf, D_MODEL), lambda t, f: (j, f, 0))],
        out_specs=pl.BlockSpec((tm, D_MODEL), lambda t, f: (t, 0)),
        scratch_shapes=[pltpu.VMEM((tm, D_MODEL), BF16), pltpu.VMEM((tm, D_MODEL), F32)],
        compiler_params=_params(("arbitrary", "arbitrary")),
        name="ffn",
    )(x, norm_g3, mods3, w1, w3, w2)


def _rope_tables():
    half = A_QK // 2
    nf = half // 2
    t = np.arange(DEC_SEQ)
    pos = np.stack([(t // GRID_W), (t % GRID_W)], axis=1).astype(np.float32)
    inv = jnp.asarray(ROPE_BASE, F32) ** (-jnp.arange(nf, dtype=F32) / nf)
    lane = np.arange(GW) % A_QK
    which = (lane >= half).astype(np.int32)
    freq = lane % nf
    ang = jnp.asarray(pos)[:, which] * inv[freq][None, :]
    return jnp.cos(ang), jnp.sin(ang)


def _tile_vec(v, reps):
    return jnp.tile(v, (1, reps))[:, None, :]


def _prepare_params(w_in, w_out, a_qnorm, a_knorm, a_lam, a_subln, b_gate, b_gate_bias, b_onorm,
                    c_dw, c_dw_b, c_ln_g, c_ln_b, d_qnorm, d_knorm):
    lr0 = 1536
    lr1 = lr0 + 2 * B_GATE_RANK
    pad = PROJ_W - w_in.shape[-1]
    w_in_p = jnp.concatenate([w_in[:, :, :lr0], w_in[:, :, lr1:], w_in[:, :, lr0:lr1],
                              jnp.zeros(w_in.shape[:2] + (pad,), w_in.dtype)], axis=-1).astype(BF16)
    hk = B_HEADS * B_DK
    wg = jnp.zeros((DEPTH, 2, GW, hk), F32)
    wg = wg.at[:, 0, 0:B_GATE_RANK].set(b_gate[:, 0])
    wg = wg.at[:, 1, B_GATE_RANK:2 * B_GATE_RANK].set(b_gate[:, 1])
    return dict(
        w_in=w_in_p, w_out=w_out.astype(BF16),
        a_qnorm=_tile_vec(a_qnorm, GW // A_QK), a_knorm=_tile_vec(a_knorm, GW // A_QK),
        a_lam=a_lam, a_subln=_tile_vec(a_subln, GW // A_V),
        b_gate=wg.astype(BF16), b_gate_bias=b_gate_bias[:, :, None, :], b_onorm=_tile_vec(b_onorm, GW // B_DV),
        c_dw=c_dw, c_dw_b=c_dw_b[:, None, :], c_ln_g=c_ln_g[:, None, :], c_ln_b=c_ln_b[:, None, :],
        d_qnorm=_tile_vec(d_qnorm, GW // D_HEAD), d_knorm=_tile_vec(d_knorm, GW // D_HEAD),
    )


def _lambda_init(i):
    return 0.8 - 0.6 * math.exp(-0.3 * i)


def kernel(x_prompt, x_sample, cache_a_k, cache_a_v, state_b, cache_d_k, cache_d_v, c, c_ctx,
           norm1_g, norm2_g, w_ada, b_ada, w_in, w_out, a_qnorm, a_knorm, a_lam, a_subln,
           b_gate, b_gate_bias, b_onorm, c_dw, c_dw_b, c_ln_g, c_ln_b, d_qnorm, d_knorm, d_rpb,
           ffn_w1, ffn_w3, ffn_w2, moe_router, moe_router_b, moe_w1, moe_w3, moe_w2):
    lp = _prepare_params(w_in, w_out, a_qnorm, a_knorm, a_lam, a_subln, b_gate, b_gate_bias, b_onorm,
                         c_dw, c_dw_b, c_ln_g, c_ln_b, d_qnorm, d_knorm)
    norm1 = norm1_g[:, None, :]
    norm2 = norm2_g[:, None, :]
    b_ada3 = b_ada[:, None, :]
    cond8 = jnp.concatenate([c_ctx[None, :], c, jnp.zeros((8 - 1 - DEC_BATCH, D_MODEL), F32)], axis=0)
    rope = _rope_tables()
    ffn_w = [w.astype(BF16) for w in (ffn_w1, ffn_w3, ffn_w2)]
    lanes = 128
    router_p = jnp.pad(moe_router, ((0, 0), (0, 0), (0, lanes - N_EXPERTS)))
    router_b_p = jnp.pad(moe_router_b, ((0, 0), (0, lanes - N_EXPERTS)))[:, None, :]
    ck_a = cache_a_k.reshape(DEC_BATCH, DEPTH, PAST_LEN, GW)
    cv_a = cache_a_v.reshape(DEC_BATCH, DEPTH, PAST_LEN, GW)
    ck_d = cache_d_k.reshape(DEC_BATCH, DEPTH, PAST_LEN, GW)
    cv_d = cache_d_v.reshape(DEC_BATCH, DEPTH, PAST_LEN, GW)
    s0_lat = _state_to_kernel_layout(state_b.transpose(1, 0, 2, 3, 4, 5).reshape(
        DEPTH * DEC_BATCH, 2, B_HEADS, B_DK, B_DV)).reshape(DEPTH, DEC_BATCH, 2, GW, B_HEADS * B_DK)

    xc = x_prompt.reshape(BATCH * SEQ, D_MODEL)
    xl = x_sample.reshape(DEC_BATCH * DEC_SEQ, D_MODEL)
    new_ak, new_av, new_sb, new_dk, new_dv = [], [], [], [], []
    for i in range(DEPTH):
        lam_init = _lambda_init(i)
        dense_ffn = i % 2 == 0
        mods3 =_adaln(cond8, w_ada, b_ada3, i)[:, None, :]

        proj = _inproj(xc, norm1, mods3, lp["w_in"], i, False, SEQ)
        o_a, ak = _attn_a(proj, lp, i, False, BATCH, SEQ, lam_init)
        o_f, s_f = _gla(proj, lp, i, BATCH, SEQ, False)
        o_b, s_b = _gla(proj, lp, i, BATCH, SEQ, True, o_fwd=o_f)
        o_c = _conv(proj, lp, i, BATCH, SEQ)
        o_d, dk = _attn_d_ctx(proj, lp, i, BATCH, SEQ)
        xc = _outproj(o_a, o_b, o_c, o_d, xc, mods3, lp["w_out"], i, False, SEQ)
        if dense_ffn:
            xc = _ffn(xc, norm2, mods3, *ffn_w, i, i // 2, False, SEQ)
        new_ak.append(ak.reshape(BATCH, SEQ, A_HEADS, 2 * A_QK))
        new_av.append(proj[:, BLK_AV * GW:(BLK_AV + 1) * GW].reshape(BATCH, SEQ, A_HEADS, A_V))
        new_sb.append(jnp.stack([_state_from_kernel_layout(s_f), _state_from_kernel_layout(s_b)], axis=1))
        new_dk.append(dk.reshape(BATCH, SEQ, D_HEADS, D_HEAD))
        new_dv.append(proj[:, BLK_DV * GW:(BLK_DV + 1) * GW].reshape(BATCH, SEQ, D_HEADS, D_HEAD))

        proj = _inproj(xl, norm1, mods3, lp["w_in"], i, True, DEC_SEQ)
        o_a, = _attn_a(proj, lp, i, True, DEC_BATCH, DEC_SEQ, lam_init, ck_a, cv_a, rope)
        o_f, _ = _gla(proj, lp, i, DEC_BATCH, DEC_SEQ, False, s0=s0_lat[i])
        o_b, _ = _gla(proj, lp, i, DEC_BATCH, DEC_SEQ, True, s0=s0_lat[i], o_fwd=o_f)
        o_c = _conv(proj, lp, i, DEC_BATCH, DEC_SEQ)
        o_d = _attn_d_lat(proj, lp, i, ck_d, cv_d, _na_bias_columns(d_rpb[i]))
        xl = _outproj(o_a, o_b, o_c, o_d, xl, mods3, lp["w_out"], i, True, DEC_SEQ)
        if dense_ffn:
            xl = _ffn(xl, norm2, mods3, *ffn_w, i, i // 2, True, DEC_SEQ)
        else:
            xc, xl = _moe(xc, xl, norm2, mods3, router_p, router_b_p, moe_w1, moe_w3, moe_w2, i, i // 2)

    return (xc.reshape(BATCH, SEQ, D_MODEL), xl.reshape(DEC_BATCH, DEC_SEQ, D_MODEL),
            jnp.stack(new_ak, axis=1), jnp.stack(new_av, axis=1), jnp.stack(new_sb, axis=1),
            jnp.stack(new_dk, axis=1), jnp.stack(new_dv, axis=1))
```

```python
import functools
import math

import numpy as np
import jax
import jax.numpy as jnp
from jax import lax
from jax.experimental import pallas as pl
from jax.experimental.pallas import tpu as pltpu

F32 = jnp.float32
BF16 = jnp.bfloat16
MIX_DTYPE = BF16

D_MODEL = 1024
BATCH = 16
SEQ = 256
DEPTH = 2
DEC_BATCH = 4
DEC_SEQ = 2048
PAST_LEN = 512
GRID_W = 64
A_HEADS = 4
A_QK = 32
A_V = 64
B_HEADS = 4
B_DK = 32
B_DV = 64
B_GATE_RANK = 16
B_TAU = 16.0
C_CH = 256
C_KSIZE = 31
D_HEADS = 4
D_HEAD = 64
NA_ROWS = 8
NA_COLS = 16
NA_QROWS = 4
NA_KROWS = 12
ROPE_BASE = 10000.0
D_FF = 2816
N_EXPERTS = 8
D_FF_EXPERT = 3584
EPS = 1e-6
NEG_INF = -1e30
LOG2_E = math.log2(math.e)

GW = 256
N_PROJ_BLOCKS = 12
PROJ_W = GW * N_PROJ_BLOCKS
BLK_AQ, BLK_AK, BLK_AV, BLK_BQK, BLK_BV, BLK_BG = 0, 1, 2, 3, 4, 5
BLK_C = 3
BLK_DQ, BLK_DK, BLK_DV, BLK_LR = 8, 9, 10, 11

GLA_SUB = 16
GLA_ROWS = 256
VMEM_LIMIT = 56 * 1024 * 1024


def _params(sem, vmem=VMEM_LIMIT):
    return pltpu.CompilerParams(dimension_semantics=sem, vmem_limit_bytes=vmem)


def _dot(a, b):
    return jnp.dot(a.astype(BF16), b.astype(BF16), preferred_element_type=F32)


def _dot_nt(a, b):
    return lax.dot_general(a.astype(BF16), b.astype(BF16), (((1,), (1,)), ((), ())),
                           preferred_element_type=F32)


def _dot_exact_rhs(x, m):
    hi = x.astype(BF16)
    lo = (x - hi.astype(F32)).astype(BF16)
    return (jnp.dot(hi, m, preferred_element_type=F32) + jnp.dot(lo, m, preferred_element_type=F32))


def _sigmoid(x):
    return 1.0 / (1.0 + jnp.exp(-x))


def _silu(x):
    return x * _sigmoid(x)


def _seg_matrix(n, seg):
    r = lax.broadcasted_iota(jnp.int32, (n, n), 0)
    c = lax.broadcasted_iota(jnp.int32, (n, n), 1)
    return jnp.where((r ^ c) < seg, 1.0, 0.0).astype(BF16)


def _seg_rmsnorm(x, seg, w):
    ms = _dot_exact_rhs(x * x, _seg_matrix(x.shape[-1], seg)) * (1.0 / seg)
    return x * lax.rsqrt(ms + EPS) * w


def _lane_mask(shape, lo, width):
    lane = lax.broadcasted_iota(jnp.int32, shape, len(shape) - 1)
    return (lane >= lo) & (lane < lo + width)


def _stack_heads(q, heads, width):
    return jnp.concatenate([jnp.where(_lane_mask(q.shape, h * width, width), q, 0.0) for h in range(heads)],
                           axis=0)


def _unstack_heads(o_all, heads, width):
    n = o_all.shape[0] // heads
    o = jnp.zeros((n, o_all.shape[1]), F32)
    for h in range(heads):
        blk = o_all[h * n:(h + 1) * n]
        o = jnp.where(_lane_mask(blk.shape, h * width, width), blk, o)
    return o


def _modulate(x, g, shift, scale):
    y = x * lax.rsqrt(jnp.mean(x * x, axis=-1, keepdims=True) + EPS)
    return y * g * (1.0 + scale) + shift


def _rope(x, cos, sin):
    lane = lax.broadcasted_iota(jnp.int32, x.shape, 1)
    w = x.shape[1]
    rot = jnp.where((lane & 15) < 8, -pltpu.roll(x, w - 8, 1), pltpu.roll(x, 8, 1))
    return x * cos + rot * sin


def _adaln_kernel(c_ref, w_ref, b_ref, o_ref):
    o_ref[...] = _dot(_silu(c_ref[...]), w_ref[...]) + b_ref[...]


def _adaln(cond8, w_ada, b_ada3, layer):
    tn = 1536
    n = 6 * D_MODEL
    return pl.pallas_call(
        _adaln_kernel,
        out_shape=jax.ShapeDtypeStruct((8, n), F32),
        grid=(n // tn,),
        in_specs=[pl.BlockSpec((8, D_MODEL), lambda j: (0, 0)),
                  pl.BlockSpec((None, D_MODEL, tn), lambda j: (layer, 0, j)),
                  pl.BlockSpec((None, 1, tn), lambda j: (layer, 0, j))],
        out_specs=pl.BlockSpec((8, tn), lambda j: (0, j)),
        compiler_params=_params(("arbitrary",)),
        name="adaln",
    )(cond8, w_ada, b_ada3)


def _mod_row(latent, tm, seq_len):
    if latent == "both":
        n_ctx = BATCH * SEQ
        return lambda t: jnp.where(t * tm < n_ctx, 0, 1 + jnp.maximum(t * tm - n_ctx, 0) // DEC_SEQ)
    if latent:
        return lambda t: 1 + (t * tm) // seq_len
    return lambda t: 0


def _inproj_kernel(x_ref, g_ref, mod_ref, w_ref, o_ref):
    d = D_MODEL
    h = _modulate(x_ref[...], g_ref[...], mod_ref[:, 0:d], mod_ref[:, d:2 * d])
    o_ref[...] = _dot(h, w_ref[...])


def _inproj(x, norm_g3, mods3, w_in, layer, latent, seq_len):
    t_tokens = x.shape[0]
    tm = 512
    row = _mod_row(latent, tm, seq_len)
    return pl.pallas_call(
        _inproj_kernel,
        out_shape=jax.ShapeDtypeStruct((t_tokens, PROJ_W), F32),
        grid=(t_tokens // tm,),
        in_specs=[pl.BlockSpec((tm, D_MODEL), lambda t: (t, 0)),
                  pl.BlockSpec((None, 1, D_MODEL), lambda t: (layer, 0, 0)),
                  pl.BlockSpec((None, 1, 6 * D_MODEL), lambda t: (row(t), 0, 0)),
                  pl.BlockSpec((None, D_MODEL, PROJ_W), lambda t: (layer, 0, 0))],
        out_specs=pl.BlockSpec((tm, PROJ_W), lambda t: (t, 0)),
        compiler_params=_params(("arbitrary",)),
        name="inproj",
    )(x, norm_g3, mods3, w_in)


def _diff_lambda(lam_ref, lam_init):
    lam = lam_ref[...]
    s1 = jnp.sum(lam[0:1] * lam[1:2], axis=1, keepdims=True)
    s2 = jnp.sum(lam[2:3] * lam[3:4], axis=1, keepdims=True)
    return jnp.exp(s1) - jnp.exp(s2) + lam_init


def _values_and_ones(v):
    one = jnp.where(lax.broadcasted_iota(jnp.int32, (v.shape[0], A_V), 1) == 0, 1.0, 0.0)
    parts = []
    for h in range(A_HEADS):
        parts += [v[:, h * A_V:(h + 1) * A_V], one]
    return jnp.concatenate(parts, axis=1).astype(BF16)


def _attn_a_kernel(*refs, n_own, n_ctx, tq, latent, lam_init):
    if latent:
        (q_ref, k_ref, v_ref, ck_ref, cv_ref, cos_ref, sin_ref, qn_ref, kn_ref, lam_ref, sub_ref,
         o_ref, ks, vs) = refs
    else:
        (q_ref, k_ref, v_ref, qn_ref, kn_ref, lam_ref, sub_ref, o_ref, ko_ref, ks, vs) = refs
    t = pl.program_id(1)

    @pl.when(t == 0)
    def _():
        kn = _seg_rmsnorm(k_ref[...], A_QK, kn_ref[...])
        if latent:
            kn = _rope(kn, cos_ref[...], sin_ref[...])
            ks[n_own:n_own + n_ctx, :] = ck_ref[...].astype(BF16)
            vs[n_own:n_own + n_ctx, :] = _values_and_ones(cv_ref[...])
        else:
            ko_ref[...] = kn
        ks[0:n_own, :] = kn.astype(BF16)
        vs[0:n_own, :] = _values_and_ones(v_ref[...])

    qn = _seg_rmsnorm(q_ref[...], A_QK, qn_ref[...])
    if latent:
        r0 = pl.multiple_of(t * tq, tq)
        qn = _rope(qn, cos_ref[pl.ds(r0, tq), :], sin_ref[pl.ds(r0, tq), :])
    qn = qn * (A_QK ** -0.5 * LOG2_E)
    lam = _diff_lambda(lam_ref, lam_init)
    k_all = ks[...]
    heads = []
    for h in range(A_HEADS):
        v_h = vs[:, h * 2 * A_V:(h + 1) * 2 * A_V]
        qm = jnp.concatenate([jnp.where(_lane_mask(qn.shape, (2 * h + m) * A_QK, A_QK), qn, 0.0)
                              for m in range(2)], axis=0)
        s = _dot_nt(qm, k_all)
        e = jnp.exp2(s - jnp.max(s, axis=-1, keepdims=True))
        acc = _dot(e, v_h)
        acc0, acc1 = acc[:tq], acc[tq:]
        inv0 = 1.0 / acc0[:, A_V:A_V + 1]
        inv1 = lam / acc1[:, A_V:A_V + 1]
        heads.append((acc0 * inv0 - acc1 * inv1)[:, :A_V])
    o = jnp.concatenate(heads, axis=1)
    o_ref[...] = (_seg_rmsnorm(o, A_V, sub_ref[...]) * (1.0 - lam_init)).astype(o_ref.dtype)


def _attn_a(proj, lp, layer, latent, nseq, seq_len, lam_init, cache_k=None, cache_v=None, rope=None):
    tq = min(512, seq_len)
    nt = seq_len // tq
    n_ctx = PAST_LEN if latent else 0
    kern = functools.partial(_attn_a_kernel, n_own=seq_len, n_ctx=n_ctx, tq=tq, latent=latent,
                             lam_init=lam_init)
    vec = lambda name: pl.BlockSpec((None, 1, GW), lambda b, t: (layer, 0, 0))
    in_specs = [pl.BlockSpec((tq, GW), lambda b, t: (b * nt + t, BLK_AQ)),
                pl.BlockSpec((seq_len, GW), lambda b, t: (b, BLK_AK)),
                pl.BlockSpec((seq_len, GW), lambda b, t: (b, BLK_AV))]
    args = [proj, proj, proj]
    if latent:
        in_specs += [pl.BlockSpec((None, None, PAST_LEN, GW), lambda b, t: (b, layer, 0, 0)),
                     pl.BlockSpec((None, None, PAST_LEN, GW), lambda b, t: (b, layer, 0, 0)),
                     pl.BlockSpec((seq_len, GW), lambda b, t: (0, 0)),
                     pl.BlockSpec((seq_len, GW), lambda b, t: (0, 0))]
        args += [cache_k, cache_v, rope[0], rope[1]]
    in_specs += [vec("q"), vec("k"),
                 pl.BlockSpec((None, 4, A_QK), lambda b, t: (layer, 0, 0)),
                 vec("s")]
    args += [lp["a_qnorm"], lp["a_knorm"], lp["a_lam"], lp["a_subln"]]
    out_shape = [jax.ShapeDtypeStruct((nseq * seq_len, GW), MIX_DTYPE)]
    out_specs = [pl.BlockSpec((tq, GW), lambda b, t: (b * nt + t, 0))]
    if not latent:
        out_shape.append(jax.ShapeDtypeStruct((nseq * seq_len, GW), F32))
        out_specs.append(pl.BlockSpec((seq_len, GW), lambda b, t: (b, 0)))
    return pl.pallas_call(
        kern,
        out_shape=out_shape,
        grid=(nseq, nt),
        in_specs=in_specs,
        out_specs=out_specs,
        scratch_shapes=[pltpu.VMEM((seq_len + n_ctx, GW), BF16), pltpu.VMEM((seq_len + n_ctx, 2 * GW), BF16)],
        compiler_params=_params(("arbitrary", "arbitrary")),
        name="attn_a_lat" if latent else "attn_a_ctx",
    )(*args)


def _attn_d_ctx_kernel(q_ref, k_ref, v_ref, qn_ref, kn_ref, o_ref, ko_ref):
    kn = _seg_rmsnorm(k_ref[...], D_HEAD, kn_ref[...])
    ko_ref[...] = kn
    qn = _seg_rmsnorm(q_ref[...], D_HEAD, qn_ref[...]) * (D_HEAD ** -0.5)
    s = _dot_nt(_stack_heads(qn, D_HEADS, D_HEAD), kn)
    e = jnp.exp(s - jnp.max(s, axis=-1, keepdims=True))
    o_all = _dot(e, v_ref[...]) * (1.0 / jnp.sum(e, axis=-1, keepdims=True))
    o_ref[...] = _unstack_heads(o_all, D_HEADS, D_HEAD).astype(o_ref.dtype)


def _attn_d_ctx(proj, lp, layer, nseq, seq_len):
    vec = pl.BlockSpec((None, 1, GW), lambda b: (layer, 0, 0))
    blk = lambda c: pl.BlockSpec((seq_len, GW), lambda b: (b, c))
    return pl.pallas_call(
        _attn_d_ctx_kernel,
        out_shape=[jax.ShapeDtypeStruct((nseq * seq_len, GW), MIX_DTYPE),
                   jax.ShapeDtypeStruct((nseq * seq_len, GW), F32)],
        grid=(nseq,),
        in_specs=[blk(BLK_DQ), blk(BLK_DK), blk(BLK_DV), vec, vec],
        out_specs=[blk(0), blk(0)],
        compiler_params=_params(("arbitrary",)),
        name="attn_d_ctx",
    )(proj, proj, proj, lp["d_qnorm"], lp["d_knorm"])


def _attn_d_lat_kernel(q_ref, k_ref, v_ref, ck_ref, cv_ref, qn_ref, kn_ref, cols_ref, o_ref,
                       ks, vs, cks, cvs, bias_ref):
    g = pl.program_id(1)
    n_rows = DEC_SEQ // GRID_W
    n_groups = n_rows // NA_QROWS
    n_loc = NA_KROWS * GRID_W

    @pl.when((pl.program_id(0) == 0) & (g == 0))
    def _():
        _fill_na_bias(cols_ref, bias_ref)

    @pl.when(g == 0)
    def _():
        ks[...] = _seg_rmsnorm(k_ref[...], D_HEAD, kn_ref[...]).astype(BF16)
        vs[...] = v_ref[...].astype(BF16)
        cks[...] = ck_ref[...].astype(BF16)
        cvs[...] = cv_ref[...].astype(BF16)

    qn = _seg_rmsnorm(q_ref[...], D_HEAD, qn_ref[...]) * (D_HEAD ** -0.5)
    row_start = jnp.clip(g * NA_QROWS - NA_ROWS // 2, 0, n_rows - NA_KROWS)
    variant = jnp.where(g == 0, 0, jnp.where(g == n_groups - 1, 2, 1))
    k0 = pl.multiple_of(row_start * GRID_W, GRID_W)
    kl = ks[pl.ds(k0, n_loc), :]
    vl = vs[pl.ds(k0, n_loc), :]
    kc = cks[...]
    vc = cvs[...]
    o = jnp.zeros(qn.shape, F32)
    for h in range(D_HEADS):
        hm = _lane_mask(qn.shape, h * D_HEAD, D_HEAD)
        qm = jnp.where(hm, qn, 0.0)
        s_loc = _dot_nt(qm, kl) + bias_ref[h, variant]
        s_ctx = _dot_nt(qm, kc)
        mx = jnp.maximum(jnp.max(s_loc, axis=-1, keepdims=True), jnp.max(s_ctx, axis=-1, keepdims=True))
        e_loc = jnp.exp(s_loc - mx)
        e_ctx = jnp.exp(s_ctx - mx)
        inv = 1.0 / (jnp.sum(e_loc, axis=-1, keepdims=True) + jnp.sum(e_ctx, axis=-1, keepdims=True))
        o = o + jnp.where(hm, (_dot(e_loc, vl) + _dot(e_ctx, vc)) * inv, 0.0)
    o_ref[...] = o.astype(o_ref.dtype)


def _na_bias_rows():
    n_rows = DEC_SEQ // GRID_W
    n_groups = n_rows // NA_QROWS
    plan = []
    for g in (0, 1, n_groups - 1):
        first_key_row = int(np.clip(g * NA_QROWS - NA_ROWS // 2, 0, n_rows - NA_KROWS))
        per_query_row = []
        for a in range(NA_QROWS):
            r = g * NA_QROWS + a
            win = int(np.clip(r - NA_ROWS // 2, 0, n_rows - NA_ROWS))
            per_query_row.append([first_key_row + i - r + NA_ROWS - 1
                                  if win <= first_key_row + i < win + NA_ROWS else None
                                  for i in range(NA_KROWS)])
        plan.append(per_query_row)
    return plan


def _fill_na_bias(cols_ref, bias_s):
    lane = lax.broadcasted_iota(jnp.int32, (GRID_W, 2 * GRID_W), 1)
    masked = jnp.full((GRID_W, 2 * GRID_W), NEG_INF, F32)
    for h in range(D_HEADS):
        for v, per_query_row in enumerate(_na_bias_rows()):
            for a, rows in enumerate(per_query_row):
                for p in range(NA_KROWS // 2):
                    left, right = rows[2 * p], rows[2 * p + 1]
                    lhs = masked if left is None else cols_ref[h, left]
                    rhs = masked if right is None else cols_ref[h, right]
                    bias_s[h, v, a * GRID_W:(a + 1) * GRID_W, 2 * p * GRID_W:2 * (p + 1) * GRID_W] = (
                        jnp.where(lane < GRID_W, lhs, rhs))


def _na_bias_columns(rpb):
    q = np.arange(GRID_W)
    kcol = np.arange(GRID_W)
    cs = np.clip(q - NA_COLS // 2, 0, GRID_W - NA_COLS)
    in_win = (kcol[None, :] >= cs[:, None]) & (kcol[None, :] < cs[:, None] + NA_COLS)
    dc = np.clip(kcol[None, :] - q[:, None] + (NA_COLS - 1), 0, 2 * NA_COLS - 2)
    pick_col = (dc[None] == np.arange(2 * NA_COLS - 1)[:, None, None]).astype(np.float32)
    cols = jnp.einsum("hrc,cqk->hrqk", rpb.astype(F32), jnp.asarray(pick_col), precision=lax.Precision.HIGHEST)
    cols = jnp.where(in_win[None, None], cols, NEG_INF)
    return jnp.concatenate([cols, cols], axis=-1)


def _attn_d_lat(proj, lp, layer, cache_k, cache_v, bias_cols):
    n_groups = DEC_SEQ // GRID_W // NA_QROWS
    tq = NA_QROWS * GRID_W
    vec = pl.BlockSpec((None, 1, GW), lambda b, r: (layer, 0, 0))
    seq = lambda c: pl.BlockSpec((DEC_SEQ, GW), lambda b, r: (b, c))
    cache = pl.BlockSpec((None, None, PAST_LEN, GW), lambda b, r: (b, layer, 0, 0))
    return pl.pallas_call(
        _attn_d_lat_kernel,
        out_shape=jax.ShapeDtypeStruct((DEC_BATCH * DEC_SEQ, GW), MIX_DTYPE),
        grid=(DEC_BATCH, n_groups),
        in_specs=[pl.BlockSpec((tq, GW), lambda b, r: (b * n_groups + r, BLK_DQ)),
                  seq(BLK_DK), seq(BLK_DV), cache, cache, vec, vec,
                  pl.BlockSpec(bias_cols.shape, lambda b, r: (0, 0, 0, 0))],
        out_specs=pl.BlockSpec((tq, GW), lambda b, r: (b * n_groups + r, 0)),
        scratch_shapes=[pltpu.VMEM((DEC_SEQ, GW), BF16), pltpu.VMEM((DEC_SEQ, GW), BF16),
                        pltpu.VMEM((PAST_LEN, GW), BF16), pltpu.VMEM((PAST_LEN, GW), BF16),
                        pltpu.VMEM((D_HEADS, 3, tq, NA_KROWS * GRID_W), F32)],
        compiler_params=_params(("arbitrary", "arbitrary")),
        name="attn_d_lat",
    )(proj, proj, proj, cache_k, cache_v, lp["d_qnorm"], lp["d_knorm"], bias_cols)


def _conv_kernel(c_ref, w_ref, b_ref, g_ref, beta_ref, o_ref, pad, *, seq_len):
    half = C_KSIZE // 2
    top = 16
    cin = c_ref[...]
    u = cin[:, :C_CH] * _sigmoid(cin[:, C_CH:])
    pad[0:top, :] = jnp.zeros((top, C_CH), F32)
    pad[top + seq_len:top + seq_len + top, :] = jnp.zeros((top, C_CH), F32)
    pad[top:top + seq_len, :] = u
    w = w_ref[...]
    rb = 256
    for r0 in range(0, seq_len, rb):
        acc = jnp.zeros((rb, C_CH), F32)
        n_hi = -(-C_KSIZE // 8)
        for lo in range(8):
            base = r0 + top - half + lo
            shifted = pad[base:base + rb + 8 * (n_hi - 1), :]
            for hi in range(n_hi):
                k = 8 * hi + lo
                if k < C_KSIZE:
                    acc = acc + shifted[8 * hi:8 * hi + rb, :] * w[k:k + 1, :]
        acc = acc + b_ref[...]
        mu = jnp.mean(acc, axis=-1, keepdims=True)
        xc = acc - mu
        y = xc * lax.rsqrt(jnp.mean(xc * xc, axis=-1, keepdims=True) + EPS) * g_ref[...] + beta_ref[...]
        o_ref[r0:r0 + rb, :] = _silu(y).astype(o_ref.dtype)


def _conv(proj, lp, layer, nseq, seq_len):
    vec = pl.BlockSpec((None, 1, C_CH), lambda b: (layer, 0, 0))
    return pl.pallas_call(
        functools.partial(_conv_kernel, seq_len=seq_len),
        out_shape=jax.ShapeDtypeStruct((nseq * seq_len, C_CH), MIX_DTYPE),
        grid=(nseq,),
        in_specs=[pl.BlockSpec((seq_len, 2 * C_CH), lambda b: (b, BLK_C)),
                  pl.BlockSpec((None, C_KSIZE, C_CH), lambda b: (layer, 0, 0)),
                  vec, vec, vec],
        out_specs=pl.BlockSpec((seq_len, C_CH), lambda b: (b, 0)),
        scratch_shapes=[pltpu.VMEM((seq_len + 32, C_CH), F32)],
        compiler_params=_params(("arbitrary",)),
        name="conv",
    )(proj, lp["c_dw"], lp["c_dw_b"], lp["c_ln_g"], lp["c_ln_b"])


def _gla_kernel(*refs, reverse, has_s0, n_blocks):
    refs = list(refs)
    qk_ref, v_ref, lr_ref, wg_ref, gb_ref = refs[:5]
    refs = refs[5:]
    s0_ref = refs.pop(0) if has_s0 else None
    if reverse:
        of_ref, bg_ref, on_ref = refs[:3]
        refs = refs[3:]
    o_ref, so_ref, st = refs
    rb, cs = GLA_ROWS, GLA_SUB
    hk = B_HEADS * B_DK
    j = pl.program_id(1)

    @pl.when(j == 0)
    def _():
        st[...] = s0_ref[...] if has_s0 else jnp.zeros(st.shape, F32)

    qk = qk_ref[...]
    q = qk[:, :hk] * (B_DK ** -0.5)
    k = qk[:, hk:]
    v = v_ref[...]
    pre = _dot(lr_ref[...], wg_ref[...]) + gb_ref[...]
    g = (jnp.minimum(pre, 0.0) - jnp.log(1.0 + jnp.exp(-jnp.abs(pre)))) * (1.0 / B_TAU)

    pos = lax.broadcasted_iota(jnp.int32, (rb, hk), 0) & (cs - 1)
    pre_sum = g
    suf_sum = g
    step = 1
    while step < cs:
        pre_sum = pre_sum + jnp.where(pos >= step, pltpu.roll(pre_sum, step, 0), 0.0)
        suf_sum = suf_sum + jnp.where(pos < cs - step, pltpu.roll(suf_sum, rb - step, 0), 0.0)
        step *= 2
    total = pre_sum + suf_sum - g
    if reverse:
        z = suf_sum
        k_dec = k * jnp.exp(pre_sum - g)
    else:
        z = pre_sum
        k_dec = k * jnp.exp(suf_sum - g)
    q_dec = q * jnp.exp(z)

    r_i = lax.broadcasted_iota(jnp.int32, (hk, GW), 0)
    c_i = lax.broadcasted_iota(jnp.int32, (hk, GW), 1)
    head_sum = jnp.where((r_i >> 5) == (c_i >> 6), 1.0, 0.0).astype(BF16)

    row_in_sub = lax.broadcasted_iota(jnp.int32, (cs, hk), 0)
    intra = []
    for n in range(rb // cs):
        qs = q[n * cs:(n + 1) * cs]
        zs = z[n * cs:(n + 1) * cs]
        pairs = []
        for jl in range(cs):
            r = n * cs + jl
            decay = jnp.exp(jnp.minimum(zs - z[r:r + 1], 0.0))
            keep = (row_in_sub <= jl) if reverse else (row_in_sub >= jl)
            pairs.append(jnp.where(keep, qs * k[r:r + 1] * decay, 0.0))
        w = jnp.dot(jnp.concatenate(pairs, axis=0).astype(BF16), head_sum, preferred_element_type=F32)
        acc = jnp.zeros((cs, GW), F32)
        for jl in range(cs):
            r = n * cs + jl
            acc = acc + w[jl * cs:(jl + 1) * cs] * v[r:r + 1]
        intra.append(acc)
    o = jnp.concatenate(intra, axis=0)

    r_s = lax.broadcasted_iota(jnp.int32, st.shape, 0)
    c_s = lax.broadcasted_iota(jnp.int32, st.shape, 1)
    diag = (r_s >> 6) == (c_s >> 5)
    n_sub = rb // cs
    row = lax.broadcasted_iota(jnp.int32, (rb, hk), 0)

    def per_sub_chunk(x):
        return jnp.concatenate([jnp.where((row >= n * cs) & (row < (n + 1) * cs), x, 0.0).astype(BF16)
                                for n in range(n_sub)], axis=1)

    kv_all = _dot(v.T, per_sub_chunk(k_dec))
    state = st[...]
    states = [None] * n_sub
    order = range(n_sub - 1, -1, -1) if reverse else range(n_sub)
    for n in order:
        states[n] = state.astype(BF16)
        decay = jnp.exp(total[n * cs:n * cs + 1, :])
        state = state * decay + jnp.where(diag, kv_all[:, n * hk:(n + 1) * hk], 0.0)
    st[...] = state
    o = o + _dot_nt(per_sub_chunk(q_dec), jnp.concatenate(states, axis=1))

    if reverse:
        o = _seg_rmsnorm(o + of_ref[...], B_DV, on_ref[...]) * _silu(bg_ref[...])
    o_ref[...] = o.astype(o_ref.dtype)

    @pl.when(j == n_blocks - 1)
    def _():
        so_ref[...] = state


def _gla(proj, lp, layer, nseq, seq_len, reverse, s0=None, o_fwd=None):
    rb = GLA_ROWS
    nb = seq_len // rb
    hk = B_HEADS * B_DK
    blk = (lambda b, j: b * nb + nb - 1 - j) if reverse else (lambda b, j: b * nb + j)
    tile = lambda c: pl.BlockSpec((rb, GW), lambda b, j: (blk(b, j), c))
    d = 1 if reverse else 0
    in_specs = [tile(BLK_BQK), tile(BLK_BV), tile(BLK_LR),
                pl.BlockSpec((None, None, GW, hk), lambda b, j: (layer, d, 0, 0)),
                pl.BlockSpec((None, None, 1, hk), lambda b, j: (layer, d, 0, 0))]
    args = [proj, proj, proj, lp["b_gate"], lp["b_gate_bias"]]
    if s0 is not None:
        in_specs.append(pl.BlockSpec((None, None, GW, hk), lambda b, j: (b, d, 0, 0)))
        args.append(s0)
    if reverse:
        in_specs += [tile(0), tile(BLK_BG), pl.BlockSpec((None, 1, GW), lambda b, j: (layer, 0, 0))]
        args += [o_fwd, proj, lp["b_onorm"]]
    return pl.pallas_call(
        functools.partial(_gla_kernel, reverse=reverse, has_s0=s0 is not None, n_blocks=nb),
        out_shape=[jax.ShapeDtypeStruct((nseq * seq_len, GW), MIX_DTYPE if reverse else F32),
                   jax.ShapeDtypeStruct((nseq, GW, hk), F32)],
        grid=(nseq, nb),
        in_specs=in_specs,
        out_specs=[tile(0), pl.BlockSpec((None, GW, hk), lambda b, j: (b, 0, 0))],
        scratch_shapes=[pltpu.VMEM((GW, hk), F32)],
        compiler_params=_params(("arbitrary", "arbitrary")),
        name="gla_bwd" if reverse else "gla_fwd",
    )(*args)


def _state_to_kernel_layout(s):
    eye = jnp.eye(B_HEADS, dtype=s.dtype)
    t = jnp.einsum("bxhde,hg->bxhegd", s, eye)
    return t.reshape(s.shape[0], 2, B_HEADS * B_DV, B_HEADS * B_DK)


def _state_from_kernel_layout(st):
    t = st.reshape(st.shape[0], B_HEADS, B_DV, B_HEADS, B_DK)
    return jnp.stack([t[:, h, :, h, :] for h in range(B_HEADS)], axis=1).transpose(0, 1, 3, 2)


def _outproj_kernel(a_ref, b_ref, c_ref, d_ref, x_ref, mod_ref, w_ref, o_ref):
    mix = jnp.concatenate([a_ref[...], b_ref[...], c_ref[...], d_ref[...]], axis=-1)
    gate = mod_ref[:, 2 * D_MODEL:3 * D_MODEL]
    o_ref[...] = x_ref[...] + gate * _dot(mix, w_ref[...])


def _outproj(o_a, o_b, o_c, o_d, x, mods3, w_out, layer, latent, seq_len):
    t_tokens = x.shape[0]
    tm = 512
    row = _mod_row(latent, tm, seq_len)
    part = pl.BlockSpec((tm, GW), lambda t: (t, 0))
    return pl.pallas_call(
        _outproj_kernel,
        out_shape=jax.ShapeDtypeStruct((t_tokens, D_MODEL), F32),
        grid=(t_tokens // tm,),
        in_specs=[part, part, part, part,
                  pl.BlockSpec((tm, D_MODEL), lambda t: (t, 0)),
                  pl.BlockSpec((None, 1, 6 * D_MODEL), lambda t: (row(t), 0, 0)),
                  pl.BlockSpec((None, 4 * GW, D_MODEL), lambda t: (layer, 0, 0))],
        out_specs=pl.BlockSpec((tm, D_MODEL), lambda t: (t, 0)),
        compiler_params=_params(("arbitrary",)),
        name="outproj",
    )(o_a, o_b, o_c, o_d, x, mods3, w_out)


MOE_TM = 1024
R_E1, R_E2, R_W1, R_W2, R_RANK1, R_RANK2 = 0, 1, 2, 3, 4, 5


def _router_kernel(xc_ref, xl_ref, g_ref, mod_ref, r_ref, rb_ref, h_ref, route_ref, cnt_ref, carry, *,
                   n_ctx_tiles):
    d = D_MODEL

    @pl.when(pl.program_id(0) == 0)
    def _():
        carry[...] = jnp.zeros(carry.shape, F32)

    x = jnp.where(pl.program_id(0) < n_ctx_tiles, xc_ref[...], xl_ref[...])
    h = _modulate(x, g_ref[...], mod_ref[:, 3 * d:4 * d], mod_ref[:, 4 * d:5 * d])
    tm = h.shape[0]
    h_ref[...] = h
    r = r_ref[...]
    h_hi = h.astype(BF16)
    h_lo = (h - h_hi.astype(F32)).astype(BF16)
    r_hi = r.astype(BF16)
    r_lo = (r - r_hi.astype(F32)).astype(BF16)
    dot = lambda a, b: jnp.dot(a, b, preferred_element_type=F32)
    logits = dot(h_hi, r_hi) + dot(h_lo, r_hi) + dot(h_hi, r_lo) + rb_ref[...]
    lane = lax.broadcasted_iota(jnp.int32, logits.shape, 1).astype(F32)
    big = float(logits.shape[1])
    logits = jnp.where(lane < N_EXPERTS, logits, -jnp.inf)
    v1 = jnp.max(logits, axis=-1, keepdims=True)
    i1 = jnp.min(jnp.where(logits == v1, lane, big), axis=-1, keepdims=True)
    rest = jnp.where(lane == i1, -jnp.inf, logits)
    v2 = jnp.max(rest, axis=-1, keepdims=True)
    i2 = jnp.min(jnp.where(rest == v2, lane, big), axis=-1, keepdims=True)
    e2 = jnp.exp(v2 - v1)
    inv = 1.0 / (1.0 + e2)
    sel = jnp.where((lane == i1) | (lane == i2), 1.0, 0.0)
    r_i = lax.broadcasted_iota(jnp.int32, (tm, tm), 0)
    c_i = lax.broadcasted_iota(jnp.int32, (tm, tm), 1)
    before = jnp.where(c_i < r_i, 1.0, 0.0).astype(BF16)
    rank = jnp.dot(before, sel.astype(BF16), preferred_element_type=F32) + carry[...]
    count = carry[...] + jnp.sum(sel, axis=0, keepdims=True)
    carry[...] = count
    cnt_ref[...] = jnp.broadcast_to(count, cnt_ref.shape)
    rank1 = jnp.sum(jnp.where(lane == i1, rank, 0.0), axis=-1, keepdims=True)
    rank2 = jnp.sum(jnp.where(lane == i2, rank, 0.0), axis=-1, keepdims=True)
    rec = jnp.zeros(logits.shape, F32)
    for pos, val in ((R_E1, i1), (R_E2, i2), (R_W1, inv), (R_W2, e2 * inv), (R_RANK1, rank1), (R_RANK2, rank2)):
        rec = jnp.where(lane == pos, val, rec)
    route_ref[...] = rec


def _two_pass_specs(tm):
    nc = BATCH * SEQ // tm
    ctx = pl.BlockSpec((tm, D_MODEL), lambda t, *_: (jnp.minimum(t, nc - 1), 0))
    lat = pl.BlockSpec((tm, D_MODEL), lambda t, *_: (jnp.maximum(t - nc, 0), 0))
    return nc, ctx, lat


def _router(xc, xl, norm_g3, mods3, router, router_b, layer, j):
    t_tokens = xc.shape[0] + xl.shape[0]
    tm = 512
    row = _mod_row("both", tm, None)
    lanes = router.shape[-1]
    nc, ctx_spec, lat_spec = _two_pass_specs(tm)
    return pl.pallas_call(
        functools.partial(_router_kernel, n_ctx_tiles=nc),
        out_shape=[jax.ShapeDtypeStruct((t_tokens, D_MODEL), F32),
                   jax.ShapeDtypeStruct((t_tokens, lanes), F32),
                   jax.ShapeDtypeStruct((8, lanes), F32)],
        grid=(t_tokens // tm,),
        in_specs=[ctx_spec, lat_spec,
                  pl.BlockSpec((None, 1, D_MODEL), lambda t: (layer, 0, 0)),
                  pl.BlockSpec((None, 1, 6 * D_MODEL), lambda t: (row(t), 0, 0)),
                  pl.BlockSpec((None, D_MODEL, lanes), lambda t: (j, 0, 0)),
                  pl.BlockSpec((None, 1, lanes), lambda t: (j, 0, 0))],
        out_specs=[pl.BlockSpec((tm, D_MODEL), lambda t: (t, 0)),
                   pl.BlockSpec((tm, lanes), lambda t: (t, 0)),
                   pl.BlockSpec((8, lanes), lambda t: (0, 0))],
        scratch_shapes=[pltpu.VMEM((1, lanes), F32)],
        compiler_params=_params(("arbitrary",)),
        name="router",
    )(xc, xl, norm_g3, mods3, router, router_b)


def _row_copy(src, src_row, dst, dst_row, sem):
    return pltpu.make_async_copy(src.at[pl.ds(src_row, 1), :], dst.at[pl.ds(dst_row, 1), :], sem)


def _dispatch_kernel(p0_ref, p1_ref, h_ref, xs_in_ref, xs_ref, sem):
    del xs_in_ref
    tm = h_ref.shape[0]
    base = pl.program_id(0) * tm

    def issue(r, carry):
        _row_copy(h_ref, r, xs_ref, p0_ref[base + r], sem.at[0]).start(priority=0)
        _row_copy(h_ref, r, xs_ref, p1_ref[base + r], sem.at[1]).start(priority=1)
        return carry

    def wait(r, carry):
        _row_copy(h_ref, r, xs_ref, p0_ref[base + r], sem.at[0]).wait()
        _row_copy(h_ref, r, xs_ref, p1_ref[base + r], sem.at[1]).wait()
        return carry

    lax.fori_loop(0, tm, issue, 0, unroll=8)
    lax.fori_loop(0, tm, wait, 0, unroll=8)


def _dispatch(pos0, pos1, h_rows, n_slots):
    t_tokens = pos0.shape[0]
    tm = 512
    xs0 = jnp.zeros((n_slots, D_MODEL), F32)
    return pl.pallas_call(
        _dispatch_kernel,
        out_shape=jax.ShapeDtypeStruct(xs0.shape, F32),
        grid_spec=pltpu.PrefetchScalarGridSpec(
            num_scalar_prefetch=2,
            grid=(t_tokens // tm,),
            in_specs=[pl.BlockSpec((tm, D_MODEL), lambda t, p0, p1: (t, 0)),
                      pl.BlockSpec(memory_space=pl.ANY)],
            out_specs=pl.BlockSpec(memory_space=pl.ANY),
            scratch_shapes=[pltpu.SemaphoreType.DMA((2,))]),
        input_output_aliases={3: 0},
        compiler_params=_params(("arbitrary",)),
        name="moe_dispatch",
    )(pos0, pos1, h_rows, xs0)


def _moe_ffn_kernel(te_ref, nu_ref, xs_ref, w1_ref, w3_ref, w2_ref, y_ref, xb, acc, *, n_f):
    del te_ref
    i = pl.program_id(0)
    f = pl.program_id(1)
    used = i < nu_ref[0]

    @pl.when(used & (f == 0))
    def _():
        xb[...] = xs_ref[...].astype(BF16)
        acc[...] = jnp.zeros(acc.shape, F32)

    @pl.when(used)
    def _():
        x = xb[...]
        a = jnp.dot(x, w1_ref[...].astype(BF16), preferred_element_type=F32)
        b = jnp.dot(x, w3_ref[...].astype(BF16), preferred_element_type=F32)
        acc[...] += _dot(_silu(a) * b, w2_ref[...])

    @pl.when(used & (f == n_f - 1))
    def _():
        y_ref[...] = acc[...]

    @pl.when(jnp.logical_not(used) & (f == n_f - 1))
    def _():
        y_ref[...] = jnp.zeros(y_ref.shape, F32)


def _moe_ffn(tile_expert, n_used, xs, w1, w3, w2, j):
    tm = MOE_TM
    n_tiles = xs.shape[0] // tm
    d_ff = w1.shape[-1]
    tf = 512
    n_f = d_ff // tf
    f_of = lambda i, f, nu: jnp.where(i < nu[0], f, n_f - 1)
    return pl.pallas_call(
        functools.partial(_moe_ffn_kernel, n_f=n_f),
        out_shape=jax.ShapeDtypeStruct(xs.shape, F32),
        grid_spec=pltpu.PrefetchScalarGridSpec(
            num_scalar_prefetch=2,
            grid=(n_tiles, n_f),
            in_specs=[pl.BlockSpec((tm, D_MODEL), lambda i, f, te, nu: (jnp.minimum(i, nu[0] - 1), 0)),
                      pl.BlockSpec((None, None, D_MODEL, tf), lambda i, f, te, nu: (j, te[i], 0, f_of(i, f, nu))),
                      pl.BlockSpec((None, None, D_MODEL, tf), lambda i, f, te, nu: (j, te[i], 0, f_of(i, f, nu))),
                      pl.BlockSpec((None, None, tf, D_MODEL), lambda i, f, te, nu: (j, te[i], f_of(i, f, nu), 0))],
            out_specs=pl.BlockSpec((tm, D_MODEL), lambda i, f, te, nu: (i, 0)),
            scratch_shapes=[pltpu.VMEM((tm, D_MODEL), BF16), pltpu.VMEM((tm, D_MODEL), F32)]),
        compiler_params=_params(("arbitrary", "arbitrary")),
        name="moe_ffn",
    )(tile_expert, n_used, xs, w1, w3, w2)


def _combine_kernel(p0_ref, p1_ref, xc_ref, xl_ref, mod_ref, route_ref, y_ref, oc_ref, ol_ref, buf0, buf1, sem, *,
                    n_ctx_tiles):
    tm = xc_ref.shape[0]
    t = pl.program_id(0)
    base = t * tm

    def issue(r, carry):
        _row_copy(y_ref, p0_ref[base + r], buf0, r, sem.at[0]).start(priority=0)
        _row_copy(y_ref, p1_ref[base + r], buf1, r, sem.at[1]).start(priority=1)
        return carry

    def wait(r, carry):
        _row_copy(y_ref, p0_ref[base + r], buf0, r, sem.at[0]).wait()
        _row_copy(y_ref, p1_ref[base + r], buf1, r, sem.at[1]).wait()
        return carry

    lax.fori_loop(0, tm, issue, 0, unroll=8)
    lax.fori_loop(0, tm, wait, 0, unroll=8)
    route = route_ref[...]
    w_a = route[:, R_W1:R_W1 + 1]
    w_b = route[:, R_W2:R_W2 + 1]
    y = w_a * buf0[...] + w_b * buf1[...]
    gated = mod_ref[:, 5 * D_MODEL:6 * D_MODEL] * y

    @pl.when(t < n_ctx_tiles)
    def _():
        oc_ref[...] = xc_ref[...] + gated

    @pl.when(t >= n_ctx_tiles)
    def _():
        ol_ref[...] = xl_ref[...] + gated


def _combine(pos0, pos1, xc, xl, mods3, route, y):
    tm = 512
    row = _mod_row("both", tm, None)
    nc, ctx_spec, lat_spec = _two_pass_specs(tm)
    return pl.pallas_call(
        functools.partial(_combine_kernel, n_ctx_tiles=nc),
        out_shape=[jax.ShapeDtypeStruct(xc.shape, F32), jax.ShapeDtypeStruct(xl.shape, F32)],
        grid_spec=pltpu.PrefetchScalarGridSpec(
            num_scalar_prefetch=2,
            grid=((xc.shape[0] + xl.shape[0]) // tm,),
            in_specs=[ctx_spec, lat_spec,
                      pl.BlockSpec((None, 1, 6 * D_MODEL), lambda t, p0, p1: (row(t), 0, 0)),
                      pl.BlockSpec((tm, route.shape[1]), lambda t, p0, p1: (t, 0)),
                      pl.BlockSpec(memory_space=pl.ANY)],
            out_specs=[ctx_spec, lat_spec],
            scratch_shapes=[pltpu.VMEM((tm, D_MODEL), F32), pltpu.VMEM((tm, D_MODEL), F32),
                            pltpu.SemaphoreType.DMA((2,))]),
        compiler_params=_params(("arbitrary",)),
        name="moe_combine",
    )(pos0, pos1, xc, xl, mods3, route, y)


def _moe(xc, xl, norm_g3, mods3, router, router_b, w1, w3, w2, layer, j):
    t_tokens = xc.shape[0] + xl.shape[0]
    tm = MOE_TM
    n_tiles = 2 * t_tokens // tm + N_EXPERTS
    h_rows, route, cnt = _router(xc, xl, norm_g3, mods3, router, router_b, layer, j)
    expert = route[:, R_E1:R_E2 + 1].astype(jnp.int32)
    rank = route[:, R_RANK1:R_RANK2 + 1].astype(jnp.int32)
    tiles = (cnt[0, :N_EXPERTS].astype(jnp.int32) + tm - 1) // tm
    ends = jnp.cumsum(tiles)
    starts = ends - tiles
    ids = jnp.arange(N_EXPERTS, dtype=jnp.int32)
    start_of = jnp.sum(jnp.where(expert[:, :, None] == ids, starts * tm, 0), axis=-1)
    pos = start_of + rank
    n_used = ends[-1:]
    tile_ids = jnp.arange(n_tiles, dtype=jnp.int32)
    tile_expert = jnp.sum((tile_ids[:, None] >= ends[None, :]).astype(jnp.int32), axis=1)
    last_expert = jnp.max(jnp.where(tiles > 0, ids, 0))
    tile_expert = jnp.where(tile_ids < n_used, tile_expert, last_expert)
    pos0, pos1 = pos[:, 0], pos[:, 1]
    xs = _dispatch(pos0, pos1, h_rows, n_tiles * tm)
    y = _moe_ffn(tile_expert, n_used, xs, w1, w3, w2, j)
    return _combine(pos0, pos1, xc, xl, mods3, route, y)


def _ffn_kernel(x_ref, g_ref, mod_ref, w1_ref, w3_ref, w2_ref, o_ref, h_s, acc, *, n_f):
    d = D_MODEL
    f = pl.program_id(1)

    @pl.when(f == 0)
    def _():
        h = _modulate(x_ref[...], g_ref[...], mod_ref[:, 3 * d:4 * d], mod_ref[:, 4 * d:5 * d])
        h_s[...] = h.astype(BF16)
        acc[...] = jnp.zeros(acc.shape, F32)

    h = h_s[...]
    a = jnp.dot(h, w1_ref[...], preferred_element_type=F32)
    b = jnp.dot(h, w3_ref[...], preferred_element_type=F32)
    acc[...] += _dot(_silu(a) * b, w2_ref[...])

    @pl.when(f == n_f - 1)
    def _():
        o_ref[...] = x_ref[...] + mod_ref[:, 5 * d:6 * d] * acc[...]


def _ffn(x, norm_g3, mods3, w1, w3, w2, layer, j, latent, seq_len):
    t_tokens = x.shape[0]
    d_ff = w1.shape[-1]
    tm = 512
    tf = d_ff // 2
    n_f = d_ff // tf
    row = _mod_row(latent, tm, seq_len)
    return pl.pallas_call(
        functools.partial(_ffn_kernel, n_f=n_f),
        out_shape=jax.ShapeDtypeStruct((t_tokens, D_MODEL), F32),
        grid=(t_tokens // tm, n_f),
        in_specs=[pl.BlockSpec((tm, D_MODEL), lambda t, f: (t, 0)),
                  pl.BlockSpec((None, 1, D_MODEL), lambda t, f: (layer, 0, 0)),
                  pl.BlockSpec((None, 1, 6 * D_MODEL), lambda t, f: (row(t), 0, 0)),
                  pl.BlockSpec((None, D_MODEL, tf), lambda t, f: (j, 0, f)),
                  pl.BlockSpec((None, D_MODEL, tf), lambda t, f: (j, 0, f)),
                  pl.BlockSpec((None, tf, D_MODEL), lambda t, f: (j, f, 0))],
        out_specs=pl.BlockSpec((tm, D_MODEL), lambda t, f: (t, 0)),
        scratch_shapes=[pltpu.VMEM((tm, D_MODEL), BF16), pltpu.VMEM((tm, D_MODEL), F32)],
        compiler_params=_params(("arbitrary", "arbitrary")),
        name="ffn",
    )(x, norm_g3, mods3, w1, w3, w2)


def _rope_tables():
    half = A_QK // 2
    nf = half // 2
    t = np.arange(DEC_SEQ)
    pos = np.stack([(t // GRID_W), (t % GRID_W)], axis=1).astype(np.float32)
    inv = jnp.asarray(ROPE_BASE, F32) ** (-jnp.arange(nf, dtype=F32) / nf)
    lane = np.arange(GW) % A_QK
    which = (lane >= half).astype(np.int32)
    freq = lane % nf
    ang = jnp.asarray(pos)[:, which] * inv[freq][None, :]
    return jnp.cos(ang), jnp.sin(ang)


def _tile_vec(v, reps):
    return jnp.tile(v, (1, reps))[:, None, :]


def _prepare_params(w_in, w_out, a_qnorm, a_knorm, a_lam, a_subln, b_gate, b_gate_bias, b_onorm,
                    c_dw, c_dw_b, c_ln_g, c_ln_b, d_qnorm, d_knorm):
    lr0 = 1536
    lr1 = lr0 + 2 * B_GATE_RANK
    pad = PROJ_W - w_in.shape[-1]
    w_in_p = jnp.concatenate([w_in[:, :, :lr0], w_in[:, :, lr1:], w_in[:, :, lr0:lr1],
                              jnp.zeros(w_in.shape[:2] + (pad,), w_in.dtype)], axis=-1).astype(BF16)
    hk = B_HEADS * B_DK
    wg = jnp.zeros((DEPTH, 2, GW, hk), F32)
    wg = wg.at[:, 0, 0:B_GATE_RANK].set(b_gate[:, 0])
    wg = wg.at[:, 1, B_GATE_RANK:2 * B_GATE_RANK].set(b_gate[:, 1])
    return dict(
        w_in=w_in_p, w_out=w_out.astype(BF16),
        a_qnorm=_tile_vec(a_qnorm, GW // A_QK), a_knorm=_tile_vec(a_knorm, GW // A_QK),
        a_lam=a_lam, a_subln=_tile_vec(a_subln, GW // A_V),
        b_gate=wg.astype(BF16), b_gate_bias=b_gate_bias[:, :, None, :], b_onorm=_tile_vec(b_onorm, GW // B_DV),
        c_dw=c_dw, c_dw_b=c_dw_b[:, None, :], c_ln_g=c_ln_g[:, None, :], c_ln_b=c_ln_b[:, None, :],
        d_qnorm=_tile_vec(d_qnorm, GW // D_HEAD), d_knorm=_tile_vec(d_knorm, GW // D_HEAD),
    )


def _lambda_init(i):
    return 0.8 - 0.6 * math.exp(-0.3 * i)


def kernel(x_prompt, x_sample, cache_a_k, cache_a_v, state_b, cache_d_k, cache_d_v, c, c_ctx,
           norm1_g, norm2_g, w_ada, b_ada, w_in, w_out, a_qnorm, a_knorm, a_lam, a_subln,
           b_gate, b_gate_bias, b_onorm, c_dw, c_dw_b, c_ln_g, c_ln_b, d_qnorm, d_knorm, d_rpb,
           ffn_w1, ffn_w3, ffn_w2, moe_router, moe_router_b, moe_w1, moe_w3, moe_w2):
    lp = _prepare_params(w_in, w_out, a_qnorm, a_knorm, a_lam, a_subln, b_gate, b_gate_bias, b_onorm,
                         c_dw, c_dw_b, c_ln_g, c_ln_b, d_qnorm, d_knorm)
    norm1 = norm1_g[:, None, :]
    norm2 = norm2_g[:, None, :]
    b_ada3 = b_ada[:, None, :]
    cond8 = jnp.concatenate([c_ctx[None, :], c, jnp.zeros((8 - 1 - DEC_BATCH, D_MODEL), F32)], axis=0)
    rope = _rope_tables()
    ffn_w = [w.astype(BF16) for w in (ffn_w1, ffn_w3, ffn_w2)]
    lanes = 128
    router_p = jnp.pad(moe_router, ((0, 0), (0, 0), (0, lanes - N_EXPERTS)))
    router_b_p = jnp.pad(moe_router_b, ((0, 0), (0, lanes - N_EXPERTS)))[:, None, :]
    ck_a = cache_a_k.reshape(DEC_BATCH, DEPTH, PAST_LEN, GW)
    cv_a = cache_a_v.reshape(DEC_BATCH, DEPTH, PAST_LEN, GW)
    ck_d = cache_d_k.reshape(DEC_BATCH, DEPTH, PAST_LEN, GW)
    cv_d = cache_d_v.reshape(DEC_BATCH, DEPTH, PAST_LEN, GW)
    s0_lat = _state_to_kernel_layout(state_b.transpose(1, 0, 2, 3, 4, 5).reshape(
        DEPTH * DEC_BATCH, 2, B_HEADS, B_DK, B_DV)).reshape(DEPTH, DEC_BATCH, 2, GW, B_HEADS * B_DK)

    xc = x_prompt.reshape(BATCH * SEQ, D_MODEL)
    xl = x_sample.reshape(DEC_BATCH * DEC_SEQ, D_MODEL)
    new_ak, new_av, new_sb, new_dk, new_dv = [], [], [], [], []
    for i in range(DEPTH):
        lam_init = _lambda_init(i)
        dense_ffn = i % 2 == 0
        mods3 =_adaln(cond8, w_ada, b_ada3, i)[:, None, :]

        proj = _inproj(xc, norm1, mods3, lp["w_in"], i, False, SEQ)
        o_a, ak = _attn_a(proj, lp, i, False, BATCH, SEQ, lam_init)
        o_f, s_f = _gla(proj, lp, i, BATCH, SEQ, False)
        o_b, s_b = _gla(proj, lp, i, BATCH, SEQ, True, o_fwd=o_f)
        o_c = _conv(proj, lp, i, BATCH, SEQ)
        o_d, dk = _attn_d_ctx(proj, lp, i, BATCH, SEQ)
        xc = _outproj(o_a, o_b, o_c, o_d, xc, mods3, lp["w_out"], i, False, SEQ)
        if dense_ffn:
            xc = _ffn(xc, norm2, mods3, *ffn_w, i, i // 2, False, SEQ)
        new_ak.append(ak.reshape(BATCH, SEQ, A_HEADS, 2 * A_QK))
        new_av.append(proj[:, BLK_AV * GW:(BLK_AV + 1) * GW].reshape(BATCH, SEQ, A_HEADS, A_V))
        new_sb.append(jnp.stack([_state_from_kernel_layout(s_f), _state_from_kernel_layout(s_b)], axis=1))
        new_dk.append(dk.reshape(BATCH, SEQ, D_HEADS, D_HEAD))
        new_dv.append(proj[:, BLK_DV * GW:(BLK_DV + 1) * GW].reshape(BATCH, SEQ, D_HEADS, D_HEAD))

        proj = _inproj(xl, norm1, mods3, lp["w_in"], i, True, DEC_SEQ)
        o_a, = _attn_a(proj, lp, i, True, DEC_BATCH, DEC_SEQ, lam_init, ck_a, cv_a, rope)
        o_f, _ = _gla(proj, lp, i, DEC_BATCH, DEC_SEQ, False, s0=s0_lat[i])
        o_b, _ = _gla(proj, lp, i, DEC_BATCH, DEC_SEQ, True, s0=s0_lat[i], o_fwd=o_f)
        o_c = _conv(proj, lp, i, DEC_BATCH, DEC_SEQ)
        o_d = _attn_d_lat(proj, lp, i, ck_d, cv_d, _na_bias_columns(d_rpb[i]))
        xl = _outproj(o_a, o_b, o_c, o_d, xl, mods3, lp["w_out"], i, True, DEC_SEQ)
        if dense_ffn:
            xl = _ffn(xl, norm2, mods3, *ffn_w, i, i // 2, True, DEC_SEQ)
        else:
            xc, xl = _moe(xc, xl, norm2, mods3, router_p, router_b_p, moe_w1, moe_w3, moe_w2, i, i // 2)

    return (xc.reshape(BATCH, SEQ, D_MODEL), xl.reshape(DEC_BATCH, DEC_SEQ, D_MODEL),
            jnp.stack(new_ak, axis=1), jnp.stack(new_av, axis=1), jnp.stack(new_sb, axis=1),
            jnp.stack(new_dk, axis=1), jnp.stack(new_dv, axis=1))
```

```python
import functools
import math

import numpy as np
import jax
import jax.numpy as jnp
from jax import lax
from jax.experimental import pallas as pl
from jax.experimental.pallas import tpu as pltpu

F32 = jnp.float32
BF16 = jnp.bfloat16
MIX_DTYPE = BF16

D_MODEL = 1024
BATCH = 16
SEQ = 256
DEPTH = 2
DEC_BATCH = 4
DEC_SEQ = 2048
PAST_LEN = 512
GRID_W = 64
A_HEADS = 4
A_QK = 32
A_V = 64
B_HEADS = 4
B_DK = 32
B_DV = 64
B_GATE_RANK = 16
B_TAU = 16.0
C_CH = 256
C_KSIZE = 31
D_HEADS = 4
D_HEAD = 64
NA_ROWS = 8
NA_COLS = 16
NA_QROWS = 4
NA_KROWS = 12
ROPE_BASE = 10000.0
D_FF = 2816
N_EXPERTS = 8
D_FF_EXPERT = 3584
EPS = 1e-6
NEG_INF = -1e30
LOG2_E = math.log2(math.e)

GW = 256
N_PROJ_BLOCKS = 12
PROJ_W = GW * N_PROJ_BLOCKS
BLK_AQ, BLK_AK, BLK_AV, BLK_BQK, BLK_BV, BLK_BG = 0, 1, 2, 3, 4, 5
BLK_C = 3
BLK_DQ, BLK_DK, BLK_DV, BLK_LR = 8, 9, 10, 11

GLA_SUB = 16
GLA_ROWS = 256
VMEM_LIMIT = 56 * 1024 * 1024


def _params(sem, vmem=VMEM_LIMIT):
    return pltpu.CompilerParams(dimension_semantics=sem, vmem_limit_bytes=vmem)


def _dot(a, b):
    return jnp.dot(a.astype(BF16), b.astype(BF16), preferred_element_type=F32)


def _dot_nt(a, b):
    return lax.dot_general(a.astype(BF16), b.astype(BF16), (((1,), (1,)), ((), ())),
                           preferred_element_type=F32)


def _dot_exact_rhs(x, m):
    hi = x.astype(BF16)
    lo = (x - hi.astype(F32)).astype(BF16)
    return (jnp.dot(hi, m, preferred_element_type=F32) + jnp.dot(lo, m, preferred_element_type=F32))


def _sigmoid(x):
    return 1.0 / (1.0 + jnp.exp(-x))


def _silu(x):
    return x * _sigmoid(x)


def _seg_matrix(n, seg):
    r = lax.broadcasted_iota(jnp.int32, (n, n), 0)
    c = lax.broadcasted_iota(jnp.int32, (n, n), 1)
    return jnp.where((r ^ c) < seg, 1.0, 0.0).astype(BF16)


def _seg_rmsnorm(x, seg, w):
    ms = _dot_exact_rhs(x * x, _seg_matrix(x.shape[-1], seg)) * (1.0 / seg)
    return x * lax.rsqrt(ms + EPS) * w


def _lane_mask(shape, lo, width):
    lane = lax.broadcasted_iota(jnp.int32, shape, len(shape) - 1)
    return (lane >= lo) & (lane < lo + width)


def _stack_heads(q, heads, width):
    return jnp.concatenate([jnp.where(_lane_mask(q.shape, h * width, width), q, 0.0) for h in range(heads)],
                           axis=0)


def _unstack_heads(o_all, heads, width):
    n = o_all.shape[0] // heads
    o = jnp.zeros((n, o_all.shape[1]), F32)
    for h in range(heads):
        blk = o_all[h * n:(h + 1) * n]
        o = jnp.where(_lane_mask(blk.shape, h * width, width), blk, o)
    return o


def _modulate(x, g, shift, scale):
    y = x * lax.rsqrt(jnp.mean(x * x, axis=-1, keepdims=True) + EPS)
    return y * g * (1.0 + scale) + shift


def _rope(x, cos, sin):
    lane = lax.broadcasted_iota(jnp.int32, x.shape, 1)
    w = x.shape[1]
    rot = jnp.where((lane & 15) < 8, -pltpu.roll(x, w - 8, 1), pltpu.roll(x, 8, 1))
    return x * cos + rot * sin


def _adaln_kernel(c_ref, w_ref, b_ref, o_ref):
    o_ref[...] = _dot(_silu(c_ref[...]), w_ref[...]) + b_ref[...]


def _adaln(cond8, w_ada, b_ada3, layer):
    tn = 1536
    n = 6 * D_MODEL
    return pl.pallas_call(
        _adaln_kernel,
        out_shape=jax.ShapeDtypeStruct((8, n), F32),
        grid=(n // tn,),
        in_specs=[pl.BlockSpec((8, D_MODEL), lambda j: (0, 0)),
                  pl.BlockSpec((None, D_MODEL, tn), lambda j: (layer, 0, j)),
                  pl.BlockSpec((None, 1, tn), lambda j: (layer, 0, j))],
        out_specs=pl.BlockSpec((8, tn), lambda j: (0, j)),
        compiler_params=_params(("arbitrary",)),
        name="adaln",
    )(cond8, w_ada, b_ada3)


def _mod_row(latent, tm, seq_len):
    if latent == "both":
        n_ctx = BATCH * SEQ
        return lambda t: jnp.where(t * tm < n_ctx, 0, 1 + jnp.maximum(t * tm - n_ctx, 0) // DEC_SEQ)
    if latent:
        return lambda t: 1 + (t * tm) // seq_len
    return lambda t: 0


def _inproj_kernel(x_ref, g_ref, mod_ref, w_ref, o_ref):
    d = D_MODEL
    h = _modulate(x_ref[...], g_ref[...], mod_ref[:, 0:d], mod_ref[:, d:2 * d])
    o_ref[...] = _dot(h, w_ref[...])


def _inproj(x, norm_g3, mods3, w_in, layer, latent, seq_len):
    t_tokens = x.shape[0]
    tm = 512
    row = _mod_row(latent, tm, seq_len)
    return pl.pallas_call(
        _inproj_kernel,
        out_shape=jax.ShapeDtypeStruct((t_tokens, PROJ_W), F32),
        grid=(t_tokens // tm,),
        in_specs=[pl.BlockSpec((tm, D_MODEL), lambda t: (t, 0)),
                  pl.BlockSpec((None, 1, D_MODEL), lambda t: (layer, 0, 0)),
                  pl.BlockSpec((None, 1, 6 * D_MODEL), lambda t: (row(t), 0, 0)),
                  pl.BlockSpec((None, D_MODEL, PROJ_W), lambda t: (layer, 0, 0))],
        out_specs=pl.BlockSpec((tm, PROJ_W), lambda t: (t, 0)),
        compiler_params=_params(("arbitrary",)),
        name="inproj",
    )(x, norm_g3, mods3, w_in)


def _diff_lambda(lam_ref, lam_init):
    lam = lam_ref[...]
    s1 = jnp.sum(lam[0:1] * lam[1:2], axis=1, keepdims=True)
    s2 = jnp.sum(lam[2:3] * lam[3:4], axis=1, keepdims=True)
    return jnp.exp(s1) - jnp.exp(s2) + lam_init


def _values_and_ones(v):
    one = jnp.where(lax.broadcasted_iota(jnp.int32, (v.shape[0], A_V), 1) == 0, 1.0, 0.0)
    parts = []
    for h in range(A_HEADS):
        parts += [v[:, h * A_V:(h + 1) * A_V], one]
    return jnp.concatenate(parts, axis=1).astype(BF16)


def _attn_a_kernel(*refs, n_own, n_ctx, tq, latent, lam_init):
    if latent:
        (q_ref, k_ref, v_ref, ck_ref, cv_ref, cos_ref, sin_ref, qn_ref, kn_ref, lam_ref, sub_ref,
         o_ref, ks, vs) = refs
    else:
        (q_ref, k_ref, v_ref, qn_ref, kn_ref, lam_ref, sub_ref, o_ref, ko_ref, ks, vs) = refs
    t = pl.program_id(1)

    @pl.when(t == 0)
    def _():
        kn = _seg_rmsnorm(k_ref[...], A_QK, kn_ref[...])
        if latent:
            kn = _rope(kn, cos_ref[...], sin_ref[...])
            ks[n_own:n_own + n_ctx, :] = ck_ref[...].astype(BF16)
            vs[n_own:n_own + n_ctx, :] = _values_and_ones(cv_ref[...])
        else:
            ko_ref[...] = kn
        ks[0:n_own, :] = kn.astype(BF16)
        vs[0:n_own, :] = _values_and_ones(v_ref[...])

    qn = _seg_rmsnorm(q_ref[...], A_QK, qn_ref[...])
    if latent:
        r0 = pl.multiple_of(t * tq, tq)
        qn = _rope(qn, cos_ref[pl.ds(r0, tq), :], sin_ref[pl.ds(r0, tq), :])
    qn = qn * (A_QK ** -0.5 * LOG2_E)
    lam = _diff_lambda(lam_ref, lam_init)
    k_all = ks[...]
    heads = []
    for h in range(A_HEADS):
        v_h = vs[:, h * 2 * A_V:(h + 1) * 2 * A_V]
        qm = jnp.concatenate([jnp.where(_lane_mask(qn.shape, (2 * h + m) * A_QK, A_QK), qn, 0.0)
                              for m in range(2)], axis=0)
        s = _dot_nt(qm, k_all)
        e = jnp.exp2(s - jnp.max(s, axis=-1, keepdims=True))
        acc = _dot(e, v_h)
        acc0, acc1 = acc[:tq], acc[tq:]
        inv0 = 1.0 / acc0[:, A_V:A_V + 1]
        inv1 = lam / acc1[:, A_V:A_V + 1]
        heads.append((acc0 * inv0 - acc1 * inv1)[:, :A_V])
    o = jnp.concatenate(heads, axis=1)
    o_ref[...] = (_seg_rmsnorm(o, A_V, sub_ref[...]) * (1.0 - lam_init)).astype(o_ref.dtype)


def _attn_a(proj, lp, layer, latent, nseq, seq_len, lam_init, cache_k=None, cache_v=None, rope=None):
    tq = min(512, seq_len)
    nt = seq_len // tq
    n_ctx = PAST_LEN if latent else 0
    kern = functools.partial(_attn_a_kernel, n_own=seq_len, n_ctx=n_ctx, tq=tq, latent=latent,
                             lam_init=lam_init)
    vec = lambda name: pl.BlockSpec((None, 1, GW), lambda b, t: (layer, 0, 0))
    in_specs = [pl.BlockSpec((tq, GW), lambda b, t: (b * nt + t, BLK_AQ)),
                pl.BlockSpec((seq_len, GW), lambda b, t: (b, BLK_AK)),
                pl.BlockSpec((seq_len, GW), lambda b, t: (b, BLK_AV))]
    args = [proj, proj, proj]
    if latent:
        in_specs += [pl.BlockSpec((None, None, PAST_LEN, GW), lambda b, t: (b, layer, 0, 0)),
                     pl.BlockSpec((None, None, PAST_LEN, GW), lambda b, t: (b, layer, 0, 0)),
                     pl.BlockSpec((seq_len, GW), lambda b, t: (0, 0)),
                     pl.BlockSpec((seq_len, GW), lambda b, t: (0, 0))]
        args += [cache_k, cache_v, rope[0], rope[1]]
    in_specs += [vec("q"), vec("k"),
                 pl.BlockSpec((None, 4, A_QK), lambda b, t: (layer, 0, 0)),
                 vec("s")]
    args += [lp["a_qnorm"], lp["a_knorm"], lp["a_lam"], lp["a_subln"]]
    out_shape = [jax.ShapeDtypeStruct((nseq * seq_len, GW), MIX_DTYPE)]
    out_specs = [pl.BlockSpec((tq, GW), lambda b, t: (b * nt + t, 0))]
    if not latent:
        out_shape.append(jax.ShapeDtypeStruct((nseq * seq_len, GW), F32))
        out_specs.append(pl.BlockSpec((seq_len, GW), lambda b, t: (b, 0)))
    return pl.pallas_call(
        kern,
        out_shape=out_shape,
        grid=(nseq, nt),
        in_specs=in_specs,
        out_specs=out_specs,
        scratch_shapes=[pltpu.VMEM((seq_len + n_ctx, GW), BF16), pltpu.VMEM((seq_len + n_ctx, 2 * GW), BF16)],
        compiler_params=_params(("arbitrary", "arbitrary")),
        name="attn_a_lat" if latent else "attn_a_ctx",
    )(*args)


def _attn_d_ctx_kernel(q_ref, k_ref, v_ref, qn_ref, kn_ref, o_ref, ko_ref):
    kn = _seg_rmsnorm(k_ref[...], D_HEAD, kn_ref[...])
    ko_ref[...] = kn
    qn = _seg_rmsnorm(q_ref[...], D_HEAD, qn_ref[...]) * (D_HEAD ** -0.5)
    s = _dot_nt(_stack_heads(qn, D_HEADS, D_HEAD), kn)
    e = jnp.exp(s - jnp.max(s, axis=-1, keepdims=True))
    o_all = _dot(e, v_ref[...]) * (1.0 / jnp.sum(e, axis=-1, keepdims=True))
    o_ref[...] = _unstack_heads(o_all, D_HEADS, D_HEAD).astype(o_ref.dtype)


def _attn_d_ctx(proj, lp, layer, nseq, seq_len):
    vec = pl.BlockSpec((None, 1, GW), lambda b: (layer, 0, 0))
    blk = lambda c: pl.BlockSpec((seq_len, GW), lambda b: (b, c))
    return pl.pallas_call(
        _attn_d_ctx_kernel,
        out_shape=[jax.ShapeDtypeStruct((nseq * seq_len, GW), MIX_DTYPE),
                   jax.ShapeDtypeStruct((nseq * seq_len, GW), F32)],
        grid=(nseq,),
        in_specs=[blk(BLK_DQ), blk(BLK_DK), blk(BLK_DV), vec, vec],
        out_specs=[blk(0), blk(0)],
        compiler_params=_params(("arbitrary",)),
        name="attn_d_ctx",
    )(proj, proj, proj, lp["d_qnorm"], lp["d_knorm"])


def _attn_d_lat_kernel(q_ref, k_ref, v_ref, ck_ref, cv_ref, qn_ref, kn_ref, cols_ref, o_ref,
                       ks, vs, cks, cvs, bias_ref):
    g = pl.program_id(1)
    n_rows = DEC_SEQ // GRID_W
    n_groups = n_rows // NA_QROWS
    n_loc = NA_KROWS * GRID_W

    @pl.when((pl.program_id(0) == 0) & (g == 0))
    def _():
        _fill_na_bias(cols_ref, bias_ref)

    @pl.when(g == 0)
    def _():
        ks[...] = _seg_rmsnorm(k_ref[...], D_HEAD, kn_ref[...]).astype(BF16)
        vs[...] = v_ref[...].astype(BF16)
        cks[...] = ck_ref[...].astype(BF16)
        cvs[...] = cv_ref[...].astype(BF16)

    qn = _seg_rmsnorm(q_ref[...], D_HEAD, qn_ref[...]) * (D_HEAD ** -0.5)
    row_start = jnp.clip(g * NA_QROWS - NA_ROWS // 2, 0, n_rows - NA_KROWS)
    variant = jnp.where(g == 0, 0, jnp.where(g == n_groups - 1, 2, 1))
    k0 = pl.multiple_of(row_start * GRID_W, GRID_W)
    kl = ks[pl.ds(k0, n_loc), :]
    vl = vs[pl.ds(k0, n_loc), :]
    kc = cks[...]
    vc = cvs[...]
    o = jnp.zeros(qn.shape, F32)
    for h in range(D_HEADS):
        hm = _lane_mask(qn.shape, h * D_HEAD, D_HEAD)
        qm = jnp.where(hm, qn, 0.0)
        s_loc = _dot_nt(qm, kl) + bias_ref[h, variant]
        s_ctx = _dot_nt(qm, kc)
        mx = jnp.maximum(jnp.max(s_loc, axis=-1, keepdims=True), jnp.max(s_ctx, axis=-1, keepdims=True))
        e_loc = jnp.exp(s_loc - mx)
        e_ctx = jnp.exp(s_ctx - mx)
        inv = 1.0 / (jnp.sum(e_loc, axis=-1, keepdims=True) + jnp.sum(e_ctx, axis=-1, keepdims=True))
        o = o + jnp.where(hm, (_dot(e_loc, vl) + _dot(e_ctx, vc)) * inv, 0.0)
    o_ref[...] = o.astype(o_ref.dtype)


def _na_bias_rows():
    n_rows = DEC_SEQ // GRID_W
    n_groups = n_rows // NA_QROWS
    plan = []
    for g in (0, 1, n_groups - 1):
        first_key_row = int(np.clip(g * NA_QROWS - NA_ROWS // 2, 0, n_rows - NA_KROWS))
        per_query_row = []
        for a in range(NA_QROWS):
            r = g * NA_QROWS + a
            win = int(np.clip(r - NA_ROWS // 2, 0, n_rows - NA_ROWS))
            per_query_row.append([first_key_row + i - r + NA_ROWS - 1
                                  if win <= first_key_row + i < win + NA_ROWS else None
                                  for i in range(NA_KROWS)])
        plan.append(per_query_row)
    return plan


def _fill_na_bias(cols_ref, bias_s):
    lane = lax.broadcasted_iota(jnp.int32, (GRID_W, 2 * GRID_W), 1)
    masked = jnp.full((GRID_W, 2 * GRID_W), NEG_INF, F32)
    for h in range(D_HEADS):
        for v, per_query_row in enumerate(_na_bias_rows()):
            for a, rows in enumerate(per_query_row):
                for p in range(NA_KROWS // 2):
                    left, right = rows[2 * p], rows[2 * p + 1]
                    lhs = masked if left is None else cols_ref[h, left]
                    rhs = masked if right is None else cols_ref[h, right]
                    bias_s[h, v, a * GRID_W:(a + 1) * GRID_W, 2 * p * GRID_W:2 * (p + 1) * GRID_W] = (
                        jnp.where(lane < GRID_W, lhs, rhs))


def _na_bias_columns(rpb):
    q = np.arange(GRID_W)
    kcol = np.arange(GRID_W)
    cs = np.clip(q - NA_COLS // 2, 0, GRID_W - NA_COLS)
    in_win = (kcol[None, :] >= cs[:, None]) & (kcol[None, :] < cs[:, None] + NA_COLS)
    dc = np.clip(kcol[None, :] - q[:, None] + (NA_COLS - 1), 0, 2 * NA_COLS - 2)
    pick_col = (dc[None] == np.arange(2 * NA_COLS - 1)[:, None, None]).astype(np.float32)
    cols = jnp.einsum("hrc,cqk->hrqk", rpb.astype(F32), jnp.asarray(pick_col), precision=lax.Precision.HIGHEST)
    cols = jnp.where(in_win[None, None], cols, NEG_INF)
    return jnp.concatenate([cols, cols], axis=-1)


def _attn_d_lat(proj, lp, layer, cache_k, cache_v, bias_cols):
    n_groups = DEC_SEQ // GRID_W // NA_QROWS
    tq = NA_QROWS * GRID_W
    vec = pl.BlockSpec((None, 1, GW), lambda b, r: (layer, 0, 0))
    seq = lambda c: pl.BlockSpec((DEC_SEQ, GW), lambda b, r: (b, c))
    cache = pl.BlockSpec((None, None, PAST_LEN, GW), lambda b, r: (b, layer, 0, 0))
    return pl.pallas_call(
        _attn_d_lat_kernel,
        out_shape=jax.ShapeDtypeStruct((DEC_BATCH * DEC_SEQ, GW), MIX_DTYPE),
        grid=(DEC_BATCH, n_groups),
        in_specs=[pl.BlockSpec((tq, GW), lambda b, r: (b * n_groups + r, BLK_DQ)),
                  seq(BLK_DK), seq(BLK_DV), cache, cache, vec, vec,
                  pl.BlockSpec(bias_cols.shape, lambda b, r: (0, 0, 0, 0))],
        out_specs=pl.BlockSpec((tq, GW), lambda b, r: (b * n_groups + r, 0)),
        scratch_shapes=[pltpu.VMEM((DEC_SEQ, GW), BF16), pltpu.VMEM((DEC_SEQ, GW), BF16),
                        pltpu.VMEM((PAST_LEN, GW), BF16), pltpu.VMEM((PAST_LEN, GW), BF16),
                        pltpu.VMEM((D_HEADS, 3, tq, NA_KROWS * GRID_W), F32)],
        compiler_params=_params(("arbitrary", "arbitrary")),
        name="attn_d_lat",
    )(proj, proj, proj, cache_k, cache_v, lp["d_qnorm"], lp["d_knorm"], bias_cols)


def _conv_kernel(c_ref, w_ref, b_ref, g_ref, beta_ref, o_ref, pad, *, seq_len):
    half = C_KSIZE // 2
    top = 16
    cin = c_ref[...]
    u = cin[:, :C_CH] * _sigmoid(cin[:, C_CH:])
    pad[0:top, :] = jnp.zeros((top, C_CH), F32)
    pad[top + seq_len:top + seq_len + top, :] = jnp.zeros((top, C_CH), F32)
    pad[top:top + seq_len, :] = u
    w = w_ref[...]
    rb = 256
    for r0 in range(0, seq_len, rb):
        acc = jnp.zeros((rb, C_CH), F32)
        n_hi = -(-C_KSIZE // 8)
        for lo in range(8):
            base = r0 + top - half + lo
            shifted = pad[base:base + rb + 8 * (n_hi - 1), :]
            for hi in range(n_hi):
                k = 8 * hi + lo
                if k < C_KSIZE:
                    acc = acc + shifted[8 * hi:8 * hi + rb, :] * w[k:k + 1, :]
        acc = acc + b_ref[...]
        mu = jnp.mean(acc, axis=-1, keepdims=True)
        xc = acc - mu
        y = xc * lax.rsqrt(jnp.mean(xc * xc, axis=-1, keepdims=True) + EPS) * g_ref[...] + beta_ref[...]
        o_ref[r0:r0 + rb, :] = _silu(y).astype(o_ref.dtype)


def _conv(proj, lp, layer, nseq, seq_len):
    vec = pl.BlockSpec((None, 1, C_CH), lambda b: (layer, 0, 0))
    return pl.pallas_call(
        functools.partial(_conv_kernel, seq_len=seq_len),
        out_shape=jax.ShapeDtypeStruct((nseq * seq_len, C_CH), MIX_DTYPE),
        grid=(nseq,),
        in_specs=[pl.BlockSpec((seq_len, 2 * C_CH), lambda b: (b, BLK_C)),
                  pl.BlockSpec((None, C_KSIZE, C_CH), lambda b: (layer, 0, 0)),
                  vec, vec, vec],
        out_specs=pl.BlockSpec((seq_len, C_CH), lambda b: (b, 0)),
        scratch_shapes=[pltpu.VMEM((seq_len + 32, C_CH), F32)],
        compiler_params=_params(("arbitrary",)),
        name="conv",
    )(proj, lp["c_dw"], lp["c_dw_b"], lp["c_ln_g"], lp["c_ln_b"])


def _gla_kernel(*refs, reverse, has_s0, n_blocks):
    refs = list(refs)
    qk_ref, v_ref, lr_ref, wg_ref, gb_ref = refs[:5]
    refs = refs[5:]
    s0_ref = refs.pop(0) if has_s0 else None
    if reverse:
        of_ref, bg_ref, on_ref = refs[:3]
        refs = refs[3:]
    o_ref, so_ref, st = refs
    rb, cs = GLA_ROWS, GLA_SUB
    hk = B_HEADS * B_DK
    j = pl.program_id(1)

    @pl.when(j == 0)
    def _():
        st[...] = s0_ref[...] if has_s0 else jnp.zeros(st.shape, F32)

    qk = qk_ref[...]
    q = qk[:, :hk] * (B_DK ** -0.5)
    k = qk[:, hk:]
    v = v_ref[...]
    pre = _dot(lr_ref[...], wg_ref[...]) + gb_ref[...]
    g = (jnp.minimum(pre, 0.0) - jnp.log(1.0 + jnp.exp(-jnp.abs(pre)))) * (1.0 / B_TAU)

    pos = lax.broadcasted_iota(jnp.int32, (rb, hk), 0) & (cs - 1)
    pre_sum = g
    suf_sum = g
    step = 1
    while step < cs:
        pre_sum = pre_sum + jnp.where(pos >= step, pltpu.roll(pre_sum, step, 0), 0.0)
        suf_sum = suf_sum + jnp.where(pos < cs - step, pltpu.roll(suf_sum, rb - step, 0), 0.0)
        step *= 2
    total = pre_sum + suf_sum - g
    if reverse:
        z = suf_sum
        k_dec = k * jnp.exp(pre_sum - g)
    else:
        z = pre_sum
        k_dec = k * jnp.exp(suf_sum - g)
    q_dec = q * jnp.exp(z)

    r_i = lax.broadcasted_iota(jnp.int32, (hk, GW), 0)
    c_i = lax.broadcasted_iota(jnp.int32, (hk, GW), 1)
    head_sum = jnp.where((r_i >> 5) == (c_i >> 6), 1.0, 0.0).astype(BF16)

    row_in_sub = lax.broadcasted_iota(jnp.int32, (cs, hk), 0)
    intra = []
    for n in range(rb // cs):
        qs = q[n * cs:(n + 1) * cs]
        zs = z[n * cs:(n + 1) * cs]
        pairs = []
        for jl in range(cs):
            r = n * cs + jl
            decay = jnp.exp(jnp.minimum(zs - z[r:r + 1], 0.0))
            keep = (row_in_sub <= jl) if reverse else (row_in_sub >= jl)
            pairs.append(jnp.where(keep, qs * k[r:r + 1] * decay, 0.0))
        w = jnp.dot(jnp.concatenate(pairs, axis=0).astype(BF16), head_sum, preferred_element_type=F32)
        acc = jnp.zeros((cs, GW), F32)
        for jl in range(cs):
            r = n * cs + jl
            acc = acc + w[jl * cs:(jl + 1) * cs] * v[r:r + 1]
        intra.append(acc)
    o = jnp.concatenate(intra, axis=0)

    r_s = lax.broadcasted_iota(jnp.int32, st.shape, 0)
    c_s = lax.broadcasted_iota(jnp.int32, st.shape, 1)
    diag = (r_s >> 6) == (c_s >> 5)
    n_sub = rb // cs
    row = lax.broadcasted_iota(jnp.int32, (rb, hk), 0)

    def per_sub_chunk(x):
        return jnp.concatenate([jnp.where((row >= n * cs) & (row < (n + 1) * cs), x, 0.0).astype(BF16)
                                for n in range(n_sub)], axis=1)

    kv_all = _dot(v.T, per_sub_chunk(k_dec))
    state = st[...]
    states = [None] * n_sub
    order = range(n_sub - 1, -1, -1) if reverse else range(n_sub)
    for n in order:
        states[n] = state.astype(BF16)
        decay = jnp.exp(total[n * cs:n * cs + 1, :])
        state = state * decay + jnp.where(diag, kv_all[:, n * hk:(n + 1) * hk], 0.0)
    st[...] = state
    o = o + _dot_nt(per_sub_chunk(q_dec), jnp.concatenate(states, axis=1))

    if reverse:
        o = _seg_rmsnorm(o + of_ref[...], B_DV, on_ref[...]) * _silu(bg_ref[...])
    o_ref[...] = o.astype(o_ref.dtype)

    @pl.when(j == n_blocks - 1)
    def _():
        so_ref[...] = state


def _gla(proj, lp, layer, nseq, seq_len, reverse, s0=None, o_fwd=None):
    rb = GLA_ROWS
    nb = seq_len // rb
    hk = B_HEADS * B_DK
    blk = (lambda b, j: b * nb + nb - 1 - j) if reverse else (lambda b, j: b * nb + j)
    tile = lambda c: pl.BlockSpec((rb, GW), lambda b, j: (blk(b, j), c))
    d = 1 if reverse else 0
    in_specs = [tile(BLK_BQK), tile(BLK_BV), tile(BLK_LR),
                pl.BlockSpec((None, None, GW, hk), lambda b, j: (layer, d, 0, 0)),
                pl.BlockSpec((None, None, 1, hk), lambda b, j: (layer, d, 0, 0))]
    args = [proj, proj, proj, lp["b_gate"], lp["b_gate_bias"]]
    if s0 is not None:
        in_specs.append(pl.BlockSpec((None, None, GW, hk), lambda b, j: (b, d, 0, 0)))
        args.append(s0)
    if reverse:
        in_specs += [tile(0), tile(BLK_BG), pl.BlockSpec((None, 1, GW), lambda b, j: (layer, 0, 0))]
        args += [o_fwd, proj, lp["b_onorm"]]
    return pl.pallas_call(
        functools.partial(_gla_kernel, reverse=reverse, has_s0=s0 is not None, n_blocks=nb),
        out_shape=[jax.ShapeDtypeStruct((nseq * seq_len, GW), MIX_DTYPE if reverse else F32),
                   jax.ShapeDtypeStruct((nseq, GW, hk), F32)],
        grid=(nseq, nb),
        in_specs=in_specs,
        out_specs=[tile(0), pl.BlockSpec((None, GW, hk), lambda b, j: (b, 0, 0))],
        scratch_shapes=[pltpu.VMEM((GW, hk), F32)],
        compiler_params=_params(("arbitrary", "arbitrary")),
        name="gla_bwd" if reverse else "gla_fwd",
    )(*args)


def _state_to_kernel_layout(s):
    eye = jnp.eye(B_HEADS, dtype=s.dtype)
    t = jnp.einsum("bxhde,hg->bxhegd", s, eye)
    return t.reshape(s.shape[0], 2, B_HEADS * B_DV, B_HEADS * B_DK)


def _state_from_kernel_layout(st):
    t = st.reshape(st.shape[0], B_HEADS, B_DV, B_HEADS, B_DK)
    return jnp.stack([t[:, h, :, h, :] for h in range(B_HEADS)], axis=1).transpose(0, 1, 3, 2)


def _outproj_kernel(a_ref, b_ref, c_ref, d_ref, x_ref, mod_ref, w_ref, o_ref):
    mix = jnp.concatenate([a_ref[...], b_ref[...], c_ref[...], d_ref[...]], axis=-1)
    gate = mod_ref[:, 2 * D_MODEL:3 * D_MODEL]
    o_ref[...] = x_ref[...] + gate * _dot(mix, w_ref[...])


def _outproj(o_a, o_b, o_c, o_d, x, mods3, w_out, layer, latent, seq_len):
    t_tokens = x.shape[0]
    tm = 512
    row = _mod_row(latent, tm, seq_len)
    part = pl.BlockSpec((tm, GW), lambda t: (t, 0))
    return pl.pallas_call(
        _outproj_kernel,
        out_shape=jax.ShapeDtypeStruct((t_tokens, D_MODEL), F32),
        grid=(t_tokens // tm,),
        in_specs=[part, part, part, part,
                  pl.BlockSpec((tm, D_MODEL), lambda t: (t, 0)),
                  pl.BlockSpec((None, 1, 6 * D_MODEL), lambda t: (row(t), 0, 0)),
                  pl.BlockSpec((None, 4 * GW, D_MODEL), lambda t: (layer, 0, 0))],
        out_specs=pl.BlockSpec((tm, D_MODEL), lambda t: (t, 0)),
        compiler_params=_params(("arbitrary",)),
        name="outproj",
    )(o_a, o_b, o_c, o_d, x, mods3, w_out)


MOE_TM = 1024
R_E1, R_E2, R_W1, R_W2, R_RANK1, R_RANK2 = 0, 1, 2, 3, 4, 5


def _router_kernel(xc_ref, xl_ref, g_ref, mod_ref, r_ref, rb_ref, h_ref, route_ref, cnt_ref, carry, *,
                   n_ctx_tiles):
    d = D_MODEL

    @pl.when(pl.program_id(0) == 0)
    def _():
        carry[...] = jnp.zeros(carry.shape, F32)

    x = jnp.where(pl.program_id(0) < n_ctx_tiles, xc_ref[...], xl_ref[...])
    h = _modulate(x, g_ref[...], mod_ref[:, 3 * d:4 * d], mod_ref[:, 4 * d:5 * d])
    tm = h.shape[0]
    h_ref[...] = h
    r = r_ref[...]
    h_hi = h.astype(BF16)
    h_lo = (h - h_hi.astype(F32)).astype(BF16)
    r_hi = r.astype(BF16)
    r_lo = (r - r_hi.astype(F32)).astype(BF16)
    dot = lambda a, b: jnp.dot(a, b, preferred_element_type=F32)
    logits = dot(h_hi, r_hi) + dot(h_lo, r_hi) + dot(h_hi, r_lo) + rb_ref[...]
    lane = lax.broadcasted_iota(jnp.int32, logits.shape, 1).astype(F32)
    big = float(logits.shape[1])
    logits = jnp.where(lane < N_EXPERTS, logits, -jnp.inf)
    v1 = jnp.max(logits, axis=-1, keepdims=True)
    i1 = jnp.min(jnp.where(logits == v1, lane, big), axis=-1, keepdims=True)
    rest = jnp.where(lane == i1, -jnp.inf, logits)
    v2 = jnp.max(rest, axis=-1, keepdims=True)
    i2 = jnp.min(jnp.where(rest == v2, lane, big), axis=-1, keepdims=True)
    e2 = jnp.exp(v2 - v1)
    inv = 1.0 / (1.0 + e2)
    sel = jnp.where((lane == i1) | (lane == i2), 1.0, 0.0)
    r_i = lax.broadcasted_iota(jnp.int32, (tm, tm), 0)
    c_i = lax.broadcasted_iota(jnp.int32, (tm, tm), 1)
    before = jnp.where(c_i < r_i, 1.0, 0.0).astype(BF16)
    rank = jnp.dot(before, sel.astype(BF16), preferred_element_type=F32) + carry[...]
    count = carry[...] + jnp.sum(sel, axis=0, keepdims=True)
    carry[...] = count
    cnt_ref[...] = jnp.broadcast_to(count, cnt_ref.shape)
    rank1 = jnp.sum(jnp.where(lane == i1, rank, 0.0), axis=-1, keepdims=True)
    rank2 = jnp.sum(jnp.where(lane == i2, rank, 0.0), axis=-1, keepdims=True)
    rec = jnp.zeros(logits.shape, F32)
    for pos, val in ((R_E1, i1), (R_E2, i2), (R_W1, inv), (R_W2, e2 * inv), (R_RANK1, rank1), (R_RANK2, rank2)):
        rec = jnp.where(lane == pos, val, rec)
    route_ref[...] = rec


def _two_pass_specs(tm):
    nc = BATCH * SEQ // tm
    ctx = pl.BlockSpec((tm, D_MODEL), lambda t, *_: (jnp.minimum(t, nc - 1), 0))
    lat = pl.BlockSpec((tm, D_MODEL), lambda t, *_: (jnp.maximum(t - nc, 0), 0))
    return nc, ctx, lat


def _router(xc, xl, norm_g3, mods3, router, router_b, layer, j):
    t_tokens = xc.shape[0] + xl.shape[0]
    tm = 512
    row = _mod_row("both", tm, None)
    lanes = router.shape[-1]
    nc, ctx_spec, lat_spec = _two_pass_specs(tm)
    return pl.pallas_call(
        functools.partial(_router_kernel, n_ctx_tiles=nc),
        out_shape=[jax.ShapeDtypeStruct((t_tokens, D_MODEL), F32),
                   jax.ShapeDtypeStruct((t_tokens, lanes), F32),
                   jax.ShapeDtypeStruct((8, lanes), F32)],
        grid=(t_tokens // tm,),
        in_specs=[ctx_spec, lat_spec,
                  pl.BlockSpec((None, 1, D_MODEL), lambda t: (layer, 0, 0)),
                  pl.BlockSpec((None, 1, 6 * D_MODEL), lambda t: (row(t), 0, 0)),
                  pl.BlockSpec((None, D_MODEL, lanes), lambda t: (j, 0, 0)),
                  pl.BlockSpec((None, 1, lanes), lambda t: (j, 0, 0))],
        out_specs=[pl.BlockSpec((tm, D_MODEL), lambda t: (t, 0)),
                   pl.BlockSpec((tm, lanes), lambda t: (t, 0)),
                   pl.BlockSpec((8, lanes), lambda t: (0, 0))],
        scratch_shapes=[pltpu.VMEM((1, lanes), F32)],
        compiler_params=_params(("arbitrary",)),
        name="router",
    )(xc, xl, norm_g3, mods3, router, router_b)


def _row_copy(src, src_row, dst, dst_row, sem):
    return pltpu.make_async_copy(src.at[pl.ds(src_row, 1), :], dst.at[pl.ds(dst_row, 1), :], sem)


def _dispatch_kernel(p0_ref, p1_ref, h_ref, xs_in_ref, xs_ref, sem):
    del xs_in_ref
    tm = h_ref.shape[0]
    base = pl.program_id(0) * tm

    def issue(r, carry):
        _row_copy(h_ref, r, xs_ref, p0_ref[base + r], sem.at[0]).start()
        _row_copy(h_ref, r, xs_ref, p1_ref[base + r], sem.at[1]).start()
        return carry

    def wait(r, carry):
        _row_copy(h_ref, r, xs_ref, p0_ref[base + r], sem.at[0]).wait()
        _row_copy(h_ref, r, xs_ref, p1_ref[base + r], sem.at[1]).wait()
        return carry

    lax.fori_loop(0, tm, issue, 0, unroll=8)
    lax.fori_loop(0, tm, wait, 0, unroll=8)


def _dispatch(pos0, pos1, h_rows, n_slots):
    t_tokens = pos0.shape[0]
    tm = 512
    xs0 = jnp.zeros((n_slots, D_MODEL), F32)
    return pl.pallas_call(
        _dispatch_kernel,
        out_shape=jax.ShapeDtypeStruct(xs0.shape, F32),
        grid_spec=pltpu.PrefetchScalarGridSpec(
            num_scalar_prefetch=2,
            grid=(t_tokens // tm,),
            in_specs=[pl.BlockSpec((tm, D_MODEL), lambda t, p0, p1: (t, 0)),
                      pl.BlockSpec(memory_space=pl.ANY)],
            out_specs=pl.BlockSpec(memory_space=pl.ANY),
            scratch_shapes=[pltpu.SemaphoreType.DMA((2,))]),
        input_output_aliases={3: 0},
        compiler_params=_params(("arbitrary",)),
        name="moe_dispatch",
    )(pos0, pos1, h_rows, xs0)


def _moe_ffn_kernel(te_ref, nu_ref, xs_ref, w1_ref, w3_ref, w2_ref, y_ref, xb, acc, *, n_f):
    del te_ref
    i = pl.program_id(0)
    f = pl.program_id(1)
    used = i < nu_ref[0]

    @pl.when(used & (f == 0))
    def _():
        xb[...] = xs_ref[...].astype(BF16)
        acc[...] = jnp.zeros(acc.shape, F32)

    @pl.when(used)
    def _():
        x = xb[...]
        a = jnp.dot(x, w1_ref[...].astype(BF16), preferred_element_type=F32)
        b = jnp.dot(x, w3_ref[...].astype(BF16), preferred_element_type=F32)
        acc[...] += _dot(_silu(a) * b, w2_ref[...])

    @pl.when(used & (f == n_f - 1))
    def _():
        y_ref[...] = acc[...]

    @pl.when(jnp.logical_not(used) & (f == n_f - 1))
    def _():
        y_ref[...] = jnp.zeros(y_ref.shape, F32)


def _moe_ffn(tile_expert, n_used, xs, w1, w3, w2, j):
    tm = MOE_TM
    n_tiles = xs.shape[0] // tm
    d_ff = w1.shape[-1]
    tf = 512
    n_f = d_ff // tf
    f_of = lambda i, f, nu: jnp.where(i < nu[0], f, n_f - 1)
    return pl.pallas_call(
        functools.partial(_moe_ffn_kernel, n_f=n_f),
        out_shape=jax.ShapeDtypeStruct(xs.shape, F32),
        grid_spec=pltpu.PrefetchScalarGridSpec(
            num_scalar_prefetch=2,
            grid=(n_tiles, n_f),
            in_specs=[pl.BlockSpec((tm, D_MODEL), lambda i, f, te, nu: (jnp.minimum(i, nu[0] - 1), 0)),
                      pl.BlockSpec((None, None, D_MODEL, tf), lambda i, f, te, nu: (j, te[i], 0, f_of(i, f, nu))),
                      pl.BlockSpec((None, None, D_MODEL, tf), lambda i, f, te, nu: (j, te[i], 0, f_of(i, f, nu))),
                      pl.BlockSpec((None, None, tf, D_MODEL), lambda i, f, te, nu: (j, te[i], f_of(i, f, nu), 0))],
            out_specs=pl.BlockSpec((tm, D_MODEL), lambda i, f, te, nu: (i, 0)),
            scratch_shapes=[pltpu.VMEM((tm, D_MODEL), BF16), pltpu.VMEM((tm, D_MODEL), F32)]),
        compiler_params=_params(("arbitrary", "arbitrary")),
        name="moe_ffn",
    )(tile_expert, n_used, xs, w1, w3, w2)


def _combine_kernel(p0_ref, p1_ref, xc_ref, xl_ref, mod_ref, route_ref, y_ref, oc_ref, ol_ref, buf0, buf1, sem, *,
                    n_ctx_tiles):
    tm = xc_ref.shape[0]
    t = pl.program_id(0)
    base = t * tm

    def issue(r, carry):
        _row_copy(y_ref, p0_ref[base + r], buf0, r, sem.at[0]).start()
        _row_copy(y_ref, p1_ref[base + r], buf1, r, sem.at[1]).start()
        return carry

    def wait(r, carry):
        _row_copy(y_ref, p0_ref[base + r], buf0, r, sem.at[0]).wait()
        _row_copy(y_ref, p1_ref[base + r], buf1, r, sem.at[1]).wait()
        return carry

    lax.fori_loop(0, tm, issue, 0, unroll=8)
    lax.fori_loop(0, tm, wait, 0, unroll=8)
    route = route_ref[...]
    w_a = route[:, R_W1:R_W1 + 1]
    w_b = route[:, R_W2:R_W2 + 1]
    y = w_a * buf0[...] + w_b * buf1[...]
    gated = mod_ref[:, 5 * D_MODEL:6 * D_MODEL] * y

    @pl.when(t < n_ctx_tiles)
    def _():
        oc_ref[...] = xc_ref[...] + gated

    @pl.when(t >= n_ctx_tiles)
    def _():
        ol_ref[...] = xl_ref[...] + gated


def _combine(pos0, pos1, xc, xl, mods3, route, y):
    tm = 512
    row = _mod_row("both", tm, None)
    nc, ctx_spec, lat_spec = _two_pass_specs(tm)
    return pl.pallas_call(
        functools.partial(_combine_kernel, n_ctx_tiles=nc),
        out_shape=[jax.ShapeDtypeStruct(xc.shape, F32), jax.ShapeDtypeStruct(xl.shape, F32)],
        grid_spec=pltpu.PrefetchScalarGridSpec(
            num_scalar_prefetch=2,
            grid=((xc.shape[0] + xl.shape[0]) // tm,),
            in_specs=[ctx_spec, lat_spec,
                      pl.BlockSpec((None, 1, 6 * D_MODEL), lambda t, p0, p1: (row(t), 0, 0)),
                      pl.BlockSpec((tm, route.shape[1]), lambda t, p0, p1: (t, 0)),
                      pl.BlockSpec(memory_space=pl.ANY)],
            out_specs=[ctx_spec, lat_spec],
            scratch_shapes=[pltpu.VMEM((tm, D_MODEL), F32), pltpu.VMEM((tm, D_MODEL), F32),
                            pltpu.SemaphoreType.DMA((2,))]),
        compiler_params=_params(("arbitrary",)),
        name="moe_combine",
    )(pos0, pos1, xc, xl, mods3, route, y)


def _moe(xc, xl, norm_g3, mods3, router, router_b, w1, w3, w2, layer, j):
    t_tokens = xc.shape[0] + xl.shape[0]
    tm = MOE_TM
    n_tiles = 2 * t_tokens // tm + N_EXPERTS
    h_rows, route, cnt = _router(xc, xl, norm_g3, mods3, router, router_b, layer, j)
    expert = route[:, R_E1:R_E2 + 1].astype(jnp.int32)
    rank = route[:, R_RANK1:R_RANK2 + 1].astype(jnp.int32)
    tiles = (cnt[0, :N_EXPERTS].astype(jnp.int32) + tm - 1) // tm
    ends = jnp.cumsum(tiles)
    starts = ends - tiles
    ids = jnp.arange(N_EXPERTS, dtype=jnp.int32)
    start_of = jnp.sum(jnp.where(expert[:, :, None] == ids, starts * tm, 0), axis=-1)
    pos = start_of + rank
    n_used = ends[-1:]
    tile_ids = jnp.arange(n_tiles, dtype=jnp.int32)
    tile_expert = jnp.sum((tile_ids[:, None] >= ends[None, :]).astype(jnp.int32), axis=1)
    last_expert = jnp.max(jnp.where(tiles > 0, ids, 0))
    tile_expert = jnp.where(tile_ids < n_used, tile_expert, last_expert)
    pos0, pos1 = pos[:, 0], pos[:, 1]
    xs = _dispatch(pos0, pos1, h_rows, n_tiles * tm)
    y = _moe_ffn(tile_expert, n_used, xs, w1, w3, w2, j)
    return _combine(pos0, pos1, xc, xl, mods3, route, y)


def _ffn_kernel(x_ref, g_ref, mod_ref, w1_ref, w3_ref, w2_ref, o_ref, h_s, acc, *, n_f):
    d = D_MODEL
    f = pl.program_id(1)

    @pl.when(f == 0)
    def _():
        h = _modulate(x_ref[...], g_ref[...], mod_ref[:, 3 * d:4 * d], mod_ref[:, 4 * d:5 * d])
        h_s[...] = h.astype(BF16)
        acc[...] = jnp.zeros(acc.shape, F32)

    h = h_s[...]
    tf = w1_ref.shape[1]
    split = tf // 2 // 128 * 128
    part = None
    for lo, hi in ((0, split), (split, tf)):
        a = jnp.dot(h, w1_ref[:, lo:hi], preferred_element_type=F32)
        b = jnp.dot(h, w3_ref[:, lo:hi], preferred_element_type=F32)
        p = _dot(_silu(a) * b, w2_ref[lo:hi, :])
        part = p if part is None else part + p
    acc[...] += part

    @pl.when(f == n_f - 1)
    def _():
        o_ref[...] = x_ref[...] + mod_ref[:, 5 * d:6 * d] * acc[...]


def _ffn(x, norm_g3, mods3, w1, w3, w2, layer, j, latent, seq_len):
    t_tokens = x.shape[0]
    d_ff = w1.shape[-1]
    tm = 512
    tf = d_ff // 2
    n_f = d_ff // tf
    row = _mod_row(latent, tm, seq_len)
    return pl.pallas_call(
        functools.partial(_ffn_kernel, n_f=n_f),
        out_shape=jax.ShapeDtypeStruct((t_tokens, D_MODEL), F32),
        grid=(t_tokens // tm, n_f),
        in_specs=[pl.BlockSpec((tm, D_MODEL), lambda t, f: (t, 0)),
                  pl.BlockSpec((None, 1, D_MODEL), lambda t, f: (layer, 0, 0)),
                  pl.BlockSpec((None, 1, 6 * D_MODEL), lambda t, f: (row(t), 0, 0)),
                  pl.BlockSpec((None, D_MODEL, tf), lambda t, f: (j, 0, f)),
                  pl.BlockSpec((None, D_MODEL, tf), lambda t, f: (j, 0, f)),
                  pl.BlockSpec((None, tf, D_MODEL), lambda t, f: (j, f, 0))],
        out_specs=pl.BlockSpec((tm, D_MODEL), lambda t, f: (t, 0)),
        scratch_shapes=[pltpu.VMEM((tm, D_MODEL), BF16), pltpu.VMEM((tm, D_MODEL), F32)],
        compiler_params=_params(("arbitrary", "arbitrary")),
        name="ffn",
    )(x, norm_g3, mods3, w1, w3, w2)


def _rope_tables():
    half = A_QK // 2
    nf = half // 2
    t = np.arange(DEC_SEQ)
    pos = np.stack([(t // GRID_W), (t % GRID_W)], axis=1).astype(np.float32)
    inv = jnp.asarray(ROPE_BASE, F32) ** (-jnp.arange(nf, dtype=F32) / nf)
    lane = np.arange(GW) % A_QK
    which = (lane >= half).astype(np.int32)
    freq = lane % nf
    ang = jnp.asarray(pos)[:, which] * inv[freq][None, :]
    return jnp.cos(ang), jnp.sin(ang)


def _tile_vec(v, reps):
    return jnp.tile(v, (1, reps))[:, None, :]


def _prepare_params(w_in, w_out, a_qnorm, a_knorm, a_lam, a_subln, b_gate, b_gate_bias, b_onorm,
                    c_dw, c_dw_b, c_ln_g, c_ln_b, d_qnorm, d_knorm):
    lr0 = 1536
    lr1 = lr0 + 2 * B_GATE_RANK
    pad = PROJ_W - w_in.shape[-1]
    w_in_p = jnp.concatenate([w_in[:, :, :lr0], w_in[:, :, lr1:], w_in[:, :, lr0:lr1],
                              jnp.zeros(w_in.shape[:2] + (pad,), w_in.dtype)], axis=-1).astype(BF16)
    hk = B_HEADS * B_DK
    wg = jnp.zeros((DEPTH, 2, GW, hk), F32)
    wg = wg.at[:, 0, 0:B_GATE_RANK].set(b_gate[:, 0])
    wg = wg.at[:, 1, B_GATE_RANK:2 * B_GATE_RANK].set(b_gate[:, 1])
    return dict(
        w_in=w_in_p, w_out=w_out.astype(BF16),
        a_qnorm=_tile_vec(a_qnorm, GW // A_QK), a_knorm=_tile_vec(a_knorm, GW // A_QK),
        a_lam=a_lam, a_subln=_tile_vec(a_subln, GW // A_V),
        b_gate=wg.astype(BF16), b_gate_bias=b_gate_bias[:, :, None, :], b_onorm=_tile_vec(b_onorm, GW // B_DV),
        c_dw=c_dw, c_dw_b=c_dw_b[:, None, :], c_ln_g=c_ln_g[:, None, :], c_ln_b=c_ln_b[:, None, :],
        d_qnorm=_tile_vec(d_qnorm, GW // D_HEAD), d_knorm=_tile_vec(d_knorm, GW // D_HEAD),
    )


def _lambda_init(i):
    return 0.8 - 0.6 * math.exp(-0.3 * i)


def kernel(x_prompt, x_sample, cache_a_k, cache_a_v, state_b, cache_d_k, cache_d_v, c, c_ctx,
           norm1_g, norm2_g, w_ada, b_ada, w_in, w_out, a_qnorm, a_knorm, a_lam, a_subln,
           b_gate, b_gate_bias, b_onorm, c_dw, c_dw_b, c_ln_g, c_ln_b, d_qnorm, d_knorm, d_rpb,
           ffn_w1, ffn_w3, ffn_w2, moe_router, moe_router_b, moe_w1, moe_w3, moe_w2):
    lp = _prepare_params(w_in, w_out, a_qnorm, a_knorm, a_lam, a_subln, b_gate, b_gate_bias, b_onorm,
                         c_dw, c_dw_b, c_ln_g, c_ln_b, d_qnorm, d_knorm)
    norm1 = norm1_g[:, None, :]
    norm2 = norm2_g[:, None, :]
    b_ada3 = b_ada[:, None, :]
    cond8 = jnp.concatenate([c_ctx[None, :], c, jnp.zeros((8 - 1 - DEC_BATCH, D_MODEL), F32)], axis=0)
    rope = _rope_tables()
    ffn_w = [w.astype(BF16) for w in (ffn_w1, ffn_w3, ffn_w2)]
    lanes = 128
    router_p = jnp.pad(moe_router, ((0, 0), (0, 0), (0, lanes - N_EXPERTS)))
    router_b_p = jnp.pad(moe_router_b, ((0, 0), (0, lanes - N_EXPERTS)))[:, None, :]
    ck_a = cache_a_k.reshape(DEC_BATCH, DEPTH, PAST_LEN, GW)
    cv_a = cache_a_v.reshape(DEC_BATCH, DEPTH, PAST_LEN, GW)
    ck_d = cache_d_k.reshape(DEC_BATCH, DEPTH, PAST_LEN, GW)
    cv_d = cache_d_v.reshape(DEC_BATCH, DEPTH, PAST_LEN, GW)
    s0_lat = _state_to_kernel_layout(state_b.transpose(1, 0, 2, 3, 4, 5).reshape(
        DEPTH * DEC_BATCH, 2, B_HEADS, B_DK, B_DV)).reshape(DEPTH, DEC_BATCH, 2, GW, B_HEADS * B_DK)

    xc = x_prompt.reshape(BATCH * SEQ, D_MODEL)
    xl = x_sample.reshape(DEC_BATCH * DEC_SEQ, D_MODEL)
    new_ak, new_av, new_sb, new_dk, new_dv = [], [], [], [], []
    for i in range(DEPTH):
        lam_init = _lambda_init(i)
        dense_ffn = i % 2 == 0
        mods3 =_adaln(cond8, w_ada, b_ada3, i)[:, None, :]

        proj = _inproj(xc, norm1, mods3, lp["w_in"], i, False, SEQ)
        o_a, ak = _attn_a(proj, lp, i, False, BATCH, SEQ, lam_init)
        o_f, s_f = _gla(proj, lp, i, BATCH, SEQ, False)
        o_b, s_b = _gla(proj, lp, i, BATCH, SEQ, True, o_fwd=o_f)
        o_c = _conv(proj, lp, i, BATCH, SEQ)
        o_d, dk = _attn_d_ctx(proj, lp, i, BATCH, SEQ)
        xc = _outproj(o_a, o_b, o_c, o_d, xc, mods3, lp["w_out"], i, False, SEQ)
        if dense_ffn:
            xc = _ffn(xc, norm2, mods3, *ffn_w, i, i // 2, False, SEQ)
        new_ak.append(ak.reshape(BATCH, SEQ, A_HEADS, 2 * A_QK))
        new_av.append(proj[:, BLK_AV * GW:(BLK_AV + 1) * GW].reshape(BATCH, SEQ, A_HEADS, A_V))
        new_sb.append(jnp.stack([_state_from_kernel_layout(s_f), _state_from_kernel_layout(s_b)], axis=1))
        new_dk.append(dk.reshape(BATCH, SEQ, D_HEADS, D_HEAD))
        new_dv.append(proj[:, BLK_DV * GW:(BLK_DV + 1) * GW].reshape(BATCH, SEQ, D_HEADS, D_HEAD))

        proj = _inproj(xl, norm1, mods3, lp["w_in"], i, True, DEC_SEQ)
        o_a, = _attn_a(proj, lp, i, True, DEC_BATCH, DEC_SEQ, lam_init, ck_a, cv_a, rope)
        o_f, _ = _gla(proj, lp, i, DEC_BATCH, DEC_SEQ, False, s0=s0_lat[i])
        o_b, _ = _gla(proj, lp, i, DEC_BATCH, DEC_SEQ, True, s0=s0_lat[i], o_fwd=o_f)
        o_c = _conv(proj, lp, i, DEC_BATCH, DEC_SEQ)
        o_d = _attn_d_lat(proj, lp, i, ck_d, cv_d, _na_bias_columns(d_rpb[i]))
        xl = _outproj(o_a, o_b, o_c, o_d, xl, mods3, lp["w_out"], i, True, DEC_SEQ)
        if dense_ffn:
            xl = _ffn(xl, norm2, mods3, *ffn_w, i, i // 2, True, DEC_SEQ)
        else:
            xc, xl = _moe(xc, xl, norm2, mods3, router_p, router_b_p, moe_w1, moe_w3, moe_w2, i, i // 2)

    return (xc.reshape(BATCH, SEQ, D_MODEL), xl.reshape(DEC_BATCH, DEC_SEQ, D_MODEL),
            jnp.stack(new_ak, axis=1), jnp.stack(new_av, axis=1), jnp.stack(new_sb, axis=1),
            jnp.stack(new_dk, axis=1), jnp.stack(new_dv, axis=1))
```
